```python
import math
import jax
import jax.numpy as jnp
from jax import lax
import numpy as np

D_MODEL = 1024
BATCH = 8
SEQ = 2048
DEPTH = 4
DEC_BATCH = 128
DEC_SEQ = 4
PAST_LEN = 8192
PAGE_SIZE = 128

N_EVEN = (DEPTH + 1) // 2
N_ODD = DEPTH // 2
MIX_W = D_MODEL
A_W = MIX_W // 2
HEAD_DIM = 64
A_HEADS = A_W // HEAD_DIM
KV_HEADS = 2
Q_PER_KV = A_HEADS // KV_HEADS
KV_W = KV_HEADS * HEAD_DIM
WINDOW = 128
SSM_W = MIX_W - A_W
SSM_CH = 16
SSM_GROUPS = SSM_W // SSM_CH
SSM_STATE = 64
IN_EVEN = A_W + 2 * KV_W + SSM_W
CHUNK = 128
GM_W = 2 * D_MODEL
GM_HEADS = 8
GM_HD = GM_W // GM_HEADS
MEM_LEN = 256
X_HEADS = 4
X_HEAD_DIM = 128
X_W = X_HEADS * X_HEAD_DIM
N_GROUPS = 4
EXPERTS_PER_GROUP = 4
N_EXPERTS = N_GROUPS * EXPERTS_PER_GROUP
TOP_K_INNER = 2
D_EXPERT = D_MODEL // 4
EPS = 1e-6
NEG_INF = -1e30

kernel_name = "hybrid_swa_s5_gmlp_hmoe_step"


def rms_norm(x, g):
    xf = x.astype(jnp.float32)
    y = xf * lax.rsqrt(jnp.mean(xf * xf, axis=-1, keepdims=True) + EPS)
    return (y * g.astype(jnp.float32)).astype(x.dtype)


def alibi_slopes():
    s = jnp.exp2(-8.0 * jnp.arange(1, A_HEADS + 1, dtype=jnp.float32) / A_HEADS)
    return s.reshape(KV_HEADS, Q_PER_KV, 1, 1)


def sink_softmax(s, sinks):
    sk = sinks.astype(jnp.float32).reshape(KV_HEADS, Q_PER_KV, 1, 1)
    m = jnp.maximum(jnp.max(s, axis=-1, keepdims=True), sk)
    p = jnp.exp(s - m)
    return p / (jnp.sum(p, axis=-1, keepdims=True) + jnp.exp(sk - m))


def swa_prompt(q, k, v, sinks):
    n, L = q.shape[0], q.shape[1]
    nb = L // WINDOW
    qb = q.reshape(n, nb, WINDOW, KV_HEADS, Q_PER_KV, HEAD_DIM)
    kb = k.reshape(n, nb, WINDOW, KV_HEADS, HEAD_DIM)
    vb = v.reshape(n, nb, WINDOW, KV_HEADS, HEAD_DIM)
    shift = ((0, 0), (1, 0), (0, 0), (0, 0), (0, 0))
    kc = jnp.concatenate([jnp.pad(kb, shift)[:, :-1], kb], axis=2)
    vc = jnp.concatenate([jnp.pad(vb, shift)[:, :-1], vb], axis=2)
    dist = (WINDOW + jnp.arange(WINDOW))[:, None] - jnp.arange(2 * WINDOW)[None, :]
    in_band = (dist >= 0) & (dist <= WINDOW)
    has_prev = (jnp.arange(nb)[:, None, None] > 0) | (jnp.arange(2 * WINDOW) >= WINDOW)[None, None, :]
    valid = in_band[None] & has_prev
    s = jnp.einsum('nbqkgd,nbskd->nbkgqs', qb, kc).astype(jnp.float32) * (HEAD_DIM ** -0.5)
    s = s - alibi_slopes() * dist.astype(jnp.float32)
    s = jnp.where(valid[None, :, None, None], s, NEG_INF)
    p = sink_softmax(s, sinks).astype(v.dtype)
    o = jnp.einsum('nbkgqs,nbskd->nbqkgd', p, vc)
    return o.reshape(n, L, A_W)


def swa_sample(q, k, v, win_k, win_v, sinks):
    n, T = q.shape[0], q.shape[1]
    kc = jnp.concatenate([win_k.astype(k.dtype), k], axis=1)
    vc = jnp.concatenate([win_v.astype(v.dtype), v], axis=1)
    qh = q.reshape(n, T, KV_HEADS, Q_PER_KV, HEAD_DIM)
    dist = (WINDOW + jnp.arange(T))[:, None] - jnp.arange(WINDOW + T)[None, :]
    valid = (dist >= 0) & (dist <= WINDOW)
    s = jnp.einsum('ntkgd,nskd->nkgts', qh, kc).astype(jnp.float32) * (HEAD_DIM ** -0.5)
    s = s - alibi_slopes() * dist.astype(jnp.float32)
    s = jnp.where(valid, s, NEG_INF)
    p = sink_softmax(s, sinks).astype(v.dtype)
    o = jnp.einsum('nkgts,nskd->ntkgd', p, vc).reshape(n, T, A_W)
    return o, kc[:, T:], vc[:, T:]


def _ssm_combine(e1, e2):
    a1r, a1i, b1r, b1i = e1
    a2r, a2i, b2r, b2i = e2
    return (a2r * a1r - a2i * a1i, a2r * a1i + a2i * a1r,
            a2r * b1r - a2i * b1i + b2r, a2r * b1i + a2i * b1r + b2i)


def s5_glu(u, h0_re, h0_im, a_re, a_im, log_dt, b_re, b_im, c_re, c_im, d_skip, w_glu):
    n, L = u.shape[0], u.shape[1]
    uf = u.astype(jnp.float32).reshape(n, L, SSM_GROUPS, SSM_CH)
    ar, ai = a_re.astype(jnp.float32), a_im.astype(jnp.float32)
    dt = jnp.exp(log_dt.astype(jnp.float32))[:, None]
    mag = jnp.exp(ar * dt)
    abr, abi = mag * jnp.cos(ai * dt), mag * jnp.sin(ai * dt)
    den = ar * ar + ai * ai
    kr = ((abr - 1.0) * ar + abi * ai) / den
    ki = (abi * ar - (abr - 1.0) * ai) / den
    br, bi = b_re.astype(jnp.float32), b_im.astype(jnp.float32)
    bbr = kr[..., None] * br - ki[..., None] * bi
    bbi = kr[..., None] * bi + ki[..., None] * br
    bur = jnp.einsum('gph,nlgh->nlgp', bbr, uf)
    bui = jnp.einsum('gph,nlgh->nlgp', bbi, uf)
    shp = bur.shape
    elems = (jnp.broadcast_to(abr, shp), jnp.broadcast_to(abi, shp), bur, bui)
    pr, pi, hr, hi = lax.associative_scan(_ssm_combine, elems, axis=1)
    h0r = h0_re.astype(jnp.float32)[:, None]
    h0i = h0_im.astype(jnp.float32)[:, None]
    hr, hi = hr + pr * h0r - pi * h0i, hi + pr * h0i + pi * h0r
    y = (jnp.einsum('ghp,nlgp->nlgh', c_re.astype(jnp.float32), hr)
         - jnp.einsum('ghp,nlgp->nlgh', c_im.astype(jnp.float32), hi)
         + d_skip.astype(jnp.float32) * uf)
    y = jax.nn.gelu(y.reshape(n, L, SSM_W)).astype(u.dtype)
    g = y @ w_glu
    out = g[..., :SSM_W] * jax.nn.sigmoid(g[..., SSM_W:])
    return out, hr[:, -1], hi[:, -1]


def gmlp_chunk_mixer(h, w_in, g_v, w_sp, b_sp, w_out):
    n, L = h.shape[0], h.shape[1]
    z = jax.nn.gelu(h @ w_in)
    u = z[..., :GM_W]
    v = rms_norm(z[..., GM_W:], g_v)
    c = min(L, CHUNK)
    vh = v.reshape(n, L // c, c, GM_HEADS, GM_HD)
    ws = jnp.where(jnp.tril(jnp.ones((c, c), dtype=bool)), w_sp[:, :c, :c], 0.0)
    mix = jnp.einsum('hij,ncjhd->ncihd', ws, vh) + b_sp[:, :c].T[None, None, :, :, None]
    out = (u * mix.reshape(n, L, GM_W)) @ w_out
    return out, v


def memory_kv(mem, g_mem, w_k, g_k, w_v):
    m = rms_norm(mem[None], g_mem[:, None, None, :])
    shp = (DEPTH,) + mem.shape[:2] + (X_HEADS, X_HEAD_DIM)
    k = rms_norm(jnp.einsum('lnmd,ldf->lnmf', m, w_k).reshape(shp), g_k[:, None, None, None, :])
    v = jnp.einsum('lnmd,ldf->lnmf', m, w_v).reshape(shp)
    return k, v


def cross_attn(h, mk, mv, w_q, g_q, w_o):
    n, L = h.shape[0], h.shape[1]
    q = rms_norm((h @ w_q).reshape(n, L, X_HEADS, X_HEAD_DIM), g_q)
    s = jnp.einsum('nlhd,nmhd->nhlm', q, mk.astype(q.dtype)).astype(jnp.float32) * (X_HEAD_DIM ** -0.5)
    p = jax.nn.softmax(s, axis=-1).astype(q.dtype)
    o = jnp.einsum('nhlm,nmhd->nlhd', p, mv.astype(q.dtype)).reshape(n, L, X_W)
    return o @ w_o


def hier_moe(h, w_coarse, b_coarse, w_fine, b_fine, w_e1, w_e3, w_e2):
    coarse = (h @ w_coarse + b_coarse).astype(jnp.float32)
    grp = jnp.argmax(coarse, axis=-1)
    p_grp = jnp.take_along_axis(jax.nn.softmax(coarse, axis=-1), grp[..., None], axis=-1)
    fine = (h @ w_fine + b_fine).astype(jnp.float32).reshape(h.shape[:-1] + (N_GROUPS, EXPERTS_PER_GROUP))
    fine_sel = jnp.take_along_axis(fine, grp[..., None, None], axis=-2)[..., 0, :]
    top_v, top_i = lax.top_k(fine_sel, TOP_K_INNER)
    wts = p_grp * jax.nn.softmax(top_v, axis=-1)
    idx = grp[..., None] * EXPERTS_PER_GROUP + top_i
    gates = jnp.einsum('nlk,nlke->nle', wts, jax.nn.one_hot(idx, N_EXPERTS, dtype=jnp.float32)).astype(h.dtype)
    a = jnp.einsum('nld,edf->nlef', h, w_e1)
    b = jnp.einsum('nld,edf->nlef', h, w_e3)
    hid = jax.nn.silu(a) * b * gates[..., None]
    return jnp.einsum('nlef,efd->nld', hid, w_e2)


def run_trunk(x, mem_k, mem_v, win_k, win_v, ssm_re, ssm_im, p):
    n, L = x.shape[0], x.shape[1]
    prompt = win_k is None
    wk_out, wv_out, hr_out, hi_out, cv_out = [], [], [], [], []
    for li in range(DEPTH):
        e = li // 2
        h = rms_norm(x, p['g_mix'][li])
        if li % 2 == 0:
            z = h @ p['w_in_even'][e]
            q = rms_norm(z[..., :A_W].reshape(n, L, A_HEADS, HEAD_DIM), p['g_q'][e])
            k = rms_norm(z[..., A_W:A_W + KV_W].reshape(n, L, KV_HEADS, HEAD_DIM), p['g_k'][e])
            v = z[..., A_W + KV_W:A_W + 2 * KV_W].reshape(n, L, KV_HEADS, HEAD_DIM)
            u = z[..., A_W + 2 * KV_W:]
            if prompt:
                att = swa_prompt(q, k, v, p['sinks'][e])
                new_k, new_v = k[:, L - WINDOW:], v[:, L - WINDOW:]
                h0_re = jnp.zeros((n, SSM_GROUPS, SSM_STATE), jnp.float32)
                h0_im = h0_re
            else:
                att, new_k, new_v = swa_sample(q, k, v, win_k[e], win_v[e], p['sinks'][e])
                h0_re, h0_im = ssm_re[e], ssm_im[e]
            ssm, h_re, h_im = s5_glu(u, h0_re, h0_im, p['ssm_a_re'][e], p['ssm_a_im'][e], p['ssm_log_dt'][e],
                                     p['ssm_b_re'][e], p['ssm_b_im'][e], p['ssm_c_re'][e], p['ssm_c_im'][e],
                                     p['ssm_d'][e], p['w_glu'][e])
            mix = jnp.concatenate([att, ssm], axis=-1) @ p['w_out_even'][e]
            wk_out.append(new_k)
            wv_out.append(new_v)
            hr_out.append(h_re)
            hi_out.append(h_im)
        else:
            mix, v_rows = gmlp_chunk_mixer(h, p['w_in_odd'][e], p['g_v'][e], p['w_spatial'][e],
                                           p['b_spatial'][e], p['w_out_odd'][e])
            if not prompt:
                cv_out.append(v_rows)
        x = x + mix
        x = x + cross_attn(rms_norm(x, p['g_xattn'][li]), mem_k[li], mem_v[li],
                           p['w_xq'][li], p['g_xq'][li], p['w_xo'][li])
        x = x + hier_moe(rms_norm(x, p['g_ffn'][li]), p['w_coarse'][li], p['b_coarse'][li],
                         p['w_fine'][li], p['b_fine'][li], p['w_e1'][li], p['w_e3'][li], p['w_e2'][li])
    cv = None if prompt else jnp.stack(cv_out)
    return x, jnp.stack(wk_out), jnp.stack(wv_out), jnp.stack(hr_out), jnp.stack(hi_out), cv


def setup_inputs(seed: int = 0) -> dict:
    key = jax.random.key(seed)
    ks = iter(jax.random.split(key, 64))

    def nrm(shape, scale=1.0):
        return jax.random.normal(next(ks), shape, jnp.float32) * scale

    def gain(shape):
        return 1.0 + 0.05 * jax.random.normal(next(ks), shape, jnp.float32)

    a_im0 = jnp.pi * jnp.arange(SSM_STATE, dtype=jnp.float32)
    return {
        'x_prompt': nrm((BATCH, SEQ, D_MODEL)),
        'x_sample': nrm((DEC_BATCH, DEC_SEQ, D_MODEL)),
        'cache_win_k': nrm((N_EVEN, DEC_BATCH, WINDOW, KV_HEADS, HEAD_DIM)),
        'cache_win_v': nrm((N_EVEN, DEC_BATCH, WINDOW, KV_HEADS, HEAD_DIM)),
        'state_ssm_re': nrm((N_EVEN, DEC_BATCH, SSM_GROUPS, SSM_STATE), 0.1),
        'state_ssm_im': nrm((N_EVEN, DEC_BATCH, SSM_GROUPS, SSM_STATE), 0.1),
        'cache_mem_k': nrm((DEPTH, DEC_BATCH, MEM_LEN, X_HEADS, X_HEAD_DIM)),
        'cache_mem_v': nrm((DEPTH, DEC_BATCH, MEM_LEN, X_HEADS, X_HEAD_DIM)),
        'mem_prompt': nrm((BATCH, MEM_LEN, D_MODEL)),
        'g_mix': gain((DEPTH, D_MODEL)),
        'g_xattn': gain((DEPTH, D_MODEL)),
        'g_ffn': gain((DEPTH, D_MODEL)),
        'g_mem': gain((DEPTH, D_MODEL)),
        'w_in_even': nrm((N_EVEN, D_MODEL, IN_EVEN), D_MODEL ** -0.5),
        'g_q': gain((N_EVEN, HEAD_DIM)),
        'g_k': gain((N_EVEN, HEAD_DIM)),
        'sinks': nrm((N_EVEN, A_HEADS), 0.5),
        'ssm_a_re': -0.5 + nrm((N_EVEN, SSM_GROUPS, SSM_STATE), 0.01),
        'ssm_a_im': a_im0 + nrm((N_EVEN, SSM_GROUPS, SSM_STATE), 0.01),
        'ssm_log_dt': jax.random.uniform(next(ks), (N_EVEN, SSM_GROUPS), jnp.float32,
                                         minval=math.log(1e-3), maxval=math.log(1e-1)),
        'ssm_b_re': nrm((N_EVEN, SSM_GROUPS, SSM_STATE, SSM_CH), (2 * SSM_CH) ** -0.5),
        'ssm_b_im': nrm((N_EVEN, SSM_GROUPS, SSM_STATE, SSM_CH), (2 * SSM_CH) ** -0.5),
        'ssm_c_re': nrm((N_EVEN, SSM_GROUPS, SSM_CH, SSM_STATE), SSM_STATE ** -0.5),
        'ssm_c_im': nrm((N_EVEN, SSM_GROUPS, SSM_CH, SSM_STATE), SSM_STATE ** -0.5),
        'ssm_d': nrm((N_EVEN, SSM_GROUPS, SSM_CH)),
        'w_glu': nrm((N_EVEN, SSM_W, 2 * SSM_W), SSM_W ** -0.5),
        'w_out_even': nrm((N_EVEN, MIX_W, D_MODEL), MIX_W ** -0.5),
        'w_in_odd': nrm((N_ODD, D_MODEL, 2 * GM_W), D_MODEL ** -0.5),
        'g_v': gain((N_ODD, GM_W)),
        'w_spatial': nrm((N_ODD, GM_HEADS, CHUNK, CHUNK), CHUNK ** -0.5),
        'b_spatial': 1.0 + nrm((N_ODD, GM_HEADS, CHUNK), 0.1),
        'w_out_odd': nrm((N_ODD, GM_W, D_MODEL), GM_W ** -0.5),
        'w_xq': nrm((DEPTH, D_MODEL, X_W), D_MODEL ** -0.5),
        'g_xq': gain((DEPTH, X_HEAD_DIM)),
        'w_xk': nrm((DEPTH, D_MODEL, X_W), D_MODEL ** -0.5),
        'g_xk': gain((DEPTH, X_HEAD_DIM)),
        'w_xv': nrm((DEPTH, D_MODEL, X_W), D_MODEL ** -0.5),
        'w_xo': nrm((DEPTH, X_W, D_MODEL), X_W ** -0.5),
        'w_coarse': nrm((DEPTH, D_MODEL, N_GROUPS), D_MODEL ** -0.5),
        'b_coarse': nrm((DEPTH, N_GROUPS), 0.01),
        'w_fine': nrm((DEPTH, D_MODEL, N_EXPERTS), D_MODEL ** -0.5),
        'b_fine': nrm((DEPTH, N_EXPERTS), 0.01),
        'w_e1': nrm((DEPTH, N_EXPERTS, D_MODEL, D_EXPERT), D_MODEL ** -0.5),
        'w_e3': nrm((DEPTH, N_EXPERTS, D_MODEL, D_EXPERT), D_MODEL ** -0.5),
        'w_e2': nrm((DEPTH, N_EXPERTS, D_EXPERT, D_MODEL), D_EXPERT ** -0.5),
    }


def reference(x_prompt, x_sample, cache_win_k, cache_win_v, state_ssm_re, state_ssm_im,
              cache_mem_k, cache_mem_v, mem_prompt,
              g_mix, g_xattn, g_ffn, g_mem,
              w_in_even, g_q, g_k, sinks,
              ssm_a_re, ssm_a_im, ssm_log_dt, ssm_b_re, ssm_b_im, ssm_c_re, ssm_c_im, ssm_d,
              w_glu, w_out_even,
              w_in_odd, g_v, w_spatial, b_spatial, w_out_odd,
              w_xq, g_xq, w_xk, g_xk, w_xv, w_xo,
              w_coarse, b_coarse, w_fine, b_fine, w_e1, w_e3, w_e2):
    p = dict(g_mix=g_mix, g_xattn=g_xattn, g_ffn=g_ffn,
             w_in_even=w_in_even, g_q=g_q, g_k=g_k, sinks=sinks,
             ssm_a_re=ssm_a_re, ssm_a_im=ssm_a_im, ssm_log_dt=ssm_log_dt,
             ssm_b_re=ssm_b_re, ssm_b_im=ssm_b_im, ssm_c_re=ssm_c_re, ssm_c_im=ssm_c_im, ssm_d=ssm_d,
             w_glu=w_glu, w_out_even=w_out_even,
             w_in_odd=w_in_odd, g_v=g_v, w_spatial=w_spatial, b_spatial=b_spatial, w_out_odd=w_out_odd,
             w_xq=w_xq, g_xq=g_xq, w_xo=w_xo,
             w_coarse=w_coarse, b_coarse=b_coarse, w_fine=w_fine, b_fine=b_fine,
             w_e1=w_e1, w_e3=w_e3, w_e2=w_e2)
    p_mem_k, p_mem_v = memory_kv(mem_prompt, g_mem, w_xk, g_xk, w_xv)
    y_prompt, p_win_k, p_win_v, p_ssm_re, p_ssm_im, _ = run_trunk(
        x_prompt, p_mem_k, p_mem_v, None, None, None, None, p)
    y_sample, s_win_k, s_win_v, s_ssm_re, s_ssm_im, s_chunk_v = run_trunk(
        x_sample, cache_mem_k, cache_mem_v, cache_win_k, cache_win_v, state_ssm_re, state_ssm_im, p)
    return (y_prompt, y_sample, p_win_k, p_win_v, p_ssm_re, p_ssm_im, p_mem_k, p_mem_v,
            s_win_k, s_win_v, s_ssm_re, s_ssm_im, s_chunk_v)
```

```python
import functools
import math

import jax
import jax.numpy as jnp
from jax import lax
from jax.experimental import pallas as pl
from jax.experimental.pallas import tpu as pltpu

F32 = jnp.float32
BF = jnp.bfloat16

D_MODEL = 1024
DEPTH = 4
A_W = 512
HEAD_DIM = 64
A_HEADS = 8
KV_HEADS = 2
Q_PER_KV = 4
KV_W = 128
WINDOW = 128
SSM_W = 512
SSM_CH = 16
SSM_GROUPS = 32
SSM_STATE = 64
SSM_FLAT = SSM_GROUPS * SSM_STATE
IN_EVEN = A_W + 2 * KV_W + SSM_W
CHUNK = 128
GM_W = 2048
GM_HEADS = 8
GM_HD = 256
MEM_LEN = 256
X_HEADS = 4
X_HEAD_DIM = 128
X_W = 512
N_GROUPS = 4
EXPERTS_PER_GROUP = 4
N_EXPERTS = 16
D_EXPERT = 256
EPS = 1e-6
NEG_INF = -1e30
LANES = 128
VMEM_LIMIT_BYTES = 52 * 1024 * 1024
ALIBI_SLOPES = tuple(2.0 ** (-8.0 * (h + 1) / A_HEADS) for h in range(A_HEADS))
SSM_SEQ_BLOCK = 8


def _cparams(*sem):
    return pltpu.CompilerParams(dimension_semantics=sem, vmem_limit_bytes=VMEM_LIMIT_BYTES)


def _rms_rows(x, g):
    return x * lax.rsqrt(jnp.mean(x * x, axis=-1, keepdims=True) + EPS) * g


def _dot(a, b):
    return jnp.dot(a, b, preferred_element_type=F32)


def _dot_nt(a, b):
    return lax.dot_general(a, b, (((1,), (1,)), ((), ())), preferred_element_type=F32)


def _full(shape):
    nd = len(shape)
    return pl.BlockSpec(shape, lambda *_: (0,) * nd)


def _even_in_kernel(x_ref, g_ref, w_ref, gq_ref, gk_ref, bd_ref, q_ref, k_ref, v_ref, u_ref):
    h = _rms_rows(x_ref[...], g_ref[...]).astype(BF)
    z = _dot(h, w_ref[...])
    q = z[:, :A_W]
    k = z[:, A_W:A_W + KV_W]
    bd = bd_ref[...]
    qms = _dot((q * q).astype(BF), bd) * (1.0 / HEAD_DIM)
    kms = _dot((k * k).astype(BF), bd[:KV_W, :KV_W]) * (1.0 / HEAD_DIM)
    q_ref[...] = q * lax.rsqrt(qms + EPS) * gq_ref[...]
    k_ref[...] = k * lax.rsqrt(kms + EPS) * gk_ref[...]
    v_ref[...] = z[:, A_W + KV_W:A_W + 2 * KV_W]
    u_ref[...] = z[:, A_W + 2 * KV_W:]


def _time_major_spec(tm, n, seq, width):
    per_seq = seq // tm
    return pl.BlockSpec((tm, width), lambda i: (i % per_seq, i // per_seq))


def _even_in_proj(x, g, w, gq, gk, bd, tm, n, seq):
    m = x.shape[0]
    row = lambda w_: pl.BlockSpec((tm, w_), lambda i: (i, 0))
    return pl.pallas_call(
        _even_in_kernel,
        grid=(m // tm,),
        in_specs=[row(D_MODEL), _full((1, D_MODEL)), _full((D_MODEL, IN_EVEN)), _full((1, A_W)),
                  _full((1, KV_W)), _full((A_W, A_W))],
        out_specs=[row(A_W), row(KV_W), row(KV_W), _time_major_spec(tm, n, seq, SSM_W) if seq % tm == 0 else row(SSM_W)],
        out_shape=[jax.ShapeDtypeStruct((m, A_W), F32), jax.ShapeDtypeStruct((m, KV_W), F32),
                   jax.ShapeDtypeStruct((m, KV_W), F32),
                   jax.ShapeDtypeStruct((seq, n * SSM_W) if seq % tm == 0 else (m, SSM_W), F32)],
        compiler_params=_cparams("parallel"),
        name="even_in_proj",
    )(x, g, w, gq, gk, bd)


def _head_rows(q, lane_lo):
    rows = []
    for kv in range(KV_HEADS):
        keep = lane_lo if kv == 0 else jnp.logical_not(lane_lo)
        for g in range(Q_PER_KV):
            rows.append(jnp.where(keep, q[:, g * LANES:(g + 1) * LANES], 0.0))
    return jnp.concatenate(rows, axis=0)


def _head_cols(o, r, lane_lo):
    cols = []
    for g in range(Q_PER_KV):
        cols.append(jnp.where(lane_lo, o[g * r:(g + 1) * r], o[(Q_PER_KV + g) * r:(Q_PER_KV + g + 1) * r]))
    return jnp.concatenate(cols, axis=1)


def _swa_prompt_kernel(sinks_ref, q_ref, kc_ref, kp_ref, vc_ref, vp_ref, o_ref):
    b = pl.program_id(1)
    lane_lo = lax.broadcasted_iota(jnp.int32, (1, LANES), 1) < HEAD_DIM
    qrows = _head_rows(q_ref[...], lane_lo).astype(BF)
    kcat = jnp.concatenate([kp_ref[...], kc_ref[...]], axis=0).astype(BF)
    vcat = jnp.concatenate([vp_ref[...], vc_ref[...]], axis=0).astype(BF)
    s = _dot_nt(qrows, kcat) * (HEAD_DIM ** -0.5)
    r = lax.broadcasted_iota(jnp.int32, (WINDOW, 2 * WINDOW), 0)
    c = lax.broadcasted_iota(jnp.int32, (WINDOW, 2 * WINDOW), 1)
    dist = WINDOW + r - c
    valid = (dist >= 0) & (dist <= WINDOW) & ((c >= WINDOW) | (b > 0))
    dist_f = dist.astype(F32)
    ps = []
    for h in range(A_HEADS):
        sh = s[h * WINDOW:(h + 1) * WINDOW] - ALIBI_SLOPES[h] * dist_f
        sh = jnp.where(valid, sh, NEG_INF)
        sk = sinks_ref[h]
        m = jnp.maximum(jnp.max(sh, axis=-1, keepdims=True), sk)
        p = jnp.exp(sh - m)
        den = jnp.sum(p, axis=-1, keepdims=True) + jnp.exp(sk - m)
        ps.append((p / den).astype(BF))
    o = _dot(jnp.concatenate(ps, axis=0), vcat)
    o_ref[...] = _head_cols(o, WINDOW, lane_lo).astype(o_ref.dtype)


def _swa_prompt(q, k, v, sinks, n, seq):
    nb = seq // WINDOW
    cur = lambda w_: pl.BlockSpec((WINDOW, w_), lambda i, b: (i * nb + b, 0))
    prev = lambda w_: pl.BlockSpec((WINDOW, w_), lambda i, b: (i * nb + jnp.maximum(b - 1, 0), 0))
    return pl.pallas_call(
        _swa_prompt_kernel,
        grid=(n, nb),
        in_specs=[pl.BlockSpec(memory_space=pltpu.SMEM), cur(A_W), cur(KV_W), prev(KV_W), cur(KV_W), prev(KV_W)],
        out_specs=cur(A_W),
        out_shape=jax.ShapeDtypeStruct((n * seq, A_W), BF),
        compiler_params=_cparams("parallel", "arbitrary"),
        name="swa_prompt",
    )(sinks, q, k, k, v, v)


SWA_SEQ_BLOCK = 8


def _swa_sample_kernel(t_new, sinks_ref, q_ref, kn_ref, vn_ref, wk_ref, wv_ref, o_ref, nwk_ref, nwv_ref):
    nseq = SWA_SEQ_BLOCK
    pair_rows = 2 * t_new
    npair = nseq // 2
    kn = kn_ref[...]
    vn = vn_ref[...]
    for i in range(nseq):
        nwk_ref[i, 0:WINDOW - t_new, :] = wk_ref[i, t_new:WINDOW, :]
        nwk_ref[i, WINDOW - t_new:WINDOW, :] = kn[i * t_new:(i + 1) * t_new, :]
        nwv_ref[i, 0:WINDOW - t_new, :] = wv_ref[i, t_new:WINDOW, :]
        nwv_ref[i, WINDOW - t_new:WINDOW, :] = vn[i * t_new:(i + 1) * t_new, :]

    lane_lo = lax.broadcasted_iota(jnp.int32, (1, LANES), 1) < HEAD_DIM
    q = q_ref[...]
    knb = kn.astype(BF)
    vnb = vn.astype(BF)
    hr = A_HEADS * pair_rows
    qrows = [_head_rows(q[j * pair_rows:(j + 1) * pair_rows], lane_lo).astype(BF) for j in range(npair)]
    scale = HEAD_DIM ** -0.5
    s_new_all = _dot_nt(jnp.concatenate(qrows, axis=0), knb) * scale

    tq = lax.broadcasted_iota(jnp.int32, (pair_rows, 2 * WINDOW), 0)
    cw = lax.broadcasted_iota(jnp.int32, (pair_rows, 2 * WINDOW), 1)
    dist_w = WINDOW + (tq % t_new) - (cw % WINDOW)
    valid_w = ((tq // t_new) == (cw // WINDOW)) & (dist_w <= WINDOW)
    dist_wf = dist_w.astype(F32)
    tqn = lax.broadcasted_iota(jnp.int32, (pair_rows, nseq * t_new), 0)
    cn = lax.broadcasted_iota(jnp.int32, (pair_rows, nseq * t_new), 1)
    dist_n = (tqn % t_new) - (cn % t_new)
    dist_nf = dist_n.astype(F32)

    p_new_all = []
    o_win_all = []
    for j in range(npair):
        kwin = jnp.concatenate([wk_ref[2 * j], wk_ref[2 * j + 1]], axis=0).astype(BF)
        vwin = jnp.concatenate([wv_ref[2 * j], wv_ref[2 * j + 1]], axis=0).astype(BF)
        s_win = _dot_nt(qrows[j], kwin) * scale
        valid_n = ((2 * j + tqn // t_new) == (cn // t_new)) & (dist_n >= 0)
        p_win = []
        for h in range(A_HEADS):
            sw = jnp.where(valid_w, s_win[h * pair_rows:(h + 1) * pair_rows] - ALIBI_SLOPES[h] * dist_wf, NEG_INF)
            sn = s_new_all[j * hr + h * pair_rows:j * hr + (h + 1) * pair_rows]
            sn = jnp.where(valid_n, sn - ALIBI_SLOPES[h] * dist_nf, NEG_INF)
            sk = sinks_ref[h]
            m = jnp.maximum(jnp.maximum(jnp.max(sw, axis=-1, keepdims=True), jnp.max(sn, axis=-1, keepdims=True)), sk)
            pw = jnp.exp(sw - m)
            pn = jnp.exp(sn - m)
            den = jnp.sum(pw, axis=-1, keepdims=True) + jnp.sum(pn, axis=-1, keepdims=True) + jnp.exp(sk - m)
            p_win.append((pw / den).astype(BF))
            p_new_all.append((pn / den).astype(BF))
        o_win_all.append(_dot(jnp.concatenate(p_win, axis=0), vwin))
    o_new = _dot(jnp.concatenate(p_new_all, axis=0), vnb)
    outs = [_head_cols(o_win_all[j] + o_new[j * hr:(j + 1) * hr], pair_rows, lane_lo) for j in range(npair)]
    o_ref[...] = jnp.concatenate(outs, axis=0).astype(o_ref.dtype)


def _swa_sample(q, kn, vn, win_k, win_v, sinks, n, t_new):
    rows = SWA_SEQ_BLOCK * t_new
    row = lambda w_: pl.BlockSpec((rows, w_), lambda i: (i, 0))
    win = pl.BlockSpec((SWA_SEQ_BLOCK, WINDOW, KV_W), lambda i: (i, 0, 0))
    return pl.pallas_call(
        functools.partial(_swa_sample_kernel, t_new),
        grid=(n // SWA_SEQ_BLOCK,),
        in_specs=[pl.BlockSpec(memory_space=pltpu.SMEM), row(A_W), row(KV_W), row(KV_W), win, win],
        out_specs=[row(A_W), win, win],
        out_shape=[jax.ShapeDtypeStruct((n * t_new, A_W), BF),
                   jax.ShapeDtypeStruct((n, WINDOW, KV_W), F32), jax.ShapeDtypeStruct((n, WINDOW, KV_W), F32)],
        compiler_params=_cparams("parallel"),
        name="swa_sample",
    )(sinks, q, kn, vn, win_k, win_v)


SSM_LANE_CHUNK = 1024


def _s5_kernel(tc, has_h0, *refs):
    if has_h0:
        (u_ref, h0r_ref, h0i_ref, abr_ref, abi_ref, bcat_ref, ccat_ref, d_ref, wglu_ref,
         o_ref, hr_ref, hi_ref, st_scr, car_scr) = refs
    else:
        (u_ref, abr_ref, abi_ref, bcat_ref, ccat_ref, d_ref, wglu_ref,
         o_ref, hr_ref, hi_ref, st_scr, car_scr) = refs
    nb = SSM_SEQ_BLOCK
    ci = pl.program_id(1)

    @pl.when(ci == 0)
    def _():
        if has_h0:
            car_scr[:, :SSM_FLAT] = h0r_ref[...]
            car_scr[:, SSM_FLAT:] = h0i_ref[...]
        else:
            car_scr[...] = jnp.zeros_like(car_scr)

    ut = u_ref[...].reshape(tc * nb, SSM_W)
    st_scr[...] = _dot(ut.astype(BF), bcat_ref[...])

    for lc in range(SSM_FLAT // SSM_LANE_CHUNK):
        lo = lc * SSM_LANE_CHUNK
        re_sl = slice(lo, lo + SSM_LANE_CHUNK)
        im_sl = slice(SSM_FLAT + lo, SSM_FLAT + lo + SSM_LANE_CHUNK)
        ar = jnp.broadcast_to(abr_ref[:, re_sl], (nb, SSM_LANE_CHUNK))
        ai = jnp.broadcast_to(abi_ref[:, re_sl], (nb, SSM_LANE_CHUNK))

        def step(t, carry):
            hr, hi = carry
            rows = pl.ds(pl.multiple_of(t * nb, nb), nb)
            nr = ar * hr - ai * hi + st_scr[rows, re_sl]
            ni = ar * hi + ai * hr + st_scr[rows, im_sl]
            st_scr[rows, re_sl] = nr
            st_scr[rows, im_sl] = ni
            return nr, ni

        hr, hi = lax.fori_loop(0, tc, step, (car_scr[:, re_sl], car_scr[:, im_sl]))
        car_scr[:, re_sl] = hr
        car_scr[:, im_sl] = hi

    y = _dot(st_scr[...].astype(BF), ccat_ref[...]) + d_ref[...] * ut
    y = jax.nn.gelu(y).astype(BF)
    g = _dot(y, wglu_ref[...])
    o_ref[...] = (g[:, :SSM_W] * jax.nn.sigmoid(g[:, SSM_W:])).reshape(tc, nb, SSM_W)

    @pl.when(ci == pl.num_programs(1) - 1)
    def _():
        hr_ref[...] = car_scr[:, :SSM_FLAT]
        hi_ref[...] = car_scr[:, SSM_FLAT:]


def _s5_glu(u, h0, ssm, n, seq, tc):
    nb = SSM_SEQ_BLOCK
    abr, abi, bcat, ccat, dsk, wglu = ssm
    u_spec = pl.BlockSpec((tc, nb, SSM_W), lambda i, c: (c, i, 0))
    st_spec = pl.BlockSpec((nb, SSM_FLAT), lambda i, c: (i, 0))
    consts = [_full((1, SSM_FLAT)), _full((1, SSM_FLAT)), _full((SSM_W, 2 * SSM_FLAT)),
              _full((2 * SSM_FLAT, SSM_W)), _full((1, SSM_W)), _full((SSM_W, 2 * SSM_W))]
    has_h0 = h0 is not None
    in_specs = [u_spec] + ([st_spec, st_spec] if has_h0 else []) + consts
    args = [u] + (list(h0) if has_h0 else []) + [abr, abi, bcat, ccat, dsk, wglu]
    return pl.pallas_call(
        functools.partial(_s5_kernel, tc, has_h0),
        grid=(n // nb, seq // tc),
        in_specs=in_specs,
        out_specs=[u_spec, st_spec, st_spec],
        out_shape=[jax.ShapeDtypeStruct((seq, n, SSM_W), F32), jax.ShapeDtypeStruct((n, SSM_FLAT), F32),
                   jax.ShapeDtypeStruct((n, SSM_FLAT), F32)],
        scratch_shapes=[pltpu.VMEM((nb * tc, 2 * SSM_FLAT), F32), pltpu.VMEM((nb, 2 * SSM_FLAT), F32)],
        compiler_params=_cparams("parallel", "arbitrary"),
        name="s5_glu",
    )(*args)


def _even_out_kernel(x_ref, a_ref, s_ref, wa_ref, ws_ref, o_ref):
    o_ref[...] = x_ref[...] + _dot(a_ref[...], wa_ref[...]) + _dot(s_ref[...].astype(BF), ws_ref[...])


def _even_out_proj(x, att, ssm, wa, ws, tm, n, seq):
    m = x.shape[0]
    row = lambda w_: pl.BlockSpec((tm, w_), lambda i: (i, 0))
    ssm_spec = _time_major_spec(tm, n, seq, SSM_W) if seq % tm == 0 else row(SSM_W)
    return pl.pallas_call(
        _even_out_kernel,
        grid=(m // tm,),
        in_specs=[row(D_MODEL), row(A_W), ssm_spec, _full((A_W, D_MODEL)), _full((SSM_W, D_MODEL))],
        out_specs=row(D_MODEL),
        out_shape=jax.ShapeDtypeStruct((m, D_MODEL), F32),
        compiler_params=_cparams("parallel"),
        name="even_out_proj",
    )(x, att, ssm, wa, ws)


def _odd_in_kernel(x_ref, g_ref, w_ref, gv_ref, u_ref, v_ref):
    h = _rms_rows(x_ref[...], g_ref[...]).astype(BF)
    u_ref[...] = jax.nn.gelu(_dot(h, w_ref[:, :GM_W])).astype(u_ref.dtype)
    zv = jax.nn.gelu(_dot(h, w_ref[:, GM_W:]))
    v_ref[...] = _rms_rows(zv, gv_ref[...]).astype(v_ref.dtype)


def _odd_in_proj(x, g, w, gv, tm, v_dtype):
    m = x.shape[0]
    row = lambda w_: pl.BlockSpec((tm, w_), lambda i: (i, 0))
    return pl.pallas_call(
        _odd_in_kernel,
        grid=(m // tm,),
        in_specs=[row(D_MODEL), _full((1, D_MODEL)), _full((D_MODEL, 2 * GM_W)), _full((1, GM_W))],
        out_specs=[row(GM_W), row(GM_W)],
        out_shape=[jax.ShapeDtypeStruct((m, GM_W), BF), jax.ShapeDtypeStruct((m, GM_W), v_dtype)],
        compiler_params=_cparams("parallel"),
        name="odd_in_proj",
    )(x, g, w, gv)


def _gmlp_mix_kernel(csize, x_ref, u_ref, v_ref, wsp_ref, bsp_ref, wo_ref, o_ref, gated_scr):
    tm = x_ref.shape[0]
    i = lax.broadcasted_iota(jnp.int32, (CHUNK, CHUNK), 0)
    j = lax.broadcasted_iota(jnp.int32, (CHUNK, CHUNK), 1)
    keep = (j <= i) & ((i // csize) == (j // csize))
    bsp = bsp_ref[...]
    for h in range(GM_HEADS):
        ws = jnp.where(keep, wsp_ref[h], 0.0).astype(BF)
        b_col = bsp[:, h:h + 1]
        cols = slice(h * GM_HD, (h + 1) * GM_HD)
        for c in range(tm // CHUNK):
            rows = slice(c * CHUNK, (c + 1) * CHUNK)
            mix = _dot(ws, v_ref[rows, cols].astype(BF)) + b_col
            gated_scr[rows, cols] = (u_ref[rows, cols].astype(F32) * mix).astype(BF)
    o_ref[...] = x_ref[...] + _dot(gated_scr[...], wo_ref[...])


def _gmlp_mix(x, u, v, wsp, bsp, wo, csize, tm):
    m = x.shape[0]
    row = lambda w_: pl.BlockSpec((tm, w_), lambda i: (i, 0))
    return pl.pallas_call(
        functools.partial(_gmlp_mix_kernel, csize),
        grid=(m // tm,),
        in_specs=[row(D_MODEL), row(GM_W), row(GM_W), _full((GM_HEADS, CHUNK, CHUNK)), _full((CHUNK, GM_HEADS)),
                  _full((GM_W, D_MODEL))],
        out_specs=row(D_MODEL),
        out_shape=jax.ShapeDtypeStruct((m, D_MODEL), F32),
        scratch_shapes=[pltpu.VMEM((tm, GM_W), BF)],
        compiler_params=_cparams("parallel"),
        name="gmlp_mix",
    )(x, u, v, wsp, bsp, wo)


def _head_norm(z, g):
    cols = []
    for h in range(X_HEADS):
        zh = z[:, h * X_HEAD_DIM:(h + 1) * X_HEAD_DIM]
        cols.append(_rms_rows(zh, g))
    return jnp.concatenate(cols, axis=1)


def _memory_kv_kernel(mem_ref, g_ref, wk_ref, gk_ref, wv_ref, k_ref, v_ref):
    m = _rms_rows(mem_ref[...], g_ref[0]).astype(BF)
    k_ref[0] = _head_norm(_dot(m, wk_ref[0]), gk_ref[0])
    v_ref[0] = _dot(m, wv_ref[0])


def _memory_kv(mem, g_mem, w_k, g_k, w_v, tm):
    m = mem.shape[0]
    per_layer = lambda a, b: pl.BlockSpec((1, a, b), lambda l, i: (l, 0, 0))
    out_spec = pl.BlockSpec((1, tm, X_W), lambda l, i: (l, i, 0))
    return pl.pallas_call(
        _memory_kv_kernel,
        grid=(DEPTH, m // tm),
        in_specs=[pl.BlockSpec((tm, D_MODEL), lambda l, i: (i, 0)), per_layer(1, D_MODEL), per_layer(D_MODEL, X_W),
                  per_layer(1, X_HEAD_DIM), per_layer(D_MODEL, X_W)],
        out_specs=[out_spec, out_spec],
        out_shape=[jax.ShapeDtypeStruct((DEPTH, m, X_W), F32), jax.ShapeDtypeStruct((DEPTH, m, X_W), F32)],
        compiler_params=_cparams("parallel", "parallel"),
        name="memory_kv",
    )(mem, g_mem, w_k, g_k, w_v)


def _xattn_prompt_kernel(x_ref, g_ref, wq_ref, gq_ref, mk_ref, mv_ref, wo_ref, o_ref):
    x = x_ref[...]
    q = _head_norm(_dot(_rms_rows(x, g_ref[...]).astype(BF), wq_ref[...]), gq_ref[...]).astype(BF)
    mk = mk_ref[...].astype(BF)
    mv = mv_ref[...].astype(BF)
    outs = []
    for h in range(X_HEADS):
        cols = slice(h * X_HEAD_DIM, (h + 1) * X_HEAD_DIM)
        s = _dot_nt(q[:, cols], mk[:, cols]) * (X_HEAD_DIM ** -0.5)
        m = jnp.max(s, axis=-1, keepdims=True)
        p = jnp.exp(s - m)
        p = (p / jnp.sum(p, axis=-1, keepdims=True)).astype(BF)
        outs.append(_dot(p, mv[:, cols]))
    o = jnp.concatenate(outs, axis=1).astype(BF)
    o_ref[...] = x + _dot(o, wo_ref[...])


def _xattn_prompt(x, g, wq, gq, mk, mv, wo, n, seq, tq):
    nq = seq // tq
    row = pl.BlockSpec((tq, D_MODEL), lambda i, j: (i * nq + j, 0))
    mem = pl.BlockSpec((MEM_LEN, X_W), lambda i, j: (i, 0))
    return pl.pallas_call(
        _xattn_prompt_kernel,
        grid=(n, nq),
        in_specs=[row, _full((1, D_MODEL)), _full((D_MODEL, X_W)), _full((1, X_HEAD_DIM)), mem, mem,
                  _full((X_W, D_MODEL))],
        out_specs=row,
        out_shape=jax.ShapeDtypeStruct((n * seq, D_MODEL), F32),
        compiler_params=_cparams("parallel", "arbitrary"),
        name="xattn_prompt",
    )(x, g, wq, gq, mk, mv, wo)


XATTN_SEQ_BLOCK = 8


def _xattn_sample_kernel(t_new, x_ref, g_ref, wq_ref, gq_ref, mk_ref, mv_ref, wo_ref, o_ref):
    pair_rows = 2 * t_new
    x = x_ref[...]
    q = _head_norm(_dot(_rms_rows(x, g_ref[...]).astype(BF), wq_ref[...]), gq_ref[...])
    lane = lax.broadcasted_iota(jnp.int32, (1, X_W), 1)
    head_mask = [(lane // X_HEAD_DIM) == h for h in range(X_HEADS)]
    tq = lax.broadcasted_iota(jnp.int32, (pair_rows, 2 * MEM_LEN), 0)
    cm = lax.broadcasted_iota(jnp.int32, (pair_rows, 2 * MEM_LEN), 1)
    same_seq = (tq // t_new) == (cm // MEM_LEN)
    outs = []
    for j in range(XATTN_SEQ_BLOCK // 2):
        qp = q[j * pair_rows:(j + 1) * pair_rows]
        qrows = jnp.concatenate([jnp.where(head_mask[h], qp, 0.0) for h in range(X_HEADS)], axis=0).astype(BF)
        kp = jnp.concatenate([mk_ref[2 * j], mk_ref[2 * j + 1]], axis=0).astype(BF)
        vp = jnp.concatenate([mv_ref[2 * j], mv_ref[2 * j + 1]], axis=0).astype(BF)
        s = _dot_nt(qrows, kp) * (X_HEAD_DIM ** -0.5)
        ps = []
        for h in range(X_HEADS):
            sh = jnp.where(same_seq, s[h * pair_rows:(h + 1) * pair_rows], NEG_INF)
            m = jnp.max(sh, axis=-1, keepdims=True)
            p = jnp.exp(sh - m)
            ps.append((p / jnp.sum(p, axis=-1, keepdims=True)).astype(BF))
        o = _dot(jnp.concatenate(ps, axis=0), vp)
        acc = jnp.where(head_mask[0], o[:pair_rows], 0.0)
        for h in range(1, X_HEADS):
            acc = acc + jnp.where(head_mask[h], o[h * pair_rows:(h + 1) * pair_rows], 0.0)
        outs.append(acc)
    o_all = jnp.concatenate(outs, axis=0).astype(BF)
    o_ref[...] = x + _dot(o_all, wo_ref[...])


def _xattn_sample(x, g, wq, gq, mk, mv, wo, n, t_new):
    rows = XATTN_SEQ_BLOCK * t_new
    row = pl.BlockSpec((rows, D_MODEL), lambda i: (i, 0))
    mem = pl.BlockSpec((XATTN_SEQ_BLOCK, MEM_LEN, X_W), lambda i: (i, 0, 0))
    return pl.pallas_call(
        functools.partial(_xattn_sample_kernel, t_new),
        grid=(n // XATTN_SEQ_BLOCK,),
        in_specs=[row, _full((1, D_MODEL)), _full((D_MODEL, X_W)), _full((1, X_HEAD_DIM)), mem, mem,
                  _full((X_W, D_MODEL))],
        out_specs=row,
        out_shape=jax.ShapeDtypeStruct((n * t_new, D_MODEL), F32),
        compiler_params=_cparams("parallel"),
        name="xattn_sample",
    )(x, g, wq, gq, mk, mv, wo)


ROUTER_COARSE_LANE = N_EXPERTS
BIG = 3.0e38


def _router_kernel(x_ref, g_ref, whi_ref, wlo_ref, b_ref, gates_ref):
    h = _rms_rows(x_ref[...], g_ref[...])
    h_hi = h.astype(BF)
    h_lo = (h - h_hi.astype(F32)).astype(BF)
    whi = whi_ref[...]
    lg = _dot(h_hi, whi) + _dot(h_lo, whi) + _dot(h_hi, wlo_ref[...]) + b_ref[...]
    lane = lax.broadcasted_iota(jnp.int32, lg.shape, 1)
    lane_f = lane.astype(F32)
    is_c = (lane >= ROUTER_COARSE_LANE) & (lane < ROUTER_COARSE_LANE + N_GROUPS)
    cmax = jnp.max(jnp.where(is_c, lg, -BIG), axis=-1, keepdims=True)
    grp_lane = jnp.min(jnp.where(is_c & (lg == cmax), lane_f, BIG), axis=-1, keepdims=True)
    grp = grp_lane.astype(jnp.int32) - ROUTER_COARSE_LANE
    p_grp = 1.0 / jnp.sum(jnp.where(is_c, jnp.exp(jnp.where(is_c, lg, cmax) - cmax), 0.0), axis=-1, keepdims=True)
    sel = (lane < N_EXPERTS) & ((lane // EXPERTS_PER_GROUP) == grp)
    v1 = jnp.max(jnp.where(sel, lg, -BIG), axis=-1, keepdims=True)
    i1 = jnp.min(jnp.where(sel & (lg == v1), lane_f, BIG), axis=-1, keepdims=True)
    sel2 = sel & (lane_f != i1)
    v2 = jnp.max(jnp.where(sel2, lg, -BIG), axis=-1, keepdims=True)
    i2 = jnp.min(jnp.where(sel2 & (lg == v2), lane_f, BIG), axis=-1, keepdims=True)
    e2 = jnp.exp(v2 - v1)
    w1 = p_grp / (1.0 + e2)
    w2 = p_grp * e2 / (1.0 + e2)
    gates_ref[...] = jnp.where(lane_f == i1, w1, 0.0) + jnp.where(lane_f == i2, w2, 0.0)


def _router(x, g, whi, wlo, b, tm):
    m = x.shape[0]
    return pl.pallas_call(
        _router_kernel,
        grid=(m // tm,),
        in_specs=[pl.BlockSpec((tm, D_MODEL), lambda i: (i, 0)), _full((1, D_MODEL)), _full((D_MODEL, LANES)),
                  _full((D_MODEL, LANES)), _full((1, LANES))],
        out_specs=pl.BlockSpec((tm, LANES), lambda i: (i, 0)),
        out_shape=jax.ShapeDtypeStruct((m, LANES), F32),
        compiler_params=_cparams("parallel"),
        name="moe_router",
    )(x, g, whi, wlo, b)


def _experts_kernel(x_ref, g_ref, gates_ref, w1_ref, w3_ref, w2_ref, o_ref, h_scr, acc_scr):
    gi = pl.program_id(1)

    @pl.when(gi == 0)
    def _():
        h_scr[...] = _rms_rows(x_ref[...], g_ref[...]).astype(BF)
        acc_scr[...] = jnp.zeros_like(acc_scr)

    h = h_scr[...]
    gates = gates_ref[...]
    lane = lax.broadcasted_iota(jnp.int32, gates.shape, 1)
    for j in range(EXPERTS_PER_GROUP):
        gcol = jnp.sum(jnp.where(lane == gi * EXPERTS_PER_GROUP + j, gates, 0.0), axis=-1, keepdims=True)
        a = _dot(h, w1_ref[j])
        b = _dot(h, w3_ref[j])
        hid = (a * jax.nn.sigmoid(a)) * b * gcol
        acc_scr[...] += _dot(hid.astype(BF), w2_ref[j])

    @pl.when(gi == pl.num_programs(1) - 1)
    def _():
        o_ref[...] = x_ref[...] + acc_scr[...]


def _experts(x, g, gates, w1, w3, w2, tm):
    m = x.shape[0]
    row = lambda w_: pl.BlockSpec((tm, w_), lambda i, e: (i, 0))
    return pl.pallas_call(
        _experts_kernel,
        grid=(m // tm, N_GROUPS),
        in_specs=[row(D_MODEL), _full((1, D_MODEL)), row(LANES),
                  pl.BlockSpec((EXPERTS_PER_GROUP, D_MODEL, D_EXPERT), lambda i, e: (e, 0, 0)),
                  pl.BlockSpec((EXPERTS_PER_GROUP, D_MODEL, D_EXPERT), lambda i, e: (e, 0, 0)),
                  pl.BlockSpec((EXPERTS_PER_GROUP, D_EXPERT, D_MODEL), lambda i, e: (e, 0, 0))],
        out_specs=row(D_MODEL),
        out_shape=jax.ShapeDtypeStruct((m, D_MODEL), F32),
        scratch_shapes=[pltpu.VMEM((tm, D_MODEL), BF), pltpu.VMEM((tm, D_MODEL), F32)],
        compiler_params=_cparams("parallel", "arbitrary"),
        name="moe_experts",
    )(x, g, gates, w1, w3, w2)


def _q_col_perm():
    idx = []
    for g in range(Q_PER_KV):
        for kv in range(KV_HEADS):
            base = (kv * Q_PER_KV + g) * HEAD_DIM
            idx.extend(range(base, base + HEAD_DIM))
    return jnp.asarray(idx, dtype=jnp.int32)


def _prep_ssm(a_re, a_im, log_dt, b_re, b_im, c_re, c_im, d_skip, w_glu):
    ar, ai = a_re.astype(F32), a_im.astype(F32)
    dt = jnp.exp(log_dt.astype(F32))[:, None]
    mag = jnp.exp(ar * dt)
    abr, abi = mag * jnp.cos(ai * dt), mag * jnp.sin(ai * dt)
    den = ar * ar + ai * ai
    kr = ((abr - 1.0) * ar + abi * ai) / den
    ki = (abi * ar - (abr - 1.0) * ai) / den
    br, bi = b_re.astype(F32), b_im.astype(F32)
    bbr = kr[..., None] * br - ki[..., None] * bi
    bbi = kr[..., None] * bi + ki[..., None] * br
    eye = jnp.eye(SSM_GROUPS, dtype=F32)
    blockdiag_in = lambda b: jnp.einsum('gph,gk->ghkp', b, eye).reshape(SSM_W, SSM_FLAT)
    blockdiag_out = lambda c: jnp.einsum('ghp,gk->gpkh', c, eye).reshape(SSM_FLAT, SSM_W)
    bcat = jnp.concatenate([blockdiag_in(bbr), blockdiag_in(bbi)], axis=1).astype(BF)
    ccat = jnp.concatenate([blockdiag_out(c_re.astype(F32)), -blockdiag_out(c_im.astype(F32))], axis=0).astype(BF)
    return (abr.reshape(1, SSM_FLAT), abi.reshape(1, SSM_FLAT), bcat, ccat,
            d_skip.astype(F32).reshape(1, SSM_W), w_glu.astype(BF))


def _run_trunk(x, n, seq, mem_k, mem_v, caches, p):
    prompt = caches is None
    tm = 512
    wk_out, wv_out, hr_out, hi_out, cv_out = [], [], [], [], []
    for li in range(DEPTH):
        e = li // 2
        if li % 2 == 0:
            q, k, v, u = _even_in_proj(x, p['g_mix'][li], p['w_in_even'][e], p['g_q'][e], p['g_k'][e], p['bd'],
                                       tm, n, seq)
            if prompt:
                att = _swa_prompt(q, k, v, p['sinks'][e], n, seq)
                k3, v3 = k.reshape(n, seq, KV_W), v.reshape(n, seq, KV_W)
                new_k, new_v = k3[:, seq - WINDOW:], v3[:, seq - WINDOW:]
                ssm, h_re, h_im = _s5_glu(u.reshape(seq, n, SSM_W), None, p['ssm'][e], n, seq, 64)
                ssm = ssm.reshape(seq, n * SSM_W)
            else:
                win_k, win_v, ssm_re, ssm_im = caches
                att, new_k, new_v = _swa_sample(q, k, v, win_k[e], win_v[e], p['sinks'][e], n, seq)
                u_tm = jnp.swapaxes(u.reshape(n, seq, SSM_W), 0, 1)
                ssm, h_re, h_im = _s5_glu(u_tm, (ssm_re[e], ssm_im[e]), p['ssm'][e], n, seq, seq)
                ssm = jnp.swapaxes(ssm, 0, 1).reshape(n * seq, SSM_W)
            x = _even_out_proj(x, att, ssm, p['w_out_att'][e], p['w_out_ssm'][e], tm, n, seq)
            wk_out.append(new_k.reshape(n, WINDOW, KV_HEADS, HEAD_DIM))
            wv_out.append(new_v.reshape(n, WINDOW, KV_HEADS, HEAD_DIM))
            hr_out.append(h_re.reshape(n, SSM_GROUPS, SSM_STATE))
            hi_out.append(h_im.reshape(n, SSM_GROUPS, SSM_STATE))
        else:
            u, v = _odd_in_proj(x, p['g_mix'][li], p['w_in_odd'][e], p['g_v'][e], 256, BF if prompt else F32)
            wsp, bsp, csize = p['gmlp_prompt'][e] if prompt else p['gmlp_sample'][e]
            x = _gmlp_mix(x, u, v, wsp, bsp, p['w_out_odd'][e], csize, tm)
            if not prompt:
                cv_out.append(v.reshape(n, seq, GM_W))
        if prompt:
            x = _xattn_prompt(x, p['g_xattn'][li], p['w_xq'][li], p['g_xq'][li],
                              mem_k[li].reshape(n * MEM_LEN, X_W), mem_v[li].reshape(n * MEM_LEN, X_W),
                              p['w_xo'][li], n, seq, 512)
        else:
            x = _xattn_sample(x, p['g_xattn'][li], p['w_xq'][li], p['g_xq'][li], mem_k[li], mem_v[li],
                              p['w_xo'][li], n, seq)
        gates = _router(x, p['g_ffn'][li], p['w_router_hi'][li], p['w_router_lo'][li], p['b_router'][li], tm)
        x = _experts(x, p['g_ffn'][li], gates, p['w_e1'][li], p['w_e3'][li], p['w_e2'][li], tm)
    cv = None if prompt else jnp.stack(cv_out)
    return x, jnp.stack(wk_out), jnp.stack(wv_out), jnp.stack(hr_out), jnp.stack(hi_out), cv


def kernel(x_prompt, x_sample, cache_win_k, cache_win_v, state_ssm_re, state_ssm_im, cache_mem_k, cache_mem_v, mem_prompt, g_mix, g_xattn, g_ffn, g_mem, w_in_even, g_q, g_k, sinks, ssm_a_re, ssm_a_im, ssm_log_dt, ssm_b_re, ssm_b_im, ssm_c_re, ssm_c_im, ssm_d, w_glu, w_out_even, w_in_odd, g_v, w_spatial, b_spatial, w_out_odd, w_xq, g_xq, w_xk, g_xk, w_xv, w_xo, w_coarse, b_coarse, w_fine, b_fine, w_e1, w_e3, w_e2):
    batch, seq = x_prompt.shape[0], x_prompt.shape[1]
    dec_batch, dec_seq = x_sample.shape[0], x_sample.shape[1]
    n_even = w_in_even.shape[0]
    n_odd = w_in_odd.shape[0]

    qperm = _q_col_perm()
    row1 = lambda a: a.astype(F32)[:, None, :]
    p = {}
    p['g_mix'], p['g_xattn'], p['g_ffn'] = row1(g_mix), row1(g_xattn), row1(g_ffn)
    w_in_q = jnp.take(w_in_even[:, :, :A_W], qperm, axis=2)
    p['w_in_even'] = jnp.concatenate([w_in_q, w_in_even[:, :, A_W:]], axis=2).astype(BF)
    p['g_q'] = jnp.tile(g_q.astype(F32), (1, A_HEADS))[:, None, :]
    p['g_k'] = jnp.tile(g_k.astype(F32), (1, KV_HEADS))[:, None, :]
    head_id = jnp.arange(A_W) // HEAD_DIM
    p['bd'] = (head_id[:, None] == head_id[None, :]).astype(BF)
    p['sinks'] = sinks.astype(F32)
    p['ssm'] = [_prep_ssm(ssm_a_re[e], ssm_a_im[e], ssm_log_dt[e], ssm_b_re[e], ssm_b_im[e], ssm_c_re[e],
                          ssm_c_im[e], ssm_d[e], w_glu[e]) for e in range(n_even)]
    p['w_out_att'] = jnp.take(w_out_even[:, :A_W, :], qperm, axis=1).astype(BF)
    p['w_out_ssm'] = w_out_even[:, A_W:, :].astype(BF)
    p['w_in_odd'] = w_in_odd.astype(BF)
    p['g_v'] = row1(g_v)
    p['w_out_odd'] = w_out_odd.astype(BF)
    cs = min(dec_seq, CHUNK)
    reps = CHUNK // cs
    p['gmlp_prompt'] = [(w_spatial[e].astype(F32), b_spatial[e].astype(F32).T, CHUNK) for e in range(n_odd)]
    p['gmlp_sample'] = [(jnp.tile(w_spatial[e, :, :cs, :cs].astype(F32), (1, reps, reps)),
                         jnp.tile(b_spatial[e, :, :cs].astype(F32).T, (reps, 1)), cs) for e in range(n_odd)]
    p['w_xq'] = w_xq.astype(BF)
    p['g_xq'] = row1(g_xq)
    p['w_xo'] = w_xo.astype(BF)
    w_router = jnp.concatenate([w_fine, w_coarse], axis=2).astype(F32)
    w_router = jnp.pad(w_router, ((0, 0), (0, 0), (0, LANES - w_router.shape[2])))
    p['w_router_hi'] = w_router.astype(BF)
    p['w_router_lo'] = (w_router - p['w_router_hi'].astype(F32)).astype(BF)
    b_router = jnp.concatenate([b_fine, b_coarse], axis=1).astype(F32)
    p['b_router'] = jnp.pad(b_router, ((0, 0), (0, LANES - b_router.shape[1])))[:, None, :]
    p['w_e1'], p['w_e3'], p['w_e2'] = w_e1.astype(BF), w_e3.astype(BF), w_e2.astype(BF)

    mem2d = mem_prompt.reshape(batch * MEM_LEN, D_MODEL)
    pk, pv = _memory_kv(mem2d, row1(g_mem), w_xk.astype(BF), row1(g_xk), w_xv.astype(BF), 512)
    pk = pk.reshape(DEPTH, batch, MEM_LEN, X_W)
    pv = pv.reshape(DEPTH, batch, MEM_LEN, X_W)
    y_p, p_wk, p_wv, p_hr, p_hi, _ = _run_trunk(x_prompt.reshape(batch * seq, D_MODEL), batch, seq, pk, pv, None, p)

    smk = cache_mem_k.reshape(DEPTH, dec_batch, MEM_LEN, X_W)
    smv = cache_mem_v.reshape(DEPTH, dec_batch, MEM_LEN, X_W)
    caches = (cache_win_k.reshape(n_even, dec_batch, WINDOW, KV_W), cache_win_v.reshape(n_even, dec_batch, WINDOW, KV_W),
              state_ssm_re.reshape(n_even, dec_batch, SSM_FLAT), state_ssm_im.reshape(n_even, dec_batch, SSM_FLAT))
    y_s, s_wk, s_wv, s_hr, s_hi, s_cv = _run_trunk(x_sample.reshape(dec_batch * dec_seq, D_MODEL), dec_batch, dec_seq,
                                                   smk, smv, caches, p)

    mem_shape = (DEPTH, batch, MEM_LEN, X_HEADS, X_HEAD_DIM)
    return (y_p.reshape(batch, seq, D_MODEL), y_s.reshape(dec_batch, dec_seq, D_MODEL), p_wk, p_wv, p_hr, p_hi,
            pk.reshape(mem_shape), pv.reshape(mem_shape), s_wk, s_wv, s_hr, s_hi, s_cv)
```

```python
import functools
import math

import jax
import jax.numpy as jnp
from jax import lax
from jax.experimental import pallas as pl
from jax.experimental.pallas import tpu as pltpu

F32 = jnp.float32
BF = jnp.bfloat16

D_MODEL = 1024
DEPTH = 4
A_W = 512
HEAD_DIM = 64
A_HEADS = 8
KV_HEADS = 2
Q_PER_KV = 4
KV_W = 128
WINDOW = 128
SSM_W = 512
SSM_CH = 16
SSM_GROUPS = 32
SSM_STATE = 64
SSM_FLAT = SSM_GROUPS * SSM_STATE
IN_EVEN = A_W + 2 * KV_W + SSM_W
CHUNK = 128
GM_W = 2048
GM_HEADS = 8
GM_HD = 256
MEM_LEN = 256
X_HEADS = 4
X_HEAD_DIM = 128
X_W = 512
N_GROUPS = 4
EXPERTS_PER_GROUP = 4
N_EXPERTS = 16
D_EXPERT = 256
EPS = 1e-6
NEG_INF = -1e30
LANES = 128
VMEM_LIMIT_BYTES = 52 * 1024 * 1024
ALIBI_SLOPES = tuple(2.0 ** (-8.0 * (h + 1) / A_HEADS) for h in range(A_HEADS))
SSM_SEQ_BLOCK = 8


def _cparams(*sem):
    return pltpu.CompilerParams(dimension_semantics=sem, vmem_limit_bytes=VMEM_LIMIT_BYTES)


def _rms_rows(x, g):
    return x * lax.rsqrt(jnp.mean(x * x, axis=-1, keepdims=True) + EPS) * g


def _dot(a, b):
    return jnp.dot(a, b, preferred_element_type=F32)


def _dot_nt(a, b):
    return lax.dot_general(a, b, (((1,), (1,)), ((), ())), preferred_element_type=F32)


def _full(shape):
    nd = len(shape)
    return pl.BlockSpec(shape, lambda *_: (0,) * nd)


def _even_in_kernel(x_ref, g_ref, w_ref, gq_ref, gk_ref, bd_ref, q_ref, k_ref, v_ref, u_ref):
    h = _rms_rows(x_ref[...], g_ref[...]).astype(BF)
    z = _dot(h, w_ref[...])
    q = z[:, :A_W]
    k = z[:, A_W:A_W + KV_W]
    bd = bd_ref[...]
    qms = _dot((q * q).astype(BF), bd) * (1.0 / HEAD_DIM)
    kms = _dot((k * k).astype(BF), bd[:KV_W, :KV_W]) * (1.0 / HEAD_DIM)
    q_ref[...] = q * lax.rsqrt(qms + EPS) * gq_ref[...]
    k_ref[...] = k * lax.rsqrt(kms + EPS) * gk_ref[...]
    v_ref[...] = z[:, A_W + KV_W:A_W + 2 * KV_W]
    u_ref[...] = z[:, A_W + 2 * KV_W:]


def _time_major_spec(tm, n, seq, width):
    per_seq = seq // tm
    return pl.BlockSpec((tm, width), lambda i: (i % per_seq, i // per_seq))


def _even_in_proj(x, g, w, gq, gk, bd, tm, n, seq):
    m = x.shape[0]
    row = lambda w_: pl.BlockSpec((tm, w_), lambda i: (i, 0))
    return pl.pallas_call(
        _even_in_kernel,
        grid=(m // tm,),
        in_specs=[row(D_MODEL), _full((1, D_MODEL)), _full((D_MODEL, IN_EVEN)), _full((1, A_W)),
                  _full((1, KV_W)), _full((A_W, A_W))],
        out_specs=[row(A_W), row(KV_W), row(KV_W), _time_major_spec(tm, n, seq, SSM_W) if seq % tm == 0 else row(SSM_W)],
        out_shape=[jax.ShapeDtypeStruct((m, A_W), F32), jax.ShapeDtypeStruct((m, KV_W), F32),
                   jax.ShapeDtypeStruct((m, KV_W), F32),
                   jax.ShapeDtypeStruct((seq, n * SSM_W) if seq % tm == 0 else (m, SSM_W), F32)],
        compiler_params=_cparams("parallel"),
        name="even_in_proj",
    )(x, g, w, gq, gk, bd)


def _head_rows(q, lane_lo):
    rows = []
    for kv in range(KV_HEADS):
        keep = lane_lo if kv == 0 else jnp.logical_not(lane_lo)
        for g in range(Q_PER_KV):
            rows.append(jnp.where(keep, q[:, g * LANES:(g + 1) * LANES], 0.0))
    return jnp.concatenate(rows, axis=0)


def _head_cols(o, r, lane_lo):
    cols = []
    for g in range(Q_PER_KV):
        cols.append(jnp.where(lane_lo, o[g * r:(g + 1) * r], o[(Q_PER_KV + g) * r:(Q_PER_KV + g + 1) * r]))
    return jnp.concatenate(cols, axis=1)


def _swa_prompt_kernel(sinks_ref, q_ref, kc_ref, kp_ref, vc_ref, vp_ref, o_ref):
    b = pl.program_id(1)
    lane_lo = lax.broadcasted_iota(jnp.int32, (1, LANES), 1) < HEAD_DIM
    qrows = _head_rows(q_ref[...], lane_lo).astype(BF)
    kcat = jnp.concatenate([kp_ref[...], kc_ref[...]], axis=0).astype(BF)
    vcat = jnp.concatenate([vp_ref[...], vc_ref[...]], axis=0).astype(BF)
    s = _dot_nt(qrows, kcat) * (HEAD_DIM ** -0.5)
    r = lax.broadcasted_iota(jnp.int32, (WINDOW, 2 * WINDOW), 0)
    c = lax.broadcasted_iota(jnp.int32, (WINDOW, 2 * WINDOW), 1)
    dist = WINDOW + r - c
    valid = (dist >= 0) & (dist <= WINDOW) & ((c >= WINDOW) | (b > 0))
    dist_f = dist.astype(F32)
    ps = []
    for h in range(A_HEADS):
        sh = s[h * WINDOW:(h + 1) * WINDOW] - ALIBI_SLOPES[h] * dist_f
        sh = jnp.where(valid, sh, NEG_INF)
        sk = sinks_ref[h]
        m = jnp.maximum(jnp.max(sh, axis=-1, keepdims=True), sk)
        p = jnp.exp(sh - m)
        den = jnp.sum(p, axis=-1, keepdims=True) + jnp.exp(sk - m)
        ps.append((p / den).astype(BF))
    o = _dot(jnp.concatenate(ps, axis=0), vcat)
    o_ref[...] = _head_cols(o, WINDOW, lane_lo).astype(o_ref.dtype)


def _swa_prompt(q, k, v, sinks, n, seq):
    nb = seq // WINDOW
    cur = lambda w_: pl.BlockSpec((WINDOW, w_), lambda i, b: (i * nb + b, 0))
    prev = lambda w_: pl.BlockSpec((WINDOW, w_), lambda i, b: (i * nb + jnp.maximum(b - 1, 0), 0))
    return pl.pallas_call(
        _swa_prompt_kernel,
        grid=(n, nb),
        in_specs=[pl.BlockSpec(memory_space=pltpu.SMEM), cur(A_W), cur(KV_W), prev(KV_W), cur(KV_W), prev(KV_W)],
        out_specs=cur(A_W),
        out_shape=jax.ShapeDtypeStruct((n * seq, A_W), BF),
        compiler_params=_cparams("parallel", "arbitrary"),
        name="swa_prompt",
    )(sinks, q, k, k, v, v)


SWA_SEQ_BLOCK = 8


def _swa_sample_kernel(t_new, sinks_ref, q_ref, kn_ref, vn_ref, wk_ref, wv_ref, o_ref, nwk_ref, nwv_ref):
    nseq = SWA_SEQ_BLOCK
    pair_rows = 2 * t_new
    npair = nseq // 2
    kn = kn_ref[...]
    vn = vn_ref[...]
    for i in range(nseq):
        nwk_ref[i, 0:WINDOW - t_new, :] = wk_ref[i, t_new:WINDOW, :]
        nwk_ref[i, WINDOW - t_new:WINDOW, :] = kn[i * t_new:(i + 1) * t_new, :]
        nwv_ref[i, 0:WINDOW - t_new, :] = wv_ref[i, t_new:WINDOW, :]
        nwv_ref[i, WINDOW - t_new:WINDOW, :] = vn[i * t_new:(i + 1) * t_new, :]

    lane_lo = lax.broadcasted_iota(jnp.int32, (1, LANES), 1) < HEAD_DIM
    q = q_ref[...]
    knb = kn.astype(BF)
    vnb = vn.astype(BF)
    hr = A_HEADS * pair_rows
    qrows = [_head_rows(q[j * pair_rows:(j + 1) * pair_rows], lane_lo).astype(BF) for j in range(npair)]
    scale = HEAD_DIM ** -0.5
    s_new_all = _dot_nt(jnp.concatenate(qrows, axis=0), knb) * scale

    tq = lax.broadcasted_iota(jnp.int32, (pair_rows, 2 * WINDOW), 0)
    cw = lax.broadcasted_iota(jnp.int32, (pair_rows, 2 * WINDOW), 1)
    dist_w = WINDOW + (tq % t_new) - (cw % WINDOW)
    valid_w = ((tq // t_new) == (cw // WINDOW)) & (dist_w <= WINDOW)
    dist_wf = dist_w.astype(F32)
    tqn = lax.broadcasted_iota(jnp.int32, (pair_rows, nseq * t_new), 0)
    cn = lax.broadcasted_iota(jnp.int32, (pair_rows, nseq * t_new), 1)
    dist_n = (tqn % t_new) - (cn % t_new)
    dist_nf = dist_n.astype(F32)

    p_new_all = []
    o_win_all = []
    for j in range(npair):
        kwin = jnp.concatenate([wk_ref[2 * j], wk_ref[2 * j + 1]], axis=0).astype(BF)
        vwin = jnp.concatenate([wv_ref[2 * j], wv_ref[2 * j + 1]], axis=0).astype(BF)
        s_win = _dot_nt(qrows[j], kwin) * scale
        valid_n = ((2 * j + tqn // t_new) == (cn // t_new)) & (dist_n >= 0)
        p_win = []
        for h in range(A_HEADS):
            sw = jnp.where(valid_w, s_win[h * pair_rows:(h + 1) * pair_rows] - ALIBI_SLOPES[h] * dist_wf, NEG_INF)
            sn = s_new_all[j * hr + h * pair_rows:j * hr + (h + 1) * pair_rows]
            sn = jnp.where(valid_n, sn - ALIBI_SLOPES[h] * dist_nf, NEG_INF)
            sk = sinks_ref[h]
            m = jnp.maximum(jnp.maximum(jnp.max(sw, axis=-1, keepdims=True), jnp.max(sn, axis=-1, keepdims=True)), sk)
            pw = jnp.exp(sw - m)
            pn = jnp.exp(sn - m)
            den = jnp.sum(pw, axis=-1, keepdims=True) + jnp.sum(pn, axis=-1, keepdims=True) + jnp.exp(sk - m)
            p_win.append((pw / den).astype(BF))
            p_new_all.append((pn / den).astype(BF))
        o_win_all.append(_dot(jnp.concatenate(p_win, axis=0), vwin))
    o_new = _dot(jnp.concatenate(p_new_all, axis=0), vnb)
    outs = [_head_cols(o_win_all[j] + o_new[j * hr:(j + 1) * hr], pair_rows, lane_lo) for j in range(npair)]
    o_ref[...] = jnp.concatenate(outs, axis=0).astype(o_ref.dtype)


def _swa_sample(q, kn, vn, win_k, win_v, sinks, n, t_new):
    rows = SWA_SEQ_BLOCK * t_new
    row = lambda w_: pl.BlockSpec((rows, w_), lambda i: (i, 0))
    win = pl.BlockSpec((SWA_SEQ_BLOCK, WINDOW, KV_W), lambda i: (i, 0, 0))
    return pl.pallas_call(
        functools.partial(_swa_sample_kernel, t_new),
        grid=(n // SWA_SEQ_BLOCK,),
        in_specs=[pl.BlockSpec(memory_space=pltpu.SMEM), row(A_W), row(KV_W), row(KV_W), win, win],
        out_specs=[row(A_W), win, win],
        out_shape=[jax.ShapeDtypeStruct((n * t_new, A_W), BF),
                   jax.ShapeDtypeStruct((n, WINDOW, KV_W), F32), jax.ShapeDtypeStruct((n, WINDOW, KV_W), F32)],
        compiler_params=_cparams("parallel"),
        name="swa_sample",
    )(sinks, q, kn, vn, win_k, win_v)


SSM_LANE_CHUNK = 1024


def _s5_kernel(tc, has_h0, *refs):
    if has_h0:
        (u_ref, h0r_ref, h0i_ref, abr_ref, abi_ref, bcat_ref, ccat_ref, d_ref, wglu_ref,
         o_ref, hr_ref, hi_ref, st_scr, car_scr) = refs
    else:
        (u_ref, abr_ref, abi_ref, bcat_ref, ccat_ref, d_ref, wglu_ref,
         o_ref, hr_ref, hi_ref, st_scr, car_scr) = refs
    nb = SSM_SEQ_BLOCK
    ci = pl.program_id(1)

    @pl.when(ci == 0)
    def _():
        if has_h0:
            car_scr[:, :SSM_FLAT] = h0r_ref[...]
            car_scr[:, SSM_FLAT:] = h0i_ref[...]
        else:
            car_scr[...] = jnp.zeros_like(car_scr)

    ut = u_ref[...].reshape(tc * nb, SSM_W)
    st_scr[...] = _dot(ut.astype(BF), bcat_ref[...])

    for lc in range(SSM_FLAT // SSM_LANE_CHUNK):
        lo = lc * SSM_LANE_CHUNK
        re_sl = slice(lo, lo + SSM_LANE_CHUNK)
        im_sl = slice(SSM_FLAT + lo, SSM_FLAT + lo + SSM_LANE_CHUNK)
        ar = jnp.broadcast_to(abr_ref[:, re_sl], (nb, SSM_LANE_CHUNK))
        ai = jnp.broadcast_to(abi_ref[:, re_sl], (nb, SSM_LANE_CHUNK))

        def step(t, carry):
            hr, hi = carry
            rows = pl.ds(pl.multiple_of(t * nb, nb), nb)
            nr = ar * hr - ai * hi + st_scr[rows, re_sl]
            ni = ar * hi + ai * hr + st_scr[rows, im_sl]
            st_scr[rows, re_sl] = nr
            st_scr[rows, im_sl] = ni
            return nr, ni

        hr, hi = lax.fori_loop(0, tc, step, (car_scr[:, re_sl], car_scr[:, im_sl]))
        car_scr[:, re_sl] = hr
        car_scr[:, im_sl] = hi

    y = _dot(st_scr[...].astype(BF), ccat_ref[...]) + d_ref[...] * ut
    y = jax.nn.gelu(y).astype(BF)
    g = _dot(y, wglu_ref[...])
    o_ref[...] = (g[:, :SSM_W] * jax.nn.sigmoid(g[:, SSM_W:])).reshape(tc, nb, SSM_W)

    @pl.when(ci == pl.num_programs(1) - 1)
    def _():
        hr_ref[...] = car_scr[:, :SSM_FLAT]
        hi_ref[...] = car_scr[:, SSM_FLAT:]


def _s5_glu(u, h0, ssm, n, seq, tc):
    nb = SSM_SEQ_BLOCK
    abr, abi, bcat, ccat, dsk, wglu = ssm
    u_spec = pl.BlockSpec((tc, nb, SSM_W), lambda i, c: (c, i, 0))
    st_spec = pl.BlockSpec((nb, SSM_FLAT), lambda i, c: (i, 0))
    consts = [_full((1, SSM_FLAT)), _full((1, SSM_FLAT)), _full((SSM_W, 2 * SSM_FLAT)),
              _full((2 * SSM_FLAT, SSM_W)), _full((1, SSM_W)), _full((SSM_W, 2 * SSM_W))]
    has_h0 = h0 is not None
    in_specs = [u_spec] + ([st_spec, st_spec] if has_h0 else []) + consts
    args = [u] + (list(h0) if has_h0 else []) + [abr, abi, bcat, ccat, dsk, wglu]
    return pl.pallas_call(
        functools.partial(_s5_kernel, tc, has_h0),
        grid=(n // nb, seq // tc),
        in_specs=in_specs,
        out_specs=[u_spec, st_spec, st_spec],
        out_shape=[jax.ShapeDtypeStruct((seq, n, SSM_W), F32), jax.ShapeDtypeStruct((n, SSM_FLAT), F32),
                   jax.ShapeDtypeStruct((n, SSM_FLAT), F32)],
        scratch_shapes=[pltpu.VMEM((nb * tc, 2 * SSM_FLAT), F32), pltpu.VMEM((nb, 2 * SSM_FLAT), F32)],
        compiler_params=_cparams("parallel", "arbitrary"),
        name="s5_glu",
    )(*args)


def _even_out_kernel(x_ref, a_ref, s_ref, wa_ref, ws_ref, o_ref):
    o_ref[...] = x_ref[...] + _dot(a_ref[...], wa_ref[...]) + _dot(s_ref[...].astype(BF), ws_ref[...])


def _even_out_proj(x, att, ssm, wa, ws, tm, n, seq):
    m = x.shape[0]
    row = lambda w_: pl.BlockSpec((tm, w_), lambda i: (i, 0))
    ssm_spec = _time_major_spec(tm, n, seq, SSM_W) if seq % tm == 0 else row(SSM_W)
    return pl.pallas_call(
        _even_out_kernel,
        grid=(m // tm,),
        in_specs=[row(D_MODEL), row(A_W), ssm_spec, _full((A_W, D_MODEL)), _full((SSM_W, D_MODEL))],
        out_specs=row(D_MODEL),
        out_shape=jax.ShapeDtypeStruct((m, D_MODEL), F32),
        compiler_params=_cparams("parallel"),
        name="even_out_proj",
    )(x, att, ssm, wa, ws)


def _odd_in_kernel(x_ref, g_ref, w_ref, gv_ref, u_ref, v_ref):
    h = _rms_rows(x_ref[...], g_ref[...]).astype(BF)
    u_ref[...] = jax.nn.gelu(_dot(h, w_ref[:, :GM_W])).astype(u_ref.dtype)
    zv = jax.nn.gelu(_dot(h, w_ref[:, GM_W:]))
    v_ref[...] = _rms_rows(zv, gv_ref[...]).astype(v_ref.dtype)


def _odd_in_proj(x, g, w, gv, tm, v_dtype):
    m = x.shape[0]
    row = lambda w_: pl.BlockSpec((tm, w_), lambda i: (i, 0))
    return pl.pallas_call(
        _odd_in_kernel,
        grid=(m // tm,),
        in_specs=[row(D_MODEL), _full((1, D_MODEL)), _full((D_MODEL, 2 * GM_W)), _full((1, GM_W))],
        out_specs=[row(GM_W), row(GM_W)],
        out_shape=[jax.ShapeDtypeStruct((m, GM_W), BF), jax.ShapeDtypeStruct((m, GM_W), v_dtype)],
        compiler_params=_cparams("parallel"),
        name="odd_in_proj",
    )(x, g, w, gv)


def _gmlp_mix_kernel(csize, x_ref, u_ref, v_ref, wsp_ref, bsp_ref, wo_ref, o_ref, gated_scr):
    tm = x_ref.shape[0]
    i = lax.broadcasted_iota(jnp.int32, (CHUNK, CHUNK), 0)
    j = lax.broadcasted_iota(jnp.int32, (CHUNK, CHUNK), 1)
    keep = (j <= i) & ((i // csize) == (j // csize))
    bsp = bsp_ref[...]
    for h in range(GM_HEADS):
        ws = jnp.where(keep, wsp_ref[h], 0.0).astype(BF)
        b_col = bsp[:, h:h + 1]
        cols = slice(h * GM_HD, (h + 1) * GM_HD)
        for c in range(tm // CHUNK):
            rows = slice(c * CHUNK, (c + 1) * CHUNK)
            mix = _dot(ws, v_ref[rows, cols].astype(BF)) + b_col
            gated_scr[rows, cols] = (u_ref[rows, cols].astype(F32) * mix).astype(BF)
    o_ref[...] = x_ref[...] + _dot(gated_scr[...], wo_ref[...])


def _gmlp_mix(x, u, v, wsp, bsp, wo, csize, tm):
    m = x.shape[0]
    row = lambda w_: pl.BlockSpec((tm, w_), lambda i: (i, 0))
    return pl.pallas_call(
        functools.partial(_gmlp_mix_kernel, csize),
        grid=(m // tm,),
        in_specs=[row(D_MODEL), row(GM_W), row(GM_W), _full((GM_HEADS, CHUNK, CHUNK)), _full((CHUNK, GM_HEADS)),
                  _full((GM_W, D_MODEL))],
        out_specs=row(D_MODEL),
        out_shape=jax.ShapeDtypeStruct((m, D_MODEL), F32),
        scratch_shapes=[pltpu.VMEM((tm, GM_W), BF)],
        compiler_params=_cparams("parallel"),
        name="gmlp_mix",
    )(x, u, v, wsp, bsp, wo)


def _head_norm(z, g):
    cols = []
    for h in range(X_HEADS):
        zh = z[:, h * X_HEAD_DIM:(h + 1) * X_HEAD_DIM]
        cols.append(_rms_rows(zh, g))
    return jnp.concatenate(cols, axis=1)


MEMKV_SEQ_BLOCK = 2
MEM_ROWS = MEM_LEN * X_HEADS


def _head_rows_of(h):
    return pl.ds(h, MEM_LEN, stride=X_HEADS)


def _memory_kv_kernel(mem_ref, g_ref, wk_ref, gk_ref, wv_ref, k_ref, v_ref):
    m = _rms_rows(mem_ref[...], g_ref[...]).astype(BF)
    k = _head_norm(_dot(m, wk_ref[...]), gk_ref[...])
    v = _dot(m, wv_ref[...])
    for s in range(MEMKV_SEQ_BLOCK):
        rows = slice(s * MEM_LEN, (s + 1) * MEM_LEN)
        for h in range(X_HEADS):
            cols = slice(h * X_HEAD_DIM, (h + 1) * X_HEAD_DIM)
            k_ref[s, _head_rows_of(h), :] = k[rows, cols]
            v_ref[s, _head_rows_of(h), :] = v[rows, cols]


def _memory_kv(mem, g_mem, w_k, g_k, w_v, n):
    tm = MEMKV_SEQ_BLOCK * MEM_LEN
    per_layer = lambda a, b: pl.BlockSpec((None, a, b), lambda l, i: (l, 0, 0))
    out_spec = pl.BlockSpec((None, MEMKV_SEQ_BLOCK, MEM_ROWS, X_HEAD_DIM), lambda l, i: (l, i, 0, 0))
    out_sds = jax.ShapeDtypeStruct((DEPTH, n, MEM_ROWS, X_HEAD_DIM), F32)
    return pl.pallas_call(
        _memory_kv_kernel,
        grid=(DEPTH, n // MEMKV_SEQ_BLOCK),
        in_specs=[pl.BlockSpec((tm, D_MODEL), lambda l, i: (i, 0)), per_layer(1, D_MODEL), per_layer(D_MODEL, X_W),
                  per_layer(1, X_HEAD_DIM), per_layer(D_MODEL, X_W)],
        out_specs=[out_spec, out_spec],
        out_shape=[out_sds, out_sds],
        compiler_params=_cparams("parallel", "parallel"),
        name="memory_kv",
    )(mem, g_mem, w_k, g_k, w_v)


def _xattn_prompt_kernel(x_ref, g_ref, wq_ref, gq_ref, mk_ref, mv_ref, wo_ref, o_ref):
    x = x_ref[...]
    q = _head_norm(_dot(_rms_rows(x, g_ref[...]).astype(BF), wq_ref[...]), gq_ref[...]).astype(BF)
    outs = []
    for h in range(X_HEADS):
        cols = slice(h * X_HEAD_DIM, (h + 1) * X_HEAD_DIM)
        s = _dot_nt(q[:, cols], mk_ref[_head_rows_of(h), :].astype(BF)) * (X_HEAD_DIM ** -0.5)
        m = jnp.max(s, axis=-1, keepdims=True)
        p = jnp.exp(s - m)
        p = (p / jnp.sum(p, axis=-1, keepdims=True)).astype(BF)
        outs.append(_dot(p, mv_ref[_head_rows_of(h), :].astype(BF)))
    o = jnp.concatenate(outs, axis=1).astype(BF)
    o_ref[...] = x + _dot(o, wo_ref[...])


def _layer_spec(li, *shape):
    nd = len(shape)
    return pl.BlockSpec((None,) + shape, lambda *_: (li,) + (0,) * nd)


def _xattn_prompt(x, li, g, wq, gq, mk, mv, wo, n, seq, tq):
    nq = seq // tq
    row = pl.BlockSpec((tq, D_MODEL), lambda i, j: (i * nq + j, 0))
    mem = pl.BlockSpec((None, None, MEM_ROWS, X_HEAD_DIM), lambda i, j: (li, i, 0, 0))
    return pl.pallas_call(
        _xattn_prompt_kernel,
        grid=(n, nq),
        in_specs=[row, _layer_spec(li, 1, D_MODEL), _layer_spec(li, D_MODEL, X_W), _layer_spec(li, 1, X_HEAD_DIM),
                  mem, mem, _layer_spec(li, X_W, D_MODEL)],
        out_specs=row,
        out_shape=jax.ShapeDtypeStruct((n * seq, D_MODEL), F32),
        compiler_params=_cparams("parallel", "arbitrary"),
        name="xattn_prompt",
    )(x, g, wq, gq, mk, mv, wo)


XATTN_SEQ_BLOCK = 8


def _xattn_sample_kernel(t_new, x_ref, g_ref, wq_ref, gq_ref, mk_ref, mv_ref, wo_ref, o_ref):
    pair_rows = 2 * t_new
    x = x_ref[...]
    q = _head_norm(_dot(_rms_rows(x, g_ref[...]).astype(BF), wq_ref[...]), gq_ref[...]).astype(BF)
    tq = lax.broadcasted_iota(jnp.int32, (pair_rows, 2 * MEM_LEN), 0)
    cm = lax.broadcasted_iota(jnp.int32, (pair_rows, 2 * MEM_LEN), 1)
    same_seq = (tq // t_new) == (cm // MEM_LEN)
    npair = XATTN_SEQ_BLOCK // 2
    scores = []
    for j in range(npair):
        qp = q[j * pair_rows:(j + 1) * pair_rows]
        for h in range(X_HEADS):
            hr = _head_rows_of(h)
            kp = jnp.concatenate([mk_ref[2 * j, hr, :], mk_ref[2 * j + 1, hr, :]], axis=0).astype(BF)
            s = _dot_nt(qp[:, h * X_HEAD_DIM:(h + 1) * X_HEAD_DIM], kp)
            scores.append(jnp.where(same_seq, s, NEG_INF))
    s = jnp.concatenate(scores, axis=0) * (X_HEAD_DIM ** -0.5)
    m = jnp.max(s, axis=-1, keepdims=True)
    p = jnp.exp(s - m)
    p = p / jnp.sum(p, axis=-1, keepdims=True)
    outs = []
    for j in range(npair):
        heads = []
        for h in range(X_HEADS):
            hr = _head_rows_of(h)
            vp = jnp.concatenate([mv_ref[2 * j, hr, :], mv_ref[2 * j + 1, hr, :]], axis=0).astype(BF)
            r0 = (j * X_HEADS + h) * pair_rows
            heads.append(_dot(p[r0:r0 + pair_rows].astype(BF), vp))
        outs.append(jnp.concatenate(heads, axis=1))
    o_all = jnp.concatenate(outs, axis=0).astype(BF)
    o_ref[...] = x + _dot(o_all, wo_ref[...])


def _xattn_sample(x, li, g, wq, gq, mk, mv, wo, n, t_new):
    rows = XATTN_SEQ_BLOCK * t_new
    row = pl.BlockSpec((rows, D_MODEL), lambda i: (i, 0))
    mem = pl.BlockSpec((None, XATTN_SEQ_BLOCK, MEM_ROWS, X_HEAD_DIM), lambda i: (li, i, 0, 0))
    return pl.pallas_call(
        functools.partial(_xattn_sample_kernel, t_new),
        grid=(n // XATTN_SEQ_BLOCK,),
        in_specs=[row, _layer_spec(li, 1, D_MODEL), _layer_spec(li, D_MODEL, X_W), _layer_spec(li, 1, X_HEAD_DIM),
                  mem, mem, _layer_spec(li, X_W, D_MODEL)],
        out_specs=row,
        out_shape=jax.ShapeDtypeStruct((n * t_new, D_MODEL), F32),
        compiler_params=_cparams("parallel"),
        name="xattn_sample",
    )(x, g, wq, gq, mk, mv, wo)


ROUTER_COARSE_LANE = N_EXPERTS
BIG = 3.0e38


def _router_kernel(x_ref, g_ref, whi_ref, wlo_ref, b_ref, gates_ref):
    h = _rms_rows(x_ref[...], g_ref[...])
    h_hi = h.astype(BF)
    h_lo = (h - h_hi.astype(F32)).astype(BF)
    whi = whi_ref[...]
    lg = _dot(h_hi, whi) + _dot(h_lo, whi) + _dot(h_hi, wlo_ref[...]) + b_ref[...]
    lane = lax.broadcasted_iota(jnp.int32, lg.shape, 1)
    lane_f = lane.astype(F32)
    is_c = (lane >= ROUTER_COARSE_LANE) & (lane < ROUTER_COARSE_LANE + N_GROUPS)
    cmax = jnp.max(jnp.where(is_c, lg, -BIG), axis=-1, keepdims=True)
    grp_lane = jnp.min(jnp.where(is_c & (lg == cmax), lane_f, BIG), axis=-1, keepdims=True)
    grp = grp_lane.astype(jnp.int32) - ROUTER_COARSE_LANE
    p_grp = 1.0 / jnp.sum(jnp.where(is_c, jnp.exp(jnp.where(is_c, lg, cmax) - cmax), 0.0), axis=-1, keepdims=True)
    sel = (lane < N_EXPERTS) & ((lane // EXPERTS_PER_GROUP) == grp)
    v1 = jnp.max(jnp.where(sel, lg, -BIG), axis=-1, keepdims=True)
    i1 = jnp.min(jnp.where(sel & (lg == v1), lane_f, BIG), axis=-1, keepdims=True)
    sel2 = sel & (lane_f != i1)
    v2 = jnp.max(jnp.where(sel2, lg, -BIG), axis=-1, keepdims=True)
    i2 = jnp.min(jnp.where(sel2 & (lg == v2), lane_f, BIG), axis=-1, keepdims=True)
    e2 = jnp.exp(v2 - v1)
    w1 = p_grp / (1.0 + e2)
    w2 = p_grp * e2 / (1.0 + e2)
    gates_ref[...] = jnp.where(lane_f == i1, w1, 0.0) + jnp.where(lane_f == i2, w2, 0.0)


def _router(x, g, whi, wlo, b, tm):
    m = x.shape[0]
    return pl.pallas_call(
        _router_kernel,
        grid=(m // tm,),
        in_specs=[pl.BlockSpec((tm, D_MODEL), lambda i: (i, 0)), _full((1, D_MODEL)), _full((D_MODEL, LANES)),
                  _full((D_MODEL, LANES)), _full((1, LANES))],
        out_specs=pl.BlockSpec((tm, LANES), lambda i: (i, 0)),
        out_shape=jax.ShapeDtypeStruct((m, LANES), F32),
        compiler_params=_cparams("parallel"),
        name="moe_router",
    )(x, g, whi, wlo, b)


def _experts_kernel(x_ref, g_ref, gates_ref, w1_ref, w3_ref, w2_ref, o_ref, h_scr, acc_scr):
    gi = pl.program_id(1)

    @pl.when(gi == 0)
    def _():
        h_scr[...] = _rms_rows(x_ref[...], g_ref[...]).astype(BF)
        acc_scr[...] = jnp.zeros_like(acc_scr)

    h = h_scr[...]
    gates = gates_ref[...]
    lane = lax.broadcasted_iota(jnp.int32, gates.shape, 1)
    for j in range(EXPERTS_PER_GROUP):
        gcol = jnp.sum(jnp.where(lane == gi * EXPERTS_PER_GROUP + j, gates, 0.0), axis=-1, keepdims=True)
        a = _dot(h, w1_ref[j])
        b = _dot(h, w3_ref[j])
        hid = (a * jax.nn.sigmoid(a)) * b * gcol
        acc_scr[...] += _dot(hid.astype(BF), w2_ref[j])

    @pl.when(gi == pl.num_programs(1) - 1)
    def _():
        o_ref[...] = x_ref[...] + acc_scr[...]


def _experts(x, g, gates, w1, w3, w2, tm):
    m = x.shape[0]
    row = lambda w_: pl.BlockSpec((tm, w_), lambda i, e: (i, 0))
    return pl.pallas_call(
        _experts_kernel,
        grid=(m // tm, N_GROUPS),
        in_specs=[row(D_MODEL), _full((1, D_MODEL)), row(LANES),
                  pl.BlockSpec((EXPERTS_PER_GROUP, D_MODEL, D_EXPERT), lambda i, e: (e, 0, 0)),
                  pl.BlockSpec((EXPERTS_PER_GROUP, D_MODEL, D_EXPERT), lambda i, e: (e, 0, 0)),
                  pl.BlockSpec((EXPERTS_PER_GROUP, D_EXPERT, D_MODEL), lambda i, e: (e, 0, 0))],
        out_specs=row(D_MODEL),
        out_shape=jax.ShapeDtypeStruct((m, D_MODEL), F32),
        scratch_shapes=[pltpu.VMEM((tm, D_MODEL), BF), pltpu.VMEM((tm, D_MODEL), F32)],
        compiler_params=_cparams("parallel", "arbitrary"),
        name="moe_experts",
    )(x, g, gates, w1, w3, w2)


def _q_col_perm():
    idx = []
    for g in range(Q_PER_KV):
        for kv in range(KV_HEADS):
            base = (kv * Q_PER_KV + g) * HEAD_DIM
            idx.extend(range(base, base + HEAD_DIM))
    return jnp.asarray(idx, dtype=jnp.int32)


def _prep_ssm(a_re, a_im, log_dt, b_re, b_im, c_re, c_im, d_skip, w_glu):
    ar, ai = a_re.astype(F32), a_im.astype(F32)
    dt = jnp.exp(log_dt.astype(F32))[:, None]
    mag = jnp.exp(ar * dt)
    abr, abi = mag * jnp.cos(ai * dt), mag * jnp.sin(ai * dt)
    den = ar * ar + ai * ai
    kr = ((abr - 1.0) * ar + abi * ai) / den
    ki = (abi * ar - (abr - 1.0) * ai) / den
    br, bi = b_re.astype(F32), b_im.astype(F32)
    bbr = kr[..., None] * br - ki[..., None] * bi
    bbi = kr[..., None] * bi + ki[..., None] * br
    eye = jnp.eye(SSM_GROUPS, dtype=F32)
    blockdiag_in = lambda b: jnp.einsum('gph,gk->ghkp', b, eye).reshape(SSM_W, SSM_FLAT)
    blockdiag_out = lambda c: jnp.einsum('ghp,gk->gpkh', c, eye).reshape(SSM_FLAT, SSM_W)
    bcat = jnp.concatenate([blockdiag_in(bbr), blockdiag_in(bbi)], axis=1).astype(BF)
    ccat = jnp.concatenate([blockdiag_out(c_re.astype(F32)), -blockdiag_out(c_im.astype(F32))], axis=0).astype(BF)
    return (abr.reshape(1, SSM_FLAT), abi.reshape(1, SSM_FLAT), bcat, ccat,
            d_skip.astype(F32).reshape(1, SSM_W), w_glu.astype(BF))


def _run_trunk(x, n, seq, mem_k, mem_v, caches, p):
    prompt = caches is None
    tm = 512
    wk_out, wv_out, hr_out, hi_out, cv_out = [], [], [], [], []
    for li in range(DEPTH):
        e = li // 2
        if li % 2 == 0:
            q, k, v, u = _even_in_proj(x, p['g_mix'][li], p['w_in_even'][e], p['g_q'][e], p['g_k'][e], p['bd'],
                                       tm, n, seq)
            if prompt:
                att = _swa_prompt(q, k, v, p['sinks'][e], n, seq)
                k3, v3 = k.reshape(n, seq, KV_W), v.reshape(n, seq, KV_W)
                new_k, new_v = k3[:, seq - WINDOW:], v3[:, seq - WINDOW:]
                ssm, h_re, h_im = _s5_glu(u.reshape(seq, n, SSM_W), None, p['ssm'][e], n, seq, 64)
                ssm = ssm.reshape(seq, n * SSM_W)
            else:
                win_k, win_v, ssm_re, ssm_im = caches
                att, new_k, new_v = _swa_sample(q, k, v, win_k[e], win_v[e], p['sinks'][e], n, seq)
                u_tm = jnp.swapaxes(u.reshape(n, seq, SSM_W), 0, 1)
                ssm, h_re, h_im = _s5_glu(u_tm, (ssm_re[e], ssm_im[e]), p['ssm'][e], n, seq, seq)
                ssm = jnp.swapaxes(ssm, 0, 1).reshape(n * seq, SSM_W)
            x = _even_out_proj(x, att, ssm, p['w_out_att'][e], p['w_out_ssm'][e], tm, n, seq)
            wk_out.append(new_k.reshape(n, WINDOW, KV_HEADS, HEAD_DIM))
            wv_out.append(new_v.reshape(n, WINDOW, KV_HEADS, HEAD_DIM))
            hr_out.append(h_re.reshape(n, SSM_GROUPS, SSM_STATE))
            hi_out.append(h_im.reshape(n, SSM_GROUPS, SSM_STATE))
        else:
            u, v = _odd_in_proj(x, p['g_mix'][li], p['w_in_odd'][e], p['g_v'][e], 256, BF if prompt else F32)
            wsp, bsp, csize = p['gmlp_prompt'][e] if prompt else p['gmlp_sample'][e]
            x = _gmlp_mix(x, u, v, wsp, bsp, p['w_out_odd'][e], csize, tm)
            if not prompt:
                cv_out.append(v.reshape(n, seq, GM_W))
        if prompt:
            x = _xattn_prompt(x, li, p['g_xattn'], p['w_xq'], p['g_xq'], mem_k, mem_v, p['w_xo'], n, seq, 512)
        else:
            x = _xattn_sample(x, li, p['g_xattn'], p['w_xq'], p['g_xq'], mem_k, mem_v, p['w_xo'], n, seq)
        gates = _router(x, p['g_ffn'][li], p['w_router_hi'][li], p['w_router_lo'][li], p['b_router'][li], tm)
        x = _experts(x, p['g_ffn'][li], gates, p['w_e1'][li], p['w_e3'][li], p['w_e2'][li], tm)
    cv = None if prompt else jnp.stack(cv_out)
    return x, jnp.stack(wk_out), jnp.stack(wv_out), jnp.stack(hr_out), jnp.stack(hi_out), cv


def kernel(x_prompt, x_sample, cache_win_k, cache_win_v, state_ssm_re, state_ssm_im, cache_mem_k, cache_mem_v, mem_prompt, g_mix, g_xattn, g_ffn, g_mem, w_in_even, g_q, g_k, sinks, ssm_a_re, ssm_a_im, ssm_log_dt, ssm_b_re, ssm_b_im, ssm_c_re, ssm_c_im, ssm_d, w_glu, w_out_even, w_in_odd, g_v, w_spatial, b_spatial, w_out_odd, w_xq, g_xq, w_xk, g_xk, w_xv, w_xo, w_coarse, b_coarse, w_fine, b_fine, w_e1, w_e3, w_e2):
    batch, seq = x_prompt.shape[0], x_prompt.shape[1]
    dec_batch, dec_seq = x_sample.shape[0], x_sample.shape[1]
    n_even = w_in_even.shape[0]
    n_odd = w_in_odd.shape[0]

    qperm = _q_col_perm()
    row1 = lambda a: a.astype(F32)[:, None, :]
    p = {}
    p['g_mix'], p['g_xattn'], p['g_ffn'] = row1(g_mix), row1(g_xattn), row1(g_ffn)
    w_in_q = jnp.take(w_in_even[:, :, :A_W], qperm, axis=2)
    p['w_in_even'] = jnp.concatenate([w_in_q, w_in_even[:, :, A_W:]], axis=2).astype(BF)
    p['g_q'] = jnp.tile(g_q.astype(F32), (1, A_HEADS))[:, None, :]
    p['g_k'] = jnp.tile(g_k.astype(F32), (1, KV_HEADS))[:, None, :]
    head_id = jnp.arange(A_W) // HEAD_DIM
    p['bd'] = (head_id[:, None] == head_id[None, :]).astype(BF)
    p['sinks'] = sinks.astype(F32)
    p['ssm'] = [_prep_ssm(ssm_a_re[e], ssm_a_im[e], ssm_log_dt[e], ssm_b_re[e], ssm_b_im[e], ssm_c_re[e],
                          ssm_c_im[e], ssm_d[e], w_glu[e]) for e in range(n_even)]
    p['w_out_att'] = jnp.take(w_out_even[:, :A_W, :], qperm, axis=1).astype(BF)
    p['w_out_ssm'] = w_out_even[:, A_W:, :].astype(BF)
    p['w_in_odd'] = w_in_odd.astype(BF)
    p['g_v'] = row1(g_v)
    p['w_out_odd'] = w_out_odd.astype(BF)
    cs = min(dec_seq, CHUNK)
    reps = CHUNK // cs
    p['gmlp_prompt'] = [(w_spatial[e].astype(F32), b_spatial[e].astype(F32).T, CHUNK) for e in range(n_odd)]
    p['gmlp_sample'] = [(jnp.tile(w_spatial[e, :, :cs, :cs].astype(F32), (1, reps, reps)),
                         jnp.tile(b_spatial[e, :, :cs].astype(F32).T, (reps, 1)), cs) for e in range(n_odd)]
    p['w_xq'] = w_xq.astype(BF)
    p['g_xq'] = row1(g_xq)
    p['w_xo'] = w_xo.astype(BF)
    w_router = jnp.concatenate([w_fine, w_coarse], axis=2).astype(F32)
    w_router = jnp.pad(w_router, ((0, 0), (0, 0), (0, LANES - w_router.shape[2])))
    p['w_router_hi'] = w_router.astype(BF)
    p['w_router_lo'] = (w_router - p['w_router_hi'].astype(F32)).astype(BF)
    b_router = jnp.concatenate([b_fine, b_coarse], axis=1).astype(F32)
    p['b_router'] = jnp.pad(b_router, ((0, 0), (0, LANES - b_router.shape[1])))[:, None, :]
    p['w_e1'], p['w_e3'], p['w_e2'] = w_e1.astype(BF), w_e3.astype(BF), w_e2.astype(BF)

    mem2d = mem_prompt.reshape(batch * MEM_LEN, D_MODEL)
    pk, pv = _memory_kv(mem2d, row1(g_mem), w_xk.astype(BF), row1(g_xk), w_xv.astype(BF), batch)
    y_p, p_wk, p_wv, p_hr, p_hi, _ = _run_trunk(x_prompt.reshape(batch * seq, D_MODEL), batch, seq, pk, pv, None, p)

    caches = (cache_win_k.reshape(n_even, dec_batch, WINDOW, KV_W), cache_win_v.reshape(n_even, dec_batch, WINDOW, KV_W),
              state_ssm_re.reshape(n_even, dec_batch, SSM_FLAT), state_ssm_im.reshape(n_even, dec_batch, SSM_FLAT))
    y_s, s_wk, s_wv, s_hr, s_hi, s_cv = _run_trunk(x_sample.reshape(dec_batch * dec_seq, D_MODEL), dec_batch, dec_seq,
                                                   cache_mem_k.reshape(DEPTH, dec_batch, MEM_ROWS, X_HEAD_DIM),
                                                   cache_mem_v.reshape(DEPTH, dec_batch, MEM_ROWS, X_HEAD_DIM), caches, p)

    mem_shape = (DEPTH, batch, MEM_LEN, X_HEADS, X_HEAD_DIM)
    return (y_p.reshape(batch, seq, D_MODEL), y_s.reshape(dec_batch, dec_seq, D_MODEL), p_wk, p_wv, p_hr, p_hi,
            pk.reshape(mem_shape), pv.reshape(mem_shape), s_wk, s_wv, s_hr, s_hi, s_cv)
```

```python
import functools
import math

import jax
import jax.numpy as jnp
from jax import lax
from jax.experimental import pallas as pl
from jax.experimental.pallas import tpu as pltpu

F32 = jnp.float32
BF = jnp.bfloat16

D_MODEL = 1024
DEPTH = 4
A_W = 512
HEAD_DIM = 64
A_HEADS = 8
KV_HEADS = 2
Q_PER_KV = 4
KV_W = 128
WINDOW = 128
SSM_W = 512
SSM_CH = 16
SSM_GROUPS = 32
SSM_STATE = 64
SSM_FLAT = SSM_GROUPS * SSM_STATE
IN_EVEN = A_W + 2 * KV_W + SSM_W
CHUNK = 128
GM_W = 2048
GM_HEADS = 8
GM_HD = 256
MEM_LEN = 256
X_HEADS = 4
X_HEAD_DIM = 128
X_W = 512
N_GROUPS = 4
EXPERTS_PER_GROUP = 4
N_EXPERTS = 16
D_EXPERT = 256
EPS = 1e-6
NEG_INF = -1e30
LANES = 128
VMEM_LIMIT_BYTES = 52 * 1024 * 1024
ALIBI_SLOPES = tuple(2.0 ** (-8.0 * (h + 1) / A_HEADS) for h in range(A_HEADS))
SSM_SEQ_BLOCK = 8


def _cparams(*sem):
    return pltpu.CompilerParams(dimension_semantics=sem, vmem_limit_bytes=VMEM_LIMIT_BYTES)


def _rms_rows(x, g):
    return x * lax.rsqrt(jnp.mean(x * x, axis=-1, keepdims=True) + EPS) * g


def _dot(a, b):
    return jnp.dot(a, b, preferred_element_type=F32)


def _dot_nt(a, b):
    return lax.dot_general(a, b, (((1,), (1,)), ((), ())), preferred_element_type=F32)


def _full(shape):
    nd = len(shape)
    return pl.BlockSpec(shape, lambda *_: (0,) * nd)


def _layer_spec(li, *shape, single_buffer=False):
    nd = len(shape)
    mode = pl.Buffered(1) if single_buffer else None
    return pl.BlockSpec((None,) + shape, lambda *_: (li,) + (0,) * nd, pipeline_mode=mode)


def _even_in_kernel(x_ref, g_ref, w_ref, gq_ref, gk_ref, bd_ref, q_ref, k_ref, v_ref, u_ref):
    h = _rms_rows(x_ref[...], g_ref[...]).astype(BF)
    z = _dot(h, w_ref[...])
    q = z[:, :A_W]
    k = z[:, A_W:A_W + KV_W]
    bd = bd_ref[...]
    qms = _dot((q * q).astype(BF), bd) * (1.0 / HEAD_DIM)
    kms = _dot((k * k).astype(BF), bd[:KV_W, :KV_W]) * (1.0 / HEAD_DIM)
    q_ref[...] = q * lax.rsqrt(qms + EPS) * gq_ref[...]
    k_ref[...] = k * lax.rsqrt(kms + EPS) * gk_ref[...]
    v_ref[...] = z[:, A_W + KV_W:A_W + 2 * KV_W]
    u_ref[...] = z[:, A_W + 2 * KV_W:]


def _time_major_spec(tm, n, seq, width):
    per_seq = seq // tm
    return pl.BlockSpec((tm, width), lambda i: (i % per_seq, i // per_seq))


def _even_in_proj(x, li, e, g, w, gq, gk, bd, tm, n, seq):
    m = x.shape[0]
    row = lambda w_: pl.BlockSpec((tm, w_), lambda i: (i, 0))
    return pl.pallas_call(
        _even_in_kernel,
        grid=(m // tm,),
        in_specs=[row(D_MODEL), _layer_spec(li, 1, D_MODEL), _layer_spec(e, D_MODEL, IN_EVEN), _layer_spec(e, 1, A_W),
                  _layer_spec(e, 1, KV_W), _full((A_W, A_W))],
        out_specs=[row(A_W), row(KV_W), row(KV_W), _time_major_spec(tm, n, seq, SSM_W) if seq % tm == 0 else row(SSM_W)],
        out_shape=[jax.ShapeDtypeStruct((m, A_W), F32), jax.ShapeDtypeStruct((m, KV_W), F32),
                   jax.ShapeDtypeStruct((m, KV_W), F32),
                   jax.ShapeDtypeStruct((seq, n * SSM_W) if seq % tm == 0 else (m, SSM_W), F32)],
        compiler_params=_cparams("parallel"),
        name="even_in_proj",
    )(x, g, w, gq, gk, bd)


def _head_rows(q, lane_lo):
    rows = []
    for kv in range(KV_HEADS):
        keep = lane_lo if kv == 0 else jnp.logical_not(lane_lo)
        for g in range(Q_PER_KV):
            rows.append(jnp.where(keep, q[:, g * LANES:(g + 1) * LANES], 0.0))
    return jnp.concatenate(rows, axis=0)


def _head_cols(o, r, lane_lo):
    cols = []
    for g in range(Q_PER_KV):
        cols.append(jnp.where(lane_lo, o[g * r:(g + 1) * r], o[(Q_PER_KV + g) * r:(Q_PER_KV + g + 1) * r]))
    return jnp.concatenate(cols, axis=1)


def _swa_prompt_kernel(e, sinks_ref, q_ref, kc_ref, kp_ref, vc_ref, vp_ref, o_ref):
    b = pl.program_id(1)
    lane_lo = lax.broadcasted_iota(jnp.int32, (1, LANES), 1) < HEAD_DIM
    qrows = _head_rows(q_ref[...], lane_lo).astype(BF)
    kcat = jnp.concatenate([kp_ref[...], kc_ref[...]], axis=0).astype(BF)
    vcat = jnp.concatenate([vp_ref[...], vc_ref[...]], axis=0).astype(BF)
    s = _dot_nt(qrows, kcat) * (HEAD_DIM ** -0.5)
    r = lax.broadcasted_iota(jnp.int32, (WINDOW, 2 * WINDOW), 0)
    c = lax.broadcasted_iota(jnp.int32, (WINDOW, 2 * WINDOW), 1)
    dist = WINDOW + r - c
    valid = (dist >= 0) & (dist <= WINDOW) & ((c >= WINDOW) | (b > 0))
    dist_f = dist.astype(F32)
    ps = []
    for h in range(A_HEADS):
        sh = s[h * WINDOW:(h + 1) * WINDOW] - ALIBI_SLOPES[h] * dist_f
        sh = jnp.where(valid, sh, NEG_INF)
        sk = sinks_ref[e * A_HEADS + h]
        m = jnp.maximum(jnp.max(sh, axis=-1, keepdims=True), sk)
        p = jnp.exp(sh - m)
        den = jnp.sum(p, axis=-1, keepdims=True) + jnp.exp(sk - m)
        ps.append((p / den).astype(BF))
    o = _dot(jnp.concatenate(ps, axis=0), vcat)
    o_ref[...] = _head_cols(o, WINDOW, lane_lo).astype(o_ref.dtype)


def _swa_prompt(q, k, v, sinks, e, n, seq):
    nb = seq // WINDOW
    cur = lambda w_: pl.BlockSpec((WINDOW, w_), lambda i, b: (i * nb + b, 0))
    prev = lambda w_: pl.BlockSpec((WINDOW, w_), lambda i, b: (i * nb + jnp.maximum(b - 1, 0), 0))
    return pl.pallas_call(
        functools.partial(_swa_prompt_kernel, e),
        grid=(n, nb),
        in_specs=[pl.BlockSpec(memory_space=pltpu.SMEM), cur(A_W), cur(KV_W), prev(KV_W), cur(KV_W), prev(KV_W)],
        out_specs=cur(A_W),
        out_shape=jax.ShapeDtypeStruct((n * seq, A_W), BF),
        compiler_params=_cparams("parallel", "arbitrary"),
        name="swa_prompt",
    )(sinks, q, k, k, v, v)


SWA_SEQ_BLOCK = 8


def _swa_sample_kernel(t_new, e, sinks_ref, q_ref, kn_ref, vn_ref, wk_ref, wv_ref, o_ref, nwk_ref, nwv_ref):
    nseq = SWA_SEQ_BLOCK
    pair_rows = 2 * t_new
    npair = nseq // 2
    kn = kn_ref[...]
    vn = vn_ref[...]
    for i in range(nseq):
        nwk_ref[i, 0:WINDOW - t_new, :] = wk_ref[i, t_new:WINDOW, :]
        nwk_ref[i, WINDOW - t_new:WINDOW, :] = kn[i * t_new:(i + 1) * t_new, :]
        nwv_ref[i, 0:WINDOW - t_new, :] = wv_ref[i, t_new:WINDOW, :]
        nwv_ref[i, WINDOW - t_new:WINDOW, :] = vn[i * t_new:(i + 1) * t_new, :]

    lane_lo = lax.broadcasted_iota(jnp.int32, (1, LANES), 1) < HEAD_DIM
    q = q_ref[...]
    knb = kn.astype(BF)
    vnb = vn.astype(BF)
    hr = A_HEADS * pair_rows
    qrows = [_head_rows(q[j * pair_rows:(j + 1) * pair_rows], lane_lo).astype(BF) for j in range(npair)]
    scale = HEAD_DIM ** -0.5
    s_new_all = _dot_nt(jnp.concatenate(qrows, axis=0), knb) * scale

    tq = lax.broadcasted_iota(jnp.int32, (pair_rows, 2 * WINDOW), 0)
    cw = lax.broadcasted_iota(jnp.int32, (pair_rows, 2 * WINDOW), 1)
    dist_w = WINDOW + (tq % t_new) - (cw % WINDOW)
    valid_w = ((tq // t_new) == (cw // WINDOW)) & (dist_w <= WINDOW)
    dist_wf = dist_w.astype(F32)
    tqn = lax.broadcasted_iota(jnp.int32, (pair_rows, nseq * t_new), 0)
    cn = lax.broadcasted_iota(jnp.int32, (pair_rows, nseq * t_new), 1)
    dist_n = (tqn % t_new) - (cn % t_new)
    dist_nf = dist_n.astype(F32)

    p_new_all = []
    o_win_all = []
    for j in range(npair):
        kwin = jnp.concatenate([wk_ref[2 * j], wk_ref[2 * j + 1]], axis=0).astype(BF)
        vwin = jnp.concatenate([wv_ref[2 * j], wv_ref[2 * j + 1]], axis=0).astype(BF)
        s_win = _dot_nt(qrows[j], kwin) * scale
        valid_n = ((2 * j + tqn // t_new) == (cn // t_new)) & (dist_n >= 0)
        p_win = []
        for h in range(A_HEADS):
            sw = jnp.where(valid_w, s_win[h * pair_rows:(h + 1) * pair_rows] - ALIBI_SLOPES[h] * dist_wf, NEG_INF)
            sn = s_new_all[j * hr + h * pair_rows:j * hr + (h + 1) * pair_rows]
            sn = jnp.where(valid_n, sn - ALIBI_SLOPES[h] * dist_nf, NEG_INF)
            sk = sinks_ref[e * A_HEADS + h]
            m = jnp.maximum(jnp.maximum(jnp.max(sw, axis=-1, keepdims=True), jnp.max(sn, axis=-1, keepdims=True)), sk)
            pw = jnp.exp(sw - m)
            pn = jnp.exp(sn - m)
            den = jnp.sum(pw, axis=-1, keepdims=True) + jnp.sum(pn, axis=-1, keepdims=True) + jnp.exp(sk - m)
            p_win.append((pw / den).astype(BF))
            p_new_all.append((pn / den).astype(BF))
        o_win_all.append(_dot(jnp.concatenate(p_win, axis=0), vwin))
    o_new = _dot(jnp.concatenate(p_new_all, axis=0), vnb)
    outs = [_head_cols(o_win_all[j] + o_new[j * hr:(j + 1) * hr], pair_rows, lane_lo) for j in range(npair)]
    o_ref[...] = jnp.concatenate(outs, axis=0).astype(o_ref.dtype)


def _swa_sample(q, kn, vn, win_k, win_v, sinks, e, n, t_new):
    rows = SWA_SEQ_BLOCK * t_new
    row = lambda w_: pl.BlockSpec((rows, w_), lambda i: (i, 0))
    win = pl.BlockSpec((SWA_SEQ_BLOCK, WINDOW, KV_W), lambda i: (i, 0, 0))
    win_in = pl.BlockSpec((None, SWA_SEQ_BLOCK, WINDOW, KV_W), lambda i: (e, i, 0, 0))
    return pl.pallas_call(
        functools.partial(_swa_sample_kernel, t_new, e),
        grid=(n // SWA_SEQ_BLOCK,),
        in_specs=[pl.BlockSpec(memory_space=pltpu.SMEM), row(A_W), row(KV_W), row(KV_W), win_in, win_in],
        out_specs=[row(A_W), win, win],
        out_shape=[jax.ShapeDtypeStruct((n * t_new, A_W), BF),
                   jax.ShapeDtypeStruct((n, WINDOW, KV_W), F32), jax.ShapeDtypeStruct((n, WINDOW, KV_W), F32)],
        compiler_params=_cparams("parallel"),
        name="swa_sample",
    )(sinks, q, kn, vn, win_k, win_v)


SSM_LANE_CHUNK = 1024
SSM_IN_TILE = LANES
SSM_IN_TILES = SSM_W // SSM_IN_TILE
SSM_ST_TILE = SSM_IN_TILE // SSM_CH * SSM_STATE
SSM_OUT_TILE = 256
SSM_OUT_TILES = SSM_W // SSM_OUT_TILE
SSM_OUT_ST = SSM_OUT_TILE // SSM_CH * SSM_STATE


def _s5_kernel(tc, has_h0, *refs):
    if has_h0:
        (u_ref, h0r_ref, h0i_ref, abr_ref, abi_ref, bcat_ref, ccat_ref, d_ref, wglu_ref,
         o_ref, hr_ref, hi_ref, st_scr, car_scr) = refs
    else:
        (u_ref, abr_ref, abi_ref, bcat_ref, ccat_ref, d_ref, wglu_ref,
         o_ref, hr_ref, hi_ref, st_scr, car_scr) = refs
    nb = SSM_SEQ_BLOCK
    ci = pl.program_id(1)

    @pl.when(ci == 0)
    def _():
        if has_h0:
            car_scr[:, :SSM_FLAT] = h0r_ref[...]
            car_scr[:, SSM_FLAT:] = h0i_ref[...]
        else:
            car_scr[...] = jnp.zeros_like(car_scr)

    ut = u_ref[...].reshape(tc * nb, SSM_W)
    ub = ut.astype(BF)
    for jt in range(SSM_IN_TILES):
        bu = _dot(ub[:, jt * SSM_IN_TILE:(jt + 1) * SSM_IN_TILE], bcat_ref[jt])
        st_scr[:, jt * SSM_ST_TILE:(jt + 1) * SSM_ST_TILE] = bu[:, :SSM_ST_TILE]
        st_scr[:, SSM_FLAT + jt * SSM_ST_TILE:SSM_FLAT + (jt + 1) * SSM_ST_TILE] = bu[:, SSM_ST_TILE:]

    for lc in range(SSM_FLAT // SSM_LANE_CHUNK):
        lo = lc * SSM_LANE_CHUNK
        re_sl = slice(lo, lo + SSM_LANE_CHUNK)
        im_sl = slice(SSM_FLAT + lo, SSM_FLAT + lo + SSM_LANE_CHUNK)
        ar = jnp.broadcast_to(abr_ref[:, re_sl], (nb, SSM_LANE_CHUNK))
        ai = jnp.broadcast_to(abi_ref[:, re_sl], (nb, SSM_LANE_CHUNK))

        def step(t, carry):
            hr, hi = carry
            rows = pl.ds(pl.multiple_of(t * nb, nb), nb)
            nr = ar * hr - ai * hi + st_scr[rows, re_sl]
            ni = ar * hi + ai * hr + st_scr[rows, im_sl]
            st_scr[rows, re_sl] = nr
            st_scr[rows, im_sl] = ni
            return nr, ni

        hr, hi = lax.fori_loop(0, tc, step, (car_scr[:, re_sl], car_scr[:, im_sl]))
        car_scr[:, re_sl] = hr
        car_scr[:, im_sl] = hi

    ys = []
    for ot in range(SSM_OUT_TILES):
        re_sl = slice(ot * SSM_OUT_ST, (ot + 1) * SSM_OUT_ST)
        im_sl = slice(SSM_FLAT + ot * SSM_OUT_ST, SSM_FLAT + (ot + 1) * SSM_OUT_ST)
        ys.append(_dot(st_scr[:, re_sl].astype(BF), ccat_ref[ot, :SSM_OUT_ST, :])
                  + _dot(st_scr[:, im_sl].astype(BF), ccat_ref[ot, SSM_OUT_ST:, :]))
    y = jnp.concatenate(ys, axis=1) + d_ref[...] * ut
    y = jax.nn.gelu(y).astype(BF)
    g = _dot(y, wglu_ref[...])
    o_ref[...] = (g[:, :SSM_W] * jax.nn.sigmoid(g[:, SSM_W:])).reshape(tc, nb, SSM_W)

    @pl.when(ci == pl.num_programs(1) - 1)
    def _():
        hr_ref[...] = car_scr[:, :SSM_FLAT]
        hi_ref[...] = car_scr[:, SSM_FLAT:]


def _s5_glu(u, h0, ssm, e, n, seq, tc):
    nb = SSM_SEQ_BLOCK
    abr, abi, bcat, ccat, dsk, wglu = ssm
    u_spec = pl.BlockSpec((tc, nb, SSM_W), lambda i, c: (c, i, 0))
    st_spec = pl.BlockSpec((nb, SSM_FLAT), lambda i, c: (i, 0))
    h0_spec = pl.BlockSpec((None, nb, SSM_FLAT), lambda i, c: (e, i, 0))
    consts = [_layer_spec(e, 1, SSM_FLAT), _layer_spec(e, 1, SSM_FLAT),
              _layer_spec(e, SSM_IN_TILES, SSM_IN_TILE, 2 * SSM_ST_TILE),
              _layer_spec(e, SSM_OUT_TILES, 2 * SSM_OUT_ST, SSM_OUT_TILE), _layer_spec(e, 1, SSM_W),
              _layer_spec(e, SSM_W, 2 * SSM_W)]
    has_h0 = h0 is not None
    in_specs = [u_spec] + ([h0_spec, h0_spec] if has_h0 else []) + consts
    args = [u] + (list(h0) if has_h0 else []) + [abr, abi, bcat, ccat, dsk, wglu]
    return pl.pallas_call(
        functools.partial(_s5_kernel, tc, has_h0),
        grid=(n // nb, seq // tc),
        in_specs=in_specs,
        out_specs=[u_spec, st_spec, st_spec],
        out_shape=[jax.ShapeDtypeStruct((seq, n, SSM_W), F32), jax.ShapeDtypeStruct((n, SSM_FLAT), F32),
                   jax.ShapeDtypeStruct((n, SSM_FLAT), F32)],
        scratch_shapes=[pltpu.VMEM((nb * tc, 2 * SSM_FLAT), F32), pltpu.VMEM((nb, 2 * SSM_FLAT), F32)],
        compiler_params=_cparams("parallel", "arbitrary"),
        name="s5_glu",
    )(*args)


def _even_out_kernel(x_ref, a_ref, s_ref, wa_ref, ws_ref, o_ref):
    o_ref[...] = x_ref[...] + _dot(a_ref[...], wa_ref[...]) + _dot(s_ref[...].astype(BF), ws_ref[...])


def _even_out_proj(x, att, ssm, e, wa, ws, tm, n, seq):
    m = x.shape[0]
    row = lambda w_: pl.BlockSpec((tm, w_), lambda i: (i, 0))
    ssm_spec = _time_major_spec(tm, n, seq, SSM_W) if seq % tm == 0 else row(SSM_W)
    return pl.pallas_call(
        _even_out_kernel,
        grid=(m // tm,),
        in_specs=[row(D_MODEL), row(A_W), ssm_spec, _layer_spec(e, A_W, D_MODEL), _layer_spec(e, SSM_W, D_MODEL)],
        out_specs=row(D_MODEL),
        out_shape=jax.ShapeDtypeStruct((m, D_MODEL), F32),
        compiler_params=_cparams("parallel"),
        name="even_out_proj",
    )(x, att, ssm, wa, ws)


def _odd_in_kernel(x_ref, g_ref, w_ref, gv_ref, u_ref, v_ref):
    h = _rms_rows(x_ref[...], g_ref[...]).astype(BF)
    u_ref[...] = jax.nn.gelu(_dot(h, w_ref[:, :GM_W])).astype(u_ref.dtype)
    zv = jax.nn.gelu(_dot(h, w_ref[:, GM_W:]))
    v_ref[...] = _rms_rows(zv, gv_ref[...]).astype(v_ref.dtype)


def _odd_in_proj(x, li, e, g, w, gv, tm, v_dtype):
    m = x.shape[0]
    row = lambda w_: pl.BlockSpec((tm, w_), lambda i: (i, 0))
    return pl.pallas_call(
        _odd_in_kernel,
        grid=(m // tm,),
        in_specs=[row(D_MODEL), _layer_spec(li, 1, D_MODEL), _layer_spec(e, D_MODEL, 2 * GM_W, single_buffer=True),
                  _layer_spec(e, 1, GM_W)],
        out_specs=[row(GM_W), row(GM_W)],
        out_shape=[jax.ShapeDtypeStruct((m, GM_W), BF), jax.ShapeDtypeStruct((m, GM_W), v_dtype)],
        compiler_params=_cparams("parallel"),
        name="odd_in_proj",
    )(x, g, w, gv)


def _gmlp_mix_kernel(csize, x_ref, u_ref, v_ref, wsp_ref, bsp_ref, wo_ref, o_ref, gated_scr):
    tm = x_ref.shape[0]
    i = lax.broadcasted_iota(jnp.int32, (CHUNK, CHUNK), 0)
    j = lax.broadcasted_iota(jnp.int32, (CHUNK, CHUNK), 1)
    keep = (j <= i) & ((i // csize) == (j // csize))
    bsp = bsp_ref[...]
    for h in range(GM_HEADS):
        ws = jnp.where(keep, wsp_ref[h], 0.0).astype(BF)
        b_col = bsp[:, h:h + 1]
        cols = slice(h * GM_HD, (h + 1) * GM_HD)
        for c in range(tm // CHUNK):
            rows = slice(c * CHUNK, (c + 1) * CHUNK)
            mix = _dot(ws, v_ref[rows, cols].astype(BF)) + b_col
            gated_scr[rows, cols] = (u_ref[rows, cols].astype(F32) * mix).astype(BF)
    o_ref[...] = x_ref[...] + _dot(gated_scr[...], wo_ref[...])


def _gmlp_mix(x, u, v, e, wsp, bsp, wo, csize, tm):
    m = x.shape[0]
    row = lambda w_: pl.BlockSpec((tm, w_), lambda i: (i, 0))
    return pl.pallas_call(
        functools.partial(_gmlp_mix_kernel, csize),
        grid=(m // tm,),
        in_specs=[row(D_MODEL), row(GM_W), row(GM_W), _layer_spec(e, GM_HEADS, CHUNK, CHUNK),
                  _layer_spec(e, CHUNK, GM_HEADS), _layer_spec(e, GM_W, D_MODEL)],
        out_specs=row(D_MODEL),
        out_shape=jax.ShapeDtypeStruct((m, D_MODEL), F32),
        scratch_shapes=[pltpu.VMEM((tm, GM_W), BF)],
        compiler_params=_cparams("parallel"),
        name="gmlp_mix",
    )(x, u, v, wsp, bsp, wo)


def _head_norm(z, g):
    cols = []
    for h in range(X_HEADS):
        zh = z[:, h * X_HEAD_DIM:(h + 1) * X_HEAD_DIM]
        cols.append(_rms_rows(zh, g))
    return jnp.concatenate(cols, axis=1)


MEMKV_SEQ_BLOCK = 2
MEM_ROWS = MEM_LEN * X_HEADS


def _head_rows_of(h):
    return pl.ds(h, MEM_LEN, stride=X_HEADS)


def _memory_kv_kernel(mem_ref, g_ref, wk_ref, gk_ref, wv_ref, k_ref, v_ref):
    m = _rms_rows(mem_ref[...], g_ref[...]).astype(BF)
    k = _head_norm(_dot(m, wk_ref[...]), gk_ref[...])
    v = _dot(m, wv_ref[...])
    for s in range(MEMKV_SEQ_BLOCK):
        rows = slice(s * MEM_LEN, (s + 1) * MEM_LEN)
        for h in range(X_HEADS):
            cols = slice(h * X_HEAD_DIM, (h + 1) * X_HEAD_DIM)
            k_ref[s, _head_rows_of(h), :] = k[rows, cols]
            v_ref[s, _head_rows_of(h), :] = v[rows, cols]


def _memory_kv(mem, g_mem, w_k, g_k, w_v, n):
    tm = MEMKV_SEQ_BLOCK * MEM_LEN
    per_layer = lambda a, b: pl.BlockSpec((None, a, b), lambda l, i: (l, 0, 0))
    out_spec = pl.BlockSpec((None, MEMKV_SEQ_BLOCK, MEM_ROWS, X_HEAD_DIM), lambda l, i: (l, i, 0, 0))
    out_sds = jax.ShapeDtypeStruct((DEPTH, n, MEM_ROWS, X_HEAD_DIM), F32)
    return pl.pallas_call(
        _memory_kv_kernel,
        grid=(DEPTH, n // MEMKV_SEQ_BLOCK),
        in_specs=[pl.BlockSpec((tm, D_MODEL), lambda l, i: (i, 0)), per_layer(1, D_MODEL), per_layer(D_MODEL, X_W),
                  per_layer(1, X_HEAD_DIM), per_layer(D_MODEL, X_W)],
        out_specs=[out_spec, out_spec],
        out_shape=[out_sds, out_sds],
        compiler_params=_cparams("parallel", "parallel"),
        name="memory_kv",
    )(mem, g_mem, w_k, g_k, w_v)


def _xattn_prompt_kernel(x_ref, g_ref, wq_ref, gq_ref, mk_ref, mv_ref, wo_ref, o_ref):
    x = x_ref[...]
    q = _head_norm(_dot(_rms_rows(x, g_ref[...]).astype(BF), wq_ref[...]), gq_ref[...]).astype(BF)
    outs = []
    for h in range(X_HEADS):
        cols = slice(h * X_HEAD_DIM, (h + 1) * X_HEAD_DIM)
        s = _dot_nt(q[:, cols], mk_ref[_head_rows_of(h), :].astype(BF)) * (X_HEAD_DIM ** -0.5)
        m = jnp.max(s, axis=-1, keepdims=True)
        p = jnp.exp(s - m)
        p = (p / jnp.sum(p, axis=-1, keepdims=True)).astype(BF)
        outs.append(_dot(p, mv_ref[_head_rows_of(h), :].astype(BF)))
    o = jnp.concatenate(outs, axis=1).astype(BF)
    o_ref[...] = x + _dot(o, wo_ref[...])


def _xattn_prompt(x, li, g, wq, gq, mk, mv, wo, n, seq, tq):
    nq = seq // tq
    row = pl.BlockSpec((tq, D_MODEL), lambda i, j: (i * nq + j, 0))
    mem = pl.BlockSpec((None, None, MEM_ROWS, X_HEAD_DIM), lambda i, j: (li, i, 0, 0))
    return pl.pallas_call(
        _xattn_prompt_kernel,
        grid=(n, nq),
        in_specs=[row, _layer_spec(li, 1, D_MODEL), _layer_spec(li, D_MODEL, X_W), _layer_spec(li, 1, X_HEAD_DIM),
                  mem, mem, _layer_spec(li, X_W, D_MODEL)],
        out_specs=row,
        out_shape=jax.ShapeDtypeStruct((n * seq, D_MODEL), F32),
        compiler_params=_cparams("parallel", "arbitrary"),
        name="xattn_prompt",
    )(x, g, wq, gq, mk, mv, wo)


XATTN_SEQ_BLOCK = 8


def _xattn_sample_kernel(t_new, x_ref, g_ref, wq_ref, gq_ref, mk_ref, mv_ref, wo_ref, o_ref):
    pair_rows = 2 * t_new
    x = x_ref[...]
    q = _head_norm(_dot(_rms_rows(x, g_ref[...]).astype(BF), wq_ref[...]), gq_ref[...]).astype(BF)
    tq = lax.broadcasted_iota(jnp.int32, (pair_rows, 2 * MEM_LEN), 0)
    cm = lax.broadcasted_iota(jnp.int32, (pair_rows, 2 * MEM_LEN), 1)
    same_seq = (tq // t_new) == (cm // MEM_LEN)
    npair = XATTN_SEQ_BLOCK // 2
    scores = []
    for j in range(npair):
        qp = q[j * pair_rows:(j + 1) * pair_rows]
        for h in range(X_HEADS):
            hr = _head_rows_of(h)
            kp = jnp.concatenate([mk_ref[2 * j, hr, :], mk_ref[2 * j + 1, hr, :]], axis=0).astype(BF)
            s = _dot_nt(qp[:, h * X_HEAD_DIM:(h + 1) * X_HEAD_DIM], kp)
            scores.append(jnp.where(same_seq, s, NEG_INF))
    s = jnp.concatenate(scores, axis=0) * (X_HEAD_DIM ** -0.5)
    m = jnp.max(s, axis=-1, keepdims=True)
    p = jnp.exp(s - m)
    p = p / jnp.sum(p, axis=-1, keepdims=True)
    outs = []
    for j in range(npair):
        heads = []
        for h in range(X_HEADS):
            hr = _head_rows_of(h)
            vp = jnp.concatenate([mv_ref[2 * j, hr, :], mv_ref[2 * j + 1, hr, :]], axis=0).astype(BF)
            r0 = (j * X_HEADS + h) * pair_rows
            heads.append(_dot(p[r0:r0 + pair_rows].astype(BF), vp))
        outs.append(jnp.concatenate(heads, axis=1))
    o_all = jnp.concatenate(outs, axis=0).astype(BF)
    o_ref[...] = x + _dot(o_all, wo_ref[...])


def _xattn_sample(x, li, g, wq, gq, mk, mv, wo, n, t_new):
    rows = XATTN_SEQ_BLOCK * t_new
    row = pl.BlockSpec((rows, D_MODEL), lambda i: (i, 0))
    mem = pl.BlockSpec((None, XATTN_SEQ_BLOCK, MEM_ROWS, X_HEAD_DIM), lambda i: (li, i, 0, 0))
    return pl.pallas_call(
        functools.partial(_xattn_sample_kernel, t_new),
        grid=(n // XATTN_SEQ_BLOCK,),
        in_specs=[row, _layer_spec(li, 1, D_MODEL), _layer_spec(li, D_MODEL, X_W), _layer_spec(li, 1, X_HEAD_DIM),
                  mem, mem, _layer_spec(li, X_W, D_MODEL)],
        out_specs=row,
        out_shape=jax.ShapeDtypeStruct((n * t_new, D_MODEL), F32),
        compiler_params=_cparams("parallel"),
        name="xattn_sample",
    )(x, g, wq, gq, mk, mv, wo)


ROUTER_ROWS = 32
ROUTER_COARSE_ROW = N_EXPERTS
BIG = 3.0e38


def _first_argmax(vals, vmax):
    idx = jnp.full(vmax.shape, len(vals) - 1, jnp.int32)
    for j in range(len(vals) - 2, -1, -1):
        idx = jnp.where(vals[j] == vmax, j, idx)
    return idx


def _router_kernel(x_ref, g_ref, whi_ref, wlo_ref, b_ref, gates_ref):
    h = _rms_rows(x_ref[...], g_ref[...])
    h_hi = h.astype(BF)
    h_lo = (h - h_hi.astype(F32)).astype(BF)
    whi = whi_ref[...]
    lg = _dot_nt(whi, h_hi) + _dot_nt(whi, h_lo) + _dot_nt(wlo_ref[...], h_hi) + b_ref[...]
    row = lambda r: lg[r:r + 1, :]
    coarse = [row(ROUTER_COARSE_ROW + g) for g in range(N_GROUPS)]
    cmax = functools.reduce(jnp.maximum, coarse)
    grp = _first_argmax(coarse, cmax)
    p_grp = 1.0 / functools.reduce(lambda a, b: a + b, [jnp.exp(c - cmax) for c in coarse])
    fine = []
    for j in range(EXPERTS_PER_GROUP):
        f = row((N_GROUPS - 1) * EXPERTS_PER_GROUP + j)
        for g in range(N_GROUPS - 2, -1, -1):
            f = jnp.where(grp == g, row(g * EXPERTS_PER_GROUP + j), f)
        fine.append(f)
    v1 = functools.reduce(jnp.maximum, fine)
    i1 = _first_argmax(fine, v1)
    rest = [jnp.where(i1 == j, -BIG, fine[j]) for j in range(EXPERTS_PER_GROUP)]
    v2 = functools.reduce(jnp.maximum, rest)
    i2 = _first_argmax(rest, v2)
    e2 = jnp.exp(v2 - v1)
    w1 = p_grp / (1.0 + e2)
    w2 = p_grp * e2 / (1.0 + e2)
    in_group = [jnp.where(i1 == j, w1, 0.0) + jnp.where(i2 == j, w2, 0.0) for j in range(EXPERTS_PER_GROUP)]
    rows = [jnp.where(grp == e // EXPERTS_PER_GROUP, in_group[e % EXPERTS_PER_GROUP], 0.0) for e in range(N_EXPERTS)]
    rows.append(jnp.zeros((LANES - N_EXPERTS, lg.shape[1]), F32))
    gates_ref[...] = jnp.concatenate(rows, axis=0).T


def _router(x, li, g, whi, wlo, b, tm):
    m = x.shape[0]
    return pl.pallas_call(
        _router_kernel,
        grid=(m // tm,),
        in_specs=[pl.BlockSpec((tm, D_MODEL), lambda i: (i, 0)), _layer_spec(li, 1, D_MODEL),
                  _layer_spec(li, ROUTER_ROWS, D_MODEL), _layer_spec(li, ROUTER_ROWS, D_MODEL),
                  _layer_spec(li, ROUTER_ROWS, 1)],
        out_specs=pl.BlockSpec((tm, LANES), lambda i: (i, 0)),
        out_shape=jax.ShapeDtypeStruct((m, LANES), F32),
        compiler_params=_cparams("parallel"),
        name="moe_router",
    )(x, g, whi, wlo, b)


def _experts_kernel(x_ref, g_ref, gates_ref, w1_ref, w3_ref, w2_ref, o_ref, h_scr, acc_scr):
    gi = pl.program_id(1)

    @pl.when(gi == 0)
    def _():
        h_scr[...] = _rms_rows(x_ref[...], g_ref[...]).astype(BF)
        acc_scr[...] = jnp.zeros_like(acc_scr)

    h = h_scr[...]
    gates = gates_ref[...]
    lane = lax.broadcasted_iota(jnp.int32, gates.shape, 1)
    for j in range(EXPERTS_PER_GROUP):
        gcol = jnp.sum(jnp.where(lane == gi * EXPERTS_PER_GROUP + j, gates, 0.0), axis=-1, keepdims=True)
        a = _dot(h, w1_ref[j])
        b = _dot(h, w3_ref[j])
        hid = (a * jax.nn.sigmoid(a)) * b * gcol
        acc_scr[...] += _dot(hid.astype(BF), w2_ref[j])

    @pl.when(gi == pl.num_programs(1) - 1)
    def _():
        o_ref[...] = x_ref[...] + acc_scr[...]


def _experts(x, li, g, gates, w1, w3, w2, tm):
    m = x.shape[0]
    row = lambda w_: pl.BlockSpec((tm, w_), lambda i, e: (i, 0))
    return pl.pallas_call(
        _experts_kernel,
        grid=(m // tm, N_GROUPS),
        in_specs=[row(D_MODEL), _layer_spec(li, 1, D_MODEL), row(LANES),
                  pl.BlockSpec((None, EXPERTS_PER_GROUP, D_MODEL, D_EXPERT), lambda i, e: (li, e, 0, 0)),
                  pl.BlockSpec((None, EXPERTS_PER_GROUP, D_MODEL, D_EXPERT), lambda i, e: (li, e, 0, 0)),
                  pl.BlockSpec((None, EXPERTS_PER_GROUP, D_EXPERT, D_MODEL), lambda i, e: (li, e, 0, 0))],
        out_specs=row(D_MODEL),
        out_shape=jax.ShapeDtypeStruct((m, D_MODEL), F32),
        scratch_shapes=[pltpu.VMEM((tm, D_MODEL), BF), pltpu.VMEM((tm, D_MODEL), F32)],
        compiler_params=_cparams("parallel", "arbitrary"),
        name="moe_experts",
    )(x, g, gates, w1, w3, w2)


def _heads_kv_major_to_g_major(w, axis):
    shape = w.shape
    split = shape[:axis] + (KV_HEADS, Q_PER_KV, HEAD_DIM) + shape[axis + 1:]
    return jnp.swapaxes(w.reshape(split), axis, axis + 1).reshape(shape)


def _prep_ssm(a_re, a_im, log_dt, b_re, b_im, c_re, c_im, d_skip, w_glu):
    ar, ai = a_re.astype(F32), a_im.astype(F32)
    dt = jnp.exp(log_dt.astype(F32))[..., None]
    mag = jnp.exp(ar * dt)
    abr, abi = mag * jnp.cos(ai * dt), mag * jnp.sin(ai * dt)
    den = ar * ar + ai * ai
    kr = ((abr - 1.0) * ar + abi * ai) / den
    ki = (abi * ar - (abr - 1.0) * ai) / den
    br, bi = b_re.astype(F32), b_im.astype(F32)
    bbr = kr[..., None] * br - ki[..., None] * bi
    bbi = kr[..., None] * bi + ki[..., None] * br
    n_layers = ar.shape[0]
    eye = jnp.eye(SSM_GROUPS, dtype=F32)
    blockdiag_in = lambda b: jnp.einsum('egph,gk->eghkp', b, eye).reshape(n_layers, SSM_W, SSM_FLAT)
    blockdiag_out = lambda c: jnp.einsum('eghp,gk->egpkh', c, eye).reshape(n_layers, SSM_FLAT, SSM_W)
    in_tiles = lambda b: jnp.stack([b[:, j * SSM_IN_TILE:(j + 1) * SSM_IN_TILE, j * SSM_ST_TILE:(j + 1) * SSM_ST_TILE]
                                    for j in range(SSM_IN_TILES)], axis=1)
    out_tiles = lambda c: jnp.stack([c[:, j * SSM_OUT_ST:(j + 1) * SSM_OUT_ST, j * SSM_OUT_TILE:(j + 1) * SSM_OUT_TILE]
                                     for j in range(SSM_OUT_TILES)], axis=1)
    bcat = jnp.concatenate([in_tiles(blockdiag_in(bbr)), in_tiles(blockdiag_in(bbi))], axis=3).astype(BF)
    ccat = jnp.concatenate([out_tiles(blockdiag_out(c_re.astype(F32))), -out_tiles(blockdiag_out(c_im.astype(F32)))],
                           axis=2).astype(BF)
    return (abr.reshape(n_layers, 1, SSM_FLAT), abi.reshape(n_layers, 1, SSM_FLAT), bcat, ccat,
            d_skip.astype(F32).reshape(n_layers, 1, SSM_W), w_glu.astype(BF))


def _run_trunk(x, n, seq, mem_k, mem_v, caches, p):
    prompt = caches is None
    tm = 512
    wk_out, wv_out, hr_out, hi_out, cv_out = [], [], [], [], []
    for li in range(DEPTH):
        e = li // 2
        if li % 2 == 0:
            q, k, v, u = _even_in_proj(x, li, e, p['g_mix'], p['w_in_even'], p['g_q'], p['g_k'], p['bd'], tm, n, seq)
            if prompt:
                att = _swa_prompt(q, k, v, p['sinks'], e, n, seq)
                k3, v3 = k.reshape(n, seq, KV_W), v.reshape(n, seq, KV_W)
                new_k, new_v = k3[:, seq - WINDOW:], v3[:, seq - WINDOW:]
                ssm, h_re, h_im = _s5_glu(u.reshape(seq, n, SSM_W), None, p['ssm'], e, n, seq, 128)
                ssm = ssm.reshape(seq, n * SSM_W)
            else:
                win_k, win_v, ssm_re, ssm_im = caches
                att, new_k, new_v = _swa_sample(q, k, v, win_k, win_v, p['sinks'], e, n, seq)
                u_tm = jnp.swapaxes(u.reshape(n, seq, SSM_W), 0, 1)
                ssm, h_re, h_im = _s5_glu(u_tm, (ssm_re, ssm_im), p['ssm'], e, n, seq, seq)
                ssm = jnp.swapaxes(ssm, 0, 1).reshape(n * seq, SSM_W)
            x = _even_out_proj(x, att, ssm, e, p['w_out_att'], p['w_out_ssm'], tm, n, seq)
            wk_out.append(new_k.reshape(n, WINDOW, KV_HEADS, HEAD_DIM))
            wv_out.append(new_v.reshape(n, WINDOW, KV_HEADS, HEAD_DIM))
            hr_out.append(h_re.reshape(n, SSM_GROUPS, SSM_STATE))
            hi_out.append(h_im.reshape(n, SSM_GROUPS, SSM_STATE))
        else:
            u, v = _odd_in_proj(x, li, e, p['g_mix'], p['w_in_odd'], p['g_v'], tm, BF if prompt else F32)
            wsp, bsp, csize = p['gmlp_prompt'] if prompt else p['gmlp_sample']
            x = _gmlp_mix(x, u, v, e, wsp, bsp, p['w_out_odd'], csize, tm)
            if not prompt:
                cv_out.append(v.reshape(n, seq, GM_W))
        if prompt:
            x = _xattn_prompt(x, li, p['g_xattn'], p['w_xq'], p['g_xq'], mem_k, mem_v, p['w_xo'], n, seq, 512)
        else:
            x = _xattn_sample(x, li, p['g_xattn'], p['w_xq'], p['g_xq'], mem_k, mem_v, p['w_xo'], n, seq)
        gates = _router(x, li, p['g_ffn'], p['w_router_hi'], p['w_router_lo'], p['b_router'], tm)
        x = _experts(x, li, p['g_ffn'], gates, p['w_e1'], p['w_e3'], p['w_e2'], tm)
    cv = None if prompt else jnp.stack(cv_out)
    return x, jnp.stack(wk_out), jnp.stack(wv_out), jnp.stack(hr_out), jnp.stack(hi_out), cv


def kernel(x_prompt, x_sample, cache_win_k, cache_win_v, state_ssm_re, state_ssm_im, cache_mem_k, cache_mem_v, mem_prompt, g_mix, g_xattn, g_ffn, g_mem, w_in_even, g_q, g_k, sinks, ssm_a_re, ssm_a_im, ssm_log_dt, ssm_b_re, ssm_b_im, ssm_c_re, ssm_c_im, ssm_d, w_glu, w_out_even, w_in_odd, g_v, w_spatial, b_spatial, w_out_odd, w_xq, g_xq, w_xk, g_xk, w_xv, w_xo, w_coarse, b_coarse, w_fine, b_fine, w_e1, w_e3, w_e2):
    batch, seq = x_prompt.shape[0], x_prompt.shape[1]
    dec_batch, dec_seq = x_sample.shape[0], x_sample.shape[1]
    n_even = w_in_even.shape[0]

    row1 = lambda a: a.astype(F32)[:, None, :]
    p = {}
    p['g_mix'], p['g_xattn'], p['g_ffn'] = row1(g_mix), row1(g_xattn), row1(g_ffn)
    w_in_q = _heads_kv_major_to_g_major(w_in_even[:, :, :A_W], 2)
    p['w_in_even'] = jnp.concatenate([w_in_q, w_in_even[:, :, A_W:]], axis=2).astype(BF)
    p['g_q'] = jnp.tile(g_q.astype(F32), (1, A_HEADS))[:, None, :]
    p['g_k'] = jnp.tile(g_k.astype(F32), (1, KV_HEADS))[:, None, :]
    head_id = jnp.arange(A_W) // HEAD_DIM
    p['bd'] = (head_id[:, None] == head_id[None, :]).astype(BF)
    p['sinks'] = sinks.astype(F32).reshape(-1)
    p['ssm'] = _prep_ssm(ssm_a_re, ssm_a_im, ssm_log_dt, ssm_b_re, ssm_b_im, ssm_c_re, ssm_c_im, ssm_d, w_glu)
    p['w_out_att'] = _heads_kv_major_to_g_major(w_out_even[:, :A_W, :], 1).astype(BF)
    p['w_out_ssm'] = w_out_even[:, A_W:, :].astype(BF)
    p['w_in_odd'] = w_in_odd.astype(BF)
    p['g_v'] = row1(g_v)
    p['w_out_odd'] = w_out_odd.astype(BF)
    cs = min(dec_seq, CHUNK)
    reps = CHUNK // cs
    p['gmlp_prompt'] = (w_spatial.astype(F32), jnp.swapaxes(b_spatial.astype(F32), 1, 2), CHUNK)
    p['gmlp_sample'] = (jnp.tile(w_spatial[:, :, :cs, :cs].astype(F32), (1, 1, reps, reps)),
                        jnp.tile(jnp.swapaxes(b_spatial[:, :, :cs].astype(F32), 1, 2), (1, reps, 1)), cs)
    p['w_xq'] = w_xq.astype(BF)
    p['g_xq'] = row1(g_xq)
    p['w_xo'] = w_xo.astype(BF)
    w_router = jnp.swapaxes(jnp.concatenate([w_fine, w_coarse], axis=2).astype(F32), 1, 2)
    w_router = jnp.pad(w_router, ((0, 0), (0, ROUTER_ROWS - w_router.shape[1]), (0, 0)))
    p['w_router_hi'] = w_router.astype(BF)
    p['w_router_lo'] = (w_router - p['w_router_hi'].astype(F32)).astype(BF)
    b_router = jnp.concatenate([b_fine, b_coarse], axis=1).astype(F32)
    p['b_router'] = jnp.pad(b_router, ((0, 0), (0, ROUTER_ROWS - b_router.shape[1])))[:, :, None]
    p['w_e1'], p['w_e3'], p['w_e2'] = w_e1.astype(BF), w_e3.astype(BF), w_e2.astype(BF)

    mem2d = mem_prompt.reshape(batch * MEM_LEN, D_MODEL)
    pk, pv = _memory_kv(mem2d, row1(g_mem), w_xk.astype(BF), row1(g_xk), w_xv.astype(BF), batch)
    y_p, p_wk, p_wv, p_hr, p_hi, _ = _run_trunk(x_prompt.reshape(batch * seq, D_MODEL), batch, seq, pk, pv, None, p)

    caches = (cache_win_k.reshape(n_even, dec_batch, WINDOW, KV_W), cache_win_v.reshape(n_even, dec_batch, WINDOW, KV_W),
              state_ssm_re.reshape(n_even, dec_batch, SSM_FLAT), state_ssm_im.reshape(n_even, dec_batch, SSM_FLAT))
    y_s, s_wk, s_wv, s_hr, s_hi, s_cv = _run_trunk(x_sample.reshape(dec_batch * dec_seq, D_MODEL), dec_batch, dec_seq,
                                                   cache_mem_k.reshape(DEPTH, dec_batch, MEM_ROWS, X_HEAD_DIM),
                                                   cache_mem_v.reshape(DEPTH, dec_batch, MEM_ROWS, X_HEAD_DIM), caches, p)

    mem_shape = (DEPTH, batch, MEM_LEN, X_HEADS, X_HEAD_DIM)
    return (y_p.reshape(batch, seq, D_MODEL), y_s.reshape(dec_batch, dec_seq, D_MODEL), p_wk, p_wv, p_hr, p_hi,
            pk.reshape(mem_shape), pv.reshape(mem_shape), s_wk, s_wv, s_hr, s_hi, s_cv)
```

```python
import functools
import math

import jax
import jax.numpy as jnp
from jax import lax
from jax.experimental import pallas as pl
from jax.experimental.pallas import tpu as pltpu

F32 = jnp.float32
BF = jnp.bfloat16

D_MODEL = 1024
DEPTH = 4
A_W = 512
HEAD_DIM = 64
A_HEADS = 8
KV_HEADS = 2
Q_PER_KV = 4
KV_W = 128
WINDOW = 128
SSM_W = 512
SSM_CH = 16
SSM_GROUPS = 32
SSM_STATE = 64
SSM_FLAT = SSM_GROUPS * SSM_STATE
IN_EVEN = A_W + 2 * KV_W + SSM_W
CHUNK = 128
GM_W = 2048
GM_HEADS = 8
GM_HD = 256
MEM_LEN = 256
X_HEADS = 4
X_HEAD_DIM = 128
X_W = 512
N_GROUPS = 4
EXPERTS_PER_GROUP = 4
N_EXPERTS = 16
D_EXPERT = 256
EPS = 1e-6
NEG_INF = -1e30
LANES = 128
VMEM_LIMIT_BYTES = 52 * 1024 * 1024
ALIBI_SLOPES = tuple(2.0 ** (-8.0 * (h + 1) / A_HEADS) for h in range(A_HEADS))
SSM_SEQ_BLOCK = 8


def _cparams(*sem):
    return pltpu.CompilerParams(dimension_semantics=sem, vmem_limit_bytes=VMEM_LIMIT_BYTES)


def _rms_rows(x, g):
    return x * lax.rsqrt(jnp.mean(x * x, axis=-1, keepdims=True) + EPS) * g


def _dot(a, b):
    return jnp.dot(a, b, preferred_element_type=F32)


def _dot_nt(a, b):
    return lax.dot_general(a, b, (((1,), (1,)), ((), ())), preferred_element_type=F32)


def _full(shape):
    nd = len(shape)
    return pl.BlockSpec(shape, lambda *_: (0,) * nd)


def _layer_spec(li, *shape, single_buffer=False):
    nd = len(shape)
    mode = pl.Buffered(1) if single_buffer else None
    return pl.BlockSpec((None,) + shape, lambda *_: (li,) + (0,) * nd, pipeline_mode=mode)


def _even_in_kernel(x_ref, g_ref, w_ref, gq_ref, gk_ref, bd_ref, q_ref, k_ref, v_ref, u_ref):
    h = _rms_rows(x_ref[...], g_ref[...]).astype(BF)
    z = _dot(h, w_ref[...])
    q = z[:, :A_W]
    k = z[:, A_W:A_W + KV_W]
    bd = bd_ref[...]
    qms = _dot((q * q).astype(BF), bd) * (1.0 / HEAD_DIM)
    kms = _dot((k * k).astype(BF), bd[:KV_W, :KV_W]) * (1.0 / HEAD_DIM)
    q_ref[...] = q * lax.rsqrt(qms + EPS) * gq_ref[...]
    k_ref[...] = k * lax.rsqrt(kms + EPS) * gk_ref[...]
    v_ref[...] = z[:, A_W + KV_W:A_W + 2 * KV_W]
    u_ref[...] = z[:, A_W + 2 * KV_W:]


def _time_major_spec(tm, n, seq, width):
    per_seq = seq // tm
    return pl.BlockSpec((tm, width), lambda i: (i % per_seq, i // per_seq))


def _even_in_proj(x, li, e, g, w, gq, gk, bd, tm, n, seq):
    m = x.shape[0]
    row = lambda w_: pl.BlockSpec((tm, w_), lambda i: (i, 0))
    return pl.pallas_call(
        _even_in_kernel,
        grid=(m // tm,),
        in_specs=[row(D_MODEL), _layer_spec(li, 1, D_MODEL), _layer_spec(e, D_MODEL, IN_EVEN), _layer_spec(e, 1, A_W),
                  _layer_spec(e, 1, KV_W), _full((A_W, A_W))],
        out_specs=[row(A_W), row(KV_W), row(KV_W), _time_major_spec(tm, n, seq, SSM_W) if seq % tm == 0 else row(SSM_W)],
        out_shape=[jax.ShapeDtypeStruct((m, A_W), F32), jax.ShapeDtypeStruct((m, KV_W), F32),
                   jax.ShapeDtypeStruct((m, KV_W), F32),
                   jax.ShapeDtypeStruct((seq, n * SSM_W) if seq % tm == 0 else (m, SSM_W), F32)],
        compiler_params=_cparams("parallel"),
        name="even_in_proj",
    )(x, g, w, gq, gk, bd)


def _head_rows(q, lane_lo):
    rows = []
    for kv in range(KV_HEADS):
        keep = lane_lo if kv == 0 else jnp.logical_not(lane_lo)
        for g in range(Q_PER_KV):
            rows.append(jnp.where(keep, q[:, g * LANES:(g + 1) * LANES], 0.0))
    return jnp.concatenate(rows, axis=0)


def _head_cols(o, r, lane_lo):
    cols = []
    for g in range(Q_PER_KV):
        cols.append(jnp.where(lane_lo, o[g * r:(g + 1) * r], o[(Q_PER_KV + g) * r:(Q_PER_KV + g + 1) * r]))
    return jnp.concatenate(cols, axis=1)


def _swa_prompt_kernel(e, sinks_ref, q_ref, kc_ref, kp_ref, vc_ref, vp_ref, o_ref):
    b = pl.program_id(1)
    lane_lo = lax.broadcasted_iota(jnp.int32, (1, LANES), 1) < HEAD_DIM
    qrows = _head_rows(q_ref[...], lane_lo).astype(BF)
    kcat = jnp.concatenate([kp_ref[...], kc_ref[...]], axis=0).astype(BF)
    vcat = jnp.concatenate([vp_ref[...], vc_ref[...]], axis=0).astype(BF)
    s = _dot_nt(qrows, kcat) * (HEAD_DIM ** -0.5)
    r = lax.broadcasted_iota(jnp.int32, (WINDOW, 2 * WINDOW), 0)
    c = lax.broadcasted_iota(jnp.int32, (WINDOW, 2 * WINDOW), 1)
    dist = WINDOW + r - c
    valid = (dist >= 0) & (dist <= WINDOW) & ((c >= WINDOW) | (b > 0))
    dist_f = dist.astype(F32)
    ps = []
    for h in range(A_HEADS):
        sh = s[h * WINDOW:(h + 1) * WINDOW] - ALIBI_SLOPES[h] * dist_f
        sh = jnp.where(valid, sh, NEG_INF)
        sk = sinks_ref[e * A_HEADS + h]
        m = jnp.maximum(jnp.max(sh, axis=-1, keepdims=True), sk)
        p = jnp.exp(sh - m)
        den = jnp.sum(p, axis=-1, keepdims=True) + jnp.exp(sk - m)
        ps.append((p / den).astype(BF))
    o = _dot(jnp.concatenate(ps, axis=0), vcat)
    o_ref[...] = _head_cols(o, WINDOW, lane_lo).astype(o_ref.dtype)


def _swa_prompt(q, k, v, sinks, e, n, seq):
    nb = seq // WINDOW
    cur = lambda w_: pl.BlockSpec((WINDOW, w_), lambda i, b: (i * nb + b, 0))
    prev = lambda w_: pl.BlockSpec((WINDOW, w_), lambda i, b: (i * nb + jnp.maximum(b - 1, 0), 0))
    return pl.pallas_call(
        functools.partial(_swa_prompt_kernel, e),
        grid=(n, nb),
        in_specs=[pl.BlockSpec(memory_space=pltpu.SMEM), cur(A_W), cur(KV_W), prev(KV_W), cur(KV_W), prev(KV_W)],
        out_specs=cur(A_W),
        out_shape=jax.ShapeDtypeStruct((n * seq, A_W), BF),
        compiler_params=_cparams("parallel", "arbitrary"),
        name="swa_prompt",
    )(sinks, q, k, k, v, v)


SWA_SEQ_BLOCK = 8


def _swa_sample_kernel(t_new, e, sinks_ref, q_ref, kn_ref, vn_ref, wk_ref, wv_ref, o_ref, nwk_ref, nwv_ref):
    nseq = SWA_SEQ_BLOCK
    pair_rows = 2 * t_new
    npair = nseq // 2
    kn = kn_ref[...]
    vn = vn_ref[...]
    for i in range(nseq):
        nwk_ref[i, 0:WINDOW - t_new, :] = wk_ref[i, t_new:WINDOW, :]
        nwk_ref[i, WINDOW - t_new:WINDOW, :] = kn[i * t_new:(i + 1) * t_new, :]
        nwv_ref[i, 0:WINDOW - t_new, :] = wv_ref[i, t_new:WINDOW, :]
        nwv_ref[i, WINDOW - t_new:WINDOW, :] = vn[i * t_new:(i + 1) * t_new, :]

    lane_lo = lax.broadcasted_iota(jnp.int32, (1, LANES), 1) < HEAD_DIM
    q = q_ref[...]
    knb = kn.astype(BF)
    vnb = vn.astype(BF)
    hr = A_HEADS * pair_rows
    qrows = [_head_rows(q[j * pair_rows:(j + 1) * pair_rows], lane_lo).astype(BF) for j in range(npair)]
    scale = HEAD_DIM ** -0.5
    s_new_all = _dot_nt(jnp.concatenate(qrows, axis=0), knb) * scale

    tq = lax.broadcasted_iota(jnp.int32, (pair_rows, 2 * WINDOW), 0)
    cw = lax.broadcasted_iota(jnp.int32, (pair_rows, 2 * WINDOW), 1)
    dist_w = WINDOW + (tq % t_new) - (cw % WINDOW)
    valid_w = ((tq // t_new) == (cw // WINDOW)) & (dist_w <= WINDOW)
    dist_wf = dist_w.astype(F32)
    tqn = lax.broadcasted_iota(jnp.int32, (pair_rows, nseq * t_new), 0)
    cn = lax.broadcasted_iota(jnp.int32, (pair_rows, nseq * t_new), 1)
    dist_n = (tqn % t_new) - (cn % t_new)
    dist_nf = dist_n.astype(F32)

    p_new_all = []
    o_win_all = []
    for j in range(npair):
        kwin = jnp.concatenate([wk_ref[2 * j], wk_ref[2 * j + 1]], axis=0).astype(BF)
        vwin = jnp.concatenate([wv_ref[2 * j], wv_ref[2 * j + 1]], axis=0).astype(BF)
        s_win = _dot_nt(qrows[j], kwin) * scale
        valid_n = ((2 * j + tqn // t_new) == (cn // t_new)) & (dist_n >= 0)
        p_win = []
        for h in range(A_HEADS):
            sw = jnp.where(valid_w, s_win[h * pair_rows:(h + 1) * pair_rows] - ALIBI_SLOPES[h] * dist_wf, NEG_INF)
            sn = s_new_all[j * hr + h * pair_rows:j * hr + (h + 1) * pair_rows]
            sn = jnp.where(valid_n, sn - ALIBI_SLOPES[h] * dist_nf, NEG_INF)
            sk = sinks_ref[e * A_HEADS + h]
            m = jnp.maximum(jnp.maximum(jnp.max(sw, axis=-1, keepdims=True), jnp.max(sn, axis=-1, keepdims=True)), sk)
            pw = jnp.exp(sw - m)
            pn = jnp.exp(sn - m)
            den = jnp.sum(pw, axis=-1, keepdims=True) + jnp.sum(pn, axis=-1, keepdims=True) + jnp.exp(sk - m)
            p_win.append((pw / den).astype(BF))
            p_new_all.append((pn / den).astype(BF))
        o_win_all.append(_dot(jnp.concatenate(p_win, axis=0), vwin))
    o_new = _dot(jnp.concatenate(p_new_all, axis=0), vnb)
    outs = [_head_cols(o_win_all[j] + o_new[j * hr:(j + 1) * hr], pair_rows, lane_lo) for j in range(npair)]
    o_ref[...] = jnp.concatenate(outs, axis=0).astype(o_ref.dtype)


def _swa_sample(q, kn, vn, win_k, win_v, sinks, e, n, t_new):
    rows = SWA_SEQ_BLOCK * t_new
    row = lambda w_: pl.BlockSpec((rows, w_), lambda i: (i, 0))
    win = pl.BlockSpec((SWA_SEQ_BLOCK, WINDOW, KV_W), lambda i: (i, 0, 0))
    win_in = pl.BlockSpec((None, SWA_SEQ_BLOCK, WINDOW, KV_W), lambda i: (e, i, 0, 0))
    return pl.pallas_call(
        functools.partial(_swa_sample_kernel, t_new, e),
        grid=(n // SWA_SEQ_BLOCK,),
        in_specs=[pl.BlockSpec(memory_space=pltpu.SMEM), row(A_W), row(KV_W), row(KV_W), win_in, win_in],
        out_specs=[row(A_W), win, win],
        out_shape=[jax.ShapeDtypeStruct((n * t_new, A_W), BF),
                   jax.ShapeDtypeStruct((n, WINDOW, KV_W), F32), jax.ShapeDtypeStruct((n, WINDOW, KV_W), F32)],
        compiler_params=_cparams("parallel"),
        name="swa_sample",
    )(sinks, q, kn, vn, win_k, win_v)


SSM_LANE_CHUNK = 1024
SSM_IN_TILE = LANES
SSM_IN_TILES = SSM_W // SSM_IN_TILE
SSM_ST_TILE = SSM_IN_TILE // SSM_CH * SSM_STATE
SSM_OUT_TILE = 256
SSM_OUT_TILES = SSM_W // SSM_OUT_TILE
SSM_OUT_ST = SSM_OUT_TILE // SSM_CH * SSM_STATE


def _s5_kernel(tc, has_h0, *refs):
    if has_h0:
        (u_ref, h0r_ref, h0i_ref, abr_ref, abi_ref, bcat_ref, ccat_ref, d_ref, wglu_ref,
         o_ref, hr_ref, hi_ref, st_scr, car_scr) = refs
    else:
        (u_ref, abr_ref, abi_ref, bcat_ref, ccat_ref, d_ref, wglu_ref,
         o_ref, hr_ref, hi_ref, st_scr, car_scr) = refs
    nb = SSM_SEQ_BLOCK
    ci = pl.program_id(1)

    @pl.when(ci == 0)
    def _():
        if has_h0:
            car_scr[:, :SSM_FLAT] = h0r_ref[...]
            car_scr[:, SSM_FLAT:] = h0i_ref[...]
        else:
            car_scr[...] = jnp.zeros_like(car_scr)

    ut = u_ref[...].reshape(tc * nb, SSM_W)
    ub = ut.astype(BF)
    for jt in range(SSM_IN_TILES):
        bu = _dot(ub[:, jt * SSM_IN_TILE:(jt + 1) * SSM_IN_TILE], bcat_ref[jt])
        st_scr[:, jt * SSM_ST_TILE:(jt + 1) * SSM_ST_TILE] = bu[:, :SSM_ST_TILE]
        st_scr[:, SSM_FLAT + jt * SSM_ST_TILE:SSM_FLAT + (jt + 1) * SSM_ST_TILE] = bu[:, SSM_ST_TILE:]

    for lc in range(SSM_FLAT // SSM_LANE_CHUNK):
        lo = lc * SSM_LANE_CHUNK
        re_sl = slice(lo, lo + SSM_LANE_CHUNK)
        im_sl = slice(SSM_FLAT + lo, SSM_FLAT + lo + SSM_LANE_CHUNK)
        ar = jnp.broadcast_to(abr_ref[:, re_sl], (nb, SSM_LANE_CHUNK))
        ai = jnp.broadcast_to(abi_ref[:, re_sl], (nb, SSM_LANE_CHUNK))

        def step(t, carry):
            hr, hi = carry
            rows = pl.ds(pl.multiple_of(t * nb, nb), nb)
            nr = ar * hr - ai * hi + st_scr[rows, re_sl]
            ni = ar * hi + ai * hr + st_scr[rows, im_sl]
            st_scr[rows, re_sl] = nr
            st_scr[rows, im_sl] = ni
            return nr, ni

        hr, hi = lax.fori_loop(0, tc, step, (car_scr[:, re_sl], car_scr[:, im_sl]))
        car_scr[:, re_sl] = hr
        car_scr[:, im_sl] = hi

    ys = []
    for ot in range(SSM_OUT_TILES):
        re_sl = slice(ot * SSM_OUT_ST, (ot + 1) * SSM_OUT_ST)
        im_sl = slice(SSM_FLAT + ot * SSM_OUT_ST, SSM_FLAT + (ot + 1) * SSM_OUT_ST)
        ys.append(_dot(st_scr[:, re_sl].astype(BF), ccat_ref[ot, :SSM_OUT_ST, :])
                  + _dot(st_scr[:, im_sl].astype(BF), ccat_ref[ot, SSM_OUT_ST:, :]))
    y = jnp.concatenate(ys, axis=1) + d_ref[...] * ut
    y = jax.nn.gelu(y).astype(BF)
    g = _dot(y, wglu_ref[...])
    o_ref[...] = (g[:, :SSM_W] * jax.nn.sigmoid(g[:, SSM_W:])).reshape(tc, nb, SSM_W)

    @pl.when(ci == pl.num_programs(1) - 1)
    def _():
        hr_ref[...] = car_scr[:, :SSM_FLAT]
        hi_ref[...] = car_scr[:, SSM_FLAT:]


def _s5_glu(u, h0, ssm, e, n, seq, tc):
    nb = SSM_SEQ_BLOCK
    abr, abi, bcat, ccat, dsk, wglu = ssm
    u_spec = pl.BlockSpec((tc, nb, SSM_W), lambda i, c: (c, i, 0))
    st_spec = pl.BlockSpec((nb, SSM_FLAT), lambda i, c: (i, 0))
    h0_spec = pl.BlockSpec((None, nb, SSM_FLAT), lambda i, c: (e, i, 0))
    consts = [_layer_spec(e, 1, SSM_FLAT), _layer_spec(e, 1, SSM_FLAT),
              _layer_spec(e, SSM_IN_TILES, SSM_IN_TILE, 2 * SSM_ST_TILE),
              _layer_spec(e, SSM_OUT_TILES, 2 * SSM_OUT_ST, SSM_OUT_TILE), _layer_spec(e, 1, SSM_W),
              _layer_spec(e, SSM_W, 2 * SSM_W)]
    has_h0 = h0 is not None
    in_specs = [u_spec] + ([h0_spec, h0_spec] if has_h0 else []) + consts
    args = [u] + (list(h0) if has_h0 else []) + [abr, abi, bcat, ccat, dsk, wglu]
    return pl.pallas_call(
        functools.partial(_s5_kernel, tc, has_h0),
        grid=(n // nb, seq // tc),
        in_specs=in_specs,
        out_specs=[u_spec, st_spec, st_spec],
        out_shape=[jax.ShapeDtypeStruct((seq, n, SSM_W), F32), jax.ShapeDtypeStruct((n, SSM_FLAT), F32),
                   jax.ShapeDtypeStruct((n, SSM_FLAT), F32)],
        scratch_shapes=[pltpu.VMEM((nb * tc, 2 * SSM_FLAT), F32), pltpu.VMEM((nb, 2 * SSM_FLAT), F32)],
        compiler_params=_cparams("parallel", "arbitrary"),
        name="s5_glu",
    )(*args)


def _even_out_kernel(x_ref, a_ref, s_ref, wa_ref, ws_ref, o_ref):
    o_ref[...] = x_ref[...] + _dot(a_ref[...], wa_ref[...]) + _dot(s_ref[...].astype(BF), ws_ref[...])


def _even_out_proj(x, att, ssm, e, wa, ws, tm, n, seq):
    m = x.shape[0]
    row = lambda w_: pl.BlockSpec((tm, w_), lambda i: (i, 0))
    ssm_spec = _time_major_spec(tm, n, seq, SSM_W) if seq % tm == 0 else row(SSM_W)
    return pl.pallas_call(
        _even_out_kernel,
        grid=(m // tm,),
        in_specs=[row(D_MODEL), row(A_W), ssm_spec, _layer_spec(e, A_W, D_MODEL), _layer_spec(e, SSM_W, D_MODEL)],
        out_specs=row(D_MODEL),
        out_shape=jax.ShapeDtypeStruct((m, D_MODEL), F32),
        compiler_params=_cparams("parallel"),
        name="even_out_proj",
    )(x, att, ssm, wa, ws)


def _odd_in_kernel(x_ref, g_ref, w_ref, gv_ref, u_ref, v_ref):
    h = _rms_rows(x_ref[...], g_ref[...]).astype(BF)
    u_ref[...] = jax.nn.gelu(_dot(h, w_ref[:, :GM_W])).astype(u_ref.dtype)
    zv = jax.nn.gelu(_dot(h, w_ref[:, GM_W:]))
    v_ref[...] = _rms_rows(zv, gv_ref[...]).astype(v_ref.dtype)


def _odd_in_proj(x, li, e, g, w, gv, tm, v_dtype):
    m = x.shape[0]
    row = lambda w_: pl.BlockSpec((tm, w_), lambda i: (i, 0))
    return pl.pallas_call(
        _odd_in_kernel,
        grid=(m // tm,),
        in_specs=[row(D_MODEL), _layer_spec(li, 1, D_MODEL), _layer_spec(e, D_MODEL, 2 * GM_W, single_buffer=True),
                  _layer_spec(e, 1, GM_W)],
        out_specs=[row(GM_W), row(GM_W)],
        out_shape=[jax.ShapeDtypeStruct((m, GM_W), BF), jax.ShapeDtypeStruct((m, GM_W), v_dtype)],
        compiler_params=_cparams("parallel"),
        name="odd_in_proj",
    )(x, g, w, gv)


def _gmlp_rows(csize, x_ref, u_ref, v_ref, wsp_ref, bsp_ref, wo_ref, gated_scr):
    tm = x_ref.shape[0]
    i = lax.broadcasted_iota(jnp.int32, (CHUNK, CHUNK), 0)
    j = lax.broadcasted_iota(jnp.int32, (CHUNK, CHUNK), 1)
    keep = (j <= i) & ((i // csize) == (j // csize))
    bsp = bsp_ref[...]
    for h in range(GM_HEADS):
        ws = jnp.where(keep, wsp_ref[h], 0.0).astype(BF)
        b_col = bsp[:, h:h + 1]
        cols = slice(h * GM_HD, (h + 1) * GM_HD)
        for c in range(tm // CHUNK):
            rows = slice(c * CHUNK, (c + 1) * CHUNK)
            mix = _dot(ws, v_ref[rows, cols].astype(BF)) + b_col
            gated_scr[rows, cols] = (u_ref[rows, cols].astype(F32) * mix).astype(BF)
    return x_ref[...] + _dot(gated_scr[...], wo_ref[...])


def _gmlp_mix_kernel(csize, x_ref, u_ref, v_ref, wsp_ref, bsp_ref, wo_ref, o_ref, gated_scr):
    o_ref[...] = _gmlp_rows(csize, x_ref, u_ref, v_ref, wsp_ref, bsp_ref, wo_ref, gated_scr)


def _gmlp_mix(x, u, v, e, wsp, bsp, wo, csize, tm):
    m = x.shape[0]
    row = lambda w_: pl.BlockSpec((tm, w_), lambda i: (i, 0))
    return pl.pallas_call(
        functools.partial(_gmlp_mix_kernel, csize),
        grid=(m // tm,),
        in_specs=[row(D_MODEL), row(GM_W), row(GM_W), _layer_spec(e, GM_HEADS, CHUNK, CHUNK),
                  _layer_spec(e, CHUNK, GM_HEADS), _layer_spec(e, GM_W, D_MODEL)],
        out_specs=row(D_MODEL),
        out_shape=jax.ShapeDtypeStruct((m, D_MODEL), F32),
        scratch_shapes=[pltpu.VMEM((tm, GM_W), BF)],
        compiler_params=_cparams("parallel"),
        name="gmlp_mix",
    )(x, u, v, wsp, bsp, wo)


def _head_norm(z, g):
    cols = []
    for h in range(X_HEADS):
        zh = z[:, h * X_HEAD_DIM:(h + 1) * X_HEAD_DIM]
        cols.append(_rms_rows(zh, g))
    return jnp.concatenate(cols, axis=1)


MEMKV_SEQ_BLOCK = 2
MEM_ROWS = MEM_LEN * X_HEADS


def _head_rows_of(h):
    return pl.ds(h, MEM_LEN, stride=X_HEADS)


def _memory_kv_kernel(mem_ref, g_ref, wk_ref, gk_ref, wv_ref, k_ref, v_ref):
    m = _rms_rows(mem_ref[...], g_ref[...]).astype(BF)
    k = _head_norm(_dot(m, wk_ref[...]), gk_ref[...])
    v = _dot(m, wv_ref[...])
    for s in range(MEMKV_SEQ_BLOCK):
        rows = slice(s * MEM_LEN, (s + 1) * MEM_LEN)
        for h in range(X_HEADS):
            cols = slice(h * X_HEAD_DIM, (h + 1) * X_HEAD_DIM)
            k_ref[s, _head_rows_of(h), :] = k[rows, cols]
            v_ref[s, _head_rows_of(h), :] = v[rows, cols]


def _memory_kv(mem, g_mem, w_k, g_k, w_v, n):
    tm = MEMKV_SEQ_BLOCK * MEM_LEN
    per_layer = lambda a, b: pl.BlockSpec((None, a, b), lambda l, i: (l, 0, 0))
    out_spec = pl.BlockSpec((None, MEMKV_SEQ_BLOCK, MEM_ROWS, X_HEAD_DIM), lambda l, i: (l, i, 0, 0))
    out_sds = jax.ShapeDtypeStruct((DEPTH, n, MEM_ROWS, X_HEAD_DIM), F32)
    return pl.pallas_call(
        _memory_kv_kernel,
        grid=(DEPTH, n // MEMKV_SEQ_BLOCK),
        in_specs=[pl.BlockSpec((tm, D_MODEL), lambda l, i: (i, 0)), per_layer(1, D_MODEL), per_layer(D_MODEL, X_W),
                  per_layer(1, X_HEAD_DIM), per_layer(D_MODEL, X_W)],
        out_specs=[out_spec, out_spec],
        out_shape=[out_sds, out_sds],
        compiler_params=_cparams("parallel", "parallel"),
        name="memory_kv",
    )(mem, g_mem, w_k, g_k, w_v)


def _xattn_rows(x, g_ref, wq_ref, gq_ref, mk_ref, mv_ref, wo_ref):
    q = _head_norm(_dot(_rms_rows(x, g_ref[...]).astype(BF), wq_ref[...]), gq_ref[...]).astype(BF)
    outs = []
    for h in range(X_HEADS):
        cols = slice(h * X_HEAD_DIM, (h + 1) * X_HEAD_DIM)
        s = _dot_nt(q[:, cols], mk_ref[_head_rows_of(h), :].astype(BF)) * (X_HEAD_DIM ** -0.5)
        m = jnp.max(s, axis=-1, keepdims=True)
        p = jnp.exp(s - m)
        p = (p / jnp.sum(p, axis=-1, keepdims=True)).astype(BF)
        outs.append(_dot(p, mv_ref[_head_rows_of(h), :].astype(BF)))
    o = jnp.concatenate(outs, axis=1).astype(BF)
    return x + _dot(o, wo_ref[...])


def _even_out_xattn_kernel(x_ref, a_ref, s_ref, wa_ref, ws_ref, *rest):
    xattn_refs, o_ref = rest[:-1], rest[-1]
    x = x_ref[...] + _dot(a_ref[...], wa_ref[...]) + _dot(s_ref[...].astype(BF), ws_ref[...])
    o_ref[...] = _xattn_rows(x, *xattn_refs)


def _gmlp_xattn_kernel(csize, x_ref, u_ref, v_ref, wsp_ref, bsp_ref, wo_ref, *rest):
    xattn_refs, o_ref, gated_scr = rest[:-2], rest[-2], rest[-1]
    x = _gmlp_rows(csize, x_ref, u_ref, v_ref, wsp_ref, bsp_ref, wo_ref, gated_scr)
    o_ref[...] = _xattn_rows(x, *xattn_refs)


def _xattn_specs(li):
    mem = pl.BlockSpec((None, None, MEM_ROWS, X_HEAD_DIM), lambda i, j: (li, i, 0, 0))
    return [_layer_spec(li, 1, D_MODEL), _layer_spec(li, D_MODEL, X_W), _layer_spec(li, 1, X_HEAD_DIM), mem, mem,
            _layer_spec(li, X_W, D_MODEL)]


def _even_out_xattn(x, att, ssm, e, wa, ws, li, xattn_args, n, seq, tq):
    nq = seq // tq
    row = lambda w_: pl.BlockSpec((tq, w_), lambda i, j: (i * nq + j, 0))
    ssm_spec = pl.BlockSpec((tq, SSM_W), lambda i, j: (j, i))
    return pl.pallas_call(
        _even_out_xattn_kernel,
        grid=(n, nq),
        in_specs=[row(D_MODEL), row(A_W), ssm_spec, _layer_spec(e, A_W, D_MODEL), _layer_spec(e, SSM_W, D_MODEL)]
        + _xattn_specs(li),
        out_specs=row(D_MODEL),
        out_shape=jax.ShapeDtypeStruct((n * seq, D_MODEL), F32),
        compiler_params=_cparams("parallel", "arbitrary"),
        name="even_out_xattn",
    )(x, att, ssm, wa, ws, *xattn_args)


def _gmlp_xattn(x, u, v, e, wsp, bsp, wo, csize, li, xattn_args, n, seq, tq):
    nq = seq // tq
    row = lambda w_: pl.BlockSpec((tq, w_), lambda i, j: (i * nq + j, 0))
    return pl.pallas_call(
        functools.partial(_gmlp_xattn_kernel, csize),
        grid=(n, nq),
        in_specs=[row(D_MODEL), row(GM_W), row(GM_W), _layer_spec(e, GM_HEADS, CHUNK, CHUNK),
                  _layer_spec(e, CHUNK, GM_HEADS), _layer_spec(e, GM_W, D_MODEL)] + _xattn_specs(li),
        out_specs=row(D_MODEL),
        out_shape=jax.ShapeDtypeStruct((n * seq, D_MODEL), F32),
        scratch_shapes=[pltpu.VMEM((tq, GM_W), BF)],
        compiler_params=_cparams("parallel", "arbitrary"),
        name="gmlp_xattn",
    )(x, u, v, wsp, bsp, wo, *xattn_args)


XATTN_SEQ_BLOCK = 8


def _xattn_sample_kernel(t_new, x_ref, g_ref, wq_ref, gq_ref, mk_ref, mv_ref, wo_ref, o_ref):
    pair_rows = 2 * t_new
    x = x_ref[...]
    q = _head_norm(_dot(_rms_rows(x, g_ref[...]).astype(BF), wq_ref[...]), gq_ref[...]).astype(BF)
    tq = lax.broadcasted_iota(jnp.int32, (pair_rows, 2 * MEM_LEN), 0)
    cm = lax.broadcasted_iota(jnp.int32, (pair_rows, 2 * MEM_LEN), 1)
    same_seq = (tq // t_new) == (cm // MEM_LEN)
    npair = XATTN_SEQ_BLOCK // 2
    scores = []
    for j in range(npair):
        qp = q[j * pair_rows:(j + 1) * pair_rows]
        for h in range(X_HEADS):
            hr = _head_rows_of(h)
            kp = jnp.concatenate([mk_ref[2 * j, hr, :], mk_ref[2 * j + 1, hr, :]], axis=0).astype(BF)
            s = _dot_nt(qp[:, h * X_HEAD_DIM:(h + 1) * X_HEAD_DIM], kp)
            scores.append(jnp.where(same_seq, s, NEG_INF))
    s = jnp.concatenate(scores, axis=0) * (X_HEAD_DIM ** -0.5)
    m = jnp.max(s, axis=-1, keepdims=True)
    p = jnp.exp(s - m)
    p = p / jnp.sum(p, axis=-1, keepdims=True)
    outs = []
    for j in range(npair):
        heads = []
        for h in range(X_HEADS):
            hr = _head_rows_of(h)
            vp = jnp.concatenate([mv_ref[2 * j, hr, :], mv_ref[2 * j + 1, hr, :]], axis=0).astype(BF)
            r0 = (j * X_HEADS + h) * pair_rows
            heads.append(_dot(p[r0:r0 + pair_rows].astype(BF), vp))
        outs.append(jnp.concatenate(heads, axis=1))
    o_all = jnp.concatenate(outs, axis=0).astype(BF)
    o_ref[...] = x + _dot(o_all, wo_ref[...])


def _xattn_sample(x, li, g, wq, gq, mk, mv, wo, n, t_new):
    rows = XATTN_SEQ_BLOCK * t_new
    row = pl.BlockSpec((rows, D_MODEL), lambda i: (i, 0))
    mem = pl.BlockSpec((None, XATTN_SEQ_BLOCK, MEM_ROWS, X_HEAD_DIM), lambda i: (li, i, 0, 0))
    return pl.pallas_call(
        functools.partial(_xattn_sample_kernel, t_new),
        grid=(n // XATTN_SEQ_BLOCK,),
        in_specs=[row, _layer_spec(li, 1, D_MODEL), _layer_spec(li, D_MODEL, X_W), _layer_spec(li, 1, X_HEAD_DIM),
                  mem, mem, _layer_spec(li, X_W, D_MODEL)],
        out_specs=row,
        out_shape=jax.ShapeDtypeStruct((n * t_new, D_MODEL), F32),
        compiler_params=_cparams("parallel"),
        name="xattn_sample",
    )(x, g, wq, gq, mk, mv, wo)


ROUTER_ROWS = 32
ROUTER_COARSE_ROW = N_EXPERTS
BIG = 3.0e38


def _first_argmax(vals, vmax):
    idx = jnp.full(vmax.shape, len(vals) - 1, jnp.int32)
    for j in range(len(vals) - 2, -1, -1):
        idx = jnp.where(vals[j] == vmax, j, idx)
    return idx


def _router_gates(h, whi, wlo, bias):
    h_hi = h.astype(BF)
    h_lo = (h - h_hi.astype(F32)).astype(BF)
    lg = _dot_nt(whi, h_hi) + _dot_nt(whi, h_lo) + _dot_nt(wlo, h_hi) + bias
    row = lambda r: lg[r:r + 1, :]
    coarse = [row(ROUTER_COARSE_ROW + g) for g in range(N_GROUPS)]
    cmax = functools.reduce(jnp.maximum, coarse)
    grp = _first_argmax(coarse, cmax)
    p_grp = 1.0 / functools.reduce(lambda a, b: a + b, [jnp.exp(c - cmax) for c in coarse])
    fine = []
    for j in range(EXPERTS_PER_GROUP):
        f = row((N_GROUPS - 1) * EXPERTS_PER_GROUP + j)
        for g in range(N_GROUPS - 2, -1, -1):
            f = jnp.where(grp == g, row(g * EXPERTS_PER_GROUP + j), f)
        fine.append(f)
    v1 = functools.reduce(jnp.maximum, fine)
    i1 = _first_argmax(fine, v1)
    rest = [jnp.where(i1 == j, -BIG, fine[j]) for j in range(EXPERTS_PER_GROUP)]
    v2 = functools.reduce(jnp.maximum, rest)
    i2 = _first_argmax(rest, v2)
    e2 = jnp.exp(v2 - v1)
    w1 = p_grp / (1.0 + e2)
    w2 = p_grp * e2 / (1.0 + e2)
    in_group = [jnp.where(i1 == j, w1, 0.0) + jnp.where(i2 == j, w2, 0.0) for j in range(EXPERTS_PER_GROUP)]
    rows = [jnp.where(grp == e // EXPERTS_PER_GROUP, in_group[e % EXPERTS_PER_GROUP], 0.0) for e in range(N_EXPERTS)]
    rows.append(jnp.zeros((LANES - N_EXPERTS, lg.shape[1]), F32))
    return jnp.concatenate(rows, axis=0).T


def _moe_kernel(x_ref, g_ref, whi_ref, wlo_ref, b_ref, w1_ref, w3_ref, w2_ref, o_ref, h_scr, gates_scr, acc_scr):
    gi = pl.program_id(1)

    @pl.when(gi == 0)
    def _():
        h = _rms_rows(x_ref[...], g_ref[...])
        h_scr[...] = h.astype(BF)
        gates_scr[...] = _router_gates(h, whi_ref[...], wlo_ref[...], b_ref[...])
        acc_scr[...] = jnp.zeros_like(acc_scr)

    h = h_scr[...]
    gates = gates_scr[...]
    lane = lax.broadcasted_iota(jnp.int32, gates.shape, 1)
    for j in range(EXPERTS_PER_GROUP):
        gcol = jnp.sum(jnp.where(lane == gi * EXPERTS_PER_GROUP + j, gates, 0.0), axis=-1, keepdims=True)
        a = _dot(h, w1_ref[j])
        b = _dot(h, w3_ref[j])
        hid = (a * jax.nn.sigmoid(a)) * b * gcol
        acc_scr[...] += _dot(hid.astype(BF), w2_ref[j])

    @pl.when(gi == pl.num_programs(1) - 1)
    def _():
        o_ref[...] = x_ref[...] + acc_scr[...]


def _moe(x, li, g, whi, wlo, b, w1, w3, w2, tm):
    m = x.shape[0]
    row = lambda w_: pl.BlockSpec((tm, w_), lambda i, e: (i, 0))
    group_w = lambda a, b_: pl.BlockSpec((None, EXPERTS_PER_GROUP, a, b_), lambda i, e: (li, e, 0, 0))
    return pl.pallas_call(
        _moe_kernel,
        grid=(m // tm, N_GROUPS),
        in_specs=[row(D_MODEL), _layer_spec(li, 1, D_MODEL), _layer_spec(li, ROUTER_ROWS, D_MODEL),
                  _layer_spec(li, ROUTER_ROWS, D_MODEL), _layer_spec(li, ROUTER_ROWS, 1),
                  group_w(D_MODEL, D_EXPERT), group_w(D_MODEL, D_EXPERT), group_w(D_EXPERT, D_MODEL)],
        out_specs=row(D_MODEL),
        out_shape=jax.ShapeDtypeStruct((m, D_MODEL), F32),
        scratch_shapes=[pltpu.VMEM((tm, D_MODEL), BF), pltpu.VMEM((tm, LANES), F32), pltpu.VMEM((tm, D_MODEL), F32)],
        compiler_params=_cparams("parallel", "arbitrary"),
        name="moe",
    )(x, g, whi, wlo, b, w1, w3, w2)


def _heads_kv_major_to_g_major(w, axis):
    shape = w.shape
    split = shape[:axis] + (KV_HEADS, Q_PER_KV, HEAD_DIM) + shape[axis + 1:]
    return jnp.swapaxes(w.reshape(split), axis, axis + 1).reshape(shape)


def _prep_ssm(a_re, a_im, log_dt, b_re, b_im, c_re, c_im, d_skip, w_glu):
    ar, ai = a_re.astype(F32), a_im.astype(F32)
    dt = jnp.exp(log_dt.astype(F32))[..., None]
    mag = jnp.exp(ar * dt)
    abr, abi = mag * jnp.cos(ai * dt), mag * jnp.sin(ai * dt)
    den = ar * ar + ai * ai
    kr = ((abr - 1.0) * ar + abi * ai) / den
    ki = (abi * ar - (abr - 1.0) * ai) / den
    br, bi = b_re.astype(F32), b_im.astype(F32)
    bbr = kr[..., None] * br - ki[..., None] * bi
    bbi = kr[..., None] * bi + ki[..., None] * br
    n_layers = ar.shape[0]
    eye = jnp.eye(SSM_GROUPS, dtype=F32)
    blockdiag_in = lambda b: jnp.einsum('egph,gk->eghkp', b, eye).reshape(n_layers, SSM_W, SSM_FLAT)
    blockdiag_out = lambda c: jnp.einsum('eghp,gk->egpkh', c, eye).reshape(n_layers, SSM_FLAT, SSM_W)
    in_tiles = lambda b: jnp.stack([b[:, j * SSM_IN_TILE:(j + 1) * SSM_IN_TILE, j * SSM_ST_TILE:(j + 1) * SSM_ST_TILE]
                                    for j in range(SSM_IN_TILES)], axis=1)
    out_tiles = lambda c: jnp.stack([c[:, j * SSM_OUT_ST:(j + 1) * SSM_OUT_ST, j * SSM_OUT_TILE:(j + 1) * SSM_OUT_TILE]
                                     for j in range(SSM_OUT_TILES)], axis=1)
    bcat = jnp.concatenate([in_tiles(blockdiag_in(bbr)), in_tiles(blockdiag_in(bbi))], axis=3).astype(BF)
    ccat = jnp.concatenate([out_tiles(blockdiag_out(c_re.astype(F32))), -out_tiles(blockdiag_out(c_im.astype(F32)))],
                           axis=2).astype(BF)
    return (abr.reshape(n_layers, 1, SSM_FLAT), abi.reshape(n_layers, 1, SSM_FLAT), bcat, ccat,
            d_skip.astype(F32).reshape(n_layers, 1, SSM_W), w_glu.astype(BF))


def _run_trunk(x, n, seq, mem_k, mem_v, caches, p):
    prompt = caches is None
    tm = 512
    wk_out, wv_out, hr_out, hi_out, cv_out = [], [], [], [], []
    xattn_args = (p['g_xattn'], p['w_xq'], p['g_xq'], mem_k, mem_v, p['w_xo'])
    for li in range(DEPTH):
        e = li // 2
        if li % 2 == 0:
            q, k, v, u = _even_in_proj(x, li, e, p['g_mix'], p['w_in_even'], p['g_q'], p['g_k'], p['bd'], tm, n, seq)
            if prompt:
                att = _swa_prompt(q, k, v, p['sinks'], e, n, seq)
                k3, v3 = k.reshape(n, seq, KV_W), v.reshape(n, seq, KV_W)
                new_k, new_v = k3[:, seq - WINDOW:], v3[:, seq - WINDOW:]
                ssm, h_re, h_im = _s5_glu(u.reshape(seq, n, SSM_W), None, p['ssm'], e, n, seq, 128)
                ssm = ssm.reshape(seq, n * SSM_W)
            else:
                win_k, win_v, ssm_re, ssm_im = caches
                att, new_k, new_v = _swa_sample(q, k, v, win_k, win_v, p['sinks'], e, n, seq)
                u_tm = jnp.swapaxes(u.reshape(n, seq, SSM_W), 0, 1)
                ssm, h_re, h_im = _s5_glu(u_tm, (ssm_re, ssm_im), p['ssm'], e, n, seq, seq)
                ssm = jnp.swapaxes(ssm, 0, 1).reshape(n * seq, SSM_W)
            if prompt:
                x = _even_out_xattn(x, att, ssm, e, p['w_out_att'], p['w_out_ssm'], li, xattn_args, n, seq, tm)
            else:
                x = _even_out_proj(x, att, ssm, e, p['w_out_att'], p['w_out_ssm'], tm, n, seq)
            wk_out.append(new_k.reshape(n, WINDOW, KV_HEADS, HEAD_DIM))
            wv_out.append(new_v.reshape(n, WINDOW, KV_HEADS, HEAD_DIM))
            hr_out.append(h_re.reshape(n, SSM_GROUPS, SSM_STATE))
            hi_out.append(h_im.reshape(n, SSM_GROUPS, SSM_STATE))
        else:
            u, v = _odd_in_proj(x, li, e, p['g_mix'], p['w_in_odd'], p['g_v'], tm, BF if prompt else F32)
            wsp, bsp, csize = p['gmlp_prompt'] if prompt else p['gmlp_sample']
            if prompt:
                x = _gmlp_xattn(x, u, v, e, wsp, bsp, p['w_out_odd'], csize, li, xattn_args, n, seq, tm)
            else:
                x = _gmlp_mix(x, u, v, e, wsp, bsp, p['w_out_odd'], csize, tm)
                cv_out.append(v.reshape(n, seq, GM_W))
        if not prompt:
            x = _xattn_sample(x, li, *xattn_args, n, seq)
        x = _moe(x, li, p['g_ffn'], p['w_router_hi'], p['w_router_lo'], p['b_router'], p['w_e1'], p['w_e3'],
                 p['w_e2'], tm)
    cv = None if prompt else jnp.stack(cv_out)
    return x, jnp.stack(wk_out), jnp.stack(wv_out), jnp.stack(hr_out), jnp.stack(hi_out), cv


def kernel(x_prompt, x_sample, cache_win_k, cache_win_v, state_ssm_re, state_ssm_im, cache_mem_k, cache_mem_v, mem_prompt, g_mix, g_xattn, g_ffn, g_mem, w_in_even, g_q, g_k, sinks, ssm_a_re, ssm_a_im, ssm_log_dt, ssm_b_re, ssm_b_im, ssm_c_re, ssm_c_im, ssm_d, w_glu, w_out_even, w_in_odd, g_v, w_spatial, b_spatial, w_out_odd, w_xq, g_xq, w_xk, g_xk, w_xv, w_xo, w_coarse, b_coarse, w_fine, b_fine, w_e1, w_e3, w_e2):
    batch, seq = x_prompt.shape[0], x_prompt.shape[1]
    dec_batch, dec_seq = x_sample.shape[0], x_sample.shape[1]
    n_even = w_in_even.shape[0]

    row1 = lambda a: a.astype(F32)[:, None, :]
    p = {}
    p['g_mix'], p['g_xattn'], p['g_ffn'] = row1(g_mix), row1(g_xattn), row1(g_ffn)
    w_in_q = _heads_kv_major_to_g_major(w_in_even[:, :, :A_W], 2)
    p['w_in_even'] = jnp.concatenate([w_in_q, w_in_even[:, :, A_W:]], axis=2).astype(BF)
    p['g_q'] = jnp.tile(g_q.astype(F32), (1, A_HEADS))[:, None, :]
    p['g_k'] = jnp.tile(g_k.astype(F32), (1, KV_HEADS))[:, None, :]
    head_id = jnp.arange(A_W) // HEAD_DIM
    p['bd'] = (head_id[:, None] == head_id[None, :]).astype(BF)
    p['sinks'] = sinks.astype(F32).reshape(-1)
    p['ssm'] = _prep_ssm(ssm_a_re, ssm_a_im, ssm_log_dt, ssm_b_re, ssm_b_im, ssm_c_re, ssm_c_im, ssm_d, w_glu)
    p['w_out_att'] = _heads_kv_major_to_g_major(w_out_even[:, :A_W, :], 1).astype(BF)
    p['w_out_ssm'] = w_out_even[:, A_W:, :].astype(BF)
    p['w_in_odd'] = w_in_odd.astype(BF)
    p['g_v'] = row1(g_v)
    p['w_out_odd'] = w_out_odd.astype(BF)
    cs = min(dec_seq, CHUNK)
    reps = CHUNK // cs
    p['gmlp_prompt'] = (w_spatial.astype(F32), jnp.swapaxes(b_spatial.astype(F32), 1, 2), CHUNK)
    p['gmlp_sample'] = (jnp.tile(w_spatial[:, :, :cs, :cs].astype(F32), (1, 1, reps, reps)),
                        jnp.tile(jnp.swapaxes(b_spatial[:, :, :cs].astype(F32), 1, 2), (1, reps, 1)), cs)
    p['w_xq'] = w_xq.astype(BF)
    p['g_xq'] = row1(g_xq)
    p['w_xo'] = w_xo.astype(BF)
    w_router = jnp.swapaxes(jnp.concatenate([w_fine, w_coarse], axis=2).astype(F32), 1, 2)
    w_router = jnp.pad(w_router, ((0, 0), (0, ROUTER_ROWS - w_router.shape[1]), (0, 0)))
    p['w_router_hi'] = w_router.astype(BF)
    p['w_router_lo'] = (w_router - p['w_router_hi'].astype(F32)).astype(BF)
    b_router = jnp.concatenate([b_fine, b_coarse], axis=1).astype(F32)
    p['b_router'] = jnp.pad(b_router, ((0, 0), (0, ROUTER_ROWS - b_router.shape[1])))[:, :, None]
    p['w_e1'], p['w_e3'], p['w_e2'] = w_e1.astype(BF), w_e3.astype(BF), w_e2.astype(BF)

    mem2d = mem_prompt.reshape(batch * MEM_LEN, D_MODEL)
    pk, pv = _memory_kv(mem2d, row1(g_mem), w_xk.astype(BF), row1(g_xk), w_xv.astype(BF), batch)
    y_p, p_wk, p_wv, p_hr, p_hi, _ = _run_trunk(x_prompt.reshape(batch * seq, D_MODEL), batch, seq, pk, pv, None, p)

    caches = (cache_win_k.reshape(n_even, dec_batch, WINDOW, KV_W), cache_win_v.reshape(n_even, dec_batch, WINDOW, KV_W),
              state_ssm_re.reshape(n_even, dec_batch, SSM_FLAT), state_ssm_im.reshape(n_even, dec_batch, SSM_FLAT))
    y_s, s_wk, s_wv, s_hr, s_hi, s_cv = _run_trunk(x_sample.reshape(dec_batch * dec_seq, D_MODEL), dec_batch, dec_seq,
                                                   cache_mem_k.reshape(DEPTH, dec_batch, MEM_ROWS, X_HEAD_DIM),
                                                   cache_mem_v.reshape(DEPTH, dec_batch, MEM_ROWS, X_HEAD_DIM), caches, p)

    mem_shape = (DEPTH, batch, MEM_LEN, X_HEADS, X_HEAD_DIM)
    return (y_p.reshape(batch, seq, D_MODEL), y_s.reshape(dec_batch, dec_seq, D_MODEL), p_wk, p_wv, p_hr, p_hi,
            pk.reshape(mem_shape), pv.reshape(mem_shape), s_wk, s_wv, s_hr, s_hi, s_cv)
```

```python
import functools
import math

import jax
import jax.numpy as jnp
from jax import lax
from jax.experimental import pallas as pl
from jax.experimental.pallas import tpu as pltpu

F32 = jnp.float32
BF = jnp.bfloat16

D_MODEL = 1024
DEPTH = 4
A_W = 512
HEAD_DIM = 64
A_HEADS = 8
KV_HEADS = 2
Q_PER_KV = 4
KV_W = 128
WINDOW = 128
SSM_W = 512
SSM_CH = 16
SSM_GROUPS = 32
SSM_STATE = 64
SSM_FLAT = SSM_GROUPS * SSM_STATE
IN_EVEN = A_W + 2 * KV_W + SSM_W
CHUNK = 128
GM_W = 2048
GM_HEADS = 8
GM_HD = 256
MEM_LEN = 256
X_HEADS = 4
X_HEAD_DIM = 128
X_W = 512
N_GROUPS = 4
EXPERTS_PER_GROUP = 4
N_EXPERTS = 16
D_EXPERT = 256
EPS = 1e-6
NEG_INF = -1e30
LANES = 128
VMEM_LIMIT_BYTES = 52 * 1024 * 1024
ALIBI_SLOPES = tuple(2.0 ** (-8.0 * (h + 1) / A_HEADS) for h in range(A_HEADS))
SSM_SEQ_BLOCK = 8


def _cparams(*sem):
    return pltpu.CompilerParams(dimension_semantics=sem, vmem_limit_bytes=VMEM_LIMIT_BYTES)


def _rms_rows(x, g):
    return x * lax.rsqrt(jnp.mean(x * x, axis=-1, keepdims=True) + EPS) * g


def _dot(a, b):
    return jnp.dot(a, b, preferred_element_type=F32)


def _dot_nt(a, b):
    return lax.dot_general(a, b, (((1,), (1,)), ((), ())), preferred_element_type=F32)


def _full(shape):
    nd = len(shape)
    return pl.BlockSpec(shape, lambda *_: (0,) * nd)


def _layer_spec(li, *shape, single_buffer=False):
    nd = len(shape)
    mode = pl.Buffered(1) if single_buffer else None
    return pl.BlockSpec((None,) + shape, lambda *_: (li,) + (0,) * nd, pipeline_mode=mode)


def _even_in_kernel(x_ref, g_ref, w_ref, gq_ref, gk_ref, bd_ref, q_ref, k_ref, v_ref, u_ref):
    h = _rms_rows(x_ref[...], g_ref[...]).astype(BF)
    z = _dot(h, w_ref[...])
    q = z[:, :A_W]
    k = z[:, A_W:A_W + KV_W]
    bd = bd_ref[...]
    qms = _dot((q * q).astype(BF), bd) * (1.0 / HEAD_DIM)
    kms = _dot((k * k).astype(BF), bd[:KV_W, :KV_W]) * (1.0 / HEAD_DIM)
    q_ref[...] = q * lax.rsqrt(qms + EPS) * gq_ref[...]
    k_ref[...] = k * lax.rsqrt(kms + EPS) * gk_ref[...]
    v_ref[...] = z[:, A_W + KV_W:A_W + 2 * KV_W]
    u_ref[...] = z[:, A_W + 2 * KV_W:]


def _time_major_spec(tm, n, seq, width):
    per_seq = seq // tm
    return pl.BlockSpec((tm, width), lambda i: (i % per_seq, i // per_seq))


def _even_in_proj(x, li, e, g, w, gq, gk, bd, tm, n, seq):
    m = n * seq
    row = lambda w_: pl.BlockSpec((tm, w_), lambda i: (i, 0))
    return pl.pallas_call(
        _even_in_kernel,
        grid=(m // tm,),
        in_specs=[row(D_MODEL), _layer_spec(li, 1, D_MODEL), _layer_spec(e, D_MODEL, IN_EVEN), _layer_spec(e, 1, A_W),
                  _layer_spec(e, 1, KV_W), _full((A_W, A_W))],
        out_specs=[row(A_W), row(KV_W), row(KV_W), _time_major_spec(tm, n, seq, SSM_W) if seq % tm == 0 else row(SSM_W)],
        out_shape=[jax.ShapeDtypeStruct((m, A_W), F32), jax.ShapeDtypeStruct((m, KV_W), F32),
                   jax.ShapeDtypeStruct((m, KV_W), F32),
                   jax.ShapeDtypeStruct((seq, n * SSM_W) if seq % tm == 0 else (m, SSM_W), F32)],
        compiler_params=_cparams("parallel"),
        name="even_in_proj",
    )(x, g, w, gq, gk, bd)


def _head_rows(q, lane_lo):
    rows = []
    for kv in range(KV_HEADS):
        keep = lane_lo if kv == 0 else jnp.logical_not(lane_lo)
        for g in range(Q_PER_KV):
            rows.append(jnp.where(keep, q[:, g * LANES:(g + 1) * LANES], 0.0))
    return jnp.concatenate(rows, axis=0)


def _head_cols(o, r, lane_lo):
    cols = []
    for g in range(Q_PER_KV):
        cols.append(jnp.where(lane_lo, o[g * r:(g + 1) * r], o[(Q_PER_KV + g) * r:(Q_PER_KV + g + 1) * r]))
    return jnp.concatenate(cols, axis=1)


def _swa_prompt_kernel(e, sinks_ref, q_ref, kc_ref, kp_ref, vc_ref, vp_ref, o_ref):
    b = pl.program_id(1)
    lane_lo = lax.broadcasted_iota(jnp.int32, (1, LANES), 1) < HEAD_DIM
    qrows = _head_rows(q_ref[...], lane_lo).astype(BF)
    kcat = jnp.concatenate([kp_ref[...], kc_ref[...]], axis=0).astype(BF)
    vcat = jnp.concatenate([vp_ref[...], vc_ref[...]], axis=0).astype(BF)
    s = _dot_nt(qrows, kcat) * (HEAD_DIM ** -0.5)
    r = lax.broadcasted_iota(jnp.int32, (WINDOW, 2 * WINDOW), 0)
    c = lax.broadcasted_iota(jnp.int32, (WINDOW, 2 * WINDOW), 1)
    dist = WINDOW + r - c
    valid = (dist >= 0) & (dist <= WINDOW) & ((c >= WINDOW) | (b > 0))
    dist_f = dist.astype(F32)
    ps = []
    for h in range(A_HEADS):
        sh = s[h * WINDOW:(h + 1) * WINDOW] - ALIBI_SLOPES[h] * dist_f
        sh = jnp.where(valid, sh, NEG_INF)
        sk = sinks_ref[e * A_HEADS + h]
        m = jnp.maximum(jnp.max(sh, axis=-1, keepdims=True), sk)
        p = jnp.exp(sh - m)
        den = jnp.sum(p, axis=-1, keepdims=True) + jnp.exp(sk - m)
        ps.append((p / den).astype(BF))
    o = _dot(jnp.concatenate(ps, axis=0), vcat)
    o_ref[...] = _head_cols(o, WINDOW, lane_lo).astype(o_ref.dtype)


def _swa_prompt(q, k, v, sinks, e, n, seq):
    nb = seq // WINDOW
    cur = lambda w_: pl.BlockSpec((WINDOW, w_), lambda i, b: (i * nb + b, 0))
    prev = lambda w_: pl.BlockSpec((WINDOW, w_), lambda i, b: (i * nb + jnp.maximum(b - 1, 0), 0))
    return pl.pallas_call(
        functools.partial(_swa_prompt_kernel, e),
        grid=(n, nb),
        in_specs=[pl.BlockSpec(memory_space=pltpu.SMEM), cur(A_W), cur(KV_W), prev(KV_W), cur(KV_W), prev(KV_W)],
        out_specs=cur(A_W),
        out_shape=jax.ShapeDtypeStruct((n * seq, A_W), BF),
        compiler_params=_cparams("parallel", "arbitrary"),
        name="swa_prompt",
    )(sinks, q, k, k, v, v)


SWA_SEQ_BLOCK = 8


def _swa_sample_kernel(t_new, e, sinks_ref, q_ref, kn_ref, vn_ref, wk_ref, wv_ref, o_ref, nwk_ref, nwv_ref):
    nseq = SWA_SEQ_BLOCK
    pair_rows = 2 * t_new
    npair = nseq // 2
    kn = kn_ref[...]
    vn = vn_ref[...]
    for i in range(nseq):
        nwk_ref[i, 0:WINDOW - t_new, :] = wk_ref[i, t_new:WINDOW, :]
        nwk_ref[i, WINDOW - t_new:WINDOW, :] = kn[i * t_new:(i + 1) * t_new, :]
        nwv_ref[i, 0:WINDOW - t_new, :] = wv_ref[i, t_new:WINDOW, :]
        nwv_ref[i, WINDOW - t_new:WINDOW, :] = vn[i * t_new:(i + 1) * t_new, :]

    lane_lo = lax.broadcasted_iota(jnp.int32, (1, LANES), 1) < HEAD_DIM
    q = q_ref[...]
    knb = kn.astype(BF)
    vnb = vn.astype(BF)
    hr = A_HEADS * pair_rows
    qrows = [_head_rows(q[j * pair_rows:(j + 1) * pair_rows], lane_lo).astype(BF) for j in range(npair)]
    scale = HEAD_DIM ** -0.5
    s_new_all = _dot_nt(jnp.concatenate(qrows, axis=0), knb) * scale

    tq = lax.broadcasted_iota(jnp.int32, (pair_rows, 2 * WINDOW), 0)
    cw = lax.broadcasted_iota(jnp.int32, (pair_rows, 2 * WINDOW), 1)
    dist_w = WINDOW + (tq % t_new) - (cw % WINDOW)
    valid_w = ((tq // t_new) == (cw // WINDOW)) & (dist_w <= WINDOW)
    dist_wf = dist_w.astype(F32)
    tqn = lax.broadcasted_iota(jnp.int32, (pair_rows, nseq * t_new), 0)
    cn = lax.broadcasted_iota(jnp.int32, (pair_rows, nseq * t_new), 1)
    dist_n = (tqn % t_new) - (cn % t_new)
    dist_nf = dist_n.astype(F32)

    p_new_all = []
    o_win_all = []
    for j in range(npair):
        kwin = jnp.concatenate([wk_ref[2 * j], wk_ref[2 * j + 1]], axis=0).astype(BF)
        vwin = jnp.concatenate([wv_ref[2 * j], wv_ref[2 * j + 1]], axis=0).astype(BF)
        s_win = _dot_nt(qrows[j], kwin) * scale
        valid_n = ((2 * j + tqn // t_new) == (cn // t_new)) & (dist_n >= 0)
        p_win = []
        for h in range(A_HEADS):
            sw = jnp.where(valid_w, s_win[h * pair_rows:(h + 1) * pair_rows] - ALIBI_SLOPES[h] * dist_wf, NEG_INF)
            sn = s_new_all[j * hr + h * pair_rows:j * hr + (h + 1) * pair_rows]
            sn = jnp.where(valid_n, sn - ALIBI_SLOPES[h] * dist_nf, NEG_INF)
            sk = sinks_ref[e * A_HEADS + h]
            m = jnp.maximum(jnp.maximum(jnp.max(sw, axis=-1, keepdims=True), jnp.max(sn, axis=-1, keepdims=True)), sk)
            pw = jnp.exp(sw - m)
            pn = jnp.exp(sn - m)
            den = jnp.sum(pw, axis=-1, keepdims=True) + jnp.sum(pn, axis=-1, keepdims=True) + jnp.exp(sk - m)
            p_win.append((pw / den).astype(BF))
            p_new_all.append((pn / den).astype(BF))
        o_win_all.append(_dot(jnp.concatenate(p_win, axis=0), vwin))
    o_new = _dot(jnp.concatenate(p_new_all, axis=0), vnb)
    outs = [_head_cols(o_win_all[j] + o_new[j * hr:(j + 1) * hr], pair_rows, lane_lo) for j in range(npair)]
    o_ref[...] = jnp.concatenate(outs, axis=0).astype(o_ref.dtype)


def _swa_sample(q, kn, vn, win_k, win_v, sinks, e, n, t_new):
    rows = SWA_SEQ_BLOCK * t_new
    row = lambda w_: pl.BlockSpec((rows, w_), lambda i: (i, 0))
    win = pl.BlockSpec((SWA_SEQ_BLOCK, WINDOW, KV_W), lambda i: (i, 0, 0))
    win_in = pl.BlockSpec((None, SWA_SEQ_BLOCK, WINDOW, KV_W), lambda i: (e, i, 0, 0))
    return pl.pallas_call(
        functools.partial(_swa_sample_kernel, t_new, e),
        grid=(n // SWA_SEQ_BLOCK,),
        in_specs=[pl.BlockSpec(memory_space=pltpu.SMEM), row(A_W), row(KV_W), row(KV_W), win_in, win_in],
        out_specs=[row(A_W), win, win],
        out_shape=[jax.ShapeDtypeStruct((n * t_new, A_W), BF),
                   jax.ShapeDtypeStruct((n, WINDOW, KV_W), F32), jax.ShapeDtypeStruct((n, WINDOW, KV_W), F32)],
        compiler_params=_cparams("parallel"),
        name="swa_sample",
    )(sinks, q, kn, vn, win_k, win_v)


SSM_LANE_CHUNK = 1024
SSM_IN_TILE = LANES
SSM_IN_TILES = SSM_W // SSM_IN_TILE
SSM_ST_TILE = SSM_IN_TILE // SSM_CH * SSM_STATE
SSM_OUT_TILE = 256
SSM_OUT_TILES = SSM_W // SSM_OUT_TILE
SSM_OUT_ST = SSM_OUT_TILE // SSM_CH * SSM_STATE


def _s5_kernel(tc, has_h0, *refs):
    if has_h0:
        (u_ref, h0r_ref, h0i_ref, abr_ref, abi_ref, bcat_ref, ccat_ref, d_ref, wglu_ref,
         o_ref, hr_ref, hi_ref, st_scr, car_scr) = refs
    else:
        (u_ref, abr_ref, abi_ref, bcat_ref, ccat_ref, d_ref, wglu_ref,
         o_ref, hr_ref, hi_ref, st_scr, car_scr) = refs
    nb = SSM_SEQ_BLOCK
    ci = pl.program_id(1)

    @pl.when(ci == 0)
    def _():
        if has_h0:
            car_scr[:, :SSM_FLAT] = h0r_ref[...]
            car_scr[:, SSM_FLAT:] = h0i_ref[...]
        else:
            car_scr[...] = jnp.zeros_like(car_scr)

    ut = u_ref[...].reshape(tc * nb, SSM_W)
    ub = ut.astype(BF)
    for jt in range(SSM_IN_TILES):
        bu = _dot(ub[:, jt * SSM_IN_TILE:(jt + 1) * SSM_IN_TILE], bcat_ref[jt])
        st_scr[:, jt * SSM_ST_TILE:(jt + 1) * SSM_ST_TILE] = bu[:, :SSM_ST_TILE]
        st_scr[:, SSM_FLAT + jt * SSM_ST_TILE:SSM_FLAT + (jt + 1) * SSM_ST_TILE] = bu[:, SSM_ST_TILE:]

    for lc in range(SSM_FLAT // SSM_LANE_CHUNK):
        lo = lc * SSM_LANE_CHUNK
        re_sl = slice(lo, lo + SSM_LANE_CHUNK)
        im_sl = slice(SSM_FLAT + lo, SSM_FLAT + lo + SSM_LANE_CHUNK)
        ar = jnp.broadcast_to(abr_ref[:, re_sl], (nb, SSM_LANE_CHUNK))
        ai = jnp.broadcast_to(abi_ref[:, re_sl], (nb, SSM_LANE_CHUNK))

        def step(t, carry):
            hr, hi = carry
            rows = pl.ds(pl.multiple_of(t * nb, nb), nb)
            nr = ar * hr - ai * hi + st_scr[rows, re_sl]
            ni = ar * hi + ai * hr + st_scr[rows, im_sl]
            st_scr[rows, re_sl] = nr
            st_scr[rows, im_sl] = ni
            return nr, ni

        hr, hi = lax.fori_loop(0, tc, step, (car_scr[:, re_sl], car_scr[:, im_sl]))
        car_scr[:, re_sl] = hr
        car_scr[:, im_sl] = hi

    ys = []
    for ot in range(SSM_OUT_TILES):
        re_sl = slice(ot * SSM_OUT_ST, (ot + 1) * SSM_OUT_ST)
        im_sl = slice(SSM_FLAT + ot * SSM_OUT_ST, SSM_FLAT + (ot + 1) * SSM_OUT_ST)
        ys.append(_dot(st_scr[:, re_sl].astype(BF), ccat_ref[ot, :SSM_OUT_ST, :])
                  + _dot(st_scr[:, im_sl].astype(BF), ccat_ref[ot, SSM_OUT_ST:, :]))
    y = jnp.concatenate(ys, axis=1) + d_ref[...] * ut
    y = jax.nn.gelu(y).astype(BF)
    g = _dot(y, wglu_ref[...])
    o_ref[...] = (g[:, :SSM_W] * jax.nn.sigmoid(g[:, SSM_W:])).reshape(tc, nb, SSM_W)

    @pl.when(ci == pl.num_programs(1) - 1)
    def _():
        hr_ref[...] = car_scr[:, :SSM_FLAT]
        hi_ref[...] = car_scr[:, SSM_FLAT:]


def _s5_glu(u, h0, ssm, e, n, seq, tc):
    nb = SSM_SEQ_BLOCK
    abr, abi, bcat, ccat, dsk, wglu = ssm
    u_spec = pl.BlockSpec((tc, nb, SSM_W), lambda i, c: (c, i, 0))
    st_spec = pl.BlockSpec((nb, SSM_FLAT), lambda i, c: (i, 0))
    h0_spec = pl.BlockSpec((None, nb, SSM_FLAT), lambda i, c: (e, i, 0))
    consts = [_layer_spec(e, 1, SSM_FLAT), _layer_spec(e, 1, SSM_FLAT),
              _layer_spec(e, SSM_IN_TILES, SSM_IN_TILE, 2 * SSM_ST_TILE),
              _layer_spec(e, SSM_OUT_TILES, 2 * SSM_OUT_ST, SSM_OUT_TILE), _layer_spec(e, 1, SSM_W),
              _layer_spec(e, SSM_W, 2 * SSM_W)]
    has_h0 = h0 is not None
    in_specs = [u_spec] + ([h0_spec, h0_spec] if has_h0 else []) + consts
    args = [u] + (list(h0) if has_h0 else []) + [abr, abi, bcat, ccat, dsk, wglu]
    return pl.pallas_call(
        functools.partial(_s5_kernel, tc, has_h0),
        grid=(n // nb, seq // tc),
        in_specs=in_specs,
        out_specs=[u_spec, st_spec, st_spec],
        out_shape=[jax.ShapeDtypeStruct((seq, n, SSM_W), F32), jax.ShapeDtypeStruct((n, SSM_FLAT), F32),
                   jax.ShapeDtypeStruct((n, SSM_FLAT), F32)],
        scratch_shapes=[pltpu.VMEM((nb * tc, 2 * SSM_FLAT), F32), pltpu.VMEM((nb, 2 * SSM_FLAT), F32)],
        compiler_params=_cparams("parallel", "arbitrary"),
        name="s5_glu",
    )(*args)


def _even_out_kernel(x_ref, a_ref, s_ref, wa_ref, ws_ref, o_ref):
    o_ref[...] = x_ref[...] + _dot(a_ref[...], wa_ref[...]) + _dot(s_ref[...].astype(BF), ws_ref[...])


def _even_out_proj(x, att, ssm, e, wa, ws, tm, n, seq):
    m = x.shape[0]
    row = lambda w_: pl.BlockSpec((tm, w_), lambda i: (i, 0))
    ssm_spec = _time_major_spec(tm, n, seq, SSM_W) if seq % tm == 0 else row(SSM_W)
    return pl.pallas_call(
        _even_out_kernel,
        grid=(m // tm,),
        in_specs=[row(D_MODEL), row(A_W), ssm_spec, _layer_spec(e, A_W, D_MODEL), _layer_spec(e, SSM_W, D_MODEL)],
        out_specs=row(D_MODEL),
        out_shape=jax.ShapeDtypeStruct((m, D_MODEL), F32),
        compiler_params=_cparams("parallel"),
        name="even_out_proj",
    )(x, att, ssm, wa, ws)


def _odd_in_kernel(x_ref, g_ref, w_ref, gv_ref, u_ref, v_ref):
    h = _rms_rows(x_ref[...], g_ref[...]).astype(BF)
    u_ref[...] = jax.nn.gelu(_dot(h, w_ref[:, :GM_W])).astype(u_ref.dtype)
    zv = jax.nn.gelu(_dot(h, w_ref[:, GM_W:]))
    v_ref[...] = _rms_rows(zv, gv_ref[...]).astype(v_ref.dtype)


def _odd_in_proj(x, m, li, e, g, w, gv, tm, v_dtype):
    row = lambda w_: pl.BlockSpec((tm, w_), lambda i: (i, 0))
    return pl.pallas_call(
        _odd_in_kernel,
        grid=(m // tm,),
        in_specs=[row(D_MODEL), _layer_spec(li, 1, D_MODEL), _layer_spec(e, D_MODEL, 2 * GM_W, single_buffer=True),
                  _layer_spec(e, 1, GM_W)],
        out_specs=[row(GM_W), row(GM_W)],
        out_shape=[jax.ShapeDtypeStruct((m, GM_W), BF), jax.ShapeDtypeStruct((m, GM_W), v_dtype)],
        compiler_params=_cparams("parallel"),
        name="odd_in_proj",
    )(x, g, w, gv)


def _gmlp_rows(csize, x_ref, u_ref, v_ref, wsp_ref, bsp_ref, wo_ref, gated_scr):
    tm = x_ref.shape[0]
    i = lax.broadcasted_iota(jnp.int32, (CHUNK, CHUNK), 0)
    j = lax.broadcasted_iota(jnp.int32, (CHUNK, CHUNK), 1)
    keep = (j <= i) & ((i // csize) == (j // csize))
    bsp = bsp_ref[...]
    for h in range(GM_HEADS):
        ws = jnp.where(keep, wsp_ref[h], 0.0).astype(BF)
        b_col = bsp[:, h:h + 1]
        cols = slice(h * GM_HD, (h + 1) * GM_HD)
        for c in range(tm // CHUNK):
            rows = slice(c * CHUNK, (c + 1) * CHUNK)
            mix = _dot(ws, v_ref[rows, cols].astype(BF)) + b_col
            gated_scr[rows, cols] = (u_ref[rows, cols].astype(F32) * mix).astype(BF)
    return x_ref[...] + _dot(gated_scr[...], wo_ref[...])


def _gmlp_mix_kernel(csize, x_ref, u_ref, v_ref, wsp_ref, bsp_ref, wo_ref, o_ref, gated_scr):
    o_ref[...] = _gmlp_rows(csize, x_ref, u_ref, v_ref, wsp_ref, bsp_ref, wo_ref, gated_scr)


def _gmlp_mix(x, u, v, e, wsp, bsp, wo, csize, tm):
    m = x.shape[0]
    row = lambda w_: pl.BlockSpec((tm, w_), lambda i: (i, 0))
    return pl.pallas_call(
        functools.partial(_gmlp_mix_kernel, csize),
        grid=(m // tm,),
        in_specs=[row(D_MODEL), row(GM_W), row(GM_W), _layer_spec(e, GM_HEADS, CHUNK, CHUNK),
                  _layer_spec(e, CHUNK, GM_HEADS), _layer_spec(e, GM_W, D_MODEL)],
        out_specs=row(D_MODEL),
        out_shape=jax.ShapeDtypeStruct((m, D_MODEL), F32),
        scratch_shapes=[pltpu.VMEM((tm, GM_W), BF)],
        compiler_params=_cparams("parallel"),
        name="gmlp_mix",
    )(x, u, v, wsp, bsp, wo)


def _head_norm(z, g):
    cols = []
    for h in range(X_HEADS):
        zh = z[:, h * X_HEAD_DIM:(h + 1) * X_HEAD_DIM]
        cols.append(_rms_rows(zh, g))
    return jnp.concatenate(cols, axis=1)


MEMKV_SEQ_BLOCK = 2
MEM_ROWS = MEM_LEN * X_HEADS


def _head_rows_of(h):
    return pl.ds(h, MEM_LEN, stride=X_HEADS)


def _memory_kv_kernel(mem_ref, g_ref, wk_ref, gk_ref, wv_ref, k_ref, v_ref):
    m = _rms_rows(mem_ref[...], g_ref[...]).astype(BF)
    k = _head_norm(_dot(m, wk_ref[...]), gk_ref[...])
    v = _dot(m, wv_ref[...])
    for s in range(MEMKV_SEQ_BLOCK):
        rows = slice(s * MEM_LEN, (s + 1) * MEM_LEN)
        for h in range(X_HEADS):
            cols = slice(h * X_HEAD_DIM, (h + 1) * X_HEAD_DIM)
            k_ref[s, _head_rows_of(h), :] = k[rows, cols]
            v_ref[s, _head_rows_of(h), :] = v[rows, cols]


def _memory_kv(mem, g_mem, w_k, g_k, w_v, n):
    tm = MEMKV_SEQ_BLOCK * MEM_LEN
    per_layer = lambda a, b: pl.BlockSpec((None, a, b), lambda l, i: (l, 0, 0))
    out_spec = pl.BlockSpec((None, MEMKV_SEQ_BLOCK, MEM_ROWS, X_HEAD_DIM), lambda l, i: (l, i, 0, 0))
    out_sds = jax.ShapeDtypeStruct((DEPTH, n, MEM_ROWS, X_HEAD_DIM), F32)
    return pl.pallas_call(
        _memory_kv_kernel,
        grid=(DEPTH, n // MEMKV_SEQ_BLOCK),
        in_specs=[pl.BlockSpec((tm, D_MODEL), lambda l, i: (i, 0)), per_layer(1, D_MODEL), per_layer(D_MODEL, X_W),
                  per_layer(1, X_HEAD_DIM), per_layer(D_MODEL, X_W)],
        out_specs=[out_spec, out_spec],
        out_shape=[out_sds, out_sds],
        compiler_params=_cparams("parallel", "parallel"),
        name="memory_kv",
    )(mem, g_mem, w_k, g_k, w_v)


def _xattn_rows(x, g_ref, wq_ref, gq_ref, mk_ref, mv_ref, wo_ref):
    q = _head_norm(_dot(_rms_rows(x, g_ref[...]).astype(BF), wq_ref[...]), gq_ref[...]).astype(BF)
    outs = []
    for h in range(X_HEADS):
        cols = slice(h * X_HEAD_DIM, (h + 1) * X_HEAD_DIM)
        s = _dot_nt(q[:, cols], mk_ref[_head_rows_of(h), :].astype(BF)) * (X_HEAD_DIM ** -0.5)
        m = jnp.max(s, axis=-1, keepdims=True)
        p = jnp.exp(s - m)
        p = (p / jnp.sum(p, axis=-1, keepdims=True)).astype(BF)
        outs.append(_dot(p, mv_ref[_head_rows_of(h), :].astype(BF)))
    o = jnp.concatenate(outs, axis=1).astype(BF)
    return x + _dot(o, wo_ref[...])


def _even_out_xattn_kernel(x_ref, a_ref, s_ref, wa_ref, ws_ref, *rest):
    xattn_refs, o_ref = rest[:-1], rest[-1]
    x = x_ref[...] + _dot(a_ref[...], wa_ref[...]) + _dot(s_ref[...].astype(BF), ws_ref[...])
    o_ref[...] = _xattn_rows(x, *xattn_refs)


def _gmlp_xattn_kernel(csize, x_ref, u_ref, v_ref, wsp_ref, bsp_ref, wo_ref, *rest):
    xattn_refs, o_ref, gated_scr = rest[:-2], rest[-2], rest[-1]
    x = _gmlp_rows(csize, x_ref, u_ref, v_ref, wsp_ref, bsp_ref, wo_ref, gated_scr)
    o_ref[...] = _xattn_rows(x, *xattn_refs)


def _xattn_specs(li):
    mem = pl.BlockSpec((None, None, MEM_ROWS, X_HEAD_DIM), lambda i, j: (li, i, 0, 0))
    return [_layer_spec(li, 1, D_MODEL), _layer_spec(li, D_MODEL, X_W), _layer_spec(li, 1, X_HEAD_DIM), mem, mem,
            _layer_spec(li, X_W, D_MODEL)]


def _even_out_xattn(x, att, ssm, e, wa, ws, li, xattn_args, n, seq, tq):
    nq = seq // tq
    row = lambda w_: pl.BlockSpec((tq, w_), lambda i, j: (i * nq + j, 0))
    ssm_spec = pl.BlockSpec((tq, SSM_W), lambda i, j: (j, i))
    return pl.pallas_call(
        _even_out_xattn_kernel,
        grid=(n, nq),
        in_specs=[row(D_MODEL), row(A_W), ssm_spec, _layer_spec(e, A_W, D_MODEL), _layer_spec(e, SSM_W, D_MODEL)]
        + _xattn_specs(li),
        out_specs=row(D_MODEL),
        out_shape=jax.ShapeDtypeStruct((n * seq, D_MODEL), F32),
        compiler_params=_cparams("parallel", "arbitrary"),
        name="even_out_xattn",
    )(x, att, ssm, wa, ws, *xattn_args)


def _gmlp_xattn(x, u, v, e, wsp, bsp, wo, csize, li, xattn_args, n, seq, tq):
    nq = seq // tq
    row = lambda w_: pl.BlockSpec((tq, w_), lambda i, j: (i * nq + j, 0))
    return pl.pallas_call(
        functools.partial(_gmlp_xattn_kernel, csize),
        grid=(n, nq),
        in_specs=[row(D_MODEL), row(GM_W), row(GM_W), _layer_spec(e, GM_HEADS, CHUNK, CHUNK),
                  _layer_spec(e, CHUNK, GM_HEADS), _layer_spec(e, GM_W, D_MODEL)] + _xattn_specs(li),
        out_specs=row(D_MODEL),
        out_shape=jax.ShapeDtypeStruct((n * seq, D_MODEL), F32),
        scratch_shapes=[pltpu.VMEM((tq, GM_W), BF)],
        compiler_params=_cparams("parallel", "arbitrary"),
        name="gmlp_xattn",
    )(x, u, v, wsp, bsp, wo, *xattn_args)


XATTN_SEQ_BLOCK = 8


def _xattn_sample_kernel(t_new, x_ref, g_ref, wq_ref, gq_ref, mk_ref, mv_ref, wo_ref, o_ref):
    pair_rows = 2 * t_new
    x = x_ref[...]
    q = _head_norm(_dot(_rms_rows(x, g_ref[...]).astype(BF), wq_ref[...]), gq_ref[...]).astype(BF)
    tq = lax.broadcasted_iota(jnp.int32, (pair_rows, 2 * MEM_LEN), 0)
    cm = lax.broadcasted_iota(jnp.int32, (pair_rows, 2 * MEM_LEN), 1)
    same_seq = (tq // t_new) == (cm // MEM_LEN)
    npair = XATTN_SEQ_BLOCK // 2
    scores = []
    for j in range(npair):
        qp = q[j * pair_rows:(j + 1) * pair_rows]
        for h in range(X_HEADS):
            hr = _head_rows_of(h)
            kp = jnp.concatenate([mk_ref[2 * j, hr, :], mk_ref[2 * j + 1, hr, :]], axis=0).astype(BF)
            s = _dot_nt(qp[:, h * X_HEAD_DIM:(h + 1) * X_HEAD_DIM], kp)
            scores.append(jnp.where(same_seq, s, NEG_INF))
    s = jnp.concatenate(scores, axis=0) * (X_HEAD_DIM ** -0.5)
    m = jnp.max(s, axis=-1, keepdims=True)
    p = jnp.exp(s - m)
    p = p / jnp.sum(p, axis=-1, keepdims=True)
    outs = []
    for j in range(npair):
        heads = []
        for h in range(X_HEADS):
            hr = _head_rows_of(h)
            vp = jnp.concatenate([mv_ref[2 * j, hr, :], mv_ref[2 * j + 1, hr, :]], axis=0).astype(BF)
            r0 = (j * X_HEADS + h) * pair_rows
            heads.append(_dot(p[r0:r0 + pair_rows].astype(BF), vp))
        outs.append(jnp.concatenate(heads, axis=1))
    o_all = jnp.concatenate(outs, axis=0).astype(BF)
    o_ref[...] = x + _dot(o_all, wo_ref[...])


def _xattn_sample(x, li, g, wq, gq, mk, mv, wo, n, t_new):
    rows = XATTN_SEQ_BLOCK * t_new
    row = pl.BlockSpec((rows, D_MODEL), lambda i: (i, 0))
    mem = pl.BlockSpec((None, XATTN_SEQ_BLOCK, MEM_ROWS, X_HEAD_DIM), lambda i: (li, i, 0, 0))
    return pl.pallas_call(
        functools.partial(_xattn_sample_kernel, t_new),
        grid=(n // XATTN_SEQ_BLOCK,),
        in_specs=[row, _layer_spec(li, 1, D_MODEL), _layer_spec(li, D_MODEL, X_W), _layer_spec(li, 1, X_HEAD_DIM),
                  mem, mem, _layer_spec(li, X_W, D_MODEL)],
        out_specs=row,
        out_shape=jax.ShapeDtypeStruct((n * t_new, D_MODEL), F32),
        compiler_params=_cparams("parallel"),
        name="xattn_sample",
    )(x, g, wq, gq, mk, mv, wo)


ROUTER_ROWS = 32
ROUTER_COARSE_ROW = N_EXPERTS
BIG = 3.0e38


def _first_argmax(vals, vmax):
    idx = jnp.full(vmax.shape, len(vals) - 1, jnp.int32)
    for j in range(len(vals) - 2, -1, -1):
        idx = jnp.where(vals[j] == vmax, j, idx)
    return idx


def _router_logits(h, whi, wlo, bias):
    h_hi = h.astype(BF)
    h_lo = (h - h_hi.astype(F32)).astype(BF)
    return _dot_nt(whi, h_hi) + _dot_nt(whi, h_lo) + _dot_nt(wlo, h_hi) + bias


def _coarse_rows(lg):
    return [lg[ROUTER_COARSE_ROW + g:ROUTER_COARSE_ROW + g + 1, :] for g in range(N_GROUPS)]


def _top_group(lg):
    coarse = _coarse_rows(lg)
    return _first_argmax(coarse, functools.reduce(jnp.maximum, coarse))


def _in_group_gates(lg, grp):
    pick = lambda rows: functools.reduce(lambda f, g: jnp.where(grp == g, rows[g], f),
                                         range(N_GROUPS - 2, -1, -1), rows[N_GROUPS - 1])
    coarse = _coarse_rows(lg)
    csel = pick(coarse)
    p_grp = 1.0 / functools.reduce(lambda a, b: a + b, [jnp.exp(c - csel) for c in coarse])
    fine = [pick([lg[g * EXPERTS_PER_GROUP + j:g * EXPERTS_PER_GROUP + j + 1, :] for g in range(N_GROUPS)])
            for j in range(EXPERTS_PER_GROUP)]
    v1 = functools.reduce(jnp.maximum, fine)
    i1 = _first_argmax(fine, v1)
    rest = [jnp.where(i1 == j, -BIG, fine[j]) for j in range(EXPERTS_PER_GROUP)]
    v2 = functools.reduce(jnp.maximum, rest)
    i2 = _first_argmax(rest, v2)
    e2 = jnp.exp(v2 - v1)
    w1 = p_grp / (1.0 + e2)
    w2 = p_grp * e2 / (1.0 + e2)
    return [jnp.where(i1 == j, w1, 0.0) + jnp.where(i2 == j, w2, 0.0) for j in range(EXPERTS_PER_GROUP)]


def _rows_to_lanes(rows, tm):
    pad = jnp.zeros((LANES - len(rows), tm), F32)
    return jnp.concatenate(list(rows) + [pad], axis=0).T


def _router_gates(h, whi, wlo, bias):
    lg = _router_logits(h, whi, wlo, bias)
    grp = _top_group(lg)
    in_group = _in_group_gates(lg, grp)
    rows = [jnp.where(grp == e // EXPERTS_PER_GROUP, in_group[e % EXPERTS_PER_GROUP], 0.0) for e in range(N_EXPERTS)]
    return _rows_to_lanes(rows, lg.shape[1])


def _moe_kernel(x_ref, g_ref, whi_ref, wlo_ref, b_ref, w1_ref, w3_ref, w2_ref, o_ref, h_scr, gates_scr, acc_scr):
    gi = pl.program_id(1)

    @pl.when(gi == 0)
    def _():
        h = _rms_rows(x_ref[...], g_ref[...])
        h_scr[...] = h.astype(BF)
        gates_scr[...] = _router_gates(h, whi_ref[...], wlo_ref[...], b_ref[...])
        acc_scr[...] = jnp.zeros_like(acc_scr)

    h = h_scr[...]
    gates = gates_scr[...]
    lane = lax.broadcasted_iota(jnp.int32, gates.shape, 1)
    for j in range(EXPERTS_PER_GROUP):
        gcol = jnp.sum(jnp.where(lane == gi * EXPERTS_PER_GROUP + j, gates, 0.0), axis=-1, keepdims=True)
        a = _dot(h, w1_ref[j])
        b = _dot(h, w3_ref[j])
        hid = (a * jax.nn.sigmoid(a)) * b * gcol
        acc_scr[...] += _dot(hid.astype(BF), w2_ref[j])

    @pl.when(gi == pl.num_programs(1) - 1)
    def _():
        o_ref[...] = x_ref[...] + acc_scr[...]


def _moe(x, li, g, whi, wlo, b, w1, w3, w2, tm):
    m = x.shape[0]
    row = lambda w_: pl.BlockSpec((tm, w_), lambda i, e: (i, 0))
    group_w = lambda a, b_: pl.BlockSpec((None, EXPERTS_PER_GROUP, a, b_), lambda i, e: (li, e, 0, 0))
    return pl.pallas_call(
        _moe_kernel,
        grid=(m // tm, N_GROUPS),
        in_specs=[row(D_MODEL), _layer_spec(li, 1, D_MODEL), _layer_spec(li, ROUTER_ROWS, D_MODEL),
                  _layer_spec(li, ROUTER_ROWS, D_MODEL), _layer_spec(li, ROUTER_ROWS, 1),
                  group_w(D_MODEL, D_EXPERT), group_w(D_MODEL, D_EXPERT), group_w(D_EXPERT, D_MODEL)],
        out_specs=row(D_MODEL),
        out_shape=jax.ShapeDtypeStruct((m, D_MODEL), F32),
        scratch_shapes=[pltpu.VMEM((tm, D_MODEL), BF), pltpu.VMEM((tm, LANES), F32), pltpu.VMEM((tm, D_MODEL), F32)],
        compiler_params=_cparams("parallel", "arbitrary"),
        name="moe",
    )(x, g, whi, wlo, b, w1, w3, w2)


MOE_TILE = 256
GROUP_ROW = 0


def _route_kernel(x_ref, g_ref, whi_ref, wlo_ref, b_ref, grp_ref):
    h = _rms_rows(x_ref[...], g_ref[...])
    grp = _top_group(_router_logits(h, whi_ref[...], wlo_ref[...], b_ref[...]))
    grp_ref[...] = jnp.concatenate([grp, jnp.zeros((7, grp.shape[1]), jnp.int32)], axis=0)


def _route(x, m, li, g, whi, wlo, b, tm):
    out = pl.pallas_call(
        _route_kernel,
        grid=(m // tm,),
        in_specs=[pl.BlockSpec((tm, D_MODEL), lambda i: (i, 0)), _layer_spec(li, 1, D_MODEL),
                  _layer_spec(li, ROUTER_ROWS, D_MODEL), _layer_spec(li, ROUTER_ROWS, D_MODEL),
                  _layer_spec(li, ROUTER_ROWS, 1)],
        out_specs=pl.BlockSpec((None, 8, tm), lambda i: (i, 0, 0)),
        out_shape=jax.ShapeDtypeStruct((m // tm, 8, tm), jnp.int32),
        compiler_params=_cparams("parallel"),
        name="moe_route",
    )(x, g, whi, wlo, b)
    return out[:, GROUP_ROW, :].reshape(m)


def _sorted_plan(grp, m):
    t = MOE_TILE
    n_tiles = m // t + N_GROUPS
    order = jnp.argsort(grp, stable=True).astype(jnp.int32)
    counts = jnp.sum((grp[None, :] == jnp.arange(N_GROUPS, dtype=jnp.int32)[:, None]).astype(jnp.int32), axis=1)
    tiles_per_group = (counts + t - 1) // t
    tile_end = jnp.cumsum(tiles_per_group)
    tile_start = tile_end - tiles_per_group
    first_token = jnp.cumsum(counts) - counts
    s = jnp.arange(n_tiles, dtype=jnp.int32)
    tile_grp = jnp.minimum(jnp.sum((s[:, None] >= tile_end[None, :]).astype(jnp.int32), axis=1), N_GROUPS - 1)
    n_active = tile_end[N_GROUPS - 1]
    tile_in_group = s - tile_start[tile_grp]
    n_valid = jnp.where(s < n_active, jnp.clip(counts[tile_grp] - tile_in_group * t, 0, t), 0)
    r = jnp.arange(t, dtype=jnp.int32)
    slot_token = order[jnp.clip(first_token[tile_grp][:, None] + tile_in_group[:, None] * t + r[None, :], 0, m - 1)]
    valid = r[None, :] < n_valid[:, None]
    src = jnp.where(valid, slot_token, 0).reshape(-1)
    dst = jnp.where(valid, slot_token, m + (s % 2)[:, None] * t + r[None, :]).reshape(-1)
    src = jnp.concatenate([src, jnp.zeros((t,), jnp.int32)])
    dst = jnp.concatenate([m + t + r, dst])
    return src.astype(jnp.int32), dst.astype(jnp.int32), tile_grp.astype(jnp.int32)


def _sorted_moe_kernel(m, src_ref, dst_ref, tgrp_ref, x_hbm, g_ref, whi_ref, wlo_ref, b_ref,
                       w1_ref, w3_ref, w2_ref, out_hbm, xbuf0, xbuf1, obuf0, obuf1, sem_g, sem_s):
    t = MOE_TILE
    i = pl.program_id(0)
    last = pl.num_programs(0) - 1
    xbuf = (xbuf0, xbuf1)
    obuf = (obuf0, obuf1)

    def gather_row(tile, slot, r):
        return pltpu.make_async_copy(x_hbm.at[pl.ds(src_ref[tile * t + r], 1), :], xbuf[slot].at[pl.ds(r, 1), :],
                                     sem_g.at[slot])

    def scatter_row(tile, slot, r):
        return pltpu.make_async_copy(obuf[slot].at[pl.ds(r, 1), :],
                                     out_hbm.at[pl.ds(dst_ref[(tile + 1) * t + r], 1), :], sem_s.at[slot])

    def whole_gather(slot):
        return pltpu.make_async_copy(x_hbm.at[pl.ds(0, t), :], xbuf[slot], sem_g.at[slot])

    def whole_scatter(slot, row0):
        return pltpu.make_async_copy(obuf[slot], out_hbm.at[pl.ds(row0, t), :], sem_s.at[slot])

    @pl.when(i == 0)
    def _():
        for slot in range(2):
            obuf[slot][...] = jnp.zeros((t, D_MODEL), F32)
        for slot in range(2):
            whole_scatter(slot, m + slot * t).start()
        for slot in range(2):
            whole_scatter(slot, m + slot * t).wait()

        def issue(r, carry):
            gather_row(0, 0, r).start()
            return carry
        lax.fori_loop(0, t, issue, 0)

    def step(cur, nxt):
        whole_gather(cur).wait()
        for r in range(t):
            gather_row(i + 1, nxt, r).start()
        for r in range(t):
            scatter_row(i - 1, nxt, r).start()

        x = xbuf[cur][...]
        h = _rms_rows(x, g_ref[...])
        lg = _router_logits(h, whi_ref[...], wlo_ref[...], b_ref[...])
        gates = _rows_to_lanes(_in_group_gates(lg, tgrp_ref[i]), t)
        hb = h.astype(BF)
        acc = jnp.zeros((t, D_MODEL), F32)
        for j in range(EXPERTS_PER_GROUP):
            a = _dot(hb, w1_ref[j])
            b = _dot(hb, w3_ref[j])
            hid = (a * jax.nn.sigmoid(a)) * b * gates[:, j:j + 1]
            acc = acc + _dot(hid.astype(BF), w2_ref[j])
        res = x + acc

        @pl.when(i >= 1)
        def _():
            whole_scatter(cur, 0).wait()

        obuf[cur][...] = res

        @pl.when(i == last)
        def _():
            def issue(r, carry):
                scatter_row(i, cur, r).start()
                return carry
            lax.fori_loop(0, t, issue, 0)
            whole_scatter(cur, 0).wait()
            whole_scatter(nxt, 0).wait()
            whole_gather(nxt).wait()

    for parity in range(2):
        @pl.when(i % 2 == parity)
        def _():
            step(parity, 1 - parity)


def _sorted_moe(x, m, li, g, whi, wlo, b, w1, w3, w2):
    t = MOE_TILE
    n_tiles = m // t + N_GROUPS
    src, dst, tile_grp = _sorted_plan(_route(x, m, li, g, whi, wlo, b, 512), m)
    group_w = lambda a, b_: pl.BlockSpec((None, EXPERTS_PER_GROUP, a, b_), lambda i, src_, dst_, tg: (li, tg[i], 0, 0))
    return pl.pallas_call(
        functools.partial(_sorted_moe_kernel, m),
        grid_spec=pltpu.PrefetchScalarGridSpec(
            num_scalar_prefetch=3,
            grid=(n_tiles,),
            in_specs=[pl.BlockSpec(memory_space=pl.ANY), _layer_spec(li, 1, D_MODEL),
                      _layer_spec(li, ROUTER_ROWS, D_MODEL), _layer_spec(li, ROUTER_ROWS, D_MODEL),
                      _layer_spec(li, ROUTER_ROWS, 1),
                      group_w(D_MODEL, D_EXPERT), group_w(D_MODEL, D_EXPERT), group_w(D_EXPERT, D_MODEL)],
            out_specs=pl.BlockSpec(memory_space=pl.ANY),
            scratch_shapes=[pltpu.VMEM((t, D_MODEL), F32)] * 4
            + [pltpu.SemaphoreType.DMA((2,)), pltpu.SemaphoreType.DMA((2,))]),
        out_shape=jax.ShapeDtypeStruct((m + 2 * t, D_MODEL), F32),
        compiler_params=_cparams("arbitrary"),
        name="moe_sorted",
    )(src, dst, tile_grp, x, g, whi, wlo, b, w1, w3, w2)


def _heads_kv_major_to_g_major(w, axis):
    shape = w.shape
    split = shape[:axis] + (KV_HEADS, Q_PER_KV, HEAD_DIM) + shape[axis + 1:]
    return jnp.swapaxes(w.reshape(split), axis, axis + 1).reshape(shape)


def _prep_ssm(a_re, a_im, log_dt, b_re, b_im, c_re, c_im, d_skip, w_glu):
    ar, ai = a_re.astype(F32), a_im.astype(F32)
    dt = jnp.exp(log_dt.astype(F32))[..., None]
    mag = jnp.exp(ar * dt)
    abr, abi = mag * jnp.cos(ai * dt), mag * jnp.sin(ai * dt)
    den = ar * ar + ai * ai
    kr = ((abr - 1.0) * ar + abi * ai) / den
    ki = (abi * ar - (abr - 1.0) * ai) / den
    br, bi = b_re.astype(F32), b_im.astype(F32)
    bbr = kr[..., None] * br - ki[..., None] * bi
    bbi = kr[..., None] * bi + ki[..., None] * br
    n_layers = ar.shape[0]
    eye = jnp.eye(SSM_GROUPS, dtype=F32)
    blockdiag_in = lambda b: jnp.einsum('egph,gk->eghkp', b, eye).reshape(n_layers, SSM_W, SSM_FLAT)
    blockdiag_out = lambda c: jnp.einsum('eghp,gk->egpkh', c, eye).reshape(n_layers, SSM_FLAT, SSM_W)
    in_tiles = lambda b: jnp.stack([b[:, j * SSM_IN_TILE:(j + 1) * SSM_IN_TILE, j * SSM_ST_TILE:(j + 1) * SSM_ST_TILE]
                                    for j in range(SSM_IN_TILES)], axis=1)
    out_tiles = lambda c: jnp.stack([c[:, j * SSM_OUT_ST:(j + 1) * SSM_OUT_ST, j * SSM_OUT_TILE:(j + 1) * SSM_OUT_TILE]
                                     for j in range(SSM_OUT_TILES)], axis=1)
    bcat = jnp.concatenate([in_tiles(blockdiag_in(bbr)), in_tiles(blockdiag_in(bbi))], axis=3).astype(BF)
    ccat = jnp.concatenate([out_tiles(blockdiag_out(c_re.astype(F32))), -out_tiles(blockdiag_out(c_im.astype(F32)))],
                           axis=2).astype(BF)
    return (abr.reshape(n_layers, 1, SSM_FLAT), abi.reshape(n_layers, 1, SSM_FLAT), bcat, ccat,
            d_skip.astype(F32).reshape(n_layers, 1, SSM_W), w_glu.astype(BF))


def _run_trunk(x, n, seq, mem_k, mem_v, caches, p):
    prompt = caches is None
    tm = 512
    wk_out, wv_out, hr_out, hi_out, cv_out = [], [], [], [], []
    xattn_args = (p['g_xattn'], p['w_xq'], p['g_xq'], mem_k, mem_v, p['w_xo'])
    for li in range(DEPTH):
        e = li // 2
        if li % 2 == 0:
            q, k, v, u = _even_in_proj(x, li, e, p['g_mix'], p['w_in_even'], p['g_q'], p['g_k'], p['bd'], tm, n, seq)
            if prompt:
                att = _swa_prompt(q, k, v, p['sinks'], e, n, seq)
                k3, v3 = k.reshape(n, seq, KV_W), v.reshape(n, seq, KV_W)
                new_k, new_v = k3[:, seq - WINDOW:], v3[:, seq - WINDOW:]
                ssm, h_re, h_im = _s5_glu(u.reshape(seq, n, SSM_W), None, p['ssm'], e, n, seq, 128)
                ssm = ssm.reshape(seq, n * SSM_W)
            else:
                win_k, win_v, ssm_re, ssm_im = caches
                att, new_k, new_v = _swa_sample(q, k, v, win_k, win_v, p['sinks'], e, n, seq)
                u_tm = jnp.swapaxes(u.reshape(n, seq, SSM_W), 0, 1)
                ssm, h_re, h_im = _s5_glu(u_tm, (ssm_re, ssm_im), p['ssm'], e, n, seq, seq)
                ssm = jnp.swapaxes(ssm, 0, 1).reshape(n * seq, SSM_W)
            if prompt:
                x = _even_out_xattn(x, att, ssm, e, p['w_out_att'], p['w_out_ssm'], li, xattn_args, n, seq, tm)
            else:
                x = _even_out_proj(x, att, ssm, e, p['w_out_att'], p['w_out_ssm'], tm, n, seq)
            wk_out.append(new_k.reshape(n, WINDOW, KV_HEADS, HEAD_DIM))
            wv_out.append(new_v.reshape(n, WINDOW, KV_HEADS, HEAD_DIM))
            hr_out.append(h_re.reshape(n, SSM_GROUPS, SSM_STATE))
            hi_out.append(h_im.reshape(n, SSM_GROUPS, SSM_STATE))
        else:
            u, v = _odd_in_proj(x, n * seq, li, e, p['g_mix'], p['w_in_odd'], p['g_v'], tm, BF if prompt else F32)
            wsp, bsp, csize = p['gmlp_prompt'] if prompt else p['gmlp_sample']
            if prompt:
                x = _gmlp_xattn(x, u, v, e, wsp, bsp, p['w_out_odd'], csize, li, xattn_args, n, seq, tm)
            else:
                x = _gmlp_mix(x, u, v, e, wsp, bsp, p['w_out_odd'], csize, tm)
                cv_out.append(v.reshape(n, seq, GM_W))
        if not prompt:
            x = _xattn_sample(x, li, *xattn_args, n, seq)
        moe_args = (li, p['g_ffn'], p['w_router_hi'], p['w_router_lo'], p['b_router'], p['w_e1'], p['w_e3'], p['w_e2'])
        if prompt:
            x = _sorted_moe(x, n * seq, *moe_args)
        else:
            x = _moe(x, *moe_args, tm)
    x = x[:n * seq]
    cv = None if prompt else jnp.stack(cv_out)
    return x, jnp.stack(wk_out), jnp.stack(wv_out), jnp.stack(hr_out), jnp.stack(hi_out), cv


def kernel(x_prompt, x_sample, cache_win_k, cache_win_v, state_ssm_re, state_ssm_im, cache_mem_k, cache_mem_v, mem_prompt, g_mix, g_xattn, g_ffn, g_mem, w_in_even, g_q, g_k, sinks, ssm_a_re, ssm_a_im, ssm_log_dt, ssm_b_re, ssm_b_im, ssm_c_re, ssm_c_im, ssm_d, w_glu, w_out_even, w_in_odd, g_v, w_spatial, b_spatial, w_out_odd, w_xq, g_xq, w_xk, g_xk, w_xv, w_xo, w_coarse, b_coarse, w_fine, b_fine, w_e1, w_e3, w_e2):
    batch, seq = x_prompt.shape[0], x_prompt.shape[1]
    dec_batch, dec_seq = x_sample.shape[0], x_sample.shape[1]
    n_even = w_in_even.shape[0]

    row1 = lambda a: a.astype(F32)[:, None, :]
    p = {}
    p['g_mix'], p['g_xattn'], p['g_ffn'] = row1(g_mix), row1(g_xattn), row1(g_ffn)
    w_in_q = _heads_kv_major_to_g_major(w_in_even[:, :, :A_W], 2)
    p['w_in_even'] = jnp.concatenate([w_in_q, w_in_even[:, :, A_W:]], axis=2).astype(BF)
    p['g_q'] = jnp.tile(g_q.astype(F32), (1, A_HEADS))[:, None, :]
    p['g_k'] = jnp.tile(g_k.astype(F32), (1, KV_HEADS))[:, None, :]
    head_id = jnp.arange(A_W) // HEAD_DIM
    p['bd'] = (head_id[:, None] == head_id[None, :]).astype(BF)
    p['sinks'] = sinks.astype(F32).reshape(-1)
    p['ssm'] = _prep_ssm(ssm_a_re, ssm_a_im, ssm_log_dt, ssm_b_re, ssm_b_im, ssm_c_re, ssm_c_im, ssm_d, w_glu)
    p['w_out_att'] = _heads_kv_major_to_g_major(w_out_even[:, :A_W, :], 1).astype(BF)
    p['w_out_ssm'] = w_out_even[:, A_W:, :].astype(BF)
    p['w_in_odd'] = w_in_odd.astype(BF)
    p['g_v'] = row1(g_v)
    p['w_out_odd'] = w_out_odd.astype(BF)
    cs = min(dec_seq, CHUNK)
    reps = CHUNK // cs
    p['gmlp_prompt'] = (w_spatial.astype(F32), jnp.swapaxes(b_spatial.astype(F32), 1, 2), CHUNK)
    p['gmlp_sample'] = (jnp.tile(w_spatial[:, :, :cs, :cs].astype(F32), (1, 1, reps, reps)),
                        jnp.tile(jnp.swapaxes(b_spatial[:, :, :cs].astype(F32), 1, 2), (1, reps, 1)), cs)
    p['w_xq'] = w_xq.astype(BF)
    p['g_xq'] = row1(g_xq)
    p['w_xo'] = w_xo.astype(BF)
    w_router = jnp.swapaxes(jnp.concatenate([w_fine, w_coarse], axis=2).astype(F32), 1, 2)
    w_router = jnp.pad(w_router, ((0, 0), (0, ROUTER_ROWS - w_router.shape[1]), (0, 0)))
    p['w_router_hi'] = w_router.astype(BF)
    p['w_router_lo'] = (w_router - p['w_router_hi'].astype(F32)).astype(BF)
    b_router = jnp.concatenate([b_fine, b_coarse], axis=1).astype(F32)
    p['b_router'] = jnp.pad(b_router, ((0, 0), (0, ROUTER_ROWS - b_router.shape[1])))[:, :, None]
    p['w_e1'], p['w_e3'], p['w_e2'] = w_e1.astype(BF), w_e3.astype(BF), w_e2.astype(BF)

    mem2d = mem_prompt.reshape(batch * MEM_LEN, D_MODEL)
    pk, pv = _memory_kv(mem2d, row1(g_mem), w_xk.astype(BF), row1(g_xk), w_xv.astype(BF), batch)
    y_p, p_wk, p_wv, p_hr, p_hi, _ = _run_trunk(x_prompt.reshape(batch * seq, D_MODEL), batch, seq, pk, pv, None, p)

    caches = (cache_win_k.reshape(n_even, dec_batch, WINDOW, KV_W), cache_win_v.reshape(n_even, dec_batch, WINDOW, KV_W),
              state_ssm_re.reshape(n_even, dec_batch, SSM_FLAT), state_ssm_im.reshape(n_even, dec_batch, SSM_FLAT))
    y_s, s_wk, s_wv, s_hr, s_hi, s_cv = _run_trunk(x_sample.reshape(dec_batch * dec_seq, D_MODEL), dec_batch, dec_seq,
                                                   cache_mem_k.reshape(DEPTH, dec_batch, MEM_ROWS, X_HEAD_DIM),
                                                   cache_mem_v.reshape(DEPTH, dec_batch, MEM_ROWS, X_HEAD_DIM), caches, p)

    mem_shape = (DEPTH, batch, MEM_LEN, X_HEADS, X_HEAD_DIM)
    return (y_p.reshape(batch, seq, D_MODEL), y_s.reshape(dec_batch, dec_seq, D_MODEL), p_wk, p_wv, p_hr, p_hi,
            pk.reshape(mem_shape), pv.reshape(mem_shape), s_wk, s_wv, s_hr, s_hi, s_cv)
```

```python
import functools
import math

import jax
import jax.numpy as jnp
from jax import lax
from jax.experimental import pallas as pl
from jax.experimental.pallas import tpu as pltpu

F32 = jnp.float32
BF = jnp.bfloat16

D_MODEL = 1024
DEPTH = 4
A_W = 512
HEAD_DIM = 64
A_HEADS = 8
KV_HEADS = 2
Q_PER_KV = 4
KV_W = 128
WINDOW = 128
SSM_W = 512
SSM_CH = 16
SSM_GROUPS = 32
SSM_STATE = 64
SSM_FLAT = SSM_GROUPS * SSM_STATE
IN_EVEN = A_W + 2 * KV_W + SSM_W
CHUNK = 128
GM_W = 2048
GM_HEADS = 8
GM_HD = 256
MEM_LEN = 256
X_HEADS = 4
X_HEAD_DIM = 128
X_W = 512
N_GROUPS = 4
EXPERTS_PER_GROUP = 4
N_EXPERTS = 16
D_EXPERT = 256
EPS = 1e-6
NEG_INF = -1e30
LANES = 128
VMEM_LIMIT_BYTES = 52 * 1024 * 1024
ALIBI_SLOPES = tuple(2.0 ** (-8.0 * (h + 1) / A_HEADS) for h in range(A_HEADS))
SSM_SEQ_BLOCK = 8


def _cparams(*sem):
    return pltpu.CompilerParams(dimension_semantics=sem, vmem_limit_bytes=VMEM_LIMIT_BYTES)


def _rms_rows(x, g):
    return x * lax.rsqrt(jnp.mean(x * x, axis=-1, keepdims=True) + EPS) * g


TT_CHUNKS = D_MODEL // LANES


def _read_tokens(ref, token_tiled):
    if not token_tiled:
        return ref[...]
    rows = ref.shape[0] // TT_CHUNKS
    return jnp.concatenate([ref[pl.ds(c, rows, stride=TT_CHUNKS), :] for c in range(TT_CHUNKS)], axis=1)


def _write_token_tiles(ref, val):
    rows = val.shape[0]
    for c in range(TT_CHUNKS):
        ref[pl.ds(c, rows, stride=TT_CHUNKS), :] = val[:, c * LANES:(c + 1) * LANES]


def _token_spec(rows, index, token_tiled):
    if token_tiled:
        return pl.BlockSpec((rows * TT_CHUNKS, LANES), lambda *ids: (index(*ids), 0))
    return pl.BlockSpec((rows, D_MODEL), lambda *ids: (index(*ids), 0))


def _dot(a, b):
    return jnp.dot(a, b, preferred_element_type=F32)


def _dot_nt(a, b):
    return lax.dot_general(a, b, (((1,), (1,)), ((), ())), preferred_element_type=F32)


def _full(shape):
    nd = len(shape)
    return pl.BlockSpec(shape, lambda *_: (0,) * nd)


def _layer_spec(li, *shape, single_buffer=False):
    nd = len(shape)
    mode = pl.Buffered(1) if single_buffer else None
    return pl.BlockSpec((None,) + shape, lambda *_: (li,) + (0,) * nd, pipeline_mode=mode)


def _even_in_kernel(token_tiled, x_ref, g_ref, w_ref, gq_ref, gk_ref, bd_ref, q_ref, k_ref, v_ref, u_ref):
    h = _rms_rows(_read_tokens(x_ref, token_tiled), g_ref[...]).astype(BF)
    z = _dot(h, w_ref[...])
    q = z[:, :A_W]
    k = z[:, A_W:A_W + KV_W]
    bd = bd_ref[...]
    qms = _dot((q * q).astype(BF), bd) * (1.0 / HEAD_DIM)
    kms = _dot((k * k).astype(BF), bd[:KV_W, :KV_W]) * (1.0 / HEAD_DIM)
    q_ref[...] = q * lax.rsqrt(qms + EPS) * gq_ref[...]
    k_ref[...] = k * lax.rsqrt(kms + EPS) * gk_ref[...]
    v_ref[...] = z[:, A_W + KV_W:A_W + 2 * KV_W]
    u_ref[...] = z[:, A_W + 2 * KV_W:]


def _time_major_spec(tm, n, seq, width):
    per_seq = seq // tm
    return pl.BlockSpec((tm, width), lambda i: (i % per_seq, i // per_seq))


def _even_in_proj(x, li, e, g, w, gq, gk, bd, tm, n, seq, token_tiled=False):
    m = n * seq
    row = lambda w_: pl.BlockSpec((tm, w_), lambda i: (i, 0))
    return pl.pallas_call(
        functools.partial(_even_in_kernel, token_tiled),
        grid=(m // tm,),
        in_specs=[_token_spec(tm, lambda i: i, token_tiled), _layer_spec(li, 1, D_MODEL),
                  _layer_spec(e, D_MODEL, IN_EVEN), _layer_spec(e, 1, A_W),
                  _layer_spec(e, 1, KV_W), _full((A_W, A_W))],
        out_specs=[row(A_W), row(KV_W), row(KV_W), _time_major_spec(tm, n, seq, SSM_W) if seq % tm == 0 else row(SSM_W)],
        out_shape=[jax.ShapeDtypeStruct((m, A_W), F32), jax.ShapeDtypeStruct((m, KV_W), F32),
                   jax.ShapeDtypeStruct((m, KV_W), F32),
                   jax.ShapeDtypeStruct((seq, n * SSM_W) if seq % tm == 0 else (m, SSM_W), F32)],
        compiler_params=_cparams("parallel"),
        name="even_in_proj",
    )(x, g, w, gq, gk, bd)


def _head_rows(q, lane_lo):
    rows = []
    for kv in range(KV_HEADS):
        keep = lane_lo if kv == 0 else jnp.logical_not(lane_lo)
        for g in range(Q_PER_KV):
            rows.append(jnp.where(keep, q[:, g * LANES:(g + 1) * LANES], 0.0))
    return jnp.concatenate(rows, axis=0)


def _head_cols(o, r, lane_lo):
    cols = []
    for g in range(Q_PER_KV):
        cols.append(jnp.where(lane_lo, o[g * r:(g + 1) * r], o[(Q_PER_KV + g) * r:(Q_PER_KV + g + 1) * r]))
    return jnp.concatenate(cols, axis=1)


def _swa_prompt_kernel(e, sinks_ref, q_ref, kc_ref, kp_ref, vc_ref, vp_ref, o_ref):
    b = pl.program_id(1)
    lane_lo = lax.broadcasted_iota(jnp.int32, (1, LANES), 1) < HEAD_DIM
    qrows = _head_rows(q_ref[...], lane_lo).astype(BF)
    kcat = jnp.concatenate([kp_ref[...], kc_ref[...]], axis=0).astype(BF)
    vcat = jnp.concatenate([vp_ref[...], vc_ref[...]], axis=0).astype(BF)
    s = _dot_nt(qrows, kcat) * (HEAD_DIM ** -0.5)
    r = lax.broadcasted_iota(jnp.int32, (WINDOW, 2 * WINDOW), 0)
    c = lax.broadcasted_iota(jnp.int32, (WINDOW, 2 * WINDOW), 1)
    dist = WINDOW + r - c
    valid = (dist >= 0) & (dist <= WINDOW) & ((c >= WINDOW) | (b > 0))
    dist_f = dist.astype(F32)
    ps = []
    for h in range(A_HEADS):
        sh = s[h * WINDOW:(h + 1) * WINDOW] - ALIBI_SLOPES[h] * dist_f
        sh = jnp.where(valid, sh, NEG_INF)
        sk = sinks_ref[e * A_HEADS + h]
        m = jnp.maximum(jnp.max(sh, axis=-1, keepdims=True), sk)
        p = jnp.exp(sh - m)
        den = jnp.sum(p, axis=-1, keepdims=True) + jnp.exp(sk - m)
        ps.append((p / den).astype(BF))
    o = _dot(jnp.concatenate(ps, axis=0), vcat)
    o_ref[...] = _head_cols(o, WINDOW, lane_lo).astype(o_ref.dtype)


def _swa_prompt(q, k, v, sinks, e, n, seq):
    nb = seq // WINDOW
    cur = lambda w_: pl.BlockSpec((WINDOW, w_), lambda i, b: (i * nb + b, 0))
    prev = lambda w_: pl.BlockSpec((WINDOW, w_), lambda i, b: (i * nb + jnp.maximum(b - 1, 0), 0))
    return pl.pallas_call(
        functools.partial(_swa_prompt_kernel, e),
        grid=(n, nb),
        in_specs=[pl.BlockSpec(memory_space=pltpu.SMEM), cur(A_W), cur(KV_W), prev(KV_W), cur(KV_W), prev(KV_W)],
        out_specs=cur(A_W),
        out_shape=jax.ShapeDtypeStruct((n * seq, A_W), BF),
        compiler_params=_cparams("parallel", "arbitrary"),
        name="swa_prompt",
    )(sinks, q, k, k, v, v)


SWA_SEQ_BLOCK = 8


def _swa_sample_kernel(t_new, e, sinks_ref, q_ref, kn_ref, vn_ref, wk_ref, wv_ref, o_ref, nwk_ref, nwv_ref):
    nseq = SWA_SEQ_BLOCK
    pair_rows = 2 * t_new
    npair = nseq // 2
    kn = kn_ref[...]
    vn = vn_ref[...]
    for i in range(nseq):
        nwk_ref[i, 0:WINDOW - t_new, :] = wk_ref[i, t_new:WINDOW, :]
        nwk_ref[i, WINDOW - t_new:WINDOW, :] = kn[i * t_new:(i + 1) * t_new, :]
        nwv_ref[i, 0:WINDOW - t_new, :] = wv_ref[i, t_new:WINDOW, :]
        nwv_ref[i, WINDOW - t_new:WINDOW, :] = vn[i * t_new:(i + 1) * t_new, :]

    lane_lo = lax.broadcasted_iota(jnp.int32, (1, LANES), 1) < HEAD_DIM
    q = q_ref[...]
    knb = kn.astype(BF)
    vnb = vn.astype(BF)
    hr = A_HEADS * pair_rows
    qrows = [_head_rows(q[j * pair_rows:(j + 1) * pair_rows], lane_lo).astype(BF) for j in range(npair)]
    scale = HEAD_DIM ** -0.5
    s_new_all = _dot_nt(jnp.concatenate(qrows, axis=0), knb) * scale

    tq = lax.broadcasted_iota(jnp.int32, (pair_rows, 2 * WINDOW), 0)
    cw = lax.broadcasted_iota(jnp.int32, (pair_rows, 2 * WINDOW), 1)
    dist_w = WINDOW + (tq % t_new) - (cw % WINDOW)
    valid_w = ((tq // t_new) == (cw // WINDOW)) & (dist_w <= WINDOW)
    dist_wf = dist_w.astype(F32)
    tqn = lax.broadcasted_iota(jnp.int32, (pair_rows, nseq * t_new), 0)
    cn = lax.broadcasted_iota(jnp.int32, (pair_rows, nseq * t_new), 1)
    dist_n = (tqn % t_new) - (cn % t_new)
    dist_nf = dist_n.astype(F32)

    p_new_all = []
    o_win_all = []
    for j in range(npair):
        kwin = jnp.concatenate([wk_ref[2 * j], wk_ref[2 * j + 1]], axis=0).astype(BF)
        vwin = jnp.concatenate([wv_ref[2 * j], wv_ref[2 * j + 1]], axis=0).astype(BF)
        s_win = _dot_nt(qrows[j], kwin) * scale
        valid_n = ((2 * j + tqn // t_new) == (cn // t_new)) & (dist_n >= 0)
        p_win = []
        for h in range(A_HEADS):
            sw = jnp.where(valid_w, s_win[h * pair_rows:(h + 1) * pair_rows] - ALIBI_SLOPES[h] * dist_wf, NEG_INF)
            sn = s_new_all[j * hr + h * pair_rows:j * hr + (h + 1) * pair_rows]
            sn = jnp.where(valid_n, sn - ALIBI_SLOPES[h] * dist_nf, NEG_INF)
            sk = sinks_ref[e * A_HEADS + h]
            m = jnp.maximum(jnp.maximum(jnp.max(sw, axis=-1, keepdims=True), jnp.max(sn, axis=-1, keepdims=True)), sk)
            pw = jnp.exp(sw - m)
            pn = jnp.exp(sn - m)
            den = jnp.sum(pw, axis=-1, keepdims=True) + jnp.sum(pn, axis=-1, keepdims=True) + jnp.exp(sk - m)
            p_win.append((pw / den).astype(BF))
            p_new_all.append((pn / den).astype(BF))
        o_win_all.append(_dot(jnp.concatenate(p_win, axis=0), vwin))
    o_new = _dot(jnp.concatenate(p_new_all, axis=0), vnb)
    outs = [_head_cols(o_win_all[j] + o_new[j * hr:(j + 1) * hr], pair_rows, lane_lo) for j in range(npair)]
    o_ref[...] = jnp.concatenate(outs, axis=0).astype(o_ref.dtype)


def _swa_sample(q, kn, vn, win_k, win_v, sinks, e, n, t_new):
    rows = SWA_SEQ_BLOCK * t_new
    row = lambda w_: pl.BlockSpec((rows, w_), lambda i: (i, 0))
    win = pl.BlockSpec((SWA_SEQ_BLOCK, WINDOW, KV_W), lambda i: (i, 0, 0))
    win_in = pl.BlockSpec((None, SWA_SEQ_BLOCK, WINDOW, KV_W), lambda i: (e, i, 0, 0))
    return pl.pallas_call(
        functools.partial(_swa_sample_kernel, t_new, e),
        grid=(n // SWA_SEQ_BLOCK,),
        in_specs=[pl.BlockSpec(memory_space=pltpu.SMEM), row(A_W), row(KV_W), row(KV_W), win_in, win_in],
        out_specs=[row(A_W), win, win],
        out_shape=[jax.ShapeDtypeStruct((n * t_new, A_W), BF),
                   jax.ShapeDtypeStruct((n, WINDOW, KV_W), F32), jax.ShapeDtypeStruct((n, WINDOW, KV_W), F32)],
        compiler_params=_cparams("parallel"),
        name="swa_sample",
    )(sinks, q, kn, vn, win_k, win_v)


SSM_LANE_CHUNK = 1024
SSM_IN_TILE = LANES
SSM_IN_TILES = SSM_W // SSM_IN_TILE
SSM_ST_TILE = SSM_IN_TILE // SSM_CH * SSM_STATE
SSM_OUT_TILE = 256
SSM_OUT_TILES = SSM_W // SSM_OUT_TILE
SSM_OUT_ST = SSM_OUT_TILE // SSM_CH * SSM_STATE


def _s5_kernel(tc, has_h0, *refs):
    if has_h0:
        (u_ref, h0r_ref, h0i_ref, abr_ref, abi_ref, bcat_ref, ccat_ref, d_ref, wglu_ref,
         o_ref, hr_ref, hi_ref, st_scr, car_scr) = refs
    else:
        (u_ref, abr_ref, abi_ref, bcat_ref, ccat_ref, d_ref, wglu_ref,
         o_ref, hr_ref, hi_ref, st_scr, car_scr) = refs
    nb = SSM_SEQ_BLOCK
    ci = pl.program_id(1)

    @pl.when(ci == 0)
    def _():
        if has_h0:
            car_scr[:, :SSM_FLAT] = h0r_ref[...]
            car_scr[:, SSM_FLAT:] = h0i_ref[...]
        else:
            car_scr[...] = jnp.zeros_like(car_scr)

    ut = u_ref[...].reshape(tc * nb, SSM_W)
    ub = ut.astype(BF)
    for jt in range(SSM_IN_TILES):
        bu = _dot(ub[:, jt * SSM_IN_TILE:(jt + 1) * SSM_IN_TILE], bcat_ref[jt])
        st_scr[:, jt * SSM_ST_TILE:(jt + 1) * SSM_ST_TILE] = bu[:, :SSM_ST_TILE]
        st_scr[:, SSM_FLAT + jt * SSM_ST_TILE:SSM_FLAT + (jt + 1) * SSM_ST_TILE] = bu[:, SSM_ST_TILE:]

    for lc in range(SSM_FLAT // SSM_LANE_CHUNK):
        lo = lc * SSM_LANE_CHUNK
        re_sl = slice(lo, lo + SSM_LANE_CHUNK)
        im_sl = slice(SSM_FLAT + lo, SSM_FLAT + lo + SSM_LANE_CHUNK)
        ar = jnp.broadcast_to(abr_ref[:, re_sl], (nb, SSM_LANE_CHUNK))
        ai = jnp.broadcast_to(abi_ref[:, re_sl], (nb, SSM_LANE_CHUNK))

        def step(t, carry):
            hr, hi = carry
            rows = pl.ds(pl.multiple_of(t * nb, nb), nb)
            nr = ar * hr - ai * hi + st_scr[rows, re_sl]
            ni = ar * hi + ai * hr + st_scr[rows, im_sl]
            st_scr[rows, re_sl] = nr
            st_scr[rows, im_sl] = ni
            return nr, ni

        hr, hi = lax.fori_loop(0, tc, step, (car_scr[:, re_sl], car_scr[:, im_sl]))
        car_scr[:, re_sl] = hr
        car_scr[:, im_sl] = hi

    ys = []
    for ot in range(SSM_OUT_TILES):
        re_sl = slice(ot * SSM_OUT_ST, (ot + 1) * SSM_OUT_ST)
        im_sl = slice(SSM_FLAT + ot * SSM_OUT_ST, SSM_FLAT + (ot + 1) * SSM_OUT_ST)
        ys.append(_dot(st_scr[:, re_sl].astype(BF), ccat_ref[ot, :SSM_OUT_ST, :])
                  + _dot(st_scr[:, im_sl].astype(BF), ccat_ref[ot, SSM_OUT_ST:, :]))
    y = jnp.concatenate(ys, axis=1) + d_ref[...] * ut
    y = jax.nn.gelu(y).astype(BF)
    g = _dot(y, wglu_ref[...])
    o_ref[...] = (g[:, :SSM_W] * jax.nn.sigmoid(g[:, SSM_W:])).reshape(tc, nb, SSM_W)

    @pl.when(ci == pl.num_programs(1) - 1)
    def _():
        hr_ref[...] = car_scr[:, :SSM_FLAT]
        hi_ref[...] = car_scr[:, SSM_FLAT:]


def _s5_glu(u, h0, ssm, e, n, seq, tc):
    nb = SSM_SEQ_BLOCK
    abr, abi, bcat, ccat, dsk, wglu = ssm
    u_spec = pl.BlockSpec((tc, nb, SSM_W), lambda i, c: (c, i, 0))
    st_spec = pl.BlockSpec((nb, SSM_FLAT), lambda i, c: (i, 0))
    h0_spec = pl.BlockSpec((None, nb, SSM_FLAT), lambda i, c: (e, i, 0))
    consts = [_layer_spec(e, 1, SSM_FLAT), _layer_spec(e, 1, SSM_FLAT),
              _layer_spec(e, SSM_IN_TILES, SSM_IN_TILE, 2 * SSM_ST_TILE),
              _layer_spec(e, SSM_OUT_TILES, 2 * SSM_OUT_ST, SSM_OUT_TILE), _layer_spec(e, 1, SSM_W),
              _layer_spec(e, SSM_W, 2 * SSM_W)]
    has_h0 = h0 is not None
    in_specs = [u_spec] + ([h0_spec, h0_spec] if has_h0 else []) + consts
    args = [u] + (list(h0) if has_h0 else []) + [abr, abi, bcat, ccat, dsk, wglu]
    return pl.pallas_call(
        functools.partial(_s5_kernel, tc, has_h0),
        grid=(n // nb, seq // tc),
        in_specs=in_specs,
        out_specs=[u_spec, st_spec, st_spec],
        out_shape=[jax.ShapeDtypeStruct((seq, n, SSM_W), F32), jax.ShapeDtypeStruct((n, SSM_FLAT), F32),
                   jax.ShapeDtypeStruct((n, SSM_FLAT), F32)],
        scratch_shapes=[pltpu.VMEM((nb * tc, 2 * SSM_FLAT), F32), pltpu.VMEM((nb, 2 * SSM_FLAT), F32)],
        compiler_params=_cparams("parallel", "arbitrary"),
        name="s5_glu",
    )(*args)


def _even_out_kernel(x_ref, a_ref, s_ref, wa_ref, ws_ref, o_ref):
    o_ref[...] = x_ref[...] + _dot(a_ref[...], wa_ref[...]) + _dot(s_ref[...].astype(BF), ws_ref[...])


def _even_out_proj(x, att, ssm, e, wa, ws, tm, n, seq):
    m = x.shape[0]
    row = lambda w_: pl.BlockSpec((tm, w_), lambda i: (i, 0))
    ssm_spec = _time_major_spec(tm, n, seq, SSM_W) if seq % tm == 0 else row(SSM_W)
    return pl.pallas_call(
        _even_out_kernel,
        grid=(m // tm,),
        in_specs=[row(D_MODEL), row(A_W), ssm_spec, _layer_spec(e, A_W, D_MODEL), _layer_spec(e, SSM_W, D_MODEL)],
        out_specs=row(D_MODEL),
        out_shape=jax.ShapeDtypeStruct((m, D_MODEL), F32),
        compiler_params=_cparams("parallel"),
        name="even_out_proj",
    )(x, att, ssm, wa, ws)


def _odd_in_kernel(token_tiled, x_ref, g_ref, w_ref, gv_ref, u_ref, v_ref):
    h = _rms_rows(_read_tokens(x_ref, token_tiled), g_ref[...]).astype(BF)
    u_ref[...] = jax.nn.gelu(_dot(h, w_ref[:, :GM_W])).astype(u_ref.dtype)
    zv = jax.nn.gelu(_dot(h, w_ref[:, GM_W:]))
    v_ref[...] = _rms_rows(zv, gv_ref[...]).astype(v_ref.dtype)


def _odd_in_proj(x, m, li, e, g, w, gv, tm, v_dtype, token_tiled=False):
    row = lambda w_: pl.BlockSpec((tm, w_), lambda i: (i, 0))
    return pl.pallas_call(
        functools.partial(_odd_in_kernel, token_tiled),
        grid=(m // tm,),
        in_specs=[_token_spec(tm, lambda i: i, token_tiled), _layer_spec(li, 1, D_MODEL),
                  _layer_spec(e, D_MODEL, 2 * GM_W, single_buffer=True),
                  _layer_spec(e, 1, GM_W)],
        out_specs=[row(GM_W), row(GM_W)],
        out_shape=[jax.ShapeDtypeStruct((m, GM_W), BF), jax.ShapeDtypeStruct((m, GM_W), v_dtype)],
        compiler_params=_cparams("parallel"),
        name="odd_in_proj",
    )(x, g, w, gv)


def _gmlp_rows(csize, x, u_ref, v_ref, wsp_ref, bsp_ref, wo_ref, gated_scr):
    tm = u_ref.shape[0]
    i = lax.broadcasted_iota(jnp.int32, (CHUNK, CHUNK), 0)
    j = lax.broadcasted_iota(jnp.int32, (CHUNK, CHUNK), 1)
    keep = (j <= i) & ((i // csize) == (j // csize))
    bsp = bsp_ref[...]
    for h in range(GM_HEADS):
        ws = jnp.where(keep, wsp_ref[h], 0.0).astype(BF)
        b_col = bsp[:, h:h + 1]
        cols = slice(h * GM_HD, (h + 1) * GM_HD)
        for c in range(tm // CHUNK):
            rows = slice(c * CHUNK, (c + 1) * CHUNK)
            mix = _dot(ws, v_ref[rows, cols].astype(BF)) + b_col
            gated_scr[rows, cols] = (u_ref[rows, cols].astype(F32) * mix).astype(BF)
    return x + _dot(gated_scr[...], wo_ref[...])


def _gmlp_mix_kernel(csize, x_ref, u_ref, v_ref, wsp_ref, bsp_ref, wo_ref, o_ref, gated_scr):
    o_ref[...] = _gmlp_rows(csize, x_ref[...], u_ref, v_ref, wsp_ref, bsp_ref, wo_ref, gated_scr)


def _gmlp_mix(x, u, v, e, wsp, bsp, wo, csize, tm):
    m = x.shape[0]
    row = lambda w_: pl.BlockSpec((tm, w_), lambda i: (i, 0))
    return pl.pallas_call(
        functools.partial(_gmlp_mix_kernel, csize),
        grid=(m // tm,),
        in_specs=[row(D_MODEL), row(GM_W), row(GM_W), _layer_spec(e, GM_HEADS, CHUNK, CHUNK),
                  _layer_spec(e, CHUNK, GM_HEADS), _layer_spec(e, GM_W, D_MODEL)],
        out_specs=row(D_MODEL),
        out_shape=jax.ShapeDtypeStruct((m, D_MODEL), F32),
        scratch_shapes=[pltpu.VMEM((tm, GM_W), BF)],
        compiler_params=_cparams("parallel"),
        name="gmlp_mix",
    )(x, u, v, wsp, bsp, wo)


def _head_norm(z, g):
    cols = []
    for h in range(X_HEADS):
        zh = z[:, h * X_HEAD_DIM:(h + 1) * X_HEAD_DIM]
        cols.append(_rms_rows(zh, g))
    return jnp.concatenate(cols, axis=1)


MEMKV_SEQ_BLOCK = 2
MEM_ROWS = MEM_LEN * X_HEADS


def _head_rows_of(h):
    return pl.ds(h, MEM_LEN, stride=X_HEADS)


def _memory_kv_kernel(mem_ref, g_ref, wk_ref, gk_ref, wv_ref, k_ref, v_ref):
    m = _rms_rows(mem_ref[...], g_ref[...]).astype(BF)
    k = _head_norm(_dot(m, wk_ref[...]), gk_ref[...])
    v = _dot(m, wv_ref[...])
    for s in range(MEMKV_SEQ_BLOCK):
        rows = slice(s * MEM_LEN, (s + 1) * MEM_LEN)
        for h in range(X_HEADS):
            cols = slice(h * X_HEAD_DIM, (h + 1) * X_HEAD_DIM)
            k_ref[s, _head_rows_of(h), :] = k[rows, cols]
            v_ref[s, _head_rows_of(h), :] = v[rows, cols]


def _memory_kv(mem, g_mem, w_k, g_k, w_v, n):
    tm = MEMKV_SEQ_BLOCK * MEM_LEN
    per_layer = lambda a, b: pl.BlockSpec((None, a, b), lambda l, i: (l, 0, 0))
    out_spec = pl.BlockSpec((None, MEMKV_SEQ_BLOCK, MEM_ROWS, X_HEAD_DIM), lambda l, i: (l, i, 0, 0))
    out_sds = jax.ShapeDtypeStruct((DEPTH, n, MEM_ROWS, X_HEAD_DIM), F32)
    return pl.pallas_call(
        _memory_kv_kernel,
        grid=(DEPTH, n // MEMKV_SEQ_BLOCK),
        in_specs=[pl.BlockSpec((tm, D_MODEL), lambda l, i: (i, 0)), per_layer(1, D_MODEL), per_layer(D_MODEL, X_W),
                  per_layer(1, X_HEAD_DIM), per_layer(D_MODEL, X_W)],
        out_specs=[out_spec, out_spec],
        out_shape=[out_sds, out_sds],
        compiler_params=_cparams("parallel", "parallel"),
        name="memory_kv",
    )(mem, g_mem, w_k, g_k, w_v)


def _xattn_rows(x, g_ref, wq_ref, gq_ref, mk_ref, mv_ref, wo_ref):
    q = _head_norm(_dot(_rms_rows(x, g_ref[...]).astype(BF), wq_ref[...]), gq_ref[...]).astype(BF)
    outs = []
    for h in range(X_HEADS):
        cols = slice(h * X_HEAD_DIM, (h + 1) * X_HEAD_DIM)
        s = _dot_nt(q[:, cols], mk_ref[_head_rows_of(h), :].astype(BF)) * (X_HEAD_DIM ** -0.5)
        m = jnp.max(s, axis=-1, keepdims=True)
        p = jnp.exp(s - m)
        p = (p / jnp.sum(p, axis=-1, keepdims=True)).astype(BF)
        outs.append(_dot(p, mv_ref[_head_rows_of(h), :].astype(BF)))
    o = jnp.concatenate(outs, axis=1).astype(BF)
    return x + _dot(o, wo_ref[...])


N_XATTN_REFS = 6
N_ROUTE_REFS = 4
GROUP_ROW = 0


def _xattn_route_store(x, refs, o_ref, grp_ref):
    y = _xattn_rows(x, *refs[:N_XATTN_REFS])
    _write_token_tiles(o_ref, y)
    gf_ref, whi_ref, wlo_ref, b_ref = refs[N_XATTN_REFS:]
    grp = _top_group(_router_logits(_rms_rows(y, gf_ref[...]), whi_ref[...], wlo_ref[...], b_ref[...]))
    grp_ref[...] = jnp.concatenate([grp, jnp.zeros((7, grp.shape[1]), jnp.int32)], axis=0)


def _even_out_xattn_kernel(token_tiled, x_ref, a_ref, s_ref, wa_ref, ws_ref, *rest):
    refs, o_ref, grp_ref = rest[:-2], rest[-2], rest[-1]
    x = (_read_tokens(x_ref, token_tiled) + _dot(a_ref[...], wa_ref[...])
         + _dot(s_ref[...].astype(BF), ws_ref[...]))
    _xattn_route_store(x, refs, o_ref, grp_ref)


def _gmlp_xattn_kernel(csize, token_tiled, x_ref, u_ref, v_ref, wsp_ref, bsp_ref, wo_ref, *rest):
    refs, o_ref, grp_ref, gated_scr = rest[:-3], rest[-3], rest[-2], rest[-1]
    x = _gmlp_rows(csize, _read_tokens(x_ref, token_tiled), u_ref, v_ref, wsp_ref, bsp_ref, wo_ref, gated_scr)
    _xattn_route_store(x, refs, o_ref, grp_ref)


def _xattn_route_specs(li):
    mem = pl.BlockSpec((None, None, MEM_ROWS, X_HEAD_DIM), lambda i, j: (li, i, 0, 0))
    return [_layer_spec(li, 1, D_MODEL), _layer_spec(li, D_MODEL, X_W), _layer_spec(li, 1, X_HEAD_DIM), mem, mem,
            _layer_spec(li, X_W, D_MODEL), _layer_spec(li, 1, D_MODEL), _layer_spec(li, ROUTER_ROWS, D_MODEL),
            _layer_spec(li, ROUTER_ROWS, D_MODEL), _layer_spec(li, ROUTER_ROWS, 1)]


def _token_tile_outputs(n, seq, tq):
    nq = seq // tq
    specs = [_token_spec(tq, lambda i, j: i * nq + j, True), pl.BlockSpec((None, 8, tq), lambda i, j: (i * nq + j, 0, 0))]
    shapes = [jax.ShapeDtypeStruct((n * seq * TT_CHUNKS, LANES), F32), jax.ShapeDtypeStruct((n * nq, 8, tq), jnp.int32)]
    return specs, shapes


def _even_out_xattn(x, token_tiled, att, ssm, e, wa, ws, li, xattn_route_args, n, seq, tq):
    nq = seq // tq
    row = lambda w_: pl.BlockSpec((tq, w_), lambda i, j: (i * nq + j, 0))
    ssm_spec = pl.BlockSpec((tq, SSM_W), lambda i, j: (j, i))
    out_specs, out_shape = _token_tile_outputs(n, seq, tq)
    y, grp = pl.pallas_call(
        functools.partial(_even_out_xattn_kernel, token_tiled),
        grid=(n, nq),
        in_specs=[_token_spec(tq, lambda i, j: i * nq + j, token_tiled), row(A_W), ssm_spec,
                  _layer_spec(e, A_W, D_MODEL), _layer_spec(e, SSM_W, D_MODEL)] + _xattn_route_specs(li),
        out_specs=out_specs,
        out_shape=out_shape,
        compiler_params=_cparams("parallel", "arbitrary"),
        name="even_out_xattn",
    )(x, att, ssm, wa, ws, *xattn_route_args)
    return y, grp[:, GROUP_ROW, :].reshape(n * seq)


def _gmlp_xattn(x, token_tiled, u, v, e, wsp, bsp, wo, csize, li, xattn_route_args, n, seq, tq):
    nq = seq // tq
    row = lambda w_: pl.BlockSpec((tq, w_), lambda i, j: (i * nq + j, 0))
    out_specs, out_shape = _token_tile_outputs(n, seq, tq)
    y, grp = pl.pallas_call(
        functools.partial(_gmlp_xattn_kernel, csize, token_tiled),
        grid=(n, nq),
        in_specs=[_token_spec(tq, lambda i, j: i * nq + j, token_tiled), row(GM_W), row(GM_W),
                  _layer_spec(e, GM_HEADS, CHUNK, CHUNK), _layer_spec(e, CHUNK, GM_HEADS),
                  _layer_spec(e, GM_W, D_MODEL)] + _xattn_route_specs(li),
        out_specs=out_specs,
        out_shape=out_shape,
        scratch_shapes=[pltpu.VMEM((tq, GM_W), BF)],
        compiler_params=_cparams("parallel", "arbitrary"),
        name="gmlp_xattn",
    )(x, u, v, wsp, bsp, wo, *xattn_route_args)
    return y, grp[:, GROUP_ROW, :].reshape(n * seq)


XATTN_SEQ_BLOCK = 8


def _xattn_sample_kernel(t_new, x_ref, g_ref, wq_ref, gq_ref, mk_ref, mv_ref, wo_ref, o_ref):
    pair_rows = 2 * t_new
    x = x_ref[...]
    q = _head_norm(_dot(_rms_rows(x, g_ref[...]).astype(BF), wq_ref[...]), gq_ref[...]).astype(BF)
    tq = lax.broadcasted_iota(jnp.int32, (pair_rows, 2 * MEM_LEN), 0)
    cm = lax.broadcasted_iota(jnp.int32, (pair_rows, 2 * MEM_LEN), 1)
    same_seq = (tq // t_new) == (cm // MEM_LEN)
    npair = XATTN_SEQ_BLOCK // 2
    scores = []
    for j in range(npair):
        qp = q[j * pair_rows:(j + 1) * pair_rows]
        for h in range(X_HEADS):
            hr = _head_rows_of(h)
            kp = jnp.concatenate([mk_ref[2 * j, hr, :], mk_ref[2 * j + 1, hr, :]], axis=0).astype(BF)
            s = _dot_nt(qp[:, h * X_HEAD_DIM:(h + 1) * X_HEAD_DIM], kp)
            scores.append(jnp.where(same_seq, s, NEG_INF))
    s = jnp.concatenate(scores, axis=0) * (X_HEAD_DIM ** -0.5)
    m = jnp.max(s, axis=-1, keepdims=True)
    p = jnp.exp(s - m)
    p = p / jnp.sum(p, axis=-1, keepdims=True)
    outs = []
    for j in range(npair):
        heads = []
        for h in range(X_HEADS):
            hr = _head_rows_of(h)
            vp = jnp.concatenate([mv_ref[2 * j, hr, :], mv_ref[2 * j + 1, hr, :]], axis=0).astype(BF)
            r0 = (j * X_HEADS + h) * pair_rows
            heads.append(_dot(p[r0:r0 + pair_rows].astype(BF), vp))
        outs.append(jnp.concatenate(heads, axis=1))
    o_all = jnp.concatenate(outs, axis=0).astype(BF)
    o_ref[...] = x + _dot(o_all, wo_ref[...])


def _xattn_sample(x, li, g, wq, gq, mk, mv, wo, n, t_new):
    rows = XATTN_SEQ_BLOCK * t_new
    row = pl.BlockSpec((rows, D_MODEL), lambda i: (i, 0))
    mem = pl.BlockSpec((None, XATTN_SEQ_BLOCK, MEM_ROWS, X_HEAD_DIM), lambda i: (li, i, 0, 0))
    return pl.pallas_call(
        functools.partial(_xattn_sample_kernel, t_new),
        grid=(n // XATTN_SEQ_BLOCK,),
        in_specs=[row, _layer_spec(li, 1, D_MODEL), _layer_spec(li, D_MODEL, X_W), _layer_spec(li, 1, X_HEAD_DIM),
                  mem, mem, _layer_spec(li, X_W, D_MODEL)],
        out_specs=row,
        out_shape=jax.ShapeDtypeStruct((n * t_new, D_MODEL), F32),
        compiler_params=_cparams("parallel"),
        name="xattn_sample",
    )(x, g, wq, gq, mk, mv, wo)


ROUTER_ROWS = 32
ROUTER_COARSE_ROW = N_EXPERTS
BIG = 3.0e38


def _first_argmax(vals, vmax):
    idx = jnp.full(vmax.shape, len(vals) - 1, jnp.int32)
    for j in range(len(vals) - 2, -1, -1):
        idx = jnp.where(vals[j] == vmax, j, idx)
    return idx


def _router_logits(h, whi, wlo, bias):
    h_hi = h.astype(BF)
    h_lo = (h - h_hi.astype(F32)).astype(BF)
    return _dot_nt(whi, h_hi) + _dot_nt(whi, h_lo) + _dot_nt(wlo, h_hi) + bias


def _coarse_rows(lg):
    return [lg[ROUTER_COARSE_ROW + g:ROUTER_COARSE_ROW + g + 1, :] for g in range(N_GROUPS)]


def _top_group(lg):
    coarse = _coarse_rows(lg)
    return _first_argmax(coarse, functools.reduce(jnp.maximum, coarse))


def _in_group_gates(lg, grp):
    pick = lambda rows: functools.reduce(lambda f, g: jnp.where(grp == g, rows[g], f),
                                         range(N_GROUPS - 2, -1, -1), rows[N_GROUPS - 1])
    coarse = _coarse_rows(lg)
    csel = pick(coarse)
    p_grp = 1.0 / functools.reduce(lambda a, b: a + b, [jnp.exp(c - csel) for c in coarse])
    fine = [pick([lg[g * EXPERTS_PER_GROUP + j:g * EXPERTS_PER_GROUP + j + 1, :] for g in range(N_GROUPS)])
            for j in range(EXPERTS_PER_GROUP)]
    v1 = functools.reduce(jnp.maximum, fine)
    i1 = _first_argmax(fine, v1)
    rest = [jnp.where(i1 == j, -BIG, fine[j]) for j in range(EXPERTS_PER_GROUP)]
    v2 = functools.reduce(jnp.maximum, rest)
    i2 = _first_argmax(rest, v2)
    e2 = jnp.exp(v2 - v1)
    w1 = p_grp / (1.0 + e2)
    w2 = p_grp * e2 / (1.0 + e2)
    return [jnp.where(i1 == j, w1, 0.0) + jnp.where(i2 == j, w2, 0.0) for j in range(EXPERTS_PER_GROUP)]


def _rows_to_lanes(rows, tm):
    pad = jnp.zeros((LANES - len(rows), tm), F32)
    return jnp.concatenate(list(rows) + [pad], axis=0).T


def _router_gates(h, whi, wlo, bias):
    lg = _router_logits(h, whi, wlo, bias)
    grp = _top_group(lg)
    in_group = _in_group_gates(lg, grp)
    rows = [jnp.where(grp == e // EXPERTS_PER_GROUP, in_group[e % EXPERTS_PER_GROUP], 0.0) for e in range(N_EXPERTS)]
    return _rows_to_lanes(rows, lg.shape[1])


def _moe_kernel(x_ref, g_ref, whi_ref, wlo_ref, b_ref, w1_ref, w3_ref, w2_ref, o_ref, h_scr, gates_scr, acc_scr):
    gi = pl.program_id(1)

    @pl.when(gi == 0)
    def _():
        h = _rms_rows(x_ref[...], g_ref[...])
        h_scr[...] = h.astype(BF)
        gates_scr[...] = _router_gates(h, whi_ref[...], wlo_ref[...], b_ref[...])
        acc_scr[...] = jnp.zeros_like(acc_scr)

    h = h_scr[...]
    gates = gates_scr[...]
    lane = lax.broadcasted_iota(jnp.int32, gates.shape, 1)
    for j in range(EXPERTS_PER_GROUP):
        gcol = jnp.sum(jnp.where(lane == gi * EXPERTS_PER_GROUP + j, gates, 0.0), axis=-1, keepdims=True)
        a = _dot(h, w1_ref[j])
        b = _dot(h, w3_ref[j])
        hid = (a * jax.nn.sigmoid(a)) * b * gcol
        acc_scr[...] += _dot(hid.astype(BF), w2_ref[j])

    @pl.when(gi == pl.num_programs(1) - 1)
    def _():
        o_ref[...] = x_ref[...] + acc_scr[...]


def _moe(x, li, g, whi, wlo, b, w1, w3, w2, tm):
    m = x.shape[0]
    row = lambda w_: pl.BlockSpec((tm, w_), lambda i, e: (i, 0))
    group_w = lambda a, b_: pl.BlockSpec((None, EXPERTS_PER_GROUP, a, b_), lambda i, e: (li, e, 0, 0))
    return pl.pallas_call(
        _moe_kernel,
        grid=(m // tm, N_GROUPS),
        in_specs=[row(D_MODEL), _layer_spec(li, 1, D_MODEL), _layer_spec(li, ROUTER_ROWS, D_MODEL),
                  _layer_spec(li, ROUTER_ROWS, D_MODEL), _layer_spec(li, ROUTER_ROWS, 1),
                  group_w(D_MODEL, D_EXPERT), group_w(D_MODEL, D_EXPERT), group_w(D_EXPERT, D_MODEL)],
        out_specs=row(D_MODEL),
        out_shape=jax.ShapeDtypeStruct((m, D_MODEL), F32),
        scratch_shapes=[pltpu.VMEM((tm, D_MODEL), BF), pltpu.VMEM((tm, LANES), F32), pltpu.VMEM((tm, D_MODEL), F32)],
        compiler_params=_cparams("parallel", "arbitrary"),
        name="moe",
    )(x, g, whi, wlo, b, w1, w3, w2)


MOE_TILE = 256


def _sorted_plan(grp, m):
    t = MOE_TILE
    n_tiles = m // t + N_GROUPS
    order = jnp.argsort(grp, stable=True).astype(jnp.int32)
    counts = jnp.sum((grp[None, :] == jnp.arange(N_GROUPS, dtype=jnp.int32)[:, None]).astype(jnp.int32), axis=1)
    tiles_per_group = (counts + t - 1) // t
    tile_end = jnp.cumsum(tiles_per_group)
    tile_start = tile_end - tiles_per_group
    first_token = jnp.cumsum(counts) - counts
    s = jnp.arange(n_tiles, dtype=jnp.int32)
    tile_grp = jnp.minimum(jnp.sum((s[:, None] >= tile_end[None, :]).astype(jnp.int32), axis=1), N_GROUPS - 1)
    n_active = tile_end[N_GROUPS - 1]
    tile_in_group = s - tile_start[tile_grp]
    n_valid = jnp.where(s < n_active, jnp.clip(counts[tile_grp] - tile_in_group * t, 0, t), 0)
    r = jnp.arange(t, dtype=jnp.int32)
    slot_token = order[jnp.clip(first_token[tile_grp][:, None] + tile_in_group[:, None] * t + r[None, :], 0, m - 1)]
    valid = r[None, :] < n_valid[:, None]
    src = jnp.where(valid, slot_token, 0).reshape(-1)
    dst = jnp.where(valid, slot_token, m + (s % 2)[:, None] * t + r[None, :]).reshape(-1)
    src = jnp.concatenate([src, jnp.zeros((t,), jnp.int32)])
    dst = jnp.concatenate([m + t + r, dst])
    return ((src * TT_CHUNKS).astype(jnp.int32), (dst * TT_CHUNKS).astype(jnp.int32), tile_grp.astype(jnp.int32))


def _sorted_moe_kernel(m, src_ref, dst_ref, tgrp_ref, x_hbm, g_ref, whi_ref, wlo_ref, b_ref,
                       w1_ref, w3_ref, w2_ref, out_hbm, xbuf0, xbuf1, obuf0, obuf1, sem_g, sem_s):
    t = MOE_TILE
    i = pl.program_id(0)
    last = pl.num_programs(0) - 1
    xbuf = (xbuf0, xbuf1)
    obuf = (obuf0, obuf1)

    c8 = TT_CHUNKS

    def gather_row(tile, slot, r):
        src = pl.multiple_of(src_ref[tile * t + r], c8)
        return pltpu.make_async_copy(x_hbm.at[pl.ds(src, c8), :], xbuf[slot].at[pl.ds(r * c8, c8), :], sem_g.at[slot])

    def scatter_row(tile, slot, r):
        dst = pl.multiple_of(dst_ref[(tile + 1) * t + r], c8)
        return pltpu.make_async_copy(obuf[slot].at[pl.ds(r * c8, c8), :], out_hbm.at[pl.ds(dst, c8), :],
                                     sem_s.at[slot])

    def whole_gather(slot):
        return pltpu.make_async_copy(x_hbm.at[pl.ds(0, t * c8), :], xbuf[slot], sem_g.at[slot])

    def whole_scatter(slot, row0):
        return pltpu.make_async_copy(obuf[slot], out_hbm.at[pl.ds(row0, t * c8), :], sem_s.at[slot])

    @pl.when(i == 0)
    def _():
        for slot in range(2):
            obuf[slot][...] = jnp.zeros((t * c8, LANES), F32)
        for slot in range(2):
            whole_scatter(slot, (m + slot * t) * c8).start()
        for slot in range(2):
            whole_scatter(slot, (m + slot * t) * c8).wait()

        def issue(r, carry):
            gather_row(0, 0, r).start()
            return carry
        lax.fori_loop(0, t, issue, 0)

    def step(cur, nxt):
        whole_gather(cur).wait()
        for r in range(t):
            gather_row(i + 1, nxt, r).start()
        for r in range(t):
            scatter_row(i - 1, nxt, r).start()

        x = _read_tokens(xbuf[cur], True)
        h = _rms_rows(x, g_ref[...])
        lg = _router_logits(h, whi_ref[...], wlo_ref[...], b_ref[...])
        gates = _rows_to_lanes(_in_group_gates(lg, tgrp_ref[i]), t)
        hb = h.astype(BF)
        acc = jnp.zeros((t, D_MODEL), F32)
        for j in range(EXPERTS_PER_GROUP):
            a = _dot(hb, w1_ref[j])
            b = _dot(hb, w3_ref[j])
            hid = (a * jax.nn.sigmoid(a)) * b * gates[:, j:j + 1]
            acc = acc + _dot(hid.astype(BF), w2_ref[j])
        res = x + acc

        @pl.when(i >= 1)
        def _():
            whole_scatter(cur, 0).wait()

        _write_token_tiles(obuf[cur], res)

        @pl.when(i == last)
        def _():
            def issue(r, carry):
                scatter_row(i, cur, r).start()
                return carry
            lax.fori_loop(0, t, issue, 0)
            whole_scatter(cur, 0).wait()
            whole_scatter(nxt, 0).wait()
            whole_gather(nxt).wait()

    for parity in range(2):
        @pl.when(i % 2 == parity)
        def _():
            step(parity, 1 - parity)


def _sorted_moe(x, grp, m, li, g, whi, wlo, b, w1, w3, w2):
    t = MOE_TILE
    n_tiles = m // t + N_GROUPS
    src, dst, tile_grp = _sorted_plan(grp, m)
    group_w = lambda a, b_: pl.BlockSpec((None, EXPERTS_PER_GROUP, a, b_), lambda i, src_, dst_, tg: (li, tg[i], 0, 0))
    return pl.pallas_call(
        functools.partial(_sorted_moe_kernel, m),
        grid_spec=pltpu.PrefetchScalarGridSpec(
            num_scalar_prefetch=3,
            grid=(n_tiles,),
            in_specs=[pl.BlockSpec(memory_space=pl.ANY), _layer_spec(li, 1, D_MODEL),
                      _layer_spec(li, ROUTER_ROWS, D_MODEL), _layer_spec(li, ROUTER_ROWS, D_MODEL),
                      _layer_spec(li, ROUTER_ROWS, 1),
                      group_w(D_MODEL, D_EXPERT), group_w(D_MODEL, D_EXPERT), group_w(D_EXPERT, D_MODEL)],
            out_specs=pl.BlockSpec(memory_space=pl.ANY),
            scratch_shapes=[pltpu.VMEM((t * TT_CHUNKS, LANES), F32)] * 4
            + [pltpu.SemaphoreType.DMA((2,)), pltpu.SemaphoreType.DMA((2,))]),
        out_shape=jax.ShapeDtypeStruct(((m + 2 * t) * TT_CHUNKS, LANES), F32),
        compiler_params=_cparams("arbitrary"),
        name="moe_sorted",
    )(src, dst, tile_grp, x, g, whi, wlo, b, w1, w3, w2)


def _heads_kv_major_to_g_major(w, axis):
    shape = w.shape
    split = shape[:axis] + (KV_HEADS, Q_PER_KV, HEAD_DIM) + shape[axis + 1:]
    return jnp.swapaxes(w.reshape(split), axis, axis + 1).reshape(shape)


def _prep_ssm(a_re, a_im, log_dt, b_re, b_im, c_re, c_im, d_skip, w_glu):
    ar, ai = a_re.astype(F32), a_im.astype(F32)
    dt = jnp.exp(log_dt.astype(F32))[..., None]
    mag = jnp.exp(ar * dt)
    abr, abi = mag * jnp.cos(ai * dt), mag * jnp.sin(ai * dt)
    den = ar * ar + ai * ai
    kr = ((abr - 1.0) * ar + abi * ai) / den
    ki = (abi * ar - (abr - 1.0) * ai) / den
    br, bi = b_re.astype(F32), b_im.astype(F32)
    bbr = kr[..., None] * br - ki[..., None] * bi
    bbi = kr[..., None] * bi + ki[..., None] * br
    n_layers = ar.shape[0]
    eye = jnp.eye(SSM_GROUPS, dtype=F32)
    blockdiag_in = lambda b: jnp.einsum('egph,gk->eghkp', b, eye).reshape(n_layers, SSM_W, SSM_FLAT)
    blockdiag_out = lambda c: jnp.einsum('eghp,gk->egpkh', c, eye).reshape(n_layers, SSM_FLAT, SSM_W)
    in_tiles = lambda b: jnp.stack([b[:, j * SSM_IN_TILE:(j + 1) * SSM_IN_TILE, j * SSM_ST_TILE:(j + 1) * SSM_ST_TILE]
                                    for j in range(SSM_IN_TILES)], axis=1)
    out_tiles = lambda c: jnp.stack([c[:, j * SSM_OUT_ST:(j + 1) * SSM_OUT_ST, j * SSM_OUT_TILE:(j + 1) * SSM_OUT_TILE]
                                     for j in range(SSM_OUT_TILES)], axis=1)
    bcat = jnp.concatenate([in_tiles(blockdiag_in(bbr)), in_tiles(blockdiag_in(bbi))], axis=3).astype(BF)
    ccat = jnp.concatenate([out_tiles(blockdiag_out(c_re.astype(F32))), -out_tiles(blockdiag_out(c_im.astype(F32)))],
                           axis=2).astype(BF)
    return (abr.reshape(n_layers, 1, SSM_FLAT), abi.reshape(n_layers, 1, SSM_FLAT), bcat, ccat,
            d_skip.astype(F32).reshape(n_layers, 1, SSM_W), w_glu.astype(BF))


def _run_trunk(x, n, seq, mem_k, mem_v, caches, p):
    prompt = caches is None
    tm = 512
    m = n * seq
    wk_out, wv_out, hr_out, hi_out, cv_out = [], [], [], [], []
    xattn_args = (p['g_xattn'], p['w_xq'], p['g_xq'], mem_k, mem_v, p['w_xo'])
    router_args = (p['g_ffn'], p['w_router_hi'], p['w_router_lo'], p['b_router'])
    expert_args = (p['w_e1'], p['w_e3'], p['w_e2'])
    tiled = False
    for li in range(DEPTH):
        e = li // 2
        if li % 2 == 0:
            q, k, v, u = _even_in_proj(x, li, e, p['g_mix'], p['w_in_even'], p['g_q'], p['g_k'], p['bd'], tm, n, seq,
                                       token_tiled=tiled)
            if prompt:
                att = _swa_prompt(q, k, v, p['sinks'], e, n, seq)
                k3, v3 = k.reshape(n, seq, KV_W), v.reshape(n, seq, KV_W)
                new_k, new_v = k3[:, seq - WINDOW:], v3[:, seq - WINDOW:]
                ssm, h_re, h_im = _s5_glu(u.reshape(seq, n, SSM_W), None, p['ssm'], e, n, seq, 128)
                x, grp = _even_out_xattn(x, tiled, att, ssm.reshape(seq, n * SSM_W), e, p['w_out_att'], p['w_out_ssm'],
                                         li, xattn_args + router_args, n, seq, tm)
            else:
                win_k, win_v, ssm_re, ssm_im = caches
                att, new_k, new_v = _swa_sample(q, k, v, win_k, win_v, p['sinks'], e, n, seq)
                u_tm = jnp.swapaxes(u.reshape(n, seq, SSM_W), 0, 1)
                ssm, h_re, h_im = _s5_glu(u_tm, (ssm_re, ssm_im), p['ssm'], e, n, seq, seq)
                ssm = jnp.swapaxes(ssm, 0, 1).reshape(m, SSM_W)
                x = _even_out_proj(x, att, ssm, e, p['w_out_att'], p['w_out_ssm'], tm, n, seq)
            wk_out.append(new_k.reshape(n, WINDOW, KV_HEADS, HEAD_DIM))
            wv_out.append(new_v.reshape(n, WINDOW, KV_HEADS, HEAD_DIM))
            hr_out.append(h_re.reshape(n, SSM_GROUPS, SSM_STATE))
            hi_out.append(h_im.reshape(n, SSM_GROUPS, SSM_STATE))
        else:
            u, v = _odd_in_proj(x, m, li, e, p['g_mix'], p['w_in_odd'], p['g_v'], tm, BF if prompt else F32,
                                token_tiled=tiled)
            wsp, bsp, csize = p['gmlp_prompt'] if prompt else p['gmlp_sample']
            if prompt:
                x, grp = _gmlp_xattn(x, tiled, u, v, e, wsp, bsp, p['w_out_odd'], csize, li, xattn_args + router_args,
                                     n, seq, tm)
            else:
                x = _gmlp_mix(x, u, v, e, wsp, bsp, p['w_out_odd'], csize, tm)
                cv_out.append(v.reshape(n, seq, GM_W))
        if prompt:
            x = _sorted_moe(x, grp, m, li, *router_args, *expert_args)
            tiled = True
        else:
            x = _xattn_sample(x, li, *xattn_args, n, seq)
            x = _moe(x, li, *router_args, *expert_args, tm)
    if tiled:
        x = x[:m * TT_CHUNKS].reshape(m, D_MODEL)
    cv = None if prompt else jnp.stack(cv_out)
    return x, jnp.stack(wk_out), jnp.stack(wv_out), jnp.stack(hr_out), jnp.stack(hi_out), cv


def kernel(x_prompt, x_sample, cache_win_k, cache_win_v, state_ssm_re, state_ssm_im, cache_mem_k, cache_mem_v, mem_prompt, g_mix, g_xattn, g_ffn, g_mem, w_in_even, g_q, g_k, sinks, ssm_a_re, ssm_a_im, ssm_log_dt, ssm_b_re, ssm_b_im, ssm_c_re, ssm_c_im, ssm_d, w_glu, w_out_even, w_in_odd, g_v, w_spatial, b_spatial, w_out_odd, w_xq, g_xq, w_xk, g_xk, w_xv, w_xo, w_coarse, b_coarse, w_fine, b_fine, w_e1, w_e3, w_e2):
    batch, seq = x_prompt.shape[0], x_prompt.shape[1]
    dec_batch, dec_seq = x_sample.shape[0], x_sample.shape[1]
    n_even = w_in_even.shape[0]

    row1 = lambda a: a.astype(F32)[:, None, :]
    p = {}
    p['g_mix'], p['g_xattn'], p['g_ffn'] = row1(g_mix), row1(g_xattn), row1(g_ffn)
    w_in_q = _heads_kv_major_to_g_major(w_in_even[:, :, :A_W], 2)
    p['w_in_even'] = jnp.concatenate([w_in_q, w_in_even[:, :, A_W:]], axis=2).astype(BF)
    p['g_q'] = jnp.tile(g_q.astype(F32), (1, A_HEADS))[:, None, :]
    p['g_k'] = jnp.tile(g_k.astype(F32), (1, KV_HEADS))[:, None, :]
    head_id = jnp.arange(A_W) // HEAD_DIM
    p['bd'] = (head_id[:, None] == head_id[None, :]).astype(BF)
    p['sinks'] = sinks.astype(F32).reshape(-1)
    p['ssm'] = _prep_ssm(ssm_a_re, ssm_a_im, ssm_log_dt, ssm_b_re, ssm_b_im, ssm_c_re, ssm_c_im, ssm_d, w_glu)
    p['w_out_att'] = _heads_kv_major_to_g_major(w_out_even[:, :A_W, :], 1).astype(BF)
    p['w_out_ssm'] = w_out_even[:, A_W:, :].astype(BF)
    p['w_in_odd'] = w_in_odd.astype(BF)
    p['g_v'] = row1(g_v)
    p['w_out_odd'] = w_out_odd.astype(BF)
    cs = min(dec_seq, CHUNK)
    reps = CHUNK // cs
    p['gmlp_prompt'] = (w_spatial.astype(F32), jnp.swapaxes(b_spatial.astype(F32), 1, 2), CHUNK)
    p['gmlp_sample'] = (jnp.tile(w_spatial[:, :, :cs, :cs].astype(F32), (1, 1, reps, reps)),
                        jnp.tile(jnp.swapaxes(b_spatial[:, :, :cs].astype(F32), 1, 2), (1, reps, 1)), cs)
    p['w_xq'] = w_xq.astype(BF)
    p['g_xq'] = row1(g_xq)
    p['w_xo'] = w_xo.astype(BF)
    w_router = jnp.swapaxes(jnp.concatenate([w_fine, w_coarse], axis=2).astype(F32), 1, 2)
    w_router = jnp.pad(w_router, ((0, 0), (0, ROUTER_ROWS - w_router.shape[1]), (0, 0)))
    p['w_router_hi'] = w_router.astype(BF)
    p['w_router_lo'] = (w_router - p['w_router_hi'].astype(F32)).astype(BF)
    b_router = jnp.concatenate([b_fine, b_coarse], axis=1).astype(F32)
    p['b_router'] = jnp.pad(b_router, ((0, 0), (0, ROUTER_ROWS - b_router.shape[1])))[:, :, None]
    p['w_e1'], p['w_e3'], p['w_e2'] = w_e1.astype(BF), w_e3.astype(BF), w_e2.astype(BF)

    mem2d = mem_prompt.reshape(batch * MEM_LEN, D_MODEL)
    pk, pv = _memory_kv(mem2d, row1(g_mem), w_xk.astype(BF), row1(g_xk), w_xv.astype(BF), batch)
    y_p, p_wk, p_wv, p_hr, p_hi, _ = _run_trunk(x_prompt.reshape(batch * seq, D_MODEL), batch, seq, pk, pv, None, p)

    caches = (cache_win_k.reshape(n_even, dec_batch, WINDOW, KV_W), cache_win_v.reshape(n_even, dec_batch, WINDOW, KV_W),
              state_ssm_re.reshape(n_even, dec_batch, SSM_FLAT), state_ssm_im.reshape(n_even, dec_batch, SSM_FLAT))
    y_s, s_wk, s_wv, s_hr, s_hi, s_cv = _run_trunk(x_sample.reshape(dec_batch * dec_seq, D_MODEL), dec_batch, dec_seq,
                                                   cache_mem_k.reshape(DEPTH, dec_batch, MEM_ROWS, X_HEAD_DIM),
                                                   cache_mem_v.reshape(DEPTH, dec_batch, MEM_ROWS, X_HEAD_DIM), caches, p)

    mem_shape = (DEPTH, batch, MEM_LEN, X_HEADS, X_HEAD_DIM)
    return (y_p.reshape(batch, seq, D_MODEL), y_s.reshape(dec_batch, dec_seq, D_MODEL), p_wk, p_wv, p_hr, p_hi,
            pk.reshape(mem_shape), pv.reshape(mem_shape), s_wk, s_wv, s_hr, s_hi, s_cv)
```

```python
import functools
import math

import jax
import jax.numpy as jnp
from jax import lax
from jax.experimental import pallas as pl
from jax.experimental.pallas import tpu as pltpu

F32 = jnp.float32
BF = jnp.bfloat16

D_MODEL = 1024
DEPTH = 4
A_W = 512
HEAD_DIM = 64
A_HEADS = 8
KV_HEADS = 2
Q_PER_KV = 4
KV_W = 128
WINDOW = 128
SSM_W = 512
SSM_CH = 16
SSM_GROUPS = 32
SSM_STATE = 64
SSM_FLAT = SSM_GROUPS * SSM_STATE
IN_EVEN = A_W + 2 * KV_W + SSM_W
CHUNK = 128
GM_W = 2048
GM_HEADS = 8
GM_HD = 256
MEM_LEN = 256
X_HEADS = 4
X_HEAD_DIM = 128
X_W = 512
N_GROUPS = 4
EXPERTS_PER_GROUP = 4
N_EXPERTS = 16
D_EXPERT = 256
EPS = 1e-6
NEG_INF = -1e30
LANES = 128
VMEM_LIMIT_BYTES = 52 * 1024 * 1024
ALIBI_SLOPES = tuple(2.0 ** (-8.0 * (h + 1) / A_HEADS) for h in range(A_HEADS))
SSM_SEQ_BLOCK = 8


def _cparams(*sem):
    return pltpu.CompilerParams(dimension_semantics=sem, vmem_limit_bytes=VMEM_LIMIT_BYTES)


def _rms_rows(x, g):
    return x * lax.rsqrt(jnp.mean(x * x, axis=-1, keepdims=True) + EPS) * g


TT_CHUNKS = D_MODEL // LANES


def _read_tokens(ref, token_tiled):
    if not token_tiled:
        return ref[...]
    rows = ref.shape[0] // TT_CHUNKS
    return jnp.concatenate([ref[pl.ds(c, rows, stride=TT_CHUNKS), :] for c in range(TT_CHUNKS)], axis=1)


def _write_token_tiles(ref, val):
    rows = val.shape[0]
    for c in range(TT_CHUNKS):
        ref[pl.ds(c, rows, stride=TT_CHUNKS), :] = val[:, c * LANES:(c + 1) * LANES]


def _token_spec(rows, index, token_tiled):
    if token_tiled:
        return pl.BlockSpec((rows * TT_CHUNKS, LANES), lambda *ids: (index(*ids), 0))
    return pl.BlockSpec((rows, D_MODEL), lambda *ids: (index(*ids), 0))


def _untile_kernel(x_ref, o_ref):
    o_ref[...] = _read_tokens(x_ref, True)


def _untile(x, m, tm):
    return pl.pallas_call(
        _untile_kernel,
        grid=(m // tm,),
        in_specs=[_token_spec(tm, lambda i: i, True)],
        out_specs=pl.BlockSpec((tm, D_MODEL), lambda i: (i, 0)),
        out_shape=jax.ShapeDtypeStruct((m, D_MODEL), F32),
        compiler_params=_cparams("parallel"),
        name="untile",
    )(x)


def _dot(a, b):
    return jnp.dot(a, b, preferred_element_type=F32)


def _dot_nt(a, b):
    return lax.dot_general(a, b, (((1,), (1,)), ((), ())), preferred_element_type=F32)


def _full(shape):
    nd = len(shape)
    return pl.BlockSpec(shape, lambda *_: (0,) * nd)


def _layer_spec(li, *shape, single_buffer=False):
    nd = len(shape)
    mode = pl.Buffered(1) if single_buffer else None
    return pl.BlockSpec((None,) + shape, lambda *_: (li,) + (0,) * nd, pipeline_mode=mode)


def _even_in_kernel(token_tiled, x_ref, g_ref, w_ref, gq_ref, gk_ref, bd_ref, q_ref, k_ref, v_ref, u_ref):
    h = _rms_rows(_read_tokens(x_ref, token_tiled), g_ref[...]).astype(BF)
    z = _dot(h, w_ref[...])
    q = z[:, :A_W]
    k = z[:, A_W:A_W + KV_W]
    bd = bd_ref[...]
    qms = _dot((q * q).astype(BF), bd) * (1.0 / HEAD_DIM)
    kms = _dot((k * k).astype(BF), bd[:KV_W, :KV_W]) * (1.0 / HEAD_DIM)
    q_ref[...] = q * lax.rsqrt(qms + EPS) * gq_ref[...]
    k_ref[...] = k * lax.rsqrt(kms + EPS) * gk_ref[...]
    v_ref[...] = z[:, A_W + KV_W:A_W + 2 * KV_W]
    u_ref[...] = z[:, A_W + 2 * KV_W:]


def _time_major_spec(tm, n, seq, width):
    per_seq = seq // tm
    return pl.BlockSpec((tm, width), lambda i: (i % per_seq, i // per_seq))


def _even_in_proj(x, li, e, g, w, gq, gk, bd, tm, n, seq, token_tiled=False):
    m = n * seq
    row = lambda w_: pl.BlockSpec((tm, w_), lambda i: (i, 0))
    return pl.pallas_call(
        functools.partial(_even_in_kernel, token_tiled),
        grid=(m // tm,),
        in_specs=[_token_spec(tm, lambda i: i, token_tiled), _layer_spec(li, 1, D_MODEL),
                  _layer_spec(e, D_MODEL, IN_EVEN), _layer_spec(e, 1, A_W),
                  _layer_spec(e, 1, KV_W), _full((A_W, A_W))],
        out_specs=[row(A_W), row(KV_W), row(KV_W), _time_major_spec(tm, n, seq, SSM_W) if seq % tm == 0 else row(SSM_W)],
        out_shape=[jax.ShapeDtypeStruct((m, A_W), F32), jax.ShapeDtypeStruct((m, KV_W), F32),
                   jax.ShapeDtypeStruct((m, KV_W), F32),
                   jax.ShapeDtypeStruct((seq, n * SSM_W) if seq % tm == 0 else (m, SSM_W), F32)],
        compiler_params=_cparams("parallel"),
        name="even_in_proj",
    )(x, g, w, gq, gk, bd)


def _head_rows(q, lane_lo):
    rows = []
    for kv in range(KV_HEADS):
        keep = lane_lo if kv == 0 else jnp.logical_not(lane_lo)
        for g in range(Q_PER_KV):
            rows.append(jnp.where(keep, q[:, g * LANES:(g + 1) * LANES], 0.0))
    return jnp.concatenate(rows, axis=0)


def _head_cols(o, r, lane_lo):
    cols = []
    for g in range(Q_PER_KV):
        cols.append(jnp.where(lane_lo, o[g * r:(g + 1) * r], o[(Q_PER_KV + g) * r:(Q_PER_KV + g + 1) * r]))
    return jnp.concatenate(cols, axis=1)


def _swa_prompt_kernel(e, sinks_ref, q_ref, kc_ref, kp_ref, vc_ref, vp_ref, o_ref):
    b = pl.program_id(1)
    lane_lo = lax.broadcasted_iota(jnp.int32, (1, LANES), 1) < HEAD_DIM
    qrows = _head_rows(q_ref[...], lane_lo).astype(BF)
    kcat = jnp.concatenate([kp_ref[...], kc_ref[...]], axis=0).astype(BF)
    vcat = jnp.concatenate([vp_ref[...], vc_ref[...]], axis=0).astype(BF)
    s = _dot_nt(qrows, kcat) * (HEAD_DIM ** -0.5)
    r = lax.broadcasted_iota(jnp.int32, (WINDOW, 2 * WINDOW), 0)
    c = lax.broadcasted_iota(jnp.int32, (WINDOW, 2 * WINDOW), 1)
    dist = WINDOW + r - c
    valid = (dist >= 0) & (dist <= WINDOW) & ((c >= WINDOW) | (b > 0))
    dist_f = dist.astype(F32)
    ps = []
    for h in range(A_HEADS):
        sh = s[h * WINDOW:(h + 1) * WINDOW] - ALIBI_SLOPES[h] * dist_f
        sh = jnp.where(valid, sh, NEG_INF)
        sk = sinks_ref[e * A_HEADS + h]
        m = jnp.maximum(jnp.max(sh, axis=-1, keepdims=True), sk)
        p = jnp.exp(sh - m)
        den = jnp.sum(p, axis=-1, keepdims=True) + jnp.exp(sk - m)
        ps.append((p / den).astype(BF))
    o = _dot(jnp.concatenate(ps, axis=0), vcat)
    o_ref[...] = _head_cols(o, WINDOW, lane_lo).astype(o_ref.dtype)


def _swa_prompt(q, k, v, sinks, e, n, seq):
    nb = seq // WINDOW
    cur = lambda w_: pl.BlockSpec((WINDOW, w_), lambda i, b: (i * nb + b, 0))
    prev = lambda w_: pl.BlockSpec((WINDOW, w_), lambda i, b: (i * nb + jnp.maximum(b - 1, 0), 0))
    return pl.pallas_call(
        functools.partial(_swa_prompt_kernel, e),
        grid=(n, nb),
        in_specs=[pl.BlockSpec(memory_space=pltpu.SMEM), cur(A_W), cur(KV_W), prev(KV_W), cur(KV_W), prev(KV_W)],
        out_specs=cur(A_W),
        out_shape=jax.ShapeDtypeStruct((n * seq, A_W), BF),
        compiler_params=_cparams("parallel", "arbitrary"),
        name="swa_prompt",
    )(sinks, q, k, k, v, v)


SWA_SEQ_BLOCK = 8


def _swa_sample_kernel(t_new, e, sinks_ref, q_ref, kn_ref, vn_ref, wk_ref, wv_ref, o_ref, nwk_ref, nwv_ref):
    nseq = SWA_SEQ_BLOCK
    pair_rows = 2 * t_new
    npair = nseq // 2
    kn = kn_ref[...]
    vn = vn_ref[...]
    for i in range(nseq):
        nwk_ref[i, 0:WINDOW - t_new, :] = wk_ref[i, t_new:WINDOW, :]
        nwk_ref[i, WINDOW - t_new:WINDOW, :] = kn[i * t_new:(i + 1) * t_new, :]
        nwv_ref[i, 0:WINDOW - t_new, :] = wv_ref[i, t_new:WINDOW, :]
        nwv_ref[i, WINDOW - t_new:WINDOW, :] = vn[i * t_new:(i + 1) * t_new, :]

    lane_lo = lax.broadcasted_iota(jnp.int32, (1, LANES), 1) < HEAD_DIM
    q = q_ref[...]
    knb = kn.astype(BF)
    vnb = vn.astype(BF)
    hr = A_HEADS * pair_rows
    qrows = [_head_rows(q[j * pair_rows:(j + 1) * pair_rows], lane_lo).astype(BF) for j in range(npair)]
    scale = HEAD_DIM ** -0.5
    s_new_all = _dot_nt(jnp.concatenate(qrows, axis=0), knb) * scale

    tq = lax.broadcasted_iota(jnp.int32, (pair_rows, 2 * WINDOW), 0)
    cw = lax.broadcasted_iota(jnp.int32, (pair_rows, 2 * WINDOW), 1)
    dist_w = WINDOW + (tq % t_new) - (cw % WINDOW)
    valid_w = ((tq // t_new) == (cw // WINDOW)) & (dist_w <= WINDOW)
    dist_wf = dist_w.astype(F32)
    tqn = lax.broadcasted_iota(jnp.int32, (pair_rows, nseq * t_new), 0)
    cn = lax.broadcasted_iota(jnp.int32, (pair_rows, nseq * t_new), 1)
    dist_n = (tqn % t_new) - (cn % t_new)
    dist_nf = dist_n.astype(F32)

    p_new_all = []
    o_win_all = []
    for j in range(npair):
        kwin = jnp.concatenate([wk_ref[2 * j], wk_ref[2 * j + 1]], axis=0).astype(BF)
        vwin = jnp.concatenate([wv_ref[2 * j], wv_ref[2 * j + 1]], axis=0).astype(BF)
        s_win = _dot_nt(qrows[j], kwin) * scale
        valid_n = ((2 * j + tqn // t_new) == (cn // t_new)) & (dist_n >= 0)
        p_win = []
        for h in range(A_HEADS):
            sw = jnp.where(valid_w, s_win[h * pair_rows:(h + 1) * pair_rows] - ALIBI_SLOPES[h] * dist_wf, NEG_INF)
            sn = s_new_all[j * hr + h * pair_rows:j * hr + (h + 1) * pair_rows]
            sn = jnp.where(valid_n, sn - ALIBI_SLOPES[h] * dist_nf, NEG_INF)
            sk = sinks_ref[e * A_HEADS + h]
            m = jnp.maximum(jnp.maximum(jnp.max(sw, axis=-1, keepdims=True), jnp.max(sn, axis=-1, keepdims=True)), sk)
            pw = jnp.exp(sw - m)
            pn = jnp.exp(sn - m)
            den = jnp.sum(pw, axis=-1, keepdims=True) + jnp.sum(pn, axis=-1, keepdims=True) + jnp.exp(sk - m)
            p_win.append((pw / den).astype(BF))
            p_new_all.append((pn / den).astype(BF))
        o_win_all.append(_dot(jnp.concatenate(p_win, axis=0), vwin))
    o_new = _dot(jnp.concatenate(p_new_all, axis=0), vnb)
    outs = [_head_cols(o_win_all[j] + o_new[j * hr:(j + 1) * hr], pair_rows, lane_lo) for j in range(npair)]
    o_ref[...] = jnp.concatenate(outs, axis=0).astype(o_ref.dtype)


def _swa_sample(q, kn, vn, win_k, win_v, sinks, e, n, t_new):
    rows = SWA_SEQ_BLOCK * t_new
    row = lambda w_: pl.BlockSpec((rows, w_), lambda i: (i, 0))
    win = pl.BlockSpec((SWA_SEQ_BLOCK, WINDOW, KV_W), lambda i: (i, 0, 0))
    win_in = pl.BlockSpec((None, SWA_SEQ_BLOCK, WINDOW, KV_W), lambda i: (e, i, 0, 0))
    return pl.pallas_call(
        functools.partial(_swa_sample_kernel, t_new, e),
        grid=(n // SWA_SEQ_BLOCK,),
        in_specs=[pl.BlockSpec(memory_space=pltpu.SMEM), row(A_W), row(KV_W), row(KV_W), win_in, win_in],
        out_specs=[row(A_W), win, win],
        out_shape=[jax.ShapeDtypeStruct((n * t_new, A_W), BF),
                   jax.ShapeDtypeStruct((n, WINDOW, KV_W), F32), jax.ShapeDtypeStruct((n, WINDOW, KV_W), F32)],
        compiler_params=_cparams("parallel"),
        name="swa_sample",
    )(sinks, q, kn, vn, win_k, win_v)


SSM_LANE_CHUNK = 1024
SSM_IN_TILE = LANES
SSM_IN_TILES = SSM_W // SSM_IN_TILE
SSM_ST_TILE = SSM_IN_TILE // SSM_CH * SSM_STATE
SSM_OUT_TILE = 256
SSM_OUT_TILES = SSM_W // SSM_OUT_TILE
SSM_OUT_ST = SSM_OUT_TILE // SSM_CH * SSM_STATE


def _s5_kernel(tc, has_h0, *refs):
    if has_h0:
        (u_ref, h0r_ref, h0i_ref, abr_ref, abi_ref, bcat_ref, ccat_ref, d_ref, wglu_ref,
         o_ref, hr_ref, hi_ref, st_scr, car_scr) = refs
    else:
        (u_ref, abr_ref, abi_ref, bcat_ref, ccat_ref, d_ref, wglu_ref,
         o_ref, hr_ref, hi_ref, st_scr, car_scr) = refs
    nb = SSM_SEQ_BLOCK
    ci = pl.program_id(1)

    @pl.when(ci == 0)
    def _():
        if has_h0:
            car_scr[:, :SSM_FLAT] = h0r_ref[...]
            car_scr[:, SSM_FLAT:] = h0i_ref[...]
        else:
            car_scr[...] = jnp.zeros_like(car_scr)

    ut = u_ref[...].reshape(tc * nb, SSM_W)
    ub = ut.astype(BF)
    for jt in range(SSM_IN_TILES):
        bu = _dot(ub[:, jt * SSM_IN_TILE:(jt + 1) * SSM_IN_TILE], bcat_ref[jt])
        st_scr[:, jt * SSM_ST_TILE:(jt + 1) * SSM_ST_TILE] = bu[:, :SSM_ST_TILE]
        st_scr[:, SSM_FLAT + jt * SSM_ST_TILE:SSM_FLAT + (jt + 1) * SSM_ST_TILE] = bu[:, SSM_ST_TILE:]

    for lc in range(SSM_FLAT // SSM_LANE_CHUNK):
        lo = lc * SSM_LANE_CHUNK
        re_sl = slice(lo, lo + SSM_LANE_CHUNK)
        im_sl = slice(SSM_FLAT + lo, SSM_FLAT + lo + SSM_LANE_CHUNK)
        ar = jnp.broadcast_to(abr_ref[:, re_sl], (nb, SSM_LANE_CHUNK))
        ai = jnp.broadcast_to(abi_ref[:, re_sl], (nb, SSM_LANE_CHUNK))

        def step(t, carry):
            hr, hi = carry
            rows = pl.ds(pl.multiple_of(t * nb, nb), nb)
            nr = ar * hr - ai * hi + st_scr[rows, re_sl]
            ni = ar * hi + ai * hr + st_scr[rows, im_sl]
            st_scr[rows, re_sl] = nr
            st_scr[rows, im_sl] = ni
            return nr, ni

        hr, hi = lax.fori_loop(0, tc, step, (car_scr[:, re_sl], car_scr[:, im_sl]))
        car_scr[:, re_sl] = hr
        car_scr[:, im_sl] = hi

    ys = []
    for ot in range(SSM_OUT_TILES):
        re_sl = slice(ot * SSM_OUT_ST, (ot + 1) * SSM_OUT_ST)
        im_sl = slice(SSM_FLAT + ot * SSM_OUT_ST, SSM_FLAT + (ot + 1) * SSM_OUT_ST)
        ys.append(_dot(st_scr[:, re_sl].astype(BF), ccat_ref[ot, :SSM_OUT_ST, :])
                  + _dot(st_scr[:, im_sl].astype(BF), ccat_ref[ot, SSM_OUT_ST:, :]))
    y = jnp.concatenate(ys, axis=1) + d_ref[...] * ut
    y = jax.nn.gelu(y).astype(BF)
    g = _dot(y, wglu_ref[...])
    o_ref[...] = (g[:, :SSM_W] * jax.nn.sigmoid(g[:, SSM_W:])).reshape(tc, nb, SSM_W)

    @pl.when(ci == pl.num_programs(1) - 1)
    def _():
        hr_ref[...] = car_scr[:, :SSM_FLAT]
        hi_ref[...] = car_scr[:, SSM_FLAT:]


def _s5_glu(u, h0, ssm, e, n, seq, tc):
    nb = SSM_SEQ_BLOCK
    abr, abi, bcat, ccat, dsk, wglu = ssm
    u_spec = pl.BlockSpec((tc, nb, SSM_W), lambda i, c: (c, i, 0))
    st_spec = pl.BlockSpec((nb, SSM_FLAT), lambda i, c: (i, 0))
    h0_spec = pl.BlockSpec((None, nb, SSM_FLAT), lambda i, c: (e, i, 0))
    consts = [_layer_spec(e, 1, SSM_FLAT), _layer_spec(e, 1, SSM_FLAT),
              _layer_spec(e, SSM_IN_TILES, SSM_IN_TILE, 2 * SSM_ST_TILE),
              _layer_spec(e, SSM_OUT_TILES, 2 * SSM_OUT_ST, SSM_OUT_TILE), _layer_spec(e, 1, SSM_W),
              _layer_spec(e, SSM_W, 2 * SSM_W)]
    has_h0 = h0 is not None
    in_specs = [u_spec] + ([h0_spec, h0_spec] if has_h0 else []) + consts
    args = [u] + (list(h0) if has_h0 else []) + [abr, abi, bcat, ccat, dsk, wglu]
    return pl.pallas_call(
        functools.partial(_s5_kernel, tc, has_h0),
        grid=(n // nb, seq // tc),
        in_specs=in_specs,
        out_specs=[u_spec, st_spec, st_spec],
        out_shape=[jax.ShapeDtypeStruct((seq, n, SSM_W), F32), jax.ShapeDtypeStruct((n, SSM_FLAT), F32),
                   jax.ShapeDtypeStruct((n, SSM_FLAT), F32)],
        scratch_shapes=[pltpu.VMEM((nb * tc, 2 * SSM_FLAT), F32), pltpu.VMEM((nb, 2 * SSM_FLAT), F32)],
        compiler_params=_cparams("parallel", "arbitrary"),
        name="s5_glu",
    )(*args)


def _even_out_kernel(x_ref, a_ref, s_ref, wa_ref, ws_ref, o_ref):
    o_ref[...] = x_ref[...] + _dot(a_ref[...], wa_ref[...]) + _dot(s_ref[...].astype(BF), ws_ref[...])


def _even_out_proj(x, att, ssm, e, wa, ws, tm, n, seq):
    m = x.shape[0]
    row = lambda w_: pl.BlockSpec((tm, w_), lambda i: (i, 0))
    ssm_spec = _time_major_spec(tm, n, seq, SSM_W) if seq % tm == 0 else row(SSM_W)
    return pl.pallas_call(
        _even_out_kernel,
        grid=(m // tm,),
        in_specs=[row(D_MODEL), row(A_W), ssm_spec, _layer_spec(e, A_W, D_MODEL), _layer_spec(e, SSM_W, D_MODEL)],
        out_specs=row(D_MODEL),
        out_shape=jax.ShapeDtypeStruct((m, D_MODEL), F32),
        compiler_params=_cparams("parallel"),
        name="even_out_proj",
    )(x, att, ssm, wa, ws)


def _odd_in_kernel(token_tiled, x_ref, g_ref, w_ref, gv_ref, u_ref, v_ref):
    h = _rms_rows(_read_tokens(x_ref, token_tiled), g_ref[...]).astype(BF)
    u_ref[...] = jax.nn.gelu(_dot(h, w_ref[:, :GM_W])).astype(u_ref.dtype)
    zv = jax.nn.gelu(_dot(h, w_ref[:, GM_W:]))
    v_ref[...] = _rms_rows(zv, gv_ref[...]).astype(v_ref.dtype)


def _odd_in_proj(x, m, li, e, g, w, gv, tm, v_dtype, token_tiled=False):
    row = lambda w_: pl.BlockSpec((tm, w_), lambda i: (i, 0))
    return pl.pallas_call(
        functools.partial(_odd_in_kernel, token_tiled),
        grid=(m // tm,),
        in_specs=[_token_spec(tm, lambda i: i, token_tiled), _layer_spec(li, 1, D_MODEL),
                  _layer_spec(e, D_MODEL, 2 * GM_W, single_buffer=True),
                  _layer_spec(e, 1, GM_W)],
        out_specs=[row(GM_W), row(GM_W)],
        out_shape=[jax.ShapeDtypeStruct((m, GM_W), BF), jax.ShapeDtypeStruct((m, GM_W), v_dtype)],
        compiler_params=_cparams("parallel"),
        name="odd_in_proj",
    )(x, g, w, gv)


def _gmlp_rows(csize, x, u_ref, v_ref, wsp_ref, bsp_ref, wo_ref, gated_scr):
    tm = u_ref.shape[0]
    i = lax.broadcasted_iota(jnp.int32, (CHUNK, CHUNK), 0)
    j = lax.broadcasted_iota(jnp.int32, (CHUNK, CHUNK), 1)
    keep = (j <= i) & ((i // csize) == (j // csize))
    bsp = bsp_ref[...]
    for h in range(GM_HEADS):
        ws = jnp.where(keep, wsp_ref[h], 0.0).astype(BF)
        b_col = bsp[:, h:h + 1]
        cols = slice(h * GM_HD, (h + 1) * GM_HD)
        for c in range(tm // CHUNK):
            rows = slice(c * CHUNK, (c + 1) * CHUNK)
            mix = _dot(ws, v_ref[rows, cols].astype(BF)) + b_col
            gated_scr[rows, cols] = (u_ref[rows, cols].astype(F32) * mix).astype(BF)
    return x + _dot(gated_scr[...], wo_ref[...])


def _gmlp_mix_kernel(csize, x_ref, u_ref, v_ref, wsp_ref, bsp_ref, wo_ref, o_ref, gated_scr):
    o_ref[...] = _gmlp_rows(csize, x_ref[...], u_ref, v_ref, wsp_ref, bsp_ref, wo_ref, gated_scr)


def _gmlp_mix(x, u, v, e, wsp, bsp, wo, csize, tm):
    m = x.shape[0]
    row = lambda w_: pl.BlockSpec((tm, w_), lambda i: (i, 0))
    return pl.pallas_call(
        functools.partial(_gmlp_mix_kernel, csize),
        grid=(m // tm,),
        in_specs=[row(D_MODEL), row(GM_W), row(GM_W), _layer_spec(e, GM_HEADS, CHUNK, CHUNK),
                  _layer_spec(e, CHUNK, GM_HEADS), _layer_spec(e, GM_W, D_MODEL)],
        out_specs=row(D_MODEL),
        out_shape=jax.ShapeDtypeStruct((m, D_MODEL), F32),
        scratch_shapes=[pltpu.VMEM((tm, GM_W), BF)],
        compiler_params=_cparams("parallel"),
        name="gmlp_mix",
    )(x, u, v, wsp, bsp, wo)


def _head_norm(z, g):
    cols = []
    for h in range(X_HEADS):
        zh = z[:, h * X_HEAD_DIM:(h + 1) * X_HEAD_DIM]
        cols.append(_rms_rows(zh, g))
    return jnp.concatenate(cols, axis=1)


MEMKV_SEQ_BLOCK = 2
MEM_ROWS = MEM_LEN * X_HEADS


def _head_rows_of(h):
    return pl.ds(h, MEM_LEN, stride=X_HEADS)


def _memory_kv_kernel(mem_ref, g_ref, wk_ref, gk_ref, wv_ref, k_ref, v_ref):
    m = _rms_rows(mem_ref[...], g_ref[...]).astype(BF)
    k = _head_norm(_dot(m, wk_ref[...]), gk_ref[...])
    v = _dot(m, wv_ref[...])
    for s in range(MEMKV_SEQ_BLOCK):
        rows = slice(s * MEM_LEN, (s + 1) * MEM_LEN)
        for h in range(X_HEADS):
            cols = slice(h * X_HEAD_DIM, (h + 1) * X_HEAD_DIM)
            k_ref[s, _head_rows_of(h), :] = k[rows, cols]
            v_ref[s, _head_rows_of(h), :] = v[rows, cols]


def _memory_kv(mem, g_mem, w_k, g_k, w_v, n):
    tm = MEMKV_SEQ_BLOCK * MEM_LEN
    per_layer = lambda a, b: pl.BlockSpec((None, a, b), lambda l, i: (l, 0, 0))
    out_spec = pl.BlockSpec((None, MEMKV_SEQ_BLOCK, MEM_ROWS, X_HEAD_DIM), lambda l, i: (l, i, 0, 0))
    out_sds = jax.ShapeDtypeStruct((DEPTH, n, MEM_ROWS, X_HEAD_DIM), F32)
    return pl.pallas_call(
        _memory_kv_kernel,
        grid=(DEPTH, n // MEMKV_SEQ_BLOCK),
        in_specs=[pl.BlockSpec((tm, D_MODEL), lambda l, i: (i, 0)), per_layer(1, D_MODEL), per_layer(D_MODEL, X_W),
                  per_layer(1, X_HEAD_DIM), per_layer(D_MODEL, X_W)],
        out_specs=[out_spec, out_spec],
        out_shape=[out_sds, out_sds],
        compiler_params=_cparams("parallel", "parallel"),
        name="memory_kv",
    )(mem, g_mem, w_k, g_k, w_v)


def _xattn_rows(x, g_ref, wq_ref, gq_ref, mk_ref, mv_ref, wo_ref):
    q = _head_norm(_dot(_rms_rows(x, g_ref[...]).astype(BF), wq_ref[...]), gq_ref[...]).astype(BF)
    outs = []
    for h in range(X_HEADS):
        cols = slice(h * X_HEAD_DIM, (h + 1) * X_HEAD_DIM)
        s = _dot_nt(q[:, cols], mk_ref[_head_rows_of(h), :].astype(BF)) * (X_HEAD_DIM ** -0.5)
        m = jnp.max(s, axis=-1, keepdims=True)
        p = jnp.exp(s - m)
        p = (p / jnp.sum(p, axis=-1, keepdims=True)).astype(BF)
        outs.append(_dot(p, mv_ref[_head_rows_of(h), :].astype(BF)))
    o = jnp.concatenate(outs, axis=1).astype(BF)
    return x + _dot(o, wo_ref[...])


N_XATTN_REFS = 6
N_ROUTE_REFS = 4
GROUP_ROW = 0


def _xattn_route_store(x, refs, o_ref, grp_ref):
    y = _xattn_rows(x, *refs[:N_XATTN_REFS])
    _write_token_tiles(o_ref, y)
    gf_ref, whi_ref, wlo_ref, b_ref = refs[N_XATTN_REFS:]
    grp = _top_group(_router_logits(_rms_rows(y, gf_ref[...]), whi_ref[...], wlo_ref[...], b_ref[...]))
    grp_ref[...] = jnp.concatenate([grp, jnp.zeros((7, grp.shape[1]), jnp.int32)], axis=0)


def _even_out_xattn_kernel(token_tiled, x_ref, a_ref, s_ref, wa_ref, ws_ref, *rest):
    refs, o_ref, grp_ref = rest[:-2], rest[-2], rest[-1]
    x = (_read_tokens(x_ref, token_tiled) + _dot(a_ref[...], wa_ref[...])
         + _dot(s_ref[...].astype(BF), ws_ref[...]))
    _xattn_route_store(x, refs, o_ref, grp_ref)


def _gmlp_xattn_kernel(csize, token_tiled, x_ref, u_ref, v_ref, wsp_ref, bsp_ref, wo_ref, *rest):
    refs, o_ref, grp_ref, gated_scr = rest[:-3], rest[-3], rest[-2], rest[-1]
    x = _gmlp_rows(csize, _read_tokens(x_ref, token_tiled), u_ref, v_ref, wsp_ref, bsp_ref, wo_ref, gated_scr)
    _xattn_route_store(x, refs, o_ref, grp_ref)


def _xattn_route_specs(li):
    mem = pl.BlockSpec((None, None, MEM_ROWS, X_HEAD_DIM), lambda i, j: (li, i, 0, 0))
    return [_layer_spec(li, 1, D_MODEL), _layer_spec(li, D_MODEL, X_W), _layer_spec(li, 1, X_HEAD_DIM), mem, mem,
            _layer_spec(li, X_W, D_MODEL), _layer_spec(li, 1, D_MODEL), _layer_spec(li, ROUTER_ROWS, D_MODEL),
            _layer_spec(li, ROUTER_ROWS, D_MODEL), _layer_spec(li, ROUTER_ROWS, 1)]


def _token_tile_outputs(n, seq, tq):
    nq = seq // tq
    specs = [_token_spec(tq, lambda i, j: i * nq + j, True), pl.BlockSpec((None, 8, tq), lambda i, j: (i * nq + j, 0, 0))]
    shapes = [jax.ShapeDtypeStruct((n * seq * TT_CHUNKS, LANES), F32), jax.ShapeDtypeStruct((n * nq, 8, tq), jnp.int32)]
    return specs, shapes


def _even_out_xattn(x, token_tiled, att, ssm, e, wa, ws, li, xattn_route_args, n, seq, tq):
    nq = seq // tq
    row = lambda w_: pl.BlockSpec((tq, w_), lambda i, j: (i * nq + j, 0))
    ssm_spec = pl.BlockSpec((tq, SSM_W), lambda i, j: (j, i))
    out_specs, out_shape = _token_tile_outputs(n, seq, tq)
    y, grp = pl.pallas_call(
        functools.partial(_even_out_xattn_kernel, token_tiled),
        grid=(n, nq),
        in_specs=[_token_spec(tq, lambda i, j: i * nq + j, token_tiled), row(A_W), ssm_spec,
                  _layer_spec(e, A_W, D_MODEL), _layer_spec(e, SSM_W, D_MODEL)] + _xattn_route_specs(li),
        out_specs=out_specs,
        out_shape=out_shape,
        compiler_params=_cparams("parallel", "arbitrary"),
        name="even_out_xattn",
    )(x, att, ssm, wa, ws, *xattn_route_args)
    return y, grp[:, GROUP_ROW, :].reshape(n * seq)


def _gmlp_xattn(x, token_tiled, u, v, e, wsp, bsp, wo, csize, li, xattn_route_args, n, seq, tq):
    nq = seq // tq
    row = lambda w_: pl.BlockSpec((tq, w_), lambda i, j: (i * nq + j, 0))
    out_specs, out_shape = _token_tile_outputs(n, seq, tq)
    y, grp = pl.pallas_call(
        functools.partial(_gmlp_xattn_kernel, csize, token_tiled),
        grid=(n, nq),
        in_specs=[_token_spec(tq, lambda i, j: i * nq + j, token_tiled), row(GM_W), row(GM_W),
                  _layer_spec(e, GM_HEADS, CHUNK, CHUNK), _layer_spec(e, CHUNK, GM_HEADS),
                  _layer_spec(e, GM_W, D_MODEL)] + _xattn_route_specs(li),
        out_specs=out_specs,
        out_shape=out_shape,
        scratch_shapes=[pltpu.VMEM((tq, GM_W), BF)],
        compiler_params=_cparams("parallel", "arbitrary"),
        name="gmlp_xattn",
    )(x, u, v, wsp, bsp, wo, *xattn_route_args)
    return y, grp[:, GROUP_ROW, :].reshape(n * seq)


XATTN_SEQ_BLOCK = 8


def _xattn_sample_kernel(t_new, x_ref, g_ref, wq_ref, gq_ref, mk_ref, mv_ref, wo_ref, o_ref):
    pair_rows = 2 * t_new
    x = x_ref[...]
    q = _head_norm(_dot(_rms_rows(x, g_ref[...]).astype(BF), wq_ref[...]), gq_ref[...]).astype(BF)
    tq = lax.broadcasted_iota(jnp.int32, (pair_rows, 2 * MEM_LEN), 0)
    cm = lax.broadcasted_iota(jnp.int32, (pair_rows, 2 * MEM_LEN), 1)
    same_seq = (tq // t_new) == (cm // MEM_LEN)
    npair = XATTN_SEQ_BLOCK // 2
    scores = []
    for j in range(npair):
        qp = q[j * pair_rows:(j + 1) * pair_rows]
        for h in range(X_HEADS):
            hr = _head_rows_of(h)
            kp = jnp.concatenate([mk_ref[2 * j, hr, :], mk_ref[2 * j + 1, hr, :]], axis=0).astype(BF)
            s = _dot_nt(qp[:, h * X_HEAD_DIM:(h + 1) * X_HEAD_DIM], kp)
            scores.append(jnp.where(same_seq, s, NEG_INF))
    s = jnp.concatenate(scores, axis=0) * (X_HEAD_DIM ** -0.5)
    m = jnp.max(s, axis=-1, keepdims=True)
    p = jnp.exp(s - m)
    p = p / jnp.sum(p, axis=-1, keepdims=True)
    outs = []
    for j in range(npair):
        heads = []
        for h in range(X_HEADS):
            hr = _head_rows_of(h)
            vp = jnp.concatenate([mv_ref[2 * j, hr, :], mv_ref[2 * j + 1, hr, :]], axis=0).astype(BF)
            r0 = (j * X_HEADS + h) * pair_rows
            heads.append(_dot(p[r0:r0 + pair_rows].astype(BF), vp))
        outs.append(jnp.concatenate(heads, axis=1))
    o_all = jnp.concatenate(outs, axis=0).astype(BF)
    o_ref[...] = x + _dot(o_all, wo_ref[...])


def _xattn_sample(x, li, g, wq, gq, mk, mv, wo, n, t_new):
    rows = XATTN_SEQ_BLOCK * t_new
    row = pl.BlockSpec((rows, D_MODEL), lambda i: (i, 0))
    mem = pl.BlockSpec((None, XATTN_SEQ_BLOCK, MEM_ROWS, X_HEAD_DIM), lambda i: (li, i, 0, 0))
    return pl.pallas_call(
        functools.partial(_xattn_sample_kernel, t_new),
        grid=(n // XATTN_SEQ_BLOCK,),
        in_specs=[row, _layer_spec(li, 1, D_MODEL), _layer_spec(li, D_MODEL, X_W), _layer_spec(li, 1, X_HEAD_DIM),
                  mem, mem, _layer_spec(li, X_W, D_MODEL)],
        out_specs=row,
        out_shape=jax.ShapeDtypeStruct((n * t_new, D_MODEL), F32),
        compiler_params=_cparams("parallel"),
        name="xattn_sample",
    )(x, g, wq, gq, mk, mv, wo)


ROUTER_ROWS = 32
ROUTER_COARSE_ROW = N_EXPERTS
BIG = 3.0e38


def _first_argmax(vals, vmax):
    idx = jnp.full(vmax.shape, len(vals) - 1, jnp.int32)
    for j in range(len(vals) - 2, -1, -1):
        idx = jnp.where(vals[j] == vmax, j, idx)
    return idx


def _router_logits(h, whi, wlo, bias):
    h_hi = h.astype(BF)
    h_lo = (h - h_hi.astype(F32)).astype(BF)
    return _dot_nt(whi, h_hi) + _dot_nt(whi, h_lo) + _dot_nt(wlo, h_hi) + bias


def _coarse_rows(lg):
    return [lg[ROUTER_COARSE_ROW + g:ROUTER_COARSE_ROW + g + 1, :] for g in range(N_GROUPS)]


def _top_group(lg):
    coarse = _coarse_rows(lg)
    return _first_argmax(coarse, functools.reduce(jnp.maximum, coarse))


def _in_group_gates(lg, grp):
    pick = lambda rows: functools.reduce(lambda f, g: jnp.where(grp == g, rows[g], f),
                                         range(N_GROUPS - 2, -1, -1), rows[N_GROUPS - 1])
    coarse = _coarse_rows(lg)
    csel = pick(coarse)
    p_grp = 1.0 / functools.reduce(lambda a, b: a + b, [jnp.exp(c - csel) for c in coarse])
    fine = [pick([lg[g * EXPERTS_PER_GROUP + j:g * EXPERTS_PER_GROUP + j + 1, :] for g in range(N_GROUPS)])
            for j in range(EXPERTS_PER_GROUP)]
    v1 = functools.reduce(jnp.maximum, fine)
    i1 = _first_argmax(fine, v1)
    rest = [jnp.where(i1 == j, -BIG, fine[j]) for j in range(EXPERTS_PER_GROUP)]
    v2 = functools.reduce(jnp.maximum, rest)
    i2 = _first_argmax(rest, v2)
    e2 = jnp.exp(v2 - v1)
    w1 = p_grp / (1.0 + e2)
    w2 = p_grp * e2 / (1.0 + e2)
    return [jnp.where(i1 == j, w1, 0.0) + jnp.where(i2 == j, w2, 0.0) for j in range(EXPERTS_PER_GROUP)]


def _rows_to_lanes(rows, tm):
    pad = jnp.zeros((LANES - len(rows), tm), F32)
    return jnp.concatenate(list(rows) + [pad], axis=0).T


def _router_gates(h, whi, wlo, bias):
    lg = _router_logits(h, whi, wlo, bias)
    grp = _top_group(lg)
    in_group = _in_group_gates(lg, grp)
    rows = [jnp.where(grp == e // EXPERTS_PER_GROUP, in_group[e % EXPERTS_PER_GROUP], 0.0) for e in range(N_EXPERTS)]
    return _rows_to_lanes(rows, lg.shape[1])


def _moe_kernel(x_ref, g_ref, whi_ref, wlo_ref, b_ref, w1_ref, w3_ref, w2_ref, o_ref, h_scr, gates_scr, acc_scr):
    gi = pl.program_id(1)

    @pl.when(gi == 0)
    def _():
        h = _rms_rows(x_ref[...], g_ref[...])
        h_scr[...] = h.astype(BF)
        gates_scr[...] = _router_gates(h, whi_ref[...], wlo_ref[...], b_ref[...])
        acc_scr[...] = jnp.zeros_like(acc_scr)

    h = h_scr[...]
    gates = gates_scr[...]
    lane = lax.broadcasted_iota(jnp.int32, gates.shape, 1)
    for j in range(EXPERTS_PER_GROUP):
        gcol = jnp.sum(jnp.where(lane == gi * EXPERTS_PER_GROUP + j, gates, 0.0), axis=-1, keepdims=True)
        a = _dot(h, w1_ref[j])
        b = _dot(h, w3_ref[j])
        hid = (a * jax.nn.sigmoid(a)) * b * gcol
        acc_scr[...] += _dot(hid.astype(BF), w2_ref[j])

    @pl.when(gi == pl.num_programs(1) - 1)
    def _():
        o_ref[...] = x_ref[...] + acc_scr[...]


def _moe(x, li, g, whi, wlo, b, w1, w3, w2, tm):
    m = x.shape[0]
    row = lambda w_: pl.BlockSpec((tm, w_), lambda i, e: (i, 0))
    group_w = lambda a, b_: pl.BlockSpec((None, EXPERTS_PER_GROUP, a, b_), lambda i, e: (li, e, 0, 0))
    return pl.pallas_call(
        _moe_kernel,
        grid=(m // tm, N_GROUPS),
        in_specs=[row(D_MODEL), _layer_spec(li, 1, D_MODEL), _layer_spec(li, ROUTER_ROWS, D_MODEL),
                  _layer_spec(li, ROUTER_ROWS, D_MODEL), _layer_spec(li, ROUTER_ROWS, 1),
                  group_w(D_MODEL, D_EXPERT), group_w(D_MODEL, D_EXPERT), group_w(D_EXPERT, D_MODEL)],
        out_specs=row(D_MODEL),
        out_shape=jax.ShapeDtypeStruct((m, D_MODEL), F32),
        scratch_shapes=[pltpu.VMEM((tm, D_MODEL), BF), pltpu.VMEM((tm, LANES), F32), pltpu.VMEM((tm, D_MODEL), F32)],
        compiler_params=_cparams("parallel", "arbitrary"),
        name="moe",
    )(x, g, whi, wlo, b, w1, w3, w2)


MOE_TILE = 256
WRITE_BACK_DMA_PRIORITY = 1


def _sorted_plan(grp, m):
    t = MOE_TILE
    n_tiles = m // t + N_GROUPS
    order = jnp.argsort(grp, stable=True).astype(jnp.int32)
    counts = jnp.sum((grp[None, :] == jnp.arange(N_GROUPS, dtype=jnp.int32)[:, None]).astype(jnp.int32), axis=1)
    tiles_per_group = (counts + t - 1) // t
    tile_end = jnp.cumsum(tiles_per_group)
    tile_start = tile_end - tiles_per_group
    first_token = jnp.cumsum(counts) - counts
    s = jnp.arange(n_tiles, dtype=jnp.int32)
    tile_grp = jnp.minimum(jnp.sum((s[:, None] >= tile_end[None, :]).astype(jnp.int32), axis=1), N_GROUPS - 1)
    n_active = tile_end[N_GROUPS - 1]
    tile_in_group = s - tile_start[tile_grp]
    n_valid = jnp.where(s < n_active, jnp.clip(counts[tile_grp] - tile_in_group * t, 0, t), 0)
    r = jnp.arange(t, dtype=jnp.int32)
    slot_token = order[jnp.clip(first_token[tile_grp][:, None] + tile_in_group[:, None] * t + r[None, :], 0, m - 1)]
    valid = r[None, :] < n_valid[:, None]
    src = jnp.where(valid, slot_token, 0).reshape(-1)
    dst = jnp.where(valid, slot_token, m + (s % 2)[:, None] * t + r[None, :]).reshape(-1)
    src = jnp.concatenate([src, jnp.zeros((t,), jnp.int32)])
    dst = jnp.concatenate([m + t + r, dst])
    return ((src * TT_CHUNKS).astype(jnp.int32), (dst * TT_CHUNKS).astype(jnp.int32), tile_grp.astype(jnp.int32))


def _sorted_moe_kernel(m, src_ref, dst_ref, tgrp_ref, x_hbm, g_ref, whi_ref, wlo_ref, b_ref,
                       w1_ref, w3_ref, w2_ref, out_hbm, xbuf0, xbuf1, obuf0, obuf1, sem_g, sem_s):
    t = MOE_TILE
    i = pl.program_id(0)
    last = pl.num_programs(0) - 1
    xbuf = (xbuf0, xbuf1)
    obuf = (obuf0, obuf1)

    c8 = TT_CHUNKS

    def gather_row(tile, slot, r):
        src = pl.multiple_of(src_ref[tile * t + r], c8)
        return pltpu.make_async_copy(x_hbm.at[pl.ds(src, c8), :], xbuf[slot].at[pl.ds(r * c8, c8), :], sem_g.at[slot])

    def scatter_row(tile, slot, r):
        dst = pl.multiple_of(dst_ref[(tile + 1) * t + r], c8)
        return pltpu.make_async_copy(obuf[slot].at[pl.ds(r * c8, c8), :], out_hbm.at[pl.ds(dst, c8), :],
                                     sem_s.at[slot])

    def whole_gather(slot):
        return pltpu.make_async_copy(x_hbm.at[pl.ds(0, t * c8), :], xbuf[slot], sem_g.at[slot])

    def whole_scatter(slot, row0):
        return pltpu.make_async_copy(obuf[slot], out_hbm.at[pl.ds(row0, t * c8), :], sem_s.at[slot])

    @pl.when(i == 0)
    def _():
        for slot in range(2):
            obuf[slot][...] = jnp.zeros((t * c8, LANES), F32)
        for slot in range(2):
            whole_scatter(slot, (m + slot * t) * c8).start()
        for slot in range(2):
            whole_scatter(slot, (m + slot * t) * c8).wait()

        def issue(r, carry):
            gather_row(0, 0, r).start()
            return carry
        lax.fori_loop(0, t, issue, 0)

    def step(cur, nxt):
        whole_gather(cur).wait()
        for r in range(t):
            gather_row(i + 1, nxt, r).start()
        for r in range(t):
            scatter_row(i - 1, nxt, r).start(priority=WRITE_BACK_DMA_PRIORITY)

        x = _read_tokens(xbuf[cur], True)
        h = _rms_rows(x, g_ref[...])
        lg = _router_logits(h, whi_ref[...], wlo_ref[...], b_ref[...])
        gates = _rows_to_lanes(_in_group_gates(lg, tgrp_ref[i]), t)
        hb = h.astype(BF)
        acc = jnp.zeros((t, D_MODEL), F32)
        for j in range(EXPERTS_PER_GROUP):
            a = _dot(hb, w1_ref[j])
            b = _dot(hb, w3_ref[j])
            hid = (a * jax.nn.sigmoid(a)) * b * gates[:, j:j + 1]
            acc = acc + _dot(hid.astype(BF), w2_ref[j])
        res = x + acc

        @pl.when(i >= 1)
        def _():
            whole_scatter(cur, 0).wait()

        _write_token_tiles(obuf[cur], res)

        @pl.when(i == last)
        def _():
            def issue(r, carry):
                scatter_row(i, cur, r).start()
                return carry
            lax.fori_loop(0, t, issue, 0)
            whole_scatter(cur, 0).wait()
            whole_scatter(nxt, 0).wait()
            whole_gather(nxt).wait()

    for parity in range(2):
        @pl.when(i % 2 == parity)
        def _():
            step(parity, 1 - parity)


def _sorted_moe(x, grp, m, li, g, whi, wlo, b, w1, w3, w2):
    t = MOE_TILE
    n_tiles = m // t + N_GROUPS
    src, dst, tile_grp = _sorted_plan(grp, m)
    group_w = lambda a, b_: pl.BlockSpec((None, EXPERTS_PER_GROUP, a, b_), lambda i, src_, dst_, tg: (li, tg[i], 0, 0))
    return pl.pallas_call(
        functools.partial(_sorted_moe_kernel, m),
        grid_spec=pltpu.PrefetchScalarGridSpec(
            num_scalar_prefetch=3,
            grid=(n_tiles,),
            in_specs=[pl.BlockSpec(memory_space=pl.ANY), _layer_spec(li, 1, D_MODEL),
                      _layer_spec(li, ROUTER_ROWS, D_MODEL), _layer_spec(li, ROUTER_ROWS, D_MODEL),
                      _layer_spec(li, ROUTER_ROWS, 1),
                      group_w(D_MODEL, D_EXPERT), group_w(D_MODEL, D_EXPERT), group_w(D_EXPERT, D_MODEL)],
            out_specs=pl.BlockSpec(memory_space=pl.ANY),
            scratch_shapes=[pltpu.VMEM((t * TT_CHUNKS, LANES), F32)] * 4
            + [pltpu.SemaphoreType.DMA((2,)), pltpu.SemaphoreType.DMA((2,))]),
        out_shape=jax.ShapeDtypeStruct(((m + 2 * t) * TT_CHUNKS, LANES), F32),
        compiler_params=_cparams("arbitrary"),
        name="moe_sorted",
    )(src, dst, tile_grp, x, g, whi, wlo, b, w1, w3, w2)


def _heads_kv_major_to_g_major(w, axis):
    shape = w.shape
    split = shape[:axis] + (KV_HEADS, Q_PER_KV, HEAD_DIM) + shape[axis + 1:]
    return jnp.swapaxes(w.reshape(split), axis, axis + 1).reshape(shape)


def _prep_ssm(a_re, a_im, log_dt, b_re, b_im, c_re, c_im, d_skip, w_glu):
    ar, ai = a_re.astype(F32), a_im.astype(F32)
    dt = jnp.exp(log_dt.astype(F32))[..., None]
    mag = jnp.exp(ar * dt)
    abr, abi = mag * jnp.cos(ai * dt), mag * jnp.sin(ai * dt)
    den = ar * ar + ai * ai
    kr = ((abr - 1.0) * ar + abi * ai) / den
    ki = (abi * ar - (abr - 1.0) * ai) / den
    br, bi = b_re.astype(F32), b_im.astype(F32)
    bbr = kr[..., None] * br - ki[..., None] * bi
    bbi = kr[..., None] * bi + ki[..., None] * br
    n_layers = ar.shape[0]
    eye = jnp.eye(SSM_GROUPS, dtype=F32)
    blockdiag_in = lambda b: jnp.einsum('egph,gk->eghkp', b, eye).reshape(n_layers, SSM_W, SSM_FLAT)
    blockdiag_out = lambda c: jnp.einsum('eghp,gk->egpkh', c, eye).reshape(n_layers, SSM_FLAT, SSM_W)
    in_tiles = lambda b: jnp.stack([b[:, j * SSM_IN_TILE:(j + 1) * SSM_IN_TILE, j * SSM_ST_TILE:(j + 1) * SSM_ST_TILE]
                                    for j in range(SSM_IN_TILES)], axis=1)
    out_tiles = lambda c: jnp.stack([c[:, j * SSM_OUT_ST:(j + 1) * SSM_OUT_ST, j * SSM_OUT_TILE:(j + 1) * SSM_OUT_TILE]
                                     for j in range(SSM_OUT_TILES)], axis=1)
    bcat = jnp.concatenate([in_tiles(blockdiag_in(bbr)), in_tiles(blockdiag_in(bbi))], axis=3).astype(BF)
    ccat = jnp.concatenate([out_tiles(blockdiag_out(c_re.astype(F32))), -out_tiles(blockdiag_out(c_im.astype(F32)))],
                           axis=2).astype(BF)
    return (abr.reshape(n_layers, 1, SSM_FLAT), abi.reshape(n_layers, 1, SSM_FLAT), bcat, ccat,
            d_skip.astype(F32).reshape(n_layers, 1, SSM_W), w_glu.astype(BF))


def _run_trunk(x, n, seq, mem_k, mem_v, caches, p):
    prompt = caches is None
    tm = 512
    m = n * seq
    wk_out, wv_out, hr_out, hi_out, cv_out = [], [], [], [], []
    xattn_args = (p['g_xattn'], p['w_xq'], p['g_xq'], mem_k, mem_v, p['w_xo'])
    router_args = (p['g_ffn'], p['w_router_hi'], p['w_router_lo'], p['b_router'])
    expert_args = (p['w_e1'], p['w_e3'], p['w_e2'])
    tiled = False
    for li in range(DEPTH):
        e = li // 2
        if li % 2 == 0:
            q, k, v, u = _even_in_proj(x, li, e, p['g_mix'], p['w_in_even'], p['g_q'], p['g_k'], p['bd'], tm, n, seq,
                                       token_tiled=tiled)
            if prompt:
                att = _swa_prompt(q, k, v, p['sinks'], e, n, seq)
                k3, v3 = k.reshape(n, seq, KV_W), v.reshape(n, seq, KV_W)
                new_k, new_v = k3[:, seq - WINDOW:], v3[:, seq - WINDOW:]
                ssm, h_re, h_im = _s5_glu(u.reshape(seq, n, SSM_W), None, p['ssm'], e, n, seq, 128)
                x, grp = _even_out_xattn(x, tiled, att, ssm.reshape(seq, n * SSM_W), e, p['w_out_att'], p['w_out_ssm'],
                                         li, xattn_args + router_args, n, seq, tm)
            else:
                win_k, win_v, ssm_re, ssm_im = caches
                att, new_k, new_v = _swa_sample(q, k, v, win_k, win_v, p['sinks'], e, n, seq)
                u_tm = jnp.swapaxes(u.reshape(n, seq, SSM_W), 0, 1)
                ssm, h_re, h_im = _s5_glu(u_tm, (ssm_re, ssm_im), p['ssm'], e, n, seq, seq)
                ssm = jnp.swapaxes(ssm, 0, 1).reshape(m, SSM_W)
                x = _even_out_proj(x, att, ssm, e, p['w_out_att'], p['w_out_ssm'], tm, n, seq)
            wk_out.append(new_k.reshape(n, WINDOW, KV_HEADS, HEAD_DIM))
            wv_out.append(new_v.reshape(n, WINDOW, KV_HEADS, HEAD_DIM))
            hr_out.append(h_re.reshape(n, SSM_GROUPS, SSM_STATE))
            hi_out.append(h_im.reshape(n, SSM_GROUPS, SSM_STATE))
        else:
            u, v = _odd_in_proj(x, m, li, e, p['g_mix'], p['w_in_odd'], p['g_v'], tm, BF if prompt else F32,
                                token_tiled=tiled)
            wsp, bsp, csize = p['gmlp_prompt'] if prompt else p['gmlp_sample']
            if prompt:
                x, grp = _gmlp_xattn(x, tiled, u, v, e, wsp, bsp, p['w_out_odd'], csize, li, xattn_args + router_args,
                                     n, seq, tm)
            else:
                x = _gmlp_mix(x, u, v, e, wsp, bsp, p['w_out_odd'], csize, tm)
                cv_out.append(v.reshape(n, seq, GM_W))
        if prompt:
            x = _sorted_moe(x, grp, m, li, *router_args, *expert_args)
            tiled = True
        else:
            x = _xattn_sample(x, li, *xattn_args, n, seq)
            x = _moe(x, li, *router_args, *expert_args, tm)
    if tiled:
        x = _untile(x, m, tm)
    cv = None if prompt else jnp.stack(cv_out)
    return x, jnp.stack(wk_out), jnp.stack(wv_out), jnp.stack(hr_out), jnp.stack(hi_out), cv


def kernel(x_prompt, x_sample, cache_win_k, cache_win_v, state_ssm_re, state_ssm_im, cache_mem_k, cache_mem_v, mem_prompt, g_mix, g_xattn, g_ffn, g_mem, w_in_even, g_q, g_k, sinks, ssm_a_re, ssm_a_im, ssm_log_dt, ssm_b_re, ssm_b_im, ssm_c_re, ssm_c_im, ssm_d, w_glu, w_out_even, w_in_odd, g_v, w_spatial, b_spatial, w_out_odd, w_xq, g_xq, w_xk, g_xk, w_xv, w_xo, w_coarse, b_coarse, w_fine, b_fine, w_e1, w_e3, w_e2):
    batch, seq = x_prompt.shape[0], x_prompt.shape[1]
    dec_batch, dec_seq = x_sample.shape[0], x_sample.shape[1]
    n_even = w_in_even.shape[0]

    row1 = lambda a: a.astype(F32)[:, None, :]
    p = {}
    p['g_mix'], p['g_xattn'], p['g_ffn'] = row1(g_mix), row1(g_xattn), row1(g_ffn)
    w_in_q = _heads_kv_major_to_g_major(w_in_even[:, :, :A_W], 2)
    p['w_in_even'] = jnp.concatenate([w_in_q, w_in_even[:, :, A_W:]], axis=2).astype(BF)
    p['g_q'] = jnp.tile(g_q.astype(F32), (1, A_HEADS))[:, None, :]
    p['g_k'] = jnp.tile(g_k.astype(F32), (1, KV_HEADS))[:, None, :]
    head_id = jnp.arange(A_W) // HEAD_DIM
    p['bd'] = (head_id[:, None] == head_id[None, :]).astype(BF)
    p['sinks'] = sinks.astype(F32).reshape(-1)
    p['ssm'] = _prep_ssm(ssm_a_re, ssm_a_im, ssm_log_dt, ssm_b_re, ssm_b_im, ssm_c_re, ssm_c_im, ssm_d, w_glu)
    p['w_out_att'] = _heads_kv_major_to_g_major(w_out_even[:, :A_W, :], 1).astype(BF)
    p['w_out_ssm'] = w_out_even[:, A_W:, :].astype(BF)
    p['w_in_odd'] = w_in_odd.astype(BF)
    p['g_v'] = row1(g_v)
    p['w_out_odd'] = w_out_odd.astype(BF)
    cs = min(dec_seq, CHUNK)
    reps = CHUNK // cs
    p['gmlp_prompt'] = (w_spatial.astype(F32), jnp.swapaxes(b_spatial.astype(F32), 1, 2), CHUNK)
    p['gmlp_sample'] = (jnp.tile(w_spatial[:, :, :cs, :cs].astype(F32), (1, 1, reps, reps)),
                        jnp.tile(jnp.swapaxes(b_spatial[:, :, :cs].astype(F32), 1, 2), (1, reps, 1)), cs)
    p['w_xq'] = w_xq.astype(BF)
    p['g_xq'] = row1(g_xq)
    p['w_xo'] = w_xo.astype(BF)
    w_router = jnp.swapaxes(jnp.concatenate([w_fine, w_coarse], axis=2).astype(F32), 1, 2)
    w_router = jnp.pad(w_router, ((0, 0), (0, ROUTER_ROWS - w_router.shape[1]), (0, 0)))
    p['w_router_hi'] = w_router.astype(BF)
    p['w_router_lo'] = (w_router - p['w_router_hi'].astype(F32)).astype(BF)
    b_router = jnp.concatenate([b_fine, b_coarse], axis=1).astype(F32)
    p['b_router'] = jnp.pad(b_router, ((0, 0), (0, ROUTER_ROWS - b_router.shape[1])))[:, :, None]
    p['w_e1'], p['w_e3'], p['w_e2'] = w_e1.astype(BF), w_e3.astype(BF), w_e2.astype(BF)

    mem2d = mem_prompt.reshape(batch * MEM_LEN, D_MODEL)
    pk, pv = _memory_kv(mem2d, row1(g_mem), w_xk.astype(BF), row1(g_xk), w_xv.astype(BF), batch)
    y_p, p_wk, p_wv, p_hr, p_hi, _ = _run_trunk(x_prompt.reshape(batch * seq, D_MODEL), batch, seq, pk, pv, None, p)

    caches = (cache_win_k.reshape(n_even, dec_batch, WINDOW, KV_W), cache_win_v.reshape(n_even, dec_batch, WINDOW, KV_W),
              state_ssm_re.reshape(n_even, dec_batch, SSM_FLAT), state_ssm_im.reshape(n_even, dec_batch, SSM_FLAT))
    y_s, s_wk, s_wv, s_hr, s_hi, s_cv = _run_trunk(x_sample.reshape(dec_batch * dec_seq, D_MODEL), dec_batch, dec_seq,
                                                   cache_mem_k.reshape(DEPTH, dec_batch, MEM_ROWS, X_HEAD_DIM),
                                                   cache_mem_v.reshape(DEPTH, dec_batch, MEM_ROWS, X_HEAD_DIM), caches, p)

    mem_shape = (DEPTH, batch, MEM_LEN, X_HEADS, X_HEAD_DIM)
    return (y_p.reshape(batch, seq, D_MODEL), y_s.reshape(dec_batch, dec_seq, D_MODEL), p_wk, p_wv, p_hr, p_hi,
            pk.reshape(mem_shape), pv.reshape(mem_shape), s_wk, s_wv, s_hr, s_hi, s_cv)
```

```python
import functools
import math

import jax
import jax.numpy as jnp
from jax import lax
from jax.experimental import pallas as pl
from jax.experimental.pallas import tpu as pltpu
from jax.experimental.pallas import tpu_sc as plsc

F32 = jnp.float32
BF = jnp.bfloat16

D_MODEL = 1024
DEPTH = 4
A_W = 512
HEAD_DIM = 64
A_HEADS = 8
KV_HEADS = 2
Q_PER_KV = 4
KV_W = 128
WINDOW = 128
SSM_W = 512
SSM_CH = 16
SSM_GROUPS = 32
SSM_STATE = 64
SSM_FLAT = SSM_GROUPS * SSM_STATE
IN_EVEN = A_W + 2 * KV_W + SSM_W
CHUNK = 128
GM_W = 2048
GM_HEADS = 8
GM_HD = 256
MEM_LEN = 256
X_HEADS = 4
X_HEAD_DIM = 128
X_W = 512
N_GROUPS = 4
EXPERTS_PER_GROUP = 4
N_EXPERTS = 16
D_EXPERT = 256
EPS = 1e-6
NEG_INF = -1e30
LANES = 128
VMEM_LIMIT_BYTES = 52 * 1024 * 1024
ALIBI_SLOPES = tuple(2.0 ** (-8.0 * (h + 1) / A_HEADS) for h in range(A_HEADS))
SSM_SEQ_BLOCK = 8


def _cparams(*sem):
    return pltpu.CompilerParams(dimension_semantics=sem, vmem_limit_bytes=VMEM_LIMIT_BYTES)


def _rms_rows(x, g):
    return x * lax.rsqrt(jnp.mean(x * x, axis=-1, keepdims=True) + EPS) * g


TT_CHUNKS = D_MODEL // LANES


def _read_tokens(ref, token_tiled):
    if not token_tiled:
        return ref[...]
    rows = ref.shape[0] // TT_CHUNKS
    return jnp.concatenate([ref[pl.ds(c, rows, stride=TT_CHUNKS), :] for c in range(TT_CHUNKS)], axis=1)


def _write_token_tiles(ref, val):
    rows = val.shape[0]
    for c in range(TT_CHUNKS):
        ref[pl.ds(c, rows, stride=TT_CHUNKS), :] = val[:, c * LANES:(c + 1) * LANES]


def _token_spec(rows, index, token_tiled):
    if token_tiled:
        return pl.BlockSpec((rows * TT_CHUNKS, LANES), lambda *ids: (index(*ids), 0))
    return pl.BlockSpec((rows, D_MODEL), lambda *ids: (index(*ids), 0))


def _untile_kernel(x_ref, o_ref):
    o_ref[...] = _read_tokens(x_ref, True)


def _untile(x, m, tm):
    return pl.pallas_call(
        _untile_kernel,
        grid=(m // tm,),
        in_specs=[_token_spec(tm, lambda i: i, True)],
        out_specs=pl.BlockSpec((tm, D_MODEL), lambda i: (i, 0)),
        out_shape=jax.ShapeDtypeStruct((m, D_MODEL), F32),
        compiler_params=_cparams("parallel"),
        name="untile",
    )(x)


def _dot(a, b):
    return jnp.dot(a, b, preferred_element_type=F32)


def _dot_nt(a, b):
    return lax.dot_general(a, b, (((1,), (1,)), ((), ())), preferred_element_type=F32)


def _full(shape):
    nd = len(shape)
    return pl.BlockSpec(shape, lambda *_: (0,) * nd)


def _layer_spec(li, *shape, single_buffer=False):
    nd = len(shape)
    mode = pl.Buffered(1) if single_buffer else None
    return pl.BlockSpec((None,) + shape, lambda *_: (li,) + (0,) * nd, pipeline_mode=mode)


def _even_in_kernel(token_tiled, x_ref, g_ref, w_ref, gq_ref, gk_ref, bd_ref, q_ref, k_ref, v_ref, u_ref):
    h = _rms_rows(_read_tokens(x_ref, token_tiled), g_ref[...]).astype(BF)
    z = _dot(h, w_ref[...])
    q = z[:, :A_W]
    k = z[:, A_W:A_W + KV_W]
    bd = bd_ref[...]
    qms = _dot((q * q).astype(BF), bd) * (1.0 / HEAD_DIM)
    kms = _dot((k * k).astype(BF), bd[:KV_W, :KV_W]) * (1.0 / HEAD_DIM)
    q_ref[...] = q * lax.rsqrt(qms + EPS) * gq_ref[...]
    k_ref[...] = k * lax.rsqrt(kms + EPS) * gk_ref[...]
    v_ref[...] = z[:, A_W + KV_W:A_W + 2 * KV_W]
    u_ref[...] = z[:, A_W + 2 * KV_W:]


def _time_major_spec(tm, n, seq, width):
    per_seq = seq // tm
    return pl.BlockSpec((tm, width), lambda i: (i % per_seq, i // per_seq))


def _even_in_proj(x, li, e, g, w, gq, gk, bd, tm, n, seq, token_tiled=False):
    m = n * seq
    row = lambda w_: pl.BlockSpec((tm, w_), lambda i: (i, 0))
    return pl.pallas_call(
        functools.partial(_even_in_kernel, token_tiled),
        grid=(m // tm,),
        in_specs=[_token_spec(tm, lambda i: i, token_tiled), _layer_spec(li, 1, D_MODEL),
                  _layer_spec(e, D_MODEL, IN_EVEN), _layer_spec(e, 1, A_W),
                  _layer_spec(e, 1, KV_W), _full((A_W, A_W))],
        out_specs=[row(A_W), row(KV_W), row(KV_W), _time_major_spec(tm, n, seq, SSM_W) if seq % tm == 0 else row(SSM_W)],
        out_shape=[jax.ShapeDtypeStruct((m, A_W), F32), jax.ShapeDtypeStruct((m, KV_W), F32),
                   jax.ShapeDtypeStruct((m, KV_W), F32),
                   jax.ShapeDtypeStruct((seq, n * SSM_W) if seq % tm == 0 else (m, SSM_W), F32)],
        compiler_params=_cparams("parallel"),
        name="even_in_proj",
    )(x, g, w, gq, gk, bd)


def _head_rows(q, lane_lo):
    rows = []
    for kv in range(KV_HEADS):
        keep = lane_lo if kv == 0 else jnp.logical_not(lane_lo)
        for g in range(Q_PER_KV):
            rows.append(jnp.where(keep, q[:, g * LANES:(g + 1) * LANES], 0.0))
    return jnp.concatenate(rows, axis=0)


def _head_cols(o, r, lane_lo):
    cols = []
    for g in range(Q_PER_KV):
        cols.append(jnp.where(lane_lo, o[g * r:(g + 1) * r], o[(Q_PER_KV + g) * r:(Q_PER_KV + g + 1) * r]))
    return jnp.concatenate(cols, axis=1)


def _swa_prompt_kernel(e, sinks_ref, q_ref, kc_ref, kp_ref, vc_ref, vp_ref, o_ref):
    b = pl.program_id(1)
    lane_lo = lax.broadcasted_iota(jnp.int32, (1, LANES), 1) < HEAD_DIM
    qrows = _head_rows(q_ref[...], lane_lo).astype(BF)
    kcat = jnp.concatenate([kp_ref[...], kc_ref[...]], axis=0).astype(BF)
    vcat = jnp.concatenate([vp_ref[...], vc_ref[...]], axis=0).astype(BF)
    s = _dot_nt(qrows, kcat) * (HEAD_DIM ** -0.5)
    r = lax.broadcasted_iota(jnp.int32, (WINDOW, 2 * WINDOW), 0)
    c = lax.broadcasted_iota(jnp.int32, (WINDOW, 2 * WINDOW), 1)
    dist = WINDOW + r - c
    valid = (dist >= 0) & (dist <= WINDOW) & ((c >= WINDOW) | (b > 0))
    dist_f = dist.astype(F32)
    ps = []
    for h in range(A_HEADS):
        sh = s[h * WINDOW:(h + 1) * WINDOW] - ALIBI_SLOPES[h] * dist_f
        sh = jnp.where(valid, sh, NEG_INF)
        sk = sinks_ref[e * A_HEADS + h]
        m = jnp.maximum(jnp.max(sh, axis=-1, keepdims=True), sk)
        p = jnp.exp(sh - m)
        den = jnp.sum(p, axis=-1, keepdims=True) + jnp.exp(sk - m)
        ps.append((p / den).astype(BF))
    o = _dot(jnp.concatenate(ps, axis=0), vcat)
    o_ref[...] = _head_cols(o, WINDOW, lane_lo).astype(o_ref.dtype)


def _swa_prompt(q, k, v, sinks, e, n, seq):
    nb = seq // WINDOW
    cur = lambda w_: pl.BlockSpec((WINDOW, w_), lambda i, b: (i * nb + b, 0))
    prev = lambda w_: pl.BlockSpec((WINDOW, w_), lambda i, b: (i * nb + jnp.maximum(b - 1, 0), 0))
    return pl.pallas_call(
        functools.partial(_swa_prompt_kernel, e),
        grid=(n, nb),
        in_specs=[pl.BlockSpec(memory_space=pltpu.SMEM), cur(A_W), cur(KV_W), prev(KV_W), cur(KV_W), prev(KV_W)],
        out_specs=cur(A_W),
        out_shape=jax.ShapeDtypeStruct((n * seq, A_W), BF),
        compiler_params=_cparams("parallel", "arbitrary"),
        name="swa_prompt",
    )(sinks, q, k, k, v, v)


SWA_SEQ_BLOCK = 8


def _swa_sample_kernel(t_new, e, sinks_ref, q_ref, kn_ref, vn_ref, wk_ref, wv_ref, o_ref, nwk_ref, nwv_ref):
    nseq = SWA_SEQ_BLOCK
    pair_rows = 2 * t_new
    npair = nseq // 2
    kn = kn_ref[...]
    vn = vn_ref[...]
    for i in range(nseq):
        nwk_ref[i, 0:WINDOW - t_new, :] = wk_ref[i, t_new:WINDOW, :]
        nwk_ref[i, WINDOW - t_new:WINDOW, :] = kn[i * t_new:(i + 1) * t_new, :]
        nwv_ref[i, 0:WINDOW - t_new, :] = wv_ref[i, t_new:WINDOW, :]
        nwv_ref[i, WINDOW - t_new:WINDOW, :] = vn[i * t_new:(i + 1) * t_new, :]

    lane_lo = lax.broadcasted_iota(jnp.int32, (1, LANES), 1) < HEAD_DIM
    q = q_ref[...]
    knb = kn.astype(BF)
    vnb = vn.astype(BF)
    hr = A_HEADS * pair_rows
    qrows = [_head_rows(q[j * pair_rows:(j + 1) * pair_rows], lane_lo).astype(BF) for j in range(npair)]
    scale = HEAD_DIM ** -0.5
    s_new_all = _dot_nt(jnp.concatenate(qrows, axis=0), knb) * scale

    tq = lax.broadcasted_iota(jnp.int32, (pair_rows, 2 * WINDOW), 0)
    cw = lax.broadcasted_iota(jnp.int32, (pair_rows, 2 * WINDOW), 1)
    dist_w = WINDOW + (tq % t_new) - (cw % WINDOW)
    valid_w = ((tq // t_new) == (cw // WINDOW)) & (dist_w <= WINDOW)
    dist_wf = dist_w.astype(F32)
    tqn = lax.broadcasted_iota(jnp.int32, (pair_rows, nseq * t_new), 0)
    cn = lax.broadcasted_iota(jnp.int32, (pair_rows, nseq * t_new), 1)
    dist_n = (tqn % t_new) - (cn % t_new)
    dist_nf = dist_n.astype(F32)

    p_new_all = []
    o_win_all = []
    for j in range(npair):
        kwin = jnp.concatenate([wk_ref[2 * j], wk_ref[2 * j + 1]], axis=0).astype(BF)
        vwin = jnp.concatenate([wv_ref[2 * j], wv_ref[2 * j + 1]], axis=0).astype(BF)
        s_win = _dot_nt(qrows[j], kwin) * scale
        valid_n = ((2 * j + tqn // t_new) == (cn // t_new)) & (dist_n >= 0)
        p_win = []
        for h in range(A_HEADS):
            sw = jnp.where(valid_w, s_win[h * pair_rows:(h + 1) * pair_rows] - ALIBI_SLOPES[h] * dist_wf, NEG_INF)
            sn = s_new_all[j * hr + h * pair_rows:j * hr + (h + 1) * pair_rows]
            sn = jnp.where(valid_n, sn - ALIBI_SLOPES[h] * dist_nf, NEG_INF)
            sk = sinks_ref[e * A_HEADS + h]
            m = jnp.maximum(jnp.maximum(jnp.max(sw, axis=-1, keepdims=True), jnp.max(sn, axis=-1, keepdims=True)), sk)
            pw = jnp.exp(sw - m)
            pn = jnp.exp(sn - m)
            den = jnp.sum(pw, axis=-1, keepdims=True) + jnp.sum(pn, axis=-1, keepdims=True) + jnp.exp(sk - m)
            p_win.append((pw / den).astype(BF))
            p_new_all.append((pn / den).astype(BF))
        o_win_all.append(_dot(jnp.concatenate(p_win, axis=0), vwin))
    o_new = _dot(jnp.concatenate(p_new_all, axis=0), vnb)
    outs = [_head_cols(o_win_all[j] + o_new[j * hr:(j + 1) * hr], pair_rows, lane_lo) for j in range(npair)]
    o_ref[...] = jnp.concatenate(outs, axis=0).astype(o_ref.dtype)


def _swa_sample(q, kn, vn, win_k, win_v, sinks, e, n, t_new):
    rows = SWA_SEQ_BLOCK * t_new
    row = lambda w_: pl.BlockSpec((rows, w_), lambda i: (i, 0))
    win = pl.BlockSpec((SWA_SEQ_BLOCK, WINDOW, KV_W), lambda i: (i, 0, 0))
    win_in = pl.BlockSpec((None, SWA_SEQ_BLOCK, WINDOW, KV_W), lambda i: (e, i, 0, 0))
    return pl.pallas_call(
        functools.partial(_swa_sample_kernel, t_new, e),
        grid=(n // SWA_SEQ_BLOCK,),
        in_specs=[pl.BlockSpec(memory_space=pltpu.SMEM), row(A_W), row(KV_W), row(KV_W), win_in, win_in],
        out_specs=[row(A_W), win, win],
        out_shape=[jax.ShapeDtypeStruct((n * t_new, A_W), BF),
                   jax.ShapeDtypeStruct((n, WINDOW, KV_W), F32), jax.ShapeDtypeStruct((n, WINDOW, KV_W), F32)],
        compiler_params=_cparams("parallel"),
        name="swa_sample",
    )(sinks, q, kn, vn, win_k, win_v)


SSM_LANE_CHUNK = 1024
SSM_IN_TILE = LANES
SSM_IN_TILES = SSM_W // SSM_IN_TILE
SSM_ST_TILE = SSM_IN_TILE // SSM_CH * SSM_STATE
SSM_OUT_TILE = 256
SSM_OUT_TILES = SSM_W // SSM_OUT_TILE
SSM_OUT_ST = SSM_OUT_TILE // SSM_CH * SSM_STATE


def _s5_kernel(tc, has_h0, *refs):
    if has_h0:
        (u_ref, h0r_ref, h0i_ref, abr_ref, abi_ref, bcat_ref, ccat_ref, d_ref, wglu_ref,
         o_ref, hr_ref, hi_ref, st_scr, car_scr) = refs
    else:
        (u_ref, abr_ref, abi_ref, bcat_ref, ccat_ref, d_ref, wglu_ref,
         o_ref, hr_ref, hi_ref, st_scr, car_scr) = refs
    nb = SSM_SEQ_BLOCK
    ci = pl.program_id(1)

    @pl.when(ci == 0)
    def _():
        if has_h0:
            car_scr[:, :SSM_FLAT] = h0r_ref[...]
            car_scr[:, SSM_FLAT:] = h0i_ref[...]
        else:
            car_scr[...] = jnp.zeros_like(car_scr)

    ut = u_ref[...].reshape(tc * nb, SSM_W)
    ub = ut.astype(BF)
    for jt in range(SSM_IN_TILES):
        bu = _dot(ub[:, jt * SSM_IN_TILE:(jt + 1) * SSM_IN_TILE], bcat_ref[jt])
        st_scr[:, jt * SSM_ST_TILE:(jt + 1) * SSM_ST_TILE] = bu[:, :SSM_ST_TILE]
        st_scr[:, SSM_FLAT + jt * SSM_ST_TILE:SSM_FLAT + (jt + 1) * SSM_ST_TILE] = bu[:, SSM_ST_TILE:]

    for lc in range(SSM_FLAT // SSM_LANE_CHUNK):
        lo = lc * SSM_LANE_CHUNK
        re_sl = slice(lo, lo + SSM_LANE_CHUNK)
        im_sl = slice(SSM_FLAT + lo, SSM_FLAT + lo + SSM_LANE_CHUNK)
        ar = jnp.broadcast_to(abr_ref[:, re_sl], (nb, SSM_LANE_CHUNK))
        ai = jnp.broadcast_to(abi_ref[:, re_sl], (nb, SSM_LANE_CHUNK))

        def step(t, carry):
            hr, hi = carry
            rows = pl.ds(pl.multiple_of(t * nb, nb), nb)
            nr = ar * hr - ai * hi + st_scr[rows, re_sl]
            ni = ar * hi + ai * hr + st_scr[rows, im_sl]
            st_scr[rows, re_sl] = nr
            st_scr[rows, im_sl] = ni
            return nr, ni

        hr, hi = lax.fori_loop(0, tc, step, (car_scr[:, re_sl], car_scr[:, im_sl]))
        car_scr[:, re_sl] = hr
        car_scr[:, im_sl] = hi

    ys = []
    for ot in range(SSM_OUT_TILES):
        re_sl = slice(ot * SSM_OUT_ST, (ot + 1) * SSM_OUT_ST)
        im_sl = slice(SSM_FLAT + ot * SSM_OUT_ST, SSM_FLAT + (ot + 1) * SSM_OUT_ST)
        ys.append(_dot(st_scr[:, re_sl].astype(BF), ccat_ref[ot, :SSM_OUT_ST, :])
                  + _dot(st_scr[:, im_sl].astype(BF), ccat_ref[ot, SSM_OUT_ST:, :]))
    y = jnp.concatenate(ys, axis=1) + d_ref[...] * ut
    y = jax.nn.gelu(y).astype(BF)
    g = _dot(y, wglu_ref[...])
    o_ref[...] = (g[:, :SSM_W] * jax.nn.sigmoid(g[:, SSM_W:])).reshape(tc, nb, SSM_W)

    @pl.when(ci == pl.num_programs(1) - 1)
    def _():
        hr_ref[...] = car_scr[:, :SSM_FLAT]
        hi_ref[...] = car_scr[:, SSM_FLAT:]


def _s5_glu(u, h0, ssm, e, n, seq, tc):
    nb = SSM_SEQ_BLOCK
    abr, abi, bcat, ccat, dsk, wglu = ssm
    u_spec = pl.BlockSpec((tc, nb, SSM_W), lambda i, c: (c, i, 0))
    st_spec = pl.BlockSpec((nb, SSM_FLAT), lambda i, c: (i, 0))
    h0_spec = pl.BlockSpec((None, nb, SSM_FLAT), lambda i, c: (e, i, 0))
    consts = [_layer_spec(e, 1, SSM_FLAT), _layer_spec(e, 1, SSM_FLAT),
              _layer_spec(e, SSM_IN_TILES, SSM_IN_TILE, 2 * SSM_ST_TILE),
              _layer_spec(e, SSM_OUT_TILES, 2 * SSM_OUT_ST, SSM_OUT_TILE), _layer_spec(e, 1, SSM_W),
              _layer_spec(e, SSM_W, 2 * SSM_W)]
    has_h0 = h0 is not None
    in_specs = [u_spec] + ([h0_spec, h0_spec] if has_h0 else []) + consts
    args = [u] + (list(h0) if has_h0 else []) + [abr, abi, bcat, ccat, dsk, wglu]
    return pl.pallas_call(
        functools.partial(_s5_kernel, tc, has_h0),
        grid=(n // nb, seq // tc),
        in_specs=in_specs,
        out_specs=[u_spec, st_spec, st_spec],
        out_shape=[jax.ShapeDtypeStruct((seq, n, SSM_W), F32), jax.ShapeDtypeStruct((n, SSM_FLAT), F32),
                   jax.ShapeDtypeStruct((n, SSM_FLAT), F32)],
        scratch_shapes=[pltpu.VMEM((nb * tc, 2 * SSM_FLAT), F32), pltpu.VMEM((nb, 2 * SSM_FLAT), F32)],
        compiler_params=_cparams("parallel", "arbitrary"),
        name="s5_glu",
    )(*args)


def _even_out_kernel(x_ref, a_ref, s_ref, wa_ref, ws_ref, o_ref):
    o_ref[...] = x_ref[...] + _dot(a_ref[...], wa_ref[...]) + _dot(s_ref[...].astype(BF), ws_ref[...])


def _even_out_proj(x, att, ssm, e, wa, ws, tm, n, seq):
    m = x.shape[0]
    row = lambda w_: pl.BlockSpec((tm, w_), lambda i: (i, 0))
    ssm_spec = _time_major_spec(tm, n, seq, SSM_W) if seq % tm == 0 else row(SSM_W)
    return pl.pallas_call(
        _even_out_kernel,
        grid=(m // tm,),
        in_specs=[row(D_MODEL), row(A_W), ssm_spec, _layer_spec(e, A_W, D_MODEL), _layer_spec(e, SSM_W, D_MODEL)],
        out_specs=row(D_MODEL),
        out_shape=jax.ShapeDtypeStruct((m, D_MODEL), F32),
        compiler_params=_cparams("parallel"),
        name="even_out_proj",
    )(x, att, ssm, wa, ws)


def _odd_in_kernel(token_tiled, x_ref, g_ref, w_ref, gv_ref, u_ref, v_ref):
    h = _rms_rows(_read_tokens(x_ref, token_tiled), g_ref[...]).astype(BF)
    u_ref[...] = jax.nn.gelu(_dot(h, w_ref[:, :GM_W])).astype(u_ref.dtype)
    zv = jax.nn.gelu(_dot(h, w_ref[:, GM_W:]))
    v_ref[...] = _rms_rows(zv, gv_ref[...]).astype(v_ref.dtype)


def _odd_in_proj(x, m, li, e, g, w, gv, tm, v_dtype, token_tiled=False):
    row = lambda w_: pl.BlockSpec((tm, w_), lambda i: (i, 0))
    return pl.pallas_call(
        functools.partial(_odd_in_kernel, token_tiled),
        grid=(m // tm,),
        in_specs=[_token_spec(tm, lambda i: i, token_tiled), _layer_spec(li, 1, D_MODEL),
                  _layer_spec(e, D_MODEL, 2 * GM_W, single_buffer=True),
                  _layer_spec(e, 1, GM_W)],
        out_specs=[row(GM_W), row(GM_W)],
        out_shape=[jax.ShapeDtypeStruct((m, GM_W), BF), jax.ShapeDtypeStruct((m, GM_W), v_dtype)],
        compiler_params=_cparams("parallel"),
        name="odd_in_proj",
    )(x, g, w, gv)


def _gmlp_rows(csize, x, u_ref, v_ref, wsp_ref, bsp_ref, wo_ref, gated_scr):
    tm = u_ref.shape[0]
    i = lax.broadcasted_iota(jnp.int32, (CHUNK, CHUNK), 0)
    j = lax.broadcasted_iota(jnp.int32, (CHUNK, CHUNK), 1)
    keep = (j <= i) & ((i // csize) == (j // csize))
    bsp = bsp_ref[...]
    for h in range(GM_HEADS):
        ws = jnp.where(keep, wsp_ref[h], 0.0).astype(BF)
        b_col = bsp[:, h:h + 1]
        cols = slice(h * GM_HD, (h + 1) * GM_HD)
        for c in range(tm // CHUNK):
            rows = slice(c * CHUNK, (c + 1) * CHUNK)
            mix = _dot(ws, v_ref[rows, cols].astype(BF)) + b_col
            gated_scr[rows, cols] = (u_ref[rows, cols].astype(F32) * mix).astype(BF)
    return x + _dot(gated_scr[...], wo_ref[...])


def _gmlp_mix_kernel(csize, x_ref, u_ref, v_ref, wsp_ref, bsp_ref, wo_ref, o_ref, gated_scr):
    o_ref[...] = _gmlp_rows(csize, x_ref[...], u_ref, v_ref, wsp_ref, bsp_ref, wo_ref, gated_scr)


def _gmlp_mix(x, u, v, e, wsp, bsp, wo, csize, tm):
    m = x.shape[0]
    row = lambda w_: pl.BlockSpec((tm, w_), lambda i: (i, 0))
    return pl.pallas_call(
        functools.partial(_gmlp_mix_kernel, csize),
        grid=(m // tm,),
        in_specs=[row(D_MODEL), row(GM_W), row(GM_W), _layer_spec(e, GM_HEADS, CHUNK, CHUNK),
                  _layer_spec(e, CHUNK, GM_HEADS), _layer_spec(e, GM_W, D_MODEL)],
        out_specs=row(D_MODEL),
        out_shape=jax.ShapeDtypeStruct((m, D_MODEL), F32),
        scratch_shapes=[pltpu.VMEM((tm, GM_W), BF)],
        compiler_params=_cparams("parallel"),
        name="gmlp_mix",
    )(x, u, v, wsp, bsp, wo)


def _head_norm(z, g):
    cols = []
    for h in range(X_HEADS):
        zh = z[:, h * X_HEAD_DIM:(h + 1) * X_HEAD_DIM]
        cols.append(_rms_rows(zh, g))
    return jnp.concatenate(cols, axis=1)


MEMKV_SEQ_BLOCK = 2
MEM_ROWS = MEM_LEN * X_HEADS


def _head_rows_of(h):
    return pl.ds(h, MEM_LEN, stride=X_HEADS)


def _memory_kv_kernel(mem_ref, g_ref, wk_ref, gk_ref, wv_ref, k_ref, v_ref):
    m = _rms_rows(mem_ref[...], g_ref[...]).astype(BF)
    k = _head_norm(_dot(m, wk_ref[...]), gk_ref[...])
    v = _dot(m, wv_ref[...])
    for s in range(MEMKV_SEQ_BLOCK):
        rows = slice(s * MEM_LEN, (s + 1) * MEM_LEN)
        for h in range(X_HEADS):
            cols = slice(h * X_HEAD_DIM, (h + 1) * X_HEAD_DIM)
            k_ref[s, _head_rows_of(h), :] = k[rows, cols]
            v_ref[s, _head_rows_of(h), :] = v[rows, cols]


def _memory_kv(mem, g_mem, w_k, g_k, w_v, n):
    tm = MEMKV_SEQ_BLOCK * MEM_LEN
    per_layer = lambda a, b: pl.BlockSpec((None, a, b), lambda l, i: (l, 0, 0))
    out_spec = pl.BlockSpec((None, MEMKV_SEQ_BLOCK, MEM_ROWS, X_HEAD_DIM), lambda l, i: (l, i, 0, 0))
    out_sds = jax.ShapeDtypeStruct((DEPTH, n, MEM_ROWS, X_HEAD_DIM), F32)
    return pl.pallas_call(
        _memory_kv_kernel,
        grid=(DEPTH, n // MEMKV_SEQ_BLOCK),
        in_specs=[pl.BlockSpec((tm, D_MODEL), lambda l, i: (i, 0)), per_layer(1, D_MODEL), per_layer(D_MODEL, X_W),
                  per_layer(1, X_HEAD_DIM), per_layer(D_MODEL, X_W)],
        out_specs=[out_spec, out_spec],
        out_shape=[out_sds, out_sds],
        compiler_params=_cparams("parallel", "parallel"),
        name="memory_kv",
    )(mem, g_mem, w_k, g_k, w_v)


def _xattn_rows(x, g_ref, wq_ref, gq_ref, mk_ref, mv_ref, wo_ref):
    q = _head_norm(_dot(_rms_rows(x, g_ref[...]).astype(BF), wq_ref[...]), gq_ref[...]).astype(BF)
    outs = []
    for h in range(X_HEADS):
        cols = slice(h * X_HEAD_DIM, (h + 1) * X_HEAD_DIM)
        s = _dot_nt(q[:, cols], mk_ref[_head_rows_of(h), :].astype(BF)) * (X_HEAD_DIM ** -0.5)
        m = jnp.max(s, axis=-1, keepdims=True)
        p = jnp.exp(s - m)
        p = (p / jnp.sum(p, axis=-1, keepdims=True)).astype(BF)
        outs.append(_dot(p, mv_ref[_head_rows_of(h), :].astype(BF)))
    o = jnp.concatenate(outs, axis=1).astype(BF)
    return x + _dot(o, wo_ref[...])


N_XATTN_REFS = 6
N_ROUTE_REFS = 4
GROUP_ROW = 0


def _xattn_route_store(x, refs, o_ref, grp_ref):
    y = _xattn_rows(x, *refs[:N_XATTN_REFS])
    _write_token_tiles(o_ref, y)
    gf_ref, whi_ref, wlo_ref, b_ref = refs[N_XATTN_REFS:]
    grp = _top_group(_router_logits(_rms_rows(y, gf_ref[...]), whi_ref[...], wlo_ref[...], b_ref[...]))
    grp_ref[...] = jnp.concatenate([grp, jnp.zeros((7, grp.shape[1]), jnp.int32)], axis=0)


def _even_out_xattn_kernel(token_tiled, x_ref, a_ref, s_ref, wa_ref, ws_ref, *rest):
    refs, o_ref, grp_ref = rest[:-2], rest[-2], rest[-1]
    x = (_read_tokens(x_ref, token_tiled) + _dot(a_ref[...], wa_ref[...])
         + _dot(s_ref[...].astype(BF), ws_ref[...]))
    _xattn_route_store(x, refs, o_ref, grp_ref)


def _gmlp_xattn_kernel(csize, token_tiled, x_ref, u_ref, v_ref, wsp_ref, bsp_ref, wo_ref, *rest):
    refs, o_ref, grp_ref, gated_scr = rest[:-3], rest[-3], rest[-2], rest[-1]
    x = _gmlp_rows(csize, _read_tokens(x_ref, token_tiled), u_ref, v_ref, wsp_ref, bsp_ref, wo_ref, gated_scr)
    _xattn_route_store(x, refs, o_ref, grp_ref)


def _xattn_route_specs(li):
    mem = pl.BlockSpec((None, None, MEM_ROWS, X_HEAD_DIM), lambda i, j: (li, i, 0, 0))
    return [_layer_spec(li, 1, D_MODEL), _layer_spec(li, D_MODEL, X_W), _layer_spec(li, 1, X_HEAD_DIM), mem, mem,
            _layer_spec(li, X_W, D_MODEL), _layer_spec(li, 1, D_MODEL), _layer_spec(li, ROUTER_ROWS, D_MODEL),
            _layer_spec(li, ROUTER_ROWS, D_MODEL), _layer_spec(li, ROUTER_ROWS, 1)]


def _token_tile_outputs(n, seq, tq):
    nq = seq // tq
    specs = [_token_spec(tq, lambda i, j: i * nq + j, True), pl.BlockSpec((None, 8, tq), lambda i, j: (i * nq + j, 0, 0))]
    shapes = [jax.ShapeDtypeStruct((n * seq * TT_CHUNKS, LANES), F32), jax.ShapeDtypeStruct((n * nq, 8, tq), jnp.int32)]
    return specs, shapes


def _even_out_xattn(x, token_tiled, att, ssm, e, wa, ws, li, xattn_route_args, n, seq, tq):
    nq = seq // tq
    row = lambda w_: pl.BlockSpec((tq, w_), lambda i, j: (i * nq + j, 0))
    ssm_spec = pl.BlockSpec((tq, SSM_W), lambda i, j: (j, i))
    out_specs, out_shape = _token_tile_outputs(n, seq, tq)
    y, grp = pl.pallas_call(
        functools.partial(_even_out_xattn_kernel, token_tiled),
        grid=(n, nq),
        in_specs=[_token_spec(tq, lambda i, j: i * nq + j, token_tiled), row(A_W), ssm_spec,
                  _layer_spec(e, A_W, D_MODEL), _layer_spec(e, SSM_W, D_MODEL)] + _xattn_route_specs(li),
        out_specs=out_specs,
        out_shape=out_shape,
        compiler_params=_cparams("parallel", "arbitrary"),
        name="even_out_xattn",
    )(x, att, ssm, wa, ws, *xattn_route_args)
    return y, grp[:, GROUP_ROW, :].reshape(n * seq)


def _gmlp_xattn(x, token_tiled, u, v, e, wsp, bsp, wo, csize, li, xattn_route_args, n, seq, tq):
    nq = seq // tq
    row = lambda w_: pl.BlockSpec((tq, w_), lambda i, j: (i * nq + j, 0))
    out_specs, out_shape = _token_tile_outputs(n, seq, tq)
    y, grp = pl.pallas_call(
        functools.partial(_gmlp_xattn_kernel, csize, token_tiled),
        grid=(n, nq),
        in_specs=[_token_spec(tq, lambda i, j: i * nq + j, token_tiled), row(GM_W), row(GM_W),
                  _layer_spec(e, GM_HEADS, CHUNK, CHUNK), _layer_spec(e, CHUNK, GM_HEADS),
                  _layer_spec(e, GM_W, D_MODEL)] + _xattn_route_specs(li),
        out_specs=out_specs,
        out_shape=out_shape,
        scratch_shapes=[pltpu.VMEM((tq, GM_W), BF)],
        compiler_params=_cparams("parallel", "arbitrary"),
        name="gmlp_xattn",
    )(x, u, v, wsp, bsp, wo, *xattn_route_args)
    return y, grp[:, GROUP_ROW, :].reshape(n * seq)


XATTN_SEQ_BLOCK = 8


def _xattn_sample_kernel(t_new, x_ref, g_ref, wq_ref, gq_ref, mk_ref, mv_ref, wo_ref, o_ref):
    pair_rows = 2 * t_new
    x = x_ref[...]
    q = _head_norm(_dot(_rms_rows(x, g_ref[...]).astype(BF), wq_ref[...]), gq_ref[...]).astype(BF)
    tq = lax.broadcasted_iota(jnp.int32, (pair_rows, 2 * MEM_LEN), 0)
    cm = lax.broadcasted_iota(jnp.int32, (pair_rows, 2 * MEM_LEN), 1)
    same_seq = (tq // t_new) == (cm // MEM_LEN)
    npair = XATTN_SEQ_BLOCK // 2
    scores = []
    for j in range(npair):
        qp = q[j * pair_rows:(j + 1) * pair_rows]
        for h in range(X_HEADS):
            hr = _head_rows_of(h)
            kp = jnp.concatenate([mk_ref[2 * j, hr, :], mk_ref[2 * j + 1, hr, :]], axis=0).astype(BF)
            s = _dot_nt(qp[:, h * X_HEAD_DIM:(h + 1) * X_HEAD_DIM], kp)
            scores.append(jnp.where(same_seq, s, NEG_INF))
    s = jnp.concatenate(scores, axis=0) * (X_HEAD_DIM ** -0.5)
    m = jnp.max(s, axis=-1, keepdims=True)
    p = jnp.exp(s - m)
    p = p / jnp.sum(p, axis=-1, keepdims=True)
    outs = []
    for j in range(npair):
        heads = []
        for h in range(X_HEADS):
            hr = _head_rows_of(h)
            vp = jnp.concatenate([mv_ref[2 * j, hr, :], mv_ref[2 * j + 1, hr, :]], axis=0).astype(BF)
            r0 = (j * X_HEADS + h) * pair_rows
            heads.append(_dot(p[r0:r0 + pair_rows].astype(BF), vp))
        outs.append(jnp.concatenate(heads, axis=1))
    o_all = jnp.concatenate(outs, axis=0).astype(BF)
    o_ref[...] = x + _dot(o_all, wo_ref[...])


def _xattn_sample(x, li, g, wq, gq, mk, mv, wo, n, t_new):
    rows = XATTN_SEQ_BLOCK * t_new
    row = pl.BlockSpec((rows, D_MODEL), lambda i: (i, 0))
    mem = pl.BlockSpec((None, XATTN_SEQ_BLOCK, MEM_ROWS, X_HEAD_DIM), lambda i: (li, i, 0, 0))
    return pl.pallas_call(
        functools.partial(_xattn_sample_kernel, t_new),
        grid=(n // XATTN_SEQ_BLOCK,),
        in_specs=[row, _layer_spec(li, 1, D_MODEL), _layer_spec(li, D_MODEL, X_W), _layer_spec(li, 1, X_HEAD_DIM),
                  mem, mem, _layer_spec(li, X_W, D_MODEL)],
        out_specs=row,
        out_shape=jax.ShapeDtypeStruct((n * t_new, D_MODEL), F32),
        compiler_params=_cparams("parallel"),
        name="xattn_sample",
    )(x, g, wq, gq, mk, mv, wo)


ROUTER_ROWS = 32
ROUTER_COARSE_ROW = N_EXPERTS
BIG = 3.0e38


def _first_argmax(vals, vmax):
    idx = jnp.full(vmax.shape, len(vals) - 1, jnp.int32)
    for j in range(len(vals) - 2, -1, -1):
        idx = jnp.where(vals[j] == vmax, j, idx)
    return idx


def _router_logits(h, whi, wlo, bias):
    h_hi = h.astype(BF)
    h_lo = (h - h_hi.astype(F32)).astype(BF)
    return _dot_nt(whi, h_hi) + _dot_nt(whi, h_lo) + _dot_nt(wlo, h_hi) + bias


def _coarse_rows(lg):
    return [lg[ROUTER_COARSE_ROW + g:ROUTER_COARSE_ROW + g + 1, :] for g in range(N_GROUPS)]


def _top_group(lg):
    coarse = _coarse_rows(lg)
    return _first_argmax(coarse, functools.reduce(jnp.maximum, coarse))


def _in_group_gates(lg, grp):
    pick = lambda rows: functools.reduce(lambda f, g: jnp.where(grp == g, rows[g], f),
                                         range(N_GROUPS - 2, -1, -1), rows[N_GROUPS - 1])
    coarse = _coarse_rows(lg)
    csel = pick(coarse)
    p_grp = 1.0 / functools.reduce(lambda a, b: a + b, [jnp.exp(c - csel) for c in coarse])
    fine = [pick([lg[g * EXPERTS_PER_GROUP + j:g * EXPERTS_PER_GROUP + j + 1, :] for g in range(N_GROUPS)])
            for j in range(EXPERTS_PER_GROUP)]
    v1 = functools.reduce(jnp.maximum, fine)
    i1 = _first_argmax(fine, v1)
    rest = [jnp.where(i1 == j, -BIG, fine[j]) for j in range(EXPERTS_PER_GROUP)]
    v2 = functools.reduce(jnp.maximum, rest)
    i2 = _first_argmax(rest, v2)
    e2 = jnp.exp(v2 - v1)
    w1 = p_grp / (1.0 + e2)
    w2 = p_grp * e2 / (1.0 + e2)
    return [jnp.where(i1 == j, w1, 0.0) + jnp.where(i2 == j, w2, 0.0) for j in range(EXPERTS_PER_GROUP)]


def _rows_to_lanes(rows, tm):
    pad = jnp.zeros((LANES - len(rows), tm), F32)
    return jnp.concatenate(list(rows) + [pad], axis=0).T


def _router_gates(h, whi, wlo, bias):
    lg = _router_logits(h, whi, wlo, bias)
    grp = _top_group(lg)
    in_group = _in_group_gates(lg, grp)
    rows = [jnp.where(grp == e // EXPERTS_PER_GROUP, in_group[e % EXPERTS_PER_GROUP], 0.0) for e in range(N_EXPERTS)]
    return _rows_to_lanes(rows, lg.shape[1])


def _moe_kernel(x_ref, g_ref, whi_ref, wlo_ref, b_ref, w1_ref, w3_ref, w2_ref, o_ref, h_scr, gates_scr, acc_scr):
    gi = pl.program_id(1)

    @pl.when(gi == 0)
    def _():
        h = _rms_rows(x_ref[...], g_ref[...])
        h_scr[...] = h.astype(BF)
        gates_scr[...] = _router_gates(h, whi_ref[...], wlo_ref[...], b_ref[...])
        acc_scr[...] = jnp.zeros_like(acc_scr)

    h = h_scr[...]
    gates = gates_scr[...]
    lane = lax.broadcasted_iota(jnp.int32, gates.shape, 1)
    for j in range(EXPERTS_PER_GROUP):
        gcol = jnp.sum(jnp.where(lane == gi * EXPERTS_PER_GROUP + j, gates, 0.0), axis=-1, keepdims=True)
        a = _dot(h, w1_ref[j])
        b = _dot(h, w3_ref[j])
        hid = (a * jax.nn.sigmoid(a)) * b * gcol
        acc_scr[...] += _dot(hid.astype(BF), w2_ref[j])

    @pl.when(gi == pl.num_programs(1) - 1)
    def _():
        o_ref[...] = x_ref[...] + acc_scr[...]


def _moe(x, li, g, whi, wlo, b, w1, w3, w2, tm):
    m = x.shape[0]
    row = lambda w_: pl.BlockSpec((tm, w_), lambda i, e: (i, 0))
    group_w = lambda a, b_: pl.BlockSpec((None, EXPERTS_PER_GROUP, a, b_), lambda i, e: (li, e, 0, 0))
    return pl.pallas_call(
        _moe_kernel,
        grid=(m // tm, N_GROUPS),
        in_specs=[row(D_MODEL), _layer_spec(li, 1, D_MODEL), _layer_spec(li, ROUTER_ROWS, D_MODEL),
                  _layer_spec(li, ROUTER_ROWS, D_MODEL), _layer_spec(li, ROUTER_ROWS, 1),
                  group_w(D_MODEL, D_EXPERT), group_w(D_MODEL, D_EXPERT), group_w(D_EXPERT, D_MODEL)],
        out_specs=row(D_MODEL),
        out_shape=jax.ShapeDtypeStruct((m, D_MODEL), F32),
        scratch_shapes=[pltpu.VMEM((tm, D_MODEL), BF), pltpu.VMEM((tm, LANES), F32), pltpu.VMEM((tm, D_MODEL), F32)],
        compiler_params=_cparams("parallel", "arbitrary"),
        name="moe",
    )(x, g, whi, wlo, b, w1, w3, w2)


MOE_TILE = 256
WRITE_BACK_DMA_PRIORITY = 1


def _sorted_plan(grp, m):
    t = MOE_TILE
    n_tiles = m // t + N_GROUPS
    order = jnp.argsort(grp, stable=True).astype(jnp.int32)
    counts = jnp.sum((grp[None, :] == jnp.arange(N_GROUPS, dtype=jnp.int32)[:, None]).astype(jnp.int32), axis=1)
    tiles_per_group = (counts + t - 1) // t
    tile_end = jnp.cumsum(tiles_per_group)
    tile_start = tile_end - tiles_per_group
    first_token = jnp.cumsum(counts) - counts
    s = jnp.arange(n_tiles, dtype=jnp.int32)
    tile_grp = jnp.minimum(jnp.sum((s[:, None] >= tile_end[None, :]).astype(jnp.int32), axis=1), N_GROUPS - 1)
    n_active = tile_end[N_GROUPS - 1]
    tile_in_group = s - tile_start[tile_grp]
    n_valid = jnp.where(s < n_active, jnp.clip(counts[tile_grp] - tile_in_group * t, 0, t), 0)
    r = jnp.arange(t, dtype=jnp.int32)
    slot_token = order[jnp.clip(first_token[tile_grp][:, None] + tile_in_group[:, None] * t + r[None, :], 0, m - 1)]
    valid = r[None, :] < n_valid[:, None]
    src = jnp.where(valid, slot_token, 0).reshape(-1)
    dst = jnp.where(valid, slot_token, m + (s % 2)[:, None] * t + r[None, :]).reshape(-1)
    dst = jnp.concatenate([m + t + r, dst])
    return src.astype(jnp.int32), (dst * TT_CHUNKS).astype(jnp.int32), tile_grp.astype(jnp.int32)


def _sorted_moe_kernel(m, dst_ref, tgrp_ref, xs_ref, g_ref, whi_ref, wlo_ref, b_ref,
                       w1_ref, w3_ref, w2_ref, out_hbm, obuf0, obuf1, sem_s):
    t = MOE_TILE
    i = pl.program_id(0)
    last = pl.num_programs(0) - 1
    obuf = (obuf0, obuf1)

    c8 = TT_CHUNKS

    def scatter_row(tile, slot, r):
        dst = pl.multiple_of(dst_ref[(tile + 1) * t + r], c8)
        return pltpu.make_async_copy(obuf[slot].at[pl.ds(r * c8, c8), :], out_hbm.at[pl.ds(dst, c8), :],
                                     sem_s.at[slot])

    def whole_scatter(slot, row0):
        return pltpu.make_async_copy(obuf[slot], out_hbm.at[pl.ds(row0, t * c8), :], sem_s.at[slot])

    @pl.when(i == 0)
    def _():
        for slot in range(2):
            obuf[slot][...] = jnp.zeros((t * c8, LANES), F32)
        for slot in range(2):
            whole_scatter(slot, (m + slot * t) * c8).start()
        for slot in range(2):
            whole_scatter(slot, (m + slot * t) * c8).wait()

    def step(cur, nxt):
        for r in range(t):
            scatter_row(i - 1, nxt, r).start()

        x = _read_tokens(xs_ref, True)
        h = _rms_rows(x, g_ref[...])
        lg = _router_logits(h, whi_ref[...], wlo_ref[...], b_ref[...])
        gates = _rows_to_lanes(_in_group_gates(lg, tgrp_ref[i]), t)
        hb = h.astype(BF)
        acc = jnp.zeros((t, D_MODEL), F32)
        for j in range(EXPERTS_PER_GROUP):
            a = _dot(hb, w1_ref[j])
            b = _dot(hb, w3_ref[j])
            hid = (a * jax.nn.sigmoid(a)) * b * gates[:, j:j + 1]
            acc = acc + _dot(hid.astype(BF), w2_ref[j])
        res = x + acc

        @pl.when(i >= 1)
        def _():
            whole_scatter(cur, 0).wait()

        _write_token_tiles(obuf[cur], res)

        @pl.when(i == last)
        def _():
            def issue(r, carry):
                scatter_row(i, cur, r).start()
                return carry
            lax.fori_loop(0, t, issue, 0)
            whole_scatter(cur, 0).wait()
            whole_scatter(nxt, 0).wait()

    for parity in range(2):
        @pl.when(i % 2 == parity)
        def _():
            step(parity, 1 - parity)


SC_CORES = 2
SC_SUBCORES = 16
SC_GATHER_CHUNK = 32


def _sc_gather_tokens(x3, idx):
    n = idx.shape[0]
    workers = SC_CORES * SC_SUBCORES
    per_worker = n // workers
    chunk = SC_GATHER_CHUNK
    assert n % workers == 0 and per_worker % chunk == 0
    mesh = plsc.VectorSubcoreMesh(core_axis_name="core", subcore_axis_name="subcore", num_cores=SC_CORES,
                                  num_subcores=SC_SUBCORES)

    @functools.partial(
        pl.kernel, out_type=jax.ShapeDtypeStruct((n, TT_CHUNKS, LANES), F32), mesh=mesh,
        scratch_types=[pltpu.VMEM((chunk,), jnp.int32), pltpu.VMEM((chunk, TT_CHUNKS, LANES), F32),
                       pltpu.SemaphoreType.DMA])
    def gather(x_hbm, idx_hbm, out_hbm, idx_v, rows_v, sem):
        worker = lax.axis_index("subcore") * SC_CORES + lax.axis_index("core")

        @pl.loop(0, per_worker // chunk)
        def _(j):
            off = pl.multiple_of(worker * per_worker + j * chunk, 8)
            pltpu.sync_copy(idx_hbm.at[pl.ds(off, chunk)], idx_v)
            pltpu.async_copy(x_hbm.at[idx_v], rows_v, sem).wait()
            pltpu.sync_copy(rows_v, out_hbm.at[pl.ds(off, chunk)])

    return gather(x3, idx)


def _sorted_moe(x, grp, m, li, g, whi, wlo, b, w1, w3, w2):
    t = MOE_TILE
    n_tiles = m // t + N_GROUPS
    src, dst, tile_grp = _sorted_plan(grp, m)
    xs = _sc_gather_tokens(x.reshape(-1, TT_CHUNKS, LANES), src).reshape(n_tiles * t * TT_CHUNKS, LANES)
    group_w = lambda a, b_: pl.BlockSpec((None, EXPERTS_PER_GROUP, a, b_), lambda i, dst_, tg: (li, tg[i], 0, 0))
    return pl.pallas_call(
        functools.partial(_sorted_moe_kernel, m),
        grid_spec=pltpu.PrefetchScalarGridSpec(
            num_scalar_prefetch=2,
            grid=(n_tiles,),
            in_specs=[pl.BlockSpec((t * TT_CHUNKS, LANES), lambda i, dst_, tg: (i, 0)), _layer_spec(li, 1, D_MODEL),
                      _layer_spec(li, ROUTER_ROWS, D_MODEL), _layer_spec(li, ROUTER_ROWS, D_MODEL),
                      _layer_spec(li, ROUTER_ROWS, 1),
                      group_w(D_MODEL, D_EXPERT), group_w(D_MODEL, D_EXPERT), group_w(D_EXPERT, D_MODEL)],
            out_specs=pl.BlockSpec(memory_space=pl.ANY),
            scratch_shapes=[pltpu.VMEM((t * TT_CHUNKS, LANES), F32)] * 2 + [pltpu.SemaphoreType.DMA((2,))]),
        out_shape=jax.ShapeDtypeStruct(((m + 2 * t) * TT_CHUNKS, LANES), F32),
        compiler_params=_cparams("arbitrary"),
        name="moe_sorted",
    )(dst, tile_grp, xs, g, whi, wlo, b, w1, w3, w2)


def _heads_kv_major_to_g_major(w, axis):
    shape = w.shape
    split = shape[:axis] + (KV_HEADS, Q_PER_KV, HEAD_DIM) + shape[axis + 1:]
    return jnp.swapaxes(w.reshape(split), axis, axis + 1).reshape(shape)


def _prep_ssm(a_re, a_im, log_dt, b_re, b_im, c_re, c_im, d_skip, w_glu):
    ar, ai = a_re.astype(F32), a_im.astype(F32)
    dt = jnp.exp(log_dt.astype(F32))[..., None]
    mag = jnp.exp(ar * dt)
    abr, abi = mag * jnp.cos(ai * dt), mag * jnp.sin(ai * dt)
    den = ar * ar + ai * ai
    kr = ((abr - 1.0) * ar + abi * ai) / den
    ki = (abi * ar - (abr - 1.0) * ai) / den
    br, bi = b_re.astype(F32), b_im.astype(F32)
    bbr = kr[..., None] * br - ki[..., None] * bi
    bbi = kr[..., None] * bi + ki[..., None] * br
    n_layers = ar.shape[0]
    eye = jnp.eye(SSM_GROUPS, dtype=F32)
    blockdiag_in = lambda b: jnp.einsum('egph,gk->eghkp', b, eye).reshape(n_layers, SSM_W, SSM_FLAT)
    blockdiag_out = lambda c: jnp.einsum('eghp,gk->egpkh', c, eye).reshape(n_layers, SSM_FLAT, SSM_W)
    in_tiles = lambda b: jnp.stack([b[:, j * SSM_IN_TILE:(j + 1) * SSM_IN_TILE, j * SSM_ST_TILE:(j + 1) * SSM_ST_TILE]
                                    for j in range(SSM_IN_TILES)], axis=1)
    out_tiles = lambda c: jnp.stack([c[:, j * SSM_OUT_ST:(j + 1) * SSM_OUT_ST, j * SSM_OUT_TILE:(j + 1) * SSM_OUT_TILE]
                                     for j in range(SSM_OUT_TILES)], axis=1)
    bcat = jnp.concatenate([in_tiles(blockdiag_in(bbr)), in_tiles(blockdiag_in(bbi))], axis=3).astype(BF)
    ccat = jnp.concatenate([out_tiles(blockdiag_out(c_re.astype(F32))), -out_tiles(blockdiag_out(c_im.astype(F32)))],
                           axis=2).astype(BF)
    return (abr.reshape(n_layers, 1, SSM_FLAT), abi.reshape(n_layers, 1, SSM_FLAT), bcat, ccat,
            d_skip.astype(F32).reshape(n_layers, 1, SSM_W), w_glu.astype(BF))


def _run_trunk(x, n, seq, mem_k, mem_v, caches, p):
    prompt = caches is None
    tm = 512
    m = n * seq
    wk_out, wv_out, hr_out, hi_out, cv_out = [], [], [], [], []
    xattn_args = (p['g_xattn'], p['w_xq'], p['g_xq'], mem_k, mem_v, p['w_xo'])
    router_args = (p['g_ffn'], p['w_router_hi'], p['w_router_lo'], p['b_router'])
    expert_args = (p['w_e1'], p['w_e3'], p['w_e2'])
    tiled = False
    for li in range(DEPTH):
        e = li // 2
        if li % 2 == 0:
            q, k, v, u = _even_in_proj(x, li, e, p['g_mix'], p['w_in_even'], p['g_q'], p['g_k'], p['bd'], tm, n, seq,
                                       token_tiled=tiled)
            if prompt:
                att = _swa_prompt(q, k, v, p['sinks'], e, n, seq)
                k3, v3 = k.reshape(n, seq, KV_W), v.reshape(n, seq, KV_W)
                new_k, new_v = k3[:, seq - WINDOW:], v3[:, seq - WINDOW:]
                ssm, h_re, h_im = _s5_glu(u.reshape(seq, n, SSM_W), None, p['ssm'], e, n, seq, 128)
                x, grp = _even_out_xattn(x, tiled, att, ssm.reshape(seq, n * SSM_W), e, p['w_out_att'], p['w_out_ssm'],
                                         li, xattn_args + router_args, n, seq, tm)
            else:
                win_k, win_v, ssm_re, ssm_im = caches
                att, new_k, new_v = _swa_sample(q, k, v, win_k, win_v, p['sinks'], e, n, seq)
                u_tm = jnp.swapaxes(u.reshape(n, seq, SSM_W), 0, 1)
                ssm, h_re, h_im = _s5_glu(u_tm, (ssm_re, ssm_im), p['ssm'], e, n, seq, seq)
                ssm = jnp.swapaxes(ssm, 0, 1).reshape(m, SSM_W)
                x = _even_out_proj(x, att, ssm, e, p['w_out_att'], p['w_out_ssm'], tm, n, seq)
            wk_out.append(new_k.reshape(n, WINDOW, KV_HEADS, HEAD_DIM))
            wv_out.append(new_v.reshape(n, WINDOW, KV_HEADS, HEAD_DIM))
            hr_out.append(h_re.reshape(n, SSM_GROUPS, SSM_STATE))
            hi_out.append(h_im.reshape(n, SSM_GROUPS, SSM_STATE))
        else:
            u, v = _odd_in_proj(x, m, li, e, p['g_mix'], p['w_in_odd'], p['g_v'], tm, BF if prompt else F32,
                                token_tiled=tiled)
            wsp, bsp, csize = p['gmlp_prompt'] if prompt else p['gmlp_sample']
            if prompt:
                x, grp = _gmlp_xattn(x, tiled, u, v, e, wsp, bsp, p['w_out_odd'], csize, li, xattn_args + router_args,
                                     n, seq, tm)
            else:
                x = _gmlp_mix(x, u, v, e, wsp, bsp, p['w_out_odd'], csize, tm)
                cv_out.append(v.reshape(n, seq, GM_W))
        if prompt:
            x = _sorted_moe(x, grp, m, li, *router_args, *expert_args)
            tiled = True
        else:
            x = _xattn_sample(x, li, *xattn_args, n, seq)
            x = _moe(x, li, *router_args, *expert_args, tm)
    if tiled:
        x = _untile(x, m, tm)
    cv = None if prompt else jnp.stack(cv_out)
    return x, jnp.stack(wk_out), jnp.stack(wv_out), jnp.stack(hr_out), jnp.stack(hi_out), cv


def kernel(x_prompt, x_sample, cache_win_k, cache_win_v, state_ssm_re, state_ssm_im, cache_mem_k, cache_mem_v, mem_prompt, g_mix, g_xattn, g_ffn, g_mem, w_in_even, g_q, g_k, sinks, ssm_a_re, ssm_a_im, ssm_log_dt, ssm_b_re, ssm_b_im, ssm_c_re, ssm_c_im, ssm_d, w_glu, w_out_even, w_in_odd, g_v, w_spatial, b_spatial, w_out_odd, w_xq, g_xq, w_xk, g_xk, w_xv, w_xo, w_coarse, b_coarse, w_fine, b_fine, w_e1, w_e3, w_e2):
    batch, seq = x_prompt.shape[0], x_prompt.shape[1]
    dec_batch, dec_seq = x_sample.shape[0], x_sample.shape[1]
    n_even = w_in_even.shape[0]

    row1 = lambda a: a.astype(F32)[:, None, :]
    p = {}
    p['g_mix'], p['g_xattn'], p['g_ffn'] = row1(g_mix), row1(g_xattn), row1(g_ffn)
    w_in_q = _heads_kv_major_to_g_major(w_in_even[:, :, :A_W], 2)
    p['w_in_even'] = jnp.concatenate([w_in_q, w_in_even[:, :, A_W:]], axis=2).astype(BF)
    p['g_q'] = jnp.tile(g_q.astype(F32), (1, A_HEADS))[:, None, :]
    p['g_k'] = jnp.tile(g_k.astype(F32), (1, KV_HEADS))[:, None, :]
    head_id = jnp.arange(A_W) // HEAD_DIM
    p['bd'] = (head_id[:, None] == head_id[None, :]).astype(BF)
    p['sinks'] = sinks.astype(F32).reshape(-1)
    p['ssm'] = _prep_ssm(ssm_a_re, ssm_a_im, ssm_log_dt, ssm_b_re, ssm_b_im, ssm_c_re, ssm_c_im, ssm_d, w_glu)
    p['w_out_att'] = _heads_kv_major_to_g_major(w_out_even[:, :A_W, :], 1).astype(BF)
    p['w_out_ssm'] = w_out_even[:, A_W:, :].astype(BF)
    p['w_in_odd'] = w_in_odd.astype(BF)
    p['g_v'] = row1(g_v)
    p['w_out_odd'] = w_out_odd.astype(BF)
    cs = min(dec_seq, CHUNK)
    reps = CHUNK // cs
    p['gmlp_prompt'] = (w_spatial.astype(F32), jnp.swapaxes(b_spatial.astype(F32), 1, 2), CHUNK)
    p['gmlp_sample'] = (jnp.tile(w_spatial[:, :, :cs, :cs].astype(F32), (1, 1, reps, reps)),
                        jnp.tile(jnp.swapaxes(b_spatial[:, :, :cs].astype(F32), 1, 2), (1, reps, 1)), cs)
    p['w_xq'] = w_xq.astype(BF)
    p['g_xq'] = row1(g_xq)
    p['w_xo'] = w_xo.astype(BF)
    w_router = jnp.swapaxes(jnp.concatenate([w_fine, w_coarse], axis=2).astype(F32), 1, 2)
    w_router = jnp.pad(w_router, ((0, 0), (0, ROUTER_ROWS - w_router.shape[1]), (0, 0)))
    p['w_router_hi'] = w_router.astype(BF)
    p['w_router_lo'] = (w_router - p['w_router_hi'].astype(F32)).astype(BF)
    b_router = jnp.concatenate([b_fine, b_coarse], axis=1).astype(F32)
    p['b_router'] = jnp.pad(b_router, ((0, 0), (0, ROUTER_ROWS - b_router.shape[1])))[:, :, None]
    p['w_e1'], p['w_e3'], p['w_e2'] = w_e1.astype(BF), w_e3.astype(BF), w_e2.astype(BF)

    mem2d = mem_prompt.reshape(batch * MEM_LEN, D_MODEL)
    pk, pv = _memory_kv(mem2d, row1(g_mem), w_xk.astype(BF), row1(g_xk), w_xv.astype(BF), batch)
    y_p, p_wk, p_wv, p_hr, p_hi, _ = _run_trunk(x_prompt.reshape(batch * seq, D_MODEL), batch, seq, pk, pv, None, p)

    caches = (cache_win_k.reshape(n_even, dec_batch, WINDOW, KV_W), cache_win_v.reshape(n_even, dec_batch, WINDOW, KV_W),
              state_ssm_re.reshape(n_even, dec_batch, SSM_FLAT), state_ssm_im.reshape(n_even, dec_batch, SSM_FLAT))
    y_s, s_wk, s_wv, s_hr, s_hi, s_cv = _run_trunk(x_sample.reshape(dec_batch * dec_seq, D_MODEL), dec_batch, dec_seq,
                                                   cache_mem_k.reshape(DEPTH, dec_batch, MEM_ROWS, X_HEAD_DIM),
                                                   cache_mem_v.reshape(DEPTH, dec_batch, MEM_ROWS, X_HEAD_DIM), caches, p)

    mem_shape = (DEPTH, batch, MEM_LEN, X_HEADS, X_HEAD_DIM)
    return (y_p.reshape(batch, seq, D_MODEL), y_s.reshape(dec_batch, dec_seq, D_MODEL), p_wk, p_wv, p_hr, p_hi,
            pk.reshape(mem_shape), pv.reshape(mem_shape), s_wk, s_wv, s_hr, s_hi, s_cv)
```

```python
import functools
import math

import jax
import jax.numpy as jnp
from jax import lax
from jax.experimental import pallas as pl
from jax.experimental.pallas import tpu as pltpu
from jax.experimental.pallas import tpu_sc as plsc

F32 = jnp.float32
BF = jnp.bfloat16

D_MODEL = 1024
DEPTH = 4
A_W = 512
HEAD_DIM = 64
A_HEADS = 8
KV_HEADS = 2
Q_PER_KV = 4
KV_W = 128
WINDOW = 128
SSM_W = 512
SSM_CH = 16
SSM_GROUPS = 32
SSM_STATE = 64
SSM_FLAT = SSM_GROUPS * SSM_STATE
IN_EVEN = A_W + 2 * KV_W + SSM_W
CHUNK = 128
GM_W = 2048
GM_HEADS = 8
GM_HD = 256
MEM_LEN = 256
X_HEADS = 4
X_HEAD_DIM = 128
X_W = 512
N_GROUPS = 4
EXPERTS_PER_GROUP = 4
N_EXPERTS = 16
D_EXPERT = 256
EPS = 1e-6
NEG_INF = -1e30
LANES = 128
VMEM_LIMIT_BYTES = 52 * 1024 * 1024
ALIBI_SLOPES = tuple(2.0 ** (-8.0 * (h + 1) / A_HEADS) for h in range(A_HEADS))
SSM_SEQ_BLOCK = 8


def _cparams(*sem):
    return pltpu.CompilerParams(dimension_semantics=sem, vmem_limit_bytes=VMEM_LIMIT_BYTES)


def _rms_rows(x, g):
    return x * lax.rsqrt(jnp.mean(x * x, axis=-1, keepdims=True) + EPS) * g


TT_CHUNKS = D_MODEL // LANES


def _read_tokens(ref, token_tiled):
    if not token_tiled:
        return ref[...]
    rows = ref.shape[0] // TT_CHUNKS
    return jnp.concatenate([ref[pl.ds(c, rows, stride=TT_CHUNKS), :] for c in range(TT_CHUNKS)], axis=1)


def _write_token_tiles(ref, val):
    rows = val.shape[0]
    for c in range(TT_CHUNKS):
        ref[pl.ds(c, rows, stride=TT_CHUNKS), :] = val[:, c * LANES:(c + 1) * LANES]


def _token_spec(rows, index, token_tiled):
    if token_tiled:
        return pl.BlockSpec((rows * TT_CHUNKS, LANES), lambda *ids: (index(*ids), 0))
    return pl.BlockSpec((rows, D_MODEL), lambda *ids: (index(*ids), 0))


def _untile_kernel(x_ref, o_ref):
    o_ref[...] = _read_tokens(x_ref, True)


def _untile(x, m, tm):
    return pl.pallas_call(
        _untile_kernel,
        grid=(m // tm,),
        in_specs=[_token_spec(tm, lambda i: i, True)],
        out_specs=pl.BlockSpec((tm, D_MODEL), lambda i: (i, 0)),
        out_shape=jax.ShapeDtypeStruct((m, D_MODEL), F32),
        compiler_params=_cparams("parallel"),
        name="untile",
    )(x)


def _dot(a, b):
    return jnp.dot(a, b, preferred_element_type=F32)


def _dot_nt(a, b):
    return lax.dot_general(a, b, (((1,), (1,)), ((), ())), preferred_element_type=F32)


def _full(shape):
    nd = len(shape)
    return pl.BlockSpec(shape, lambda *_: (0,) * nd)


def _layer_spec(li, *shape, single_buffer=False):
    nd = len(shape)
    mode = pl.Buffered(1) if single_buffer else None
    return pl.BlockSpec((None,) + shape, lambda *_: (li,) + (0,) * nd, pipeline_mode=mode)


def _even_in_kernel(token_tiled, x_ref, g_ref, w_ref, gq_ref, gk_ref, bd_ref, q_ref, k_ref, v_ref, u_ref):
    h = _rms_rows(_read_tokens(x_ref, token_tiled), g_ref[...]).astype(BF)
    z = _dot(h, w_ref[...])
    q = z[:, :A_W]
    k = z[:, A_W:A_W + KV_W]
    bd = bd_ref[...]
    qms = _dot((q * q).astype(BF), bd) * (1.0 / HEAD_DIM)
    kms = _dot((k * k).astype(BF), bd[:KV_W, :KV_W]) * (1.0 / HEAD_DIM)
    q_ref[...] = q * lax.rsqrt(qms + EPS) * gq_ref[...]
    k_ref[...] = k * lax.rsqrt(kms + EPS) * gk_ref[...]
    v_ref[...] = z[:, A_W + KV_W:A_W + 2 * KV_W]
    u_ref[...] = z[:, A_W + 2 * KV_W:]


def _time_major_spec(tm, n, seq, width):
    per_seq = seq // tm
    return pl.BlockSpec((tm, width), lambda i: (i % per_seq, i // per_seq))


def _even_in_proj(x, li, e, g, w, gq, gk, bd, tm, n, seq, token_tiled=False):
    m = n * seq
    row = lambda w_: pl.BlockSpec((tm, w_), lambda i: (i, 0))
    return pl.pallas_call(
        functools.partial(_even_in_kernel, token_tiled),
        grid=(m // tm,),
        in_specs=[_token_spec(tm, lambda i: i, token_tiled), _layer_spec(li, 1, D_MODEL),
                  _layer_spec(e, D_MODEL, IN_EVEN), _layer_spec(e, 1, A_W),
                  _layer_spec(e, 1, KV_W), _full((A_W, A_W))],
        out_specs=[row(A_W), row(KV_W), row(KV_W), _time_major_spec(tm, n, seq, SSM_W) if seq % tm == 0 else row(SSM_W)],
        out_shape=[jax.ShapeDtypeStruct((m, A_W), F32), jax.ShapeDtypeStruct((m, KV_W), F32),
                   jax.ShapeDtypeStruct((m, KV_W), F32),
                   jax.ShapeDtypeStruct((seq, n * SSM_W) if seq % tm == 0 else (m, SSM_W), F32)],
        compiler_params=_cparams("parallel"),
        name="even_in_proj",
    )(x, g, w, gq, gk, bd)


def _head_rows(q, lane_lo):
    rows = []
    for kv in range(KV_HEADS):
        keep = lane_lo if kv == 0 else jnp.logical_not(lane_lo)
        for g in range(Q_PER_KV):
            rows.append(jnp.where(keep, q[:, g * LANES:(g + 1) * LANES], 0.0))
    return jnp.concatenate(rows, axis=0)


def _head_cols(o, r, lane_lo):
    cols = []
    for g in range(Q_PER_KV):
        cols.append(jnp.where(lane_lo, o[g * r:(g + 1) * r], o[(Q_PER_KV + g) * r:(Q_PER_KV + g + 1) * r]))
    return jnp.concatenate(cols, axis=1)


def _swa_prompt_kernel(e, sinks_ref, q_ref, kc_ref, kp_ref, vc_ref, vp_ref, o_ref):
    b = pl.program_id(1)
    lane_lo = lax.broadcasted_iota(jnp.int32, (1, LANES), 1) < HEAD_DIM
    qrows = _head_rows(q_ref[...], lane_lo).astype(BF)
    kcat = jnp.concatenate([kp_ref[...], kc_ref[...]], axis=0).astype(BF)
    vcat = jnp.concatenate([vp_ref[...], vc_ref[...]], axis=0).astype(BF)
    s = _dot_nt(qrows, kcat) * (HEAD_DIM ** -0.5)
    r = lax.broadcasted_iota(jnp.int32, (WINDOW, 2 * WINDOW), 0)
    c = lax.broadcasted_iota(jnp.int32, (WINDOW, 2 * WINDOW), 1)
    dist = WINDOW + r - c
    valid = (dist >= 0) & (dist <= WINDOW) & ((c >= WINDOW) | (b > 0))
    dist_f = dist.astype(F32)
    ps = []
    for h in range(A_HEADS):
        sh = s[h * WINDOW:(h + 1) * WINDOW] - ALIBI_SLOPES[h] * dist_f
        sh = jnp.where(valid, sh, NEG_INF)
        sk = sinks_ref[e * A_HEADS + h]
        m = jnp.maximum(jnp.max(sh, axis=-1, keepdims=True), sk)
        p = jnp.exp(sh - m)
        den = jnp.sum(p, axis=-1, keepdims=True) + jnp.exp(sk - m)
        ps.append((p / den).astype(BF))
    o = _dot(jnp.concatenate(ps, axis=0), vcat)
    o_ref[...] = _head_cols(o, WINDOW, lane_lo).astype(o_ref.dtype)


def _swa_prompt(q, k, v, sinks, e, n, seq):
    nb = seq // WINDOW
    cur = lambda w_: pl.BlockSpec((WINDOW, w_), lambda i, b: (i * nb + b, 0))
    prev = lambda w_: pl.BlockSpec((WINDOW, w_), lambda i, b: (i * nb + jnp.maximum(b - 1, 0), 0))
    return pl.pallas_call(
        functools.partial(_swa_prompt_kernel, e),
        grid=(n, nb),
        in_specs=[pl.BlockSpec(memory_space=pltpu.SMEM), cur(A_W), cur(KV_W), prev(KV_W), cur(KV_W), prev(KV_W)],
        out_specs=cur(A_W),
        out_shape=jax.ShapeDtypeStruct((n * seq, A_W), BF),
        compiler_params=_cparams("parallel", "arbitrary"),
        name="swa_prompt",
    )(sinks, q, k, k, v, v)


SWA_SEQ_BLOCK = 8


def _swa_sample_kernel(t_new, e, sinks_ref, q_ref, kn_ref, vn_ref, wk_ref, wv_ref, o_ref, nwk_ref, nwv_ref):
    nseq = SWA_SEQ_BLOCK
    pair_rows = 2 * t_new
    npair = nseq // 2
    kn = kn_ref[...]
    vn = vn_ref[...]
    for i in range(nseq):
        nwk_ref[i, 0:WINDOW - t_new, :] = wk_ref[i, t_new:WINDOW, :]
        nwk_ref[i, WINDOW - t_new:WINDOW, :] = kn[i * t_new:(i + 1) * t_new, :]
        nwv_ref[i, 0:WINDOW - t_new, :] = wv_ref[i, t_new:WINDOW, :]
        nwv_ref[i, WINDOW - t_new:WINDOW, :] = vn[i * t_new:(i + 1) * t_new, :]

    lane_lo = lax.broadcasted_iota(jnp.int32, (1, LANES), 1) < HEAD_DIM
    q = q_ref[...]
    knb = kn.astype(BF)
    vnb = vn.astype(BF)
    hr = A_HEADS * pair_rows
    qrows = [_head_rows(q[j * pair_rows:(j + 1) * pair_rows], lane_lo).astype(BF) for j in range(npair)]
    scale = HEAD_DIM ** -0.5
    s_new_all = _dot_nt(jnp.concatenate(qrows, axis=0), knb) * scale

    tq = lax.broadcasted_iota(jnp.int32, (pair_rows, 2 * WINDOW), 0)
    cw = lax.broadcasted_iota(jnp.int32, (pair_rows, 2 * WINDOW), 1)
    dist_w = WINDOW + (tq % t_new) - (cw % WINDOW)
    valid_w = ((tq // t_new) == (cw // WINDOW)) & (dist_w <= WINDOW)
    dist_wf = dist_w.astype(F32)
    tqn = lax.broadcasted_iota(jnp.int32, (pair_rows, nseq * t_new), 0)
    cn = lax.broadcasted_iota(jnp.int32, (pair_rows, nseq * t_new), 1)
    dist_n = (tqn % t_new) - (cn % t_new)
    dist_nf = dist_n.astype(F32)

    p_new_all = []
    o_win_all = []
    for j in range(npair):
        kwin = jnp.concatenate([wk_ref[2 * j], wk_ref[2 * j + 1]], axis=0).astype(BF)
        vwin = jnp.concatenate([wv_ref[2 * j], wv_ref[2 * j + 1]], axis=0).astype(BF)
        s_win = _dot_nt(qrows[j], kwin) * scale
        valid_n = ((2 * j + tqn // t_new) == (cn // t_new)) & (dist_n >= 0)
        p_win = []
        for h in range(A_HEADS):
            sw = jnp.where(valid_w, s_win[h * pair_rows:(h + 1) * pair_rows] - ALIBI_SLOPES[h] * dist_wf, NEG_INF)
            sn = s_new_all[j * hr + h * pair_rows:j * hr + (h + 1) * pair_rows]
            sn = jnp.where(valid_n, sn - ALIBI_SLOPES[h] * dist_nf, NEG_INF)
            sk = sinks_ref[e * A_HEADS + h]
            m = jnp.maximum(jnp.maximum(jnp.max(sw, axis=-1, keepdims=True), jnp.max(sn, axis=-1, keepdims=True)), sk)
            pw = jnp.exp(sw - m)
            pn = jnp.exp(sn - m)
            den = jnp.sum(pw, axis=-1, keepdims=True) + jnp.sum(pn, axis=-1, keepdims=True) + jnp.exp(sk - m)
            p_win.append((pw / den).astype(BF))
            p_new_all.append((pn / den).astype(BF))
        o_win_all.append(_dot(jnp.concatenate(p_win, axis=0), vwin))
    o_new = _dot(jnp.concatenate(p_new_all, axis=0), vnb)
    outs = [_head_cols(o_win_all[j] + o_new[j * hr:(j + 1) * hr], pair_rows, lane_lo) for j in range(npair)]
    o_ref[...] = jnp.concatenate(outs, axis=0).astype(o_ref.dtype)


def _swa_sample(q, kn, vn, win_k, win_v, sinks, e, n, t_new):
    rows = SWA_SEQ_BLOCK * t_new
    row = lambda w_: pl.BlockSpec((rows, w_), lambda i: (i, 0))
    win = pl.BlockSpec((SWA_SEQ_BLOCK, WINDOW, KV_W), lambda i: (i, 0, 0))
    win_in = pl.BlockSpec((None, SWA_SEQ_BLOCK, WINDOW, KV_W), lambda i: (e, i, 0, 0))
    return pl.pallas_call(
        functools.partial(_swa_sample_kernel, t_new, e),
        grid=(n // SWA_SEQ_BLOCK,),
        in_specs=[pl.BlockSpec(memory_space=pltpu.SMEM), row(A_W), row(KV_W), row(KV_W), win_in, win_in],
        out_specs=[row(A_W), win, win],
        out_shape=[jax.ShapeDtypeStruct((n * t_new, A_W), BF),
                   jax.ShapeDtypeStruct((n, WINDOW, KV_W), F32), jax.ShapeDtypeStruct((n, WINDOW, KV_W), F32)],
        compiler_params=_cparams("parallel"),
        name="swa_sample",
    )(sinks, q, kn, vn, win_k, win_v)


SSM_LANE_CHUNK = 1024
SSM_IN_TILE = LANES
SSM_IN_TILES = SSM_W // SSM_IN_TILE
SSM_ST_TILE = SSM_IN_TILE // SSM_CH * SSM_STATE
SSM_OUT_TILE = 256
SSM_OUT_TILES = SSM_W // SSM_OUT_TILE
SSM_OUT_ST = SSM_OUT_TILE // SSM_CH * SSM_STATE


def _s5_kernel(tc, has_h0, *refs):
    if has_h0:
        (u_ref, h0r_ref, h0i_ref, abr_ref, abi_ref, bcat_ref, ccat_ref, d_ref, wglu_ref,
         o_ref, hr_ref, hi_ref, st_scr, car_scr) = refs
    else:
        (u_ref, abr_ref, abi_ref, bcat_ref, ccat_ref, d_ref, wglu_ref,
         o_ref, hr_ref, hi_ref, st_scr, car_scr) = refs
    nb = SSM_SEQ_BLOCK
    ci = pl.program_id(1)

    @pl.when(ci == 0)
    def _():
        if has_h0:
            car_scr[:, :SSM_FLAT] = h0r_ref[...]
            car_scr[:, SSM_FLAT:] = h0i_ref[...]
        else:
            car_scr[...] = jnp.zeros_like(car_scr)

    ut = u_ref[...].reshape(tc * nb, SSM_W)
    ub = ut.astype(BF)
    for jt in range(SSM_IN_TILES):
        bu = _dot(ub[:, jt * SSM_IN_TILE:(jt + 1) * SSM_IN_TILE], bcat_ref[jt])
        st_scr[:, jt * SSM_ST_TILE:(jt + 1) * SSM_ST_TILE] = bu[:, :SSM_ST_TILE]
        st_scr[:, SSM_FLAT + jt * SSM_ST_TILE:SSM_FLAT + (jt + 1) * SSM_ST_TILE] = bu[:, SSM_ST_TILE:]

    for lc in range(SSM_FLAT // SSM_LANE_CHUNK):
        lo = lc * SSM_LANE_CHUNK
        re_sl = slice(lo, lo + SSM_LANE_CHUNK)
        im_sl = slice(SSM_FLAT + lo, SSM_FLAT + lo + SSM_LANE_CHUNK)
        ar = jnp.broadcast_to(abr_ref[:, re_sl], (nb, SSM_LANE_CHUNK))
        ai = jnp.broadcast_to(abi_ref[:, re_sl], (nb, SSM_LANE_CHUNK))

        def step(t, carry):
            hr, hi = carry
            rows = pl.ds(pl.multiple_of(t * nb, nb), nb)
            nr = ar * hr - ai * hi + st_scr[rows, re_sl]
            ni = ar * hi + ai * hr + st_scr[rows, im_sl]
            st_scr[rows, re_sl] = nr
            st_scr[rows, im_sl] = ni
            return nr, ni

        hr, hi = lax.fori_loop(0, tc, step, (car_scr[:, re_sl], car_scr[:, im_sl]))
        car_scr[:, re_sl] = hr
        car_scr[:, im_sl] = hi

    ys = []
    for ot in range(SSM_OUT_TILES):
        re_sl = slice(ot * SSM_OUT_ST, (ot + 1) * SSM_OUT_ST)
        im_sl = slice(SSM_FLAT + ot * SSM_OUT_ST, SSM_FLAT + (ot + 1) * SSM_OUT_ST)
        ys.append(_dot(st_scr[:, re_sl].astype(BF), ccat_ref[ot, :SSM_OUT_ST, :])
                  + _dot(st_scr[:, im_sl].astype(BF), ccat_ref[ot, SSM_OUT_ST:, :]))
    y = jnp.concatenate(ys, axis=1) + d_ref[...] * ut
    y = jax.nn.gelu(y).astype(BF)
    g = _dot(y, wglu_ref[...])
    o_ref[...] = (g[:, :SSM_W] * jax.nn.sigmoid(g[:, SSM_W:])).reshape(tc, nb, SSM_W)

    @pl.when(ci == pl.num_programs(1) - 1)
    def _():
        hr_ref[...] = car_scr[:, :SSM_FLAT]
        hi_ref[...] = car_scr[:, SSM_FLAT:]


def _s5_glu(u, h0, ssm, e, n, seq, tc):
    nb = SSM_SEQ_BLOCK
    abr, abi, bcat, ccat, dsk, wglu = ssm
    u_spec = pl.BlockSpec((tc, nb, SSM_W), lambda i, c: (c, i, 0))
    st_spec = pl.BlockSpec((nb, SSM_FLAT), lambda i, c: (i, 0))
    h0_spec = pl.BlockSpec((None, nb, SSM_FLAT), lambda i, c: (e, i, 0))
    consts = [_layer_spec(e, 1, SSM_FLAT), _layer_spec(e, 1, SSM_FLAT),
              _layer_spec(e, SSM_IN_TILES, SSM_IN_TILE, 2 * SSM_ST_TILE),
              _layer_spec(e, SSM_OUT_TILES, 2 * SSM_OUT_ST, SSM_OUT_TILE), _layer_spec(e, 1, SSM_W),
              _layer_spec(e, SSM_W, 2 * SSM_W)]
    has_h0 = h0 is not None
    in_specs = [u_spec] + ([h0_spec, h0_spec] if has_h0 else []) + consts
    args = [u] + (list(h0) if has_h0 else []) + [abr, abi, bcat, ccat, dsk, wglu]
    return pl.pallas_call(
        functools.partial(_s5_kernel, tc, has_h0),
        grid=(n // nb, seq // tc),
        in_specs=in_specs,
        out_specs=[u_spec, st_spec, st_spec],
        out_shape=[jax.ShapeDtypeStruct((seq, n, SSM_W), F32), jax.ShapeDtypeStruct((n, SSM_FLAT), F32),
                   jax.ShapeDtypeStruct((n, SSM_FLAT), F32)],
        scratch_shapes=[pltpu.VMEM((nb * tc, 2 * SSM_FLAT), F32), pltpu.VMEM((nb, 2 * SSM_FLAT), F32)],
        compiler_params=_cparams("parallel", "arbitrary"),
        name="s5_glu",
    )(*args)


def _even_out_kernel(x_ref, a_ref, s_ref, wa_ref, ws_ref, o_ref):
    o_ref[...] = x_ref[...] + _dot(a_ref[...], wa_ref[...]) + _dot(s_ref[...].astype(BF), ws_ref[...])


def _even_out_proj(x, att, ssm, e, wa, ws, tm, n, seq):
    m = x.shape[0]
    row = lambda w_: pl.BlockSpec((tm, w_), lambda i: (i, 0))
    ssm_spec = _time_major_spec(tm, n, seq, SSM_W) if seq % tm == 0 else row(SSM_W)
    return pl.pallas_call(
        _even_out_kernel,
        grid=(m // tm,),
        in_specs=[row(D_MODEL), row(A_W), ssm_spec, _layer_spec(e, A_W, D_MODEL), _layer_spec(e, SSM_W, D_MODEL)],
        out_specs=row(D_MODEL),
        out_shape=jax.ShapeDtypeStruct((m, D_MODEL), F32),
        compiler_params=_cparams("parallel"),
        name="even_out_proj",
    )(x, att, ssm, wa, ws)


def _odd_in_kernel(token_tiled, x_ref, g_ref, w_ref, gv_ref, u_ref, v_ref):
    h = _rms_rows(_read_tokens(x_ref, token_tiled), g_ref[...]).astype(BF)
    u_ref[...] = jax.nn.gelu(_dot(h, w_ref[:, :GM_W])).astype(u_ref.dtype)
    zv = jax.nn.gelu(_dot(h, w_ref[:, GM_W:]))
    v_ref[...] = _rms_rows(zv, gv_ref[...]).astype(v_ref.dtype)


def _odd_in_proj(x, m, li, e, g, w, gv, tm, v_dtype, token_tiled=False):
    row = lambda w_: pl.BlockSpec((tm, w_), lambda i: (i, 0))
    return pl.pallas_call(
        functools.partial(_odd_in_kernel, token_tiled),
        grid=(m // tm,),
        in_specs=[_token_spec(tm, lambda i: i, token_tiled), _layer_spec(li, 1, D_MODEL),
                  _layer_spec(e, D_MODEL, 2 * GM_W, single_buffer=True),
                  _layer_spec(e, 1, GM_W)],
        out_specs=[row(GM_W), row(GM_W)],
        out_shape=[jax.ShapeDtypeStruct((m, GM_W), BF), jax.ShapeDtypeStruct((m, GM_W), v_dtype)],
        compiler_params=_cparams("parallel"),
        name="odd_in_proj",
    )(x, g, w, gv)


def _gmlp_rows(csize, x, u_ref, v_ref, wsp_ref, bsp_ref, wo_ref, gated_scr):
    tm = u_ref.shape[0]
    i = lax.broadcasted_iota(jnp.int32, (CHUNK, CHUNK), 0)
    j = lax.broadcasted_iota(jnp.int32, (CHUNK, CHUNK), 1)
    keep = (j <= i) & ((i // csize) == (j // csize))
    bsp = bsp_ref[...]
    for h in range(GM_HEADS):
        ws = jnp.where(keep, wsp_ref[h], 0.0).astype(BF)
        b_col = bsp[:, h:h + 1]
        cols = slice(h * GM_HD, (h + 1) * GM_HD)
        for c in range(tm // CHUNK):
            rows = slice(c * CHUNK, (c + 1) * CHUNK)
            mix = _dot(ws, v_ref[rows, cols].astype(BF)) + b_col
            gated_scr[rows, cols] = (u_ref[rows, cols].astype(F32) * mix).astype(BF)
    return x + _dot(gated_scr[...], wo_ref[...])


def _gmlp_mix_kernel(csize, x_ref, u_ref, v_ref, wsp_ref, bsp_ref, wo_ref, o_ref, gated_scr):
    o_ref[...] = _gmlp_rows(csize, x_ref[...], u_ref, v_ref, wsp_ref, bsp_ref, wo_ref, gated_scr)


def _gmlp_mix(x, u, v, e, wsp, bsp, wo, csize, tm):
    m = x.shape[0]
    row = lambda w_: pl.BlockSpec((tm, w_), lambda i: (i, 0))
    return pl.pallas_call(
        functools.partial(_gmlp_mix_kernel, csize),
        grid=(m // tm,),
        in_specs=[row(D_MODEL), row(GM_W), row(GM_W), _layer_spec(e, GM_HEADS, CHUNK, CHUNK),
                  _layer_spec(e, CHUNK, GM_HEADS), _layer_spec(e, GM_W, D_MODEL)],
        out_specs=row(D_MODEL),
        out_shape=jax.ShapeDtypeStruct((m, D_MODEL), F32),
        scratch_shapes=[pltpu.VMEM((tm, GM_W), BF)],
        compiler_params=_cparams("parallel"),
        name="gmlp_mix",
    )(x, u, v, wsp, bsp, wo)


def _head_norm(z, g):
    cols = []
    for h in range(X_HEADS):
        zh = z[:, h * X_HEAD_DIM:(h + 1) * X_HEAD_DIM]
        cols.append(_rms_rows(zh, g))
    return jnp.concatenate(cols, axis=1)


MEMKV_SEQ_BLOCK = 2
MEM_ROWS = MEM_LEN * X_HEADS


def _head_rows_of(h):
    return pl.ds(h, MEM_LEN, stride=X_HEADS)


def _memory_kv_kernel(mem_ref, g_ref, wk_ref, gk_ref, wv_ref, k_ref, v_ref):
    m = _rms_rows(mem_ref[...], g_ref[...]).astype(BF)
    k = _head_norm(_dot(m, wk_ref[...]), gk_ref[...])
    v = _dot(m, wv_ref[...])
    for s in range(MEMKV_SEQ_BLOCK):
        rows = slice(s * MEM_LEN, (s + 1) * MEM_LEN)
        for h in range(X_HEADS):
            cols = slice(h * X_HEAD_DIM, (h + 1) * X_HEAD_DIM)
            k_ref[s, _head_rows_of(h), :] = k[rows, cols]
            v_ref[s, _head_rows_of(h), :] = v[rows, cols]


def _memory_kv(mem, g_mem, w_k, g_k, w_v, n):
    tm = MEMKV_SEQ_BLOCK * MEM_LEN
    per_layer = lambda a, b: pl.BlockSpec((None, a, b), lambda l, i: (l, 0, 0))
    out_spec = pl.BlockSpec((None, MEMKV_SEQ_BLOCK, MEM_ROWS, X_HEAD_DIM), lambda l, i: (l, i, 0, 0))
    out_sds = jax.ShapeDtypeStruct((DEPTH, n, MEM_ROWS, X_HEAD_DIM), F32)
    return pl.pallas_call(
        _memory_kv_kernel,
        grid=(DEPTH, n // MEMKV_SEQ_BLOCK),
        in_specs=[pl.BlockSpec((tm, D_MODEL), lambda l, i: (i, 0)), per_layer(1, D_MODEL), per_layer(D_MODEL, X_W),
                  per_layer(1, X_HEAD_DIM), per_layer(D_MODEL, X_W)],
        out_specs=[out_spec, out_spec],
        out_shape=[out_sds, out_sds],
        compiler_params=_cparams("parallel", "parallel"),
        name="memory_kv",
    )(mem, g_mem, w_k, g_k, w_v)


def _xattn_rows(x, g_ref, wq_ref, gq_ref, mk_ref, mv_ref, wo_ref):
    q = _head_norm(_dot(_rms_rows(x, g_ref[...]).astype(BF), wq_ref[...]), gq_ref[...]).astype(BF)
    outs = []
    for h in range(X_HEADS):
        cols = slice(h * X_HEAD_DIM, (h + 1) * X_HEAD_DIM)
        s = _dot_nt(q[:, cols], mk_ref[_head_rows_of(h), :].astype(BF)) * (X_HEAD_DIM ** -0.5)
        m = jnp.max(s, axis=-1, keepdims=True)
        p = jnp.exp(s - m)
        p = (p / jnp.sum(p, axis=-1, keepdims=True)).astype(BF)
        outs.append(_dot(p, mv_ref[_head_rows_of(h), :].astype(BF)))
    o = jnp.concatenate(outs, axis=1).astype(BF)
    return x + _dot(o, wo_ref[...])


N_XATTN_REFS = 6
N_ROUTE_REFS = 4
GROUP_ROW = 0


def _xattn_route_store(x, refs, o_ref, grp_ref):
    y = _xattn_rows(x, *refs[:N_XATTN_REFS])
    _write_token_tiles(o_ref, y)
    gf_ref, whi_ref, wlo_ref, b_ref = refs[N_XATTN_REFS:]
    grp = _top_group(_router_logits(_rms_rows(y, gf_ref[...]), whi_ref[...], wlo_ref[...], b_ref[...]))
    grp_ref[...] = jnp.concatenate([grp, jnp.zeros((7, grp.shape[1]), jnp.int32)], axis=0)


def _even_out_xattn_kernel(token_tiled, x_ref, a_ref, s_ref, wa_ref, ws_ref, *rest):
    refs, o_ref, grp_ref = rest[:-2], rest[-2], rest[-1]
    x = (_read_tokens(x_ref, token_tiled) + _dot(a_ref[...], wa_ref[...])
         + _dot(s_ref[...].astype(BF), ws_ref[...]))
    _xattn_route_store(x, refs, o_ref, grp_ref)


def _gmlp_xattn_kernel(csize, token_tiled, x_ref, u_ref, v_ref, wsp_ref, bsp_ref, wo_ref, *rest):
    refs, o_ref, grp_ref, gated_scr = rest[:-3], rest[-3], rest[-2], rest[-1]
    x = _gmlp_rows(csize, _read_tokens(x_ref, token_tiled), u_ref, v_ref, wsp_ref, bsp_ref, wo_ref, gated_scr)
    _xattn_route_store(x, refs, o_ref, grp_ref)


def _xattn_route_specs(li):
    mem = pl.BlockSpec((None, None, MEM_ROWS, X_HEAD_DIM), lambda i, j: (li, i, 0, 0))
    return [_layer_spec(li, 1, D_MODEL), _layer_spec(li, D_MODEL, X_W), _layer_spec(li, 1, X_HEAD_DIM), mem, mem,
            _layer_spec(li, X_W, D_MODEL), _layer_spec(li, 1, D_MODEL), _layer_spec(li, ROUTER_ROWS, D_MODEL),
            _layer_spec(li, ROUTER_ROWS, D_MODEL), _layer_spec(li, ROUTER_ROWS, 1)]


def _token_tile_outputs(n, seq, tq):
    nq = seq // tq
    specs = [_token_spec(tq, lambda i, j: i * nq + j, True), pl.BlockSpec((None, 8, tq), lambda i, j: (i * nq + j, 0, 0))]
    shapes = [jax.ShapeDtypeStruct((n * seq * TT_CHUNKS, LANES), F32), jax.ShapeDtypeStruct((n * nq, 8, tq), jnp.int32)]
    return specs, shapes


def _even_out_xattn(x, token_tiled, att, ssm, e, wa, ws, li, xattn_route_args, n, seq, tq):
    nq = seq // tq
    row = lambda w_: pl.BlockSpec((tq, w_), lambda i, j: (i * nq + j, 0))
    ssm_spec = pl.BlockSpec((tq, SSM_W), lambda i, j: (j, i))
    out_specs, out_shape = _token_tile_outputs(n, seq, tq)
    y, grp = pl.pallas_call(
        functools.partial(_even_out_xattn_kernel, token_tiled),
        grid=(n, nq),
        in_specs=[_token_spec(tq, lambda i, j: i * nq + j, token_tiled), row(A_W), ssm_spec,
                  _layer_spec(e, A_W, D_MODEL), _layer_spec(e, SSM_W, D_MODEL)] + _xattn_route_specs(li),
        out_specs=out_specs,
        out_shape=out_shape,
        compiler_params=_cparams("parallel", "arbitrary"),
        name="even_out_xattn",
    )(x, att, ssm, wa, ws, *xattn_route_args)
    return y, grp[:, GROUP_ROW, :].reshape(n * seq)


def _gmlp_xattn(x, token_tiled, u, v, e, wsp, bsp, wo, csize, li, xattn_route_args, n, seq, tq):
    nq = seq // tq
    row = lambda w_: pl.BlockSpec((tq, w_), lambda i, j: (i * nq + j, 0))
    out_specs, out_shape = _token_tile_outputs(n, seq, tq)
    y, grp = pl.pallas_call(
        functools.partial(_gmlp_xattn_kernel, csize, token_tiled),
        grid=(n, nq),
        in_specs=[_token_spec(tq, lambda i, j: i * nq + j, token_tiled), row(GM_W), row(GM_W),
                  _layer_spec(e, GM_HEADS, CHUNK, CHUNK), _layer_spec(e, CHUNK, GM_HEADS),
                  _layer_spec(e, GM_W, D_MODEL)] + _xattn_route_specs(li),
        out_specs=out_specs,
        out_shape=out_shape,
        scratch_shapes=[pltpu.VMEM((tq, GM_W), BF)],
        compiler_params=_cparams("parallel", "arbitrary"),
        name="gmlp_xattn",
    )(x, u, v, wsp, bsp, wo, *xattn_route_args)
    return y, grp[:, GROUP_ROW, :].reshape(n * seq)


XATTN_SEQ_BLOCK = 8


def _xattn_sample_kernel(t_new, x_ref, g_ref, wq_ref, gq_ref, mk_ref, mv_ref, wo_ref, o_ref):
    pair_rows = 2 * t_new
    x = x_ref[...]
    q = _head_norm(_dot(_rms_rows(x, g_ref[...]).astype(BF), wq_ref[...]), gq_ref[...]).astype(BF)
    tq = lax.broadcasted_iota(jnp.int32, (pair_rows, 2 * MEM_LEN), 0)
    cm = lax.broadcasted_iota(jnp.int32, (pair_rows, 2 * MEM_LEN), 1)
    same_seq = (tq // t_new) == (cm // MEM_LEN)
    npair = XATTN_SEQ_BLOCK // 2
    scores = []
    for j in range(npair):
        qp = q[j * pair_rows:(j + 1) * pair_rows]
        for h in range(X_HEADS):
            hr = _head_rows_of(h)
            kp = jnp.concatenate([mk_ref[2 * j, hr, :], mk_ref[2 * j + 1, hr, :]], axis=0).astype(BF)
            s = _dot_nt(qp[:, h * X_HEAD_DIM:(h + 1) * X_HEAD_DIM], kp)
            scores.append(jnp.where(same_seq, s, NEG_INF))
    s = jnp.concatenate(scores, axis=0) * (X_HEAD_DIM ** -0.5)
    m = jnp.max(s, axis=-1, keepdims=True)
    p = jnp.exp(s - m)
    p = p / jnp.sum(p, axis=-1, keepdims=True)
    outs = []
    for j in range(npair):
        heads = []
        for h in range(X_HEADS):
            hr = _head_rows_of(h)
            vp = jnp.concatenate([mv_ref[2 * j, hr, :], mv_ref[2 * j + 1, hr, :]], axis=0).astype(BF)
            r0 = (j * X_HEADS + h) * pair_rows
            heads.append(_dot(p[r0:r0 + pair_rows].astype(BF), vp))
        outs.append(jnp.concatenate(heads, axis=1))
    o_all = jnp.concatenate(outs, axis=0).astype(BF)
    o_ref[...] = x + _dot(o_all, wo_ref[...])


def _xattn_sample(x, li, g, wq, gq, mk, mv, wo, n, t_new):
    rows = XATTN_SEQ_BLOCK * t_new
    row = pl.BlockSpec((rows, D_MODEL), lambda i: (i, 0))
    mem = pl.BlockSpec((None, XATTN_SEQ_BLOCK, MEM_ROWS, X_HEAD_DIM), lambda i: (li, i, 0, 0))
    return pl.pallas_call(
        functools.partial(_xattn_sample_kernel, t_new),
        grid=(n // XATTN_SEQ_BLOCK,),
        in_specs=[row, _layer_spec(li, 1, D_MODEL), _layer_spec(li, D_MODEL, X_W), _layer_spec(li, 1, X_HEAD_DIM),
                  mem, mem, _layer_spec(li, X_W, D_MODEL)],
        out_specs=row,
        out_shape=jax.ShapeDtypeStruct((n * t_new, D_MODEL), F32),
        compiler_params=_cparams("parallel"),
        name="xattn_sample",
    )(x, g, wq, gq, mk, mv, wo)


ROUTER_ROWS = 32
ROUTER_COARSE_ROW = N_EXPERTS
BIG = 3.0e38


def _first_argmax(vals, vmax):
    idx = jnp.full(vmax.shape, len(vals) - 1, jnp.int32)
    for j in range(len(vals) - 2, -1, -1):
        idx = jnp.where(vals[j] == vmax, j, idx)
    return idx


def _router_logits(h, whi, wlo, bias):
    h_hi = h.astype(BF)
    h_lo = (h - h_hi.astype(F32)).astype(BF)
    return _dot_nt(whi, h_hi) + _dot_nt(whi, h_lo) + _dot_nt(wlo, h_hi) + bias


def _coarse_rows(lg):
    return [lg[ROUTER_COARSE_ROW + g:ROUTER_COARSE_ROW + g + 1, :] for g in range(N_GROUPS)]


def _top_group(lg):
    coarse = _coarse_rows(lg)
    return _first_argmax(coarse, functools.reduce(jnp.maximum, coarse))


def _in_group_gates(lg, grp):
    pick = lambda rows: functools.reduce(lambda f, g: jnp.where(grp == g, rows[g], f),
                                         range(N_GROUPS - 2, -1, -1), rows[N_GROUPS - 1])
    coarse = _coarse_rows(lg)
    csel = pick(coarse)
    p_grp = 1.0 / functools.reduce(lambda a, b: a + b, [jnp.exp(c - csel) for c in coarse])
    fine = [pick([lg[g * EXPERTS_PER_GROUP + j:g * EXPERTS_PER_GROUP + j + 1, :] for g in range(N_GROUPS)])
            for j in range(EXPERTS_PER_GROUP)]
    v1 = functools.reduce(jnp.maximum, fine)
    i1 = _first_argmax(fine, v1)
    rest = [jnp.where(i1 == j, -BIG, fine[j]) for j in range(EXPERTS_PER_GROUP)]
    v2 = functools.reduce(jnp.maximum, rest)
    i2 = _first_argmax(rest, v2)
    e2 = jnp.exp(v2 - v1)
    w1 = p_grp / (1.0 + e2)
    w2 = p_grp * e2 / (1.0 + e2)
    return [jnp.where(i1 == j, w1, 0.0) + jnp.where(i2 == j, w2, 0.0) for j in range(EXPERTS_PER_GROUP)]


def _rows_to_lanes(rows, tm):
    pad = jnp.zeros((LANES - len(rows), tm), F32)
    return jnp.concatenate(list(rows) + [pad], axis=0).T


def _router_gates(h, whi, wlo, bias):
    lg = _router_logits(h, whi, wlo, bias)
    grp = _top_group(lg)
    in_group = _in_group_gates(lg, grp)
    rows = [jnp.where(grp == e // EXPERTS_PER_GROUP, in_group[e % EXPERTS_PER_GROUP], 0.0) for e in range(N_EXPERTS)]
    return _rows_to_lanes(rows, lg.shape[1])


def _moe_kernel(x_ref, g_ref, whi_ref, wlo_ref, b_ref, w1_ref, w3_ref, w2_ref, o_ref, h_scr, gates_scr, acc_scr):
    gi = pl.program_id(1)

    @pl.when(gi == 0)
    def _():
        h = _rms_rows(x_ref[...], g_ref[...])
        h_scr[...] = h.astype(BF)
        gates_scr[...] = _router_gates(h, whi_ref[...], wlo_ref[...], b_ref[...])
        acc_scr[...] = jnp.zeros_like(acc_scr)

    h = h_scr[...]
    gates = gates_scr[...]
    lane = lax.broadcasted_iota(jnp.int32, gates.shape, 1)
    for j in range(EXPERTS_PER_GROUP):
        gcol = jnp.sum(jnp.where(lane == gi * EXPERTS_PER_GROUP + j, gates, 0.0), axis=-1, keepdims=True)
        a = _dot(h, w1_ref[j])
        b = _dot(h, w3_ref[j])
        hid = (a * jax.nn.sigmoid(a)) * b * gcol
        acc_scr[...] += _dot(hid.astype(BF), w2_ref[j])

    @pl.when(gi == pl.num_programs(1) - 1)
    def _():
        o_ref[...] = x_ref[...] + acc_scr[...]


def _moe(x, li, g, whi, wlo, b, w1, w3, w2, tm):
    m = x.shape[0]
    row = lambda w_: pl.BlockSpec((tm, w_), lambda i, e: (i, 0))
    group_w = lambda a, b_: pl.BlockSpec((None, EXPERTS_PER_GROUP, a, b_), lambda i, e: (li, e, 0, 0))
    return pl.pallas_call(
        _moe_kernel,
        grid=(m // tm, N_GROUPS),
        in_specs=[row(D_MODEL), _layer_spec(li, 1, D_MODEL), _layer_spec(li, ROUTER_ROWS, D_MODEL),
                  _layer_spec(li, ROUTER_ROWS, D_MODEL), _layer_spec(li, ROUTER_ROWS, 1),
                  group_w(D_MODEL, D_EXPERT), group_w(D_MODEL, D_EXPERT), group_w(D_EXPERT, D_MODEL)],
        out_specs=row(D_MODEL),
        out_shape=jax.ShapeDtypeStruct((m, D_MODEL), F32),
        scratch_shapes=[pltpu.VMEM((tm, D_MODEL), BF), pltpu.VMEM((tm, LANES), F32), pltpu.VMEM((tm, D_MODEL), F32)],
        compiler_params=_cparams("parallel", "arbitrary"),
        name="moe",
    )(x, g, whi, wlo, b, w1, w3, w2)


MOE_TILE = 256
WRITE_BACK_DMA_PRIORITY = 1


def _sorted_plan(grp, m):
    t = MOE_TILE
    n_tiles = m // t + N_GROUPS
    order = jnp.argsort(grp, stable=True).astype(jnp.int32)
    counts = jnp.sum((grp[None, :] == jnp.arange(N_GROUPS, dtype=jnp.int32)[:, None]).astype(jnp.int32), axis=1)
    tiles_per_group = (counts + t - 1) // t
    tile_end = jnp.cumsum(tiles_per_group)
    tile_start = tile_end - tiles_per_group
    first_token = jnp.cumsum(counts) - counts
    s = jnp.arange(n_tiles, dtype=jnp.int32)
    tile_grp = jnp.minimum(jnp.sum((s[:, None] >= tile_end[None, :]).astype(jnp.int32), axis=1), N_GROUPS - 1)
    n_active = tile_end[N_GROUPS - 1]
    tile_in_group = s - tile_start[tile_grp]
    n_valid = jnp.where(s < n_active, jnp.clip(counts[tile_grp] - tile_in_group * t, 0, t), 0)
    r = jnp.arange(t, dtype=jnp.int32)
    slot_token = order[jnp.clip(first_token[tile_grp][:, None] + tile_in_group[:, None] * t + r[None, :], 0, m - 1)]
    valid = r[None, :] < n_valid[:, None]
    src = jnp.where(valid, slot_token, 0).reshape(-1)
    dst = jnp.where(valid, slot_token, m + (s % 2)[:, None] * t + r[None, :]).reshape(-1)
    dst = jnp.concatenate([m + t + r, dst])
    return src.astype(jnp.int32), (dst * TT_CHUNKS).astype(jnp.int32), tile_grp.astype(jnp.int32)


def _sorted_moe_kernel(m, dst_ref, tgrp_ref, xs_ref, g_ref, whi_ref, wlo_ref, b_ref,
                       w1_ref, w3_ref, w2_ref, out_hbm, obuf0, obuf1, sem_s):
    t = MOE_TILE
    i = pl.program_id(0)
    last = pl.num_programs(0) - 1
    obuf = (obuf0, obuf1)

    c8 = TT_CHUNKS

    def scatter_row(tile, slot, r):
        dst = pl.multiple_of(dst_ref[(tile + 1) * t + r], c8)
        return pltpu.make_async_copy(obuf[slot].at[pl.ds(r * c8, c8), :], out_hbm.at[pl.ds(dst, c8), :],
                                     sem_s.at[slot])

    def whole_scatter(slot, row0):
        return pltpu.make_async_copy(obuf[slot], out_hbm.at[pl.ds(row0, t * c8), :], sem_s.at[slot])

    @pl.when(i == 0)
    def _():
        for slot in range(2):
            obuf[slot][...] = jnp.zeros((t * c8, LANES), F32)
        for slot in range(2):
            whole_scatter(slot, (m + slot * t) * c8).start()
        for slot in range(2):
            whole_scatter(slot, (m + slot * t) * c8).wait()

    def step(cur, nxt):
        for r in range(t):
            scatter_row(i - 1, nxt, r).start()

        x = _read_tokens(xs_ref, True)
        h = _rms_rows(x, g_ref[...])
        lg = _router_logits(h, whi_ref[...], wlo_ref[...], b_ref[...])
        gates = _rows_to_lanes(_in_group_gates(lg, tgrp_ref[i]), t)
        hb = h.astype(BF)
        acc = jnp.zeros((t, D_MODEL), F32)
        for j in range(EXPERTS_PER_GROUP):
            a = _dot(hb, w1_ref[j])
            b = _dot(hb, w3_ref[j])
            hid = (a * jax.nn.sigmoid(a)) * b * gates[:, j:j + 1]
            acc = acc + _dot(hid.astype(BF), w2_ref[j])
        res = x + acc

        @pl.when(i >= 1)
        def _():
            whole_scatter(cur, 0).wait()

        _write_token_tiles(obuf[cur], res)

        @pl.when(i == last)
        def _():
            def issue(r, carry):
                scatter_row(i, cur, r).start()
                return carry
            lax.fori_loop(0, t, issue, 0)
            whole_scatter(cur, 0).wait()
            whole_scatter(nxt, 0).wait()

    for parity in range(2):
        @pl.when(i % 2 == parity)
        def _():
            step(parity, 1 - parity)


SC_CORES = 2
SC_SUBCORES = 16
SC_GATHER_CHUNK = 16


def _sc_gather_tokens(x3, idx):
    n = idx.shape[0]
    workers = SC_CORES * SC_SUBCORES
    chunk = SC_GATHER_CHUNK
    per_worker = n // workers
    n_chunks = per_worker // chunk
    assert n == workers * n_chunks * chunk and n_chunks % 2 == 0
    mesh = plsc.VectorSubcoreMesh(core_axis_name="core", subcore_axis_name="subcore", num_cores=SC_CORES,
                                  num_subcores=SC_SUBCORES)
    buf = pltpu.VMEM((chunk, TT_CHUNKS, LANES), F32)

    @functools.partial(
        pl.kernel, out_type=jax.ShapeDtypeStruct((n, TT_CHUNKS, LANES), F32), mesh=mesh,
        scratch_types=[pltpu.VMEM((n_chunks, chunk), jnp.int32), buf, buf, pltpu.SemaphoreType.DMA((2,)),
                       pltpu.SemaphoreType.DMA((2,))])
    def gather(x_hbm, idx_hbm, out_hbm, idx_v, rows0, rows1, sem_in, sem_out):
        worker = lax.axis_index("subcore") * SC_CORES + lax.axis_index("core")
        rows = (rows0, rows1)
        pltpu.sync_copy(idx_hbm.at[worker], idx_v)

        def fetch(j, b):
            return pltpu.make_async_copy(x_hbm.at[idx_v.at[j]], rows[b], sem_in.at[b])

        def put(j, b):
            off = pl.multiple_of(worker * per_worker + j * chunk, 8)
            return pltpu.make_async_copy(rows[b], out_hbm.at[pl.ds(off, chunk)], sem_out.at[b])

        fetch(0, 0).start()

        @pl.loop(0, n_chunks // 2)
        def _(p):
            j = 2 * p

            @pl.when(p > 0)
            def _():
                put(j - 1, 1).wait()

            fetch(j + 1, 1).start()
            fetch(j, 0).wait()
            put(j, 0).start()
            put(j, 0).wait()

            @pl.when(j + 2 < n_chunks)
            def _():
                fetch(j + 2, 0).start()

            fetch(j + 1, 1).wait()
            put(j + 1, 1).start()

        put(n_chunks - 1, 1).wait()

    return gather(x3, idx.reshape(workers, n_chunks, chunk))


def _sorted_moe(x, grp, m, li, g, whi, wlo, b, w1, w3, w2):
    t = MOE_TILE
    n_tiles = m // t + N_GROUPS
    src, dst, tile_grp = _sorted_plan(grp, m)
    xs = _sc_gather_tokens(x.reshape(-1, TT_CHUNKS, LANES), src).reshape(n_tiles * t * TT_CHUNKS, LANES)
    group_w = lambda a, b_: pl.BlockSpec((None, EXPERTS_PER_GROUP, a, b_), lambda i, dst_, tg: (li, tg[i], 0, 0))
    return pl.pallas_call(
        functools.partial(_sorted_moe_kernel, m),
        grid_spec=pltpu.PrefetchScalarGridSpec(
            num_scalar_prefetch=2,
            grid=(n_tiles,),
            in_specs=[pl.BlockSpec((t * TT_CHUNKS, LANES), lambda i, dst_, tg: (i, 0)), _layer_spec(li, 1, D_MODEL),
                      _layer_spec(li, ROUTER_ROWS, D_MODEL), _layer_spec(li, ROUTER_ROWS, D_MODEL),
                      _layer_spec(li, ROUTER_ROWS, 1),
                      group_w(D_MODEL, D_EXPERT), group_w(D_MODEL, D_EXPERT), group_w(D_EXPERT, D_MODEL)],
            out_specs=pl.BlockSpec(memory_space=pl.ANY),
            scratch_shapes=[pltpu.VMEM((t * TT_CHUNKS, LANES), F32)] * 2 + [pltpu.SemaphoreType.DMA((2,))]),
        out_shape=jax.ShapeDtypeStruct(((m + 2 * t) * TT_CHUNKS, LANES), F32),
        compiler_params=_cparams("arbitrary"),
        name="moe_sorted",
    )(dst, tile_grp, xs, g, whi, wlo, b, w1, w3, w2)


def _heads_kv_major_to_g_major(w, axis):
    shape = w.shape
    split = shape[:axis] + (KV_HEADS, Q_PER_KV, HEAD_DIM) + shape[axis + 1:]
    return jnp.swapaxes(w.reshape(split), axis, axis + 1).reshape(shape)


def _prep_ssm(a_re, a_im, log_dt, b_re, b_im, c_re, c_im, d_skip, w_glu):
    ar, ai = a_re.astype(F32), a_im.astype(F32)
    dt = jnp.exp(log_dt.astype(F32))[..., None]
    mag = jnp.exp(ar * dt)
    abr, abi = mag * jnp.cos(ai * dt), mag * jnp.sin(ai * dt)
    den = ar * ar + ai * ai
    kr = ((abr - 1.0) * ar + abi * ai) / den
    ki = (abi * ar - (abr - 1.0) * ai) / den
    br, bi = b_re.astype(F32), b_im.astype(F32)
    bbr = kr[..., None] * br - ki[..., None] * bi
    bbi = kr[..., None] * bi + ki[..., None] * br
    n_layers = ar.shape[0]
    eye = jnp.eye(SSM_GROUPS, dtype=F32)
    blockdiag_in = lambda b: jnp.einsum('egph,gk->eghkp', b, eye).reshape(n_layers, SSM_W, SSM_FLAT)
    blockdiag_out = lambda c: jnp.einsum('eghp,gk->egpkh', c, eye).reshape(n_layers, SSM_FLAT, SSM_W)
    in_tiles = lambda b: jnp.stack([b[:, j * SSM_IN_TILE:(j + 1) * SSM_IN_TILE, j * SSM_ST_TILE:(j + 1) * SSM_ST_TILE]
                                    for j in range(SSM_IN_TILES)], axis=1)
    out_tiles = lambda c: jnp.stack([c[:, j * SSM_OUT_ST:(j + 1) * SSM_OUT_ST, j * SSM_OUT_TILE:(j + 1) * SSM_OUT_TILE]
                                     for j in range(SSM_OUT_TILES)], axis=1)
    bcat = jnp.concatenate([in_tiles(blockdiag_in(bbr)), in_tiles(blockdiag_in(bbi))], axis=3).astype(BF)
    ccat = jnp.concatenate([out_tiles(blockdiag_out(c_re.astype(F32))), -out_tiles(blockdiag_out(c_im.astype(F32)))],
                           axis=2).astype(BF)
    return (abr.reshape(n_layers, 1, SSM_FLAT), abi.reshape(n_layers, 1, SSM_FLAT), bcat, ccat,
            d_skip.astype(F32).reshape(n_layers, 1, SSM_W), w_glu.astype(BF))


def _run_trunk(x, n, seq, mem_k, mem_v, caches, p):
    prompt = caches is None
    tm = 512
    m = n * seq
    wk_out, wv_out, hr_out, hi_out, cv_out = [], [], [], [], []
    xattn_args = (p['g_xattn'], p['w_xq'], p['g_xq'], mem_k, mem_v, p['w_xo'])
    router_args = (p['g_ffn'], p['w_router_hi'], p['w_router_lo'], p['b_router'])
    expert_args = (p['w_e1'], p['w_e3'], p['w_e2'])
    tiled = False
    for li in range(DEPTH):
        e = li // 2
        if li % 2 == 0:
            q, k, v, u = _even_in_proj(x, li, e, p['g_mix'], p['w_in_even'], p['g_q'], p['g_k'], p['bd'], tm, n, seq,
                                       token_tiled=tiled)
            if prompt:
                att = _swa_prompt(q, k, v, p['sinks'], e, n, seq)
                k3, v3 = k.reshape(n, seq, KV_W), v.reshape(n, seq, KV_W)
                new_k, new_v = k3[:, seq - WINDOW:], v3[:, seq - WINDOW:]
                ssm, h_re, h_im = _s5_glu(u.reshape(seq, n, SSM_W), None, p['ssm'], e, n, seq, 128)
                x, grp = _even_out_xattn(x, tiled, att, ssm.reshape(seq, n * SSM_W), e, p['w_out_att'], p['w_out_ssm'],
                                         li, xattn_args + router_args, n, seq, tm)
            else:
                win_k, win_v, ssm_re, ssm_im = caches
                att, new_k, new_v = _swa_sample(q, k, v, win_k, win_v, p['sinks'], e, n, seq)
                u_tm = jnp.swapaxes(u.reshape(n, seq, SSM_W), 0, 1)
                ssm, h_re, h_im = _s5_glu(u_tm, (ssm_re, ssm_im), p['ssm'], e, n, seq, seq)
                ssm = jnp.swapaxes(ssm, 0, 1).reshape(m, SSM_W)
                x = _even_out_proj(x, att, ssm, e, p['w_out_att'], p['w_out_ssm'], tm, n, seq)
            wk_out.append(new_k.reshape(n, WINDOW, KV_HEADS, HEAD_DIM))
            wv_out.append(new_v.reshape(n, WINDOW, KV_HEADS, HEAD_DIM))
            hr_out.append(h_re.reshape(n, SSM_GROUPS, SSM_STATE))
            hi_out.append(h_im.reshape(n, SSM_GROUPS, SSM_STATE))
        else:
            u, v = _odd_in_proj(x, m, li, e, p['g_mix'], p['w_in_odd'], p['g_v'], tm, BF if prompt else F32,
                                token_tiled=tiled)
            wsp, bsp, csize = p['gmlp_prompt'] if prompt else p['gmlp_sample']
            if prompt:
                x, grp = _gmlp_xattn(x, tiled, u, v, e, wsp, bsp, p['w_out_odd'], csize, li, xattn_args + router_args,
                                     n, seq, tm)
            else:
                x = _gmlp_mix(x, u, v, e, wsp, bsp, p['w_out_odd'], csize, tm)
                cv_out.append(v.reshape(n, seq, GM_W))
        if prompt:
            x = _sorted_moe(x, grp, m, li, *router_args, *expert_args)
            tiled = True
        else:
            x = _xattn_sample(x, li, *xattn_args, n, seq)
            x = _moe(x, li, *router_args, *expert_args, tm)
    if tiled:
        x = _untile(x, m, tm)
    cv = None if prompt else jnp.stack(cv_out)
    return x, jnp.stack(wk_out), jnp.stack(wv_out), jnp.stack(hr_out), jnp.stack(hi_out), cv


def kernel(x_prompt, x_sample, cache_win_k, cache_win_v, state_ssm_re, state_ssm_im, cache_mem_k, cache_mem_v, mem_prompt, g_mix, g_xattn, g_ffn, g_mem, w_in_even, g_q, g_k, sinks, ssm_a_re, ssm_a_im, ssm_log_dt, ssm_b_re, ssm_b_im, ssm_c_re, ssm_c_im, ssm_d, w_glu, w_out_even, w_in_odd, g_v, w_spatial, b_spatial, w_out_odd, w_xq, g_xq, w_xk, g_xk, w_xv, w_xo, w_coarse, b_coarse, w_fine, b_fine, w_e1, w_e3, w_e2):
    batch, seq = x_prompt.shape[0], x_prompt.shape[1]
    dec_batch, dec_seq = x_sample.shape[0], x_sample.shape[1]
    n_even = w_in_even.shape[0]

    row1 = lambda a: a.astype(F32)[:, None, :]
    p = {}
    p['g_mix'], p['g_xattn'], p['g_ffn'] = row1(g_mix), row1(g_xattn), row1(g_ffn)
    w_in_q = _heads_kv_major_to_g_major(w_in_even[:, :, :A_W], 2)
    p['w_in_even'] = jnp.concatenate([w_in_q, w_in_even[:, :, A_W:]], axis=2).astype(BF)
    p['g_q'] = jnp.tile(g_q.astype(F32), (1, A_HEADS))[:, None, :]
    p['g_k'] = jnp.tile(g_k.astype(F32), (1, KV_HEADS))[:, None, :]
    head_id = jnp.arange(A_W) // HEAD_DIM
    p['bd'] = (head_id[:, None] == head_id[None, :]).astype(BF)
    p['sinks'] = sinks.astype(F32).reshape(-1)
    p['ssm'] = _prep_ssm(ssm_a_re, ssm_a_im, ssm_log_dt, ssm_b_re, ssm_b_im, ssm_c_re, ssm_c_im, ssm_d, w_glu)
    p['w_out_att'] = _heads_kv_major_to_g_major(w_out_even[:, :A_W, :], 1).astype(BF)
    p['w_out_ssm'] = w_out_even[:, A_W:, :].astype(BF)
    p['w_in_odd'] = w_in_odd.astype(BF)
    p['g_v'] = row1(g_v)
    p['w_out_odd'] = w_out_odd.astype(BF)
    cs = min(dec_seq, CHUNK)
    reps = CHUNK // cs
    p['gmlp_prompt'] = (w_spatial.astype(F32), jnp.swapaxes(b_spatial.astype(F32), 1, 2), CHUNK)
    p['gmlp_sample'] = (jnp.tile(w_spatial[:, :, :cs, :cs].astype(F32), (1, 1, reps, reps)),
                        jnp.tile(jnp.swapaxes(b_spatial[:, :, :cs].astype(F32), 1, 2), (1, reps, 1)), cs)
    p['w_xq'] = w_xq.astype(BF)
    p['g_xq'] = row1(g_xq)
    p['w_xo'] = w_xo.astype(BF)
    w_router = jnp.swapaxes(jnp.concatenate([w_fine, w_coarse], axis=2).astype(F32), 1, 2)
    w_router = jnp.pad(w_router, ((0, 0), (0, ROUTER_ROWS - w_router.shape[1]), (0, 0)))
    p['w_router_hi'] = w_router.astype(BF)
    p['w_router_lo'] = (w_router - p['w_router_hi'].astype(F32)).astype(BF)
    b_router = jnp.concatenate([b_fine, b_coarse], axis=1).astype(F32)
    p['b_router'] = jnp.pad(b_router, ((0, 0), (0, ROUTER_ROWS - b_router.shape[1])))[:, :, None]
    p['w_e1'], p['w_e3'], p['w_e2'] = w_e1.astype(BF), w_e3.astype(BF), w_e2.astype(BF)

    mem2d = mem_prompt.reshape(batch * MEM_LEN, D_MODEL)
    pk, pv = _memory_kv(mem2d, row1(g_mem), w_xk.astype(BF), row1(g_xk), w_xv.astype(BF), batch)
    y_p, p_wk, p_wv, p_hr, p_hi, _ = _run_trunk(x_prompt.reshape(batch * seq, D_MODEL), batch, seq, pk, pv, None, p)

    caches = (cache_win_k.reshape(n_even, dec_batch, WINDOW, KV_W), cache_win_v.reshape(n_even, dec_batch, WINDOW, KV_W),
              state_ssm_re.reshape(n_even, dec_batch, SSM_FLAT), state_ssm_im.reshape(n_even, dec_batch, SSM_FLAT))
    y_s, s_wk, s_wv, s_hr, s_hi, s_cv = _run_trunk(x_sample.reshape(dec_batch * dec_seq, D_MODEL), dec_batch, dec_seq,
                                                   cache_mem_k.reshape(DEPTH, dec_batch, MEM_ROWS, X_HEAD_DIM),
                                                   cache_mem_v.reshape(DEPTH, dec_batch, MEM_ROWS, X_HEAD_DIM), caches, p)

    mem_shape = (DEPTH, batch, MEM_LEN, X_HEADS, X_HEAD_DIM)
    return (y_p.reshape(batch, seq, D_MODEL), y_s.reshape(dec_batch, dec_seq, D_MODEL), p_wk, p_wv, p_hr, p_hi,
            pk.reshape(mem_shape), pv.reshape(mem_shape), s_wk, s_wv, s_hr, s_hi, s_cv)
```

```python
import functools
import math

import jax
import jax.numpy as jnp
from jax import lax
from jax.experimental import pallas as pl
from jax.experimental.pallas import tpu as pltpu

F32 = jnp.float32
BF = jnp.bfloat16

D_MODEL = 1024
DEPTH = 4
A_W = 512
HEAD_DIM = 64
A_HEADS = 8
KV_HEADS = 2
Q_PER_KV = 4
KV_W = 128
WINDOW = 128
SSM_W = 512
SSM_CH = 16
SSM_GROUPS = 32
SSM_STATE = 64
SSM_FLAT = SSM_GROUPS * SSM_STATE
IN_EVEN = A_W + 2 * KV_W + SSM_W
CHUNK = 128
GM_W = 2048
GM_HEADS = 8
GM_HD = 256
MEM_LEN = 256
X_HEADS = 4
X_HEAD_DIM = 128
X_W = 512
N_GROUPS = 4
EXPERTS_PER_GROUP = 4
N_EXPERTS = 16
D_EXPERT = 256
EPS = 1e-6
NEG_INF = -1e30
LANES = 128
VMEM_LIMIT_BYTES = 52 * 1024 * 1024
ALIBI_SLOPES = tuple(2.0 ** (-8.0 * (h + 1) / A_HEADS) for h in range(A_HEADS))
SSM_SEQ_BLOCK = 8


def _cparams(*sem):
    return pltpu.CompilerParams(dimension_semantics=sem, vmem_limit_bytes=VMEM_LIMIT_BYTES)


def _rms_rows(x, g):
    return x * lax.rsqrt(jnp.mean(x * x, axis=-1, keepdims=True) + EPS) * g


TT_CHUNKS = D_MODEL // LANES


def _read_tokens(ref, token_tiled):
    if not token_tiled:
        return ref[...]
    rows = ref.shape[0] // TT_CHUNKS
    return jnp.concatenate([ref[pl.ds(c, rows, stride=TT_CHUNKS), :] for c in range(TT_CHUNKS)], axis=1)


def _write_token_tiles(ref, val):
    rows = val.shape[0]
    for c in range(TT_CHUNKS):
        ref[pl.ds(c, rows, stride=TT_CHUNKS), :] = val[:, c * LANES:(c + 1) * LANES]


def _token_spec(rows, index, token_tiled):
    if token_tiled:
        return pl.BlockSpec((rows * TT_CHUNKS, LANES), lambda *ids: (index(*ids), 0))
    return pl.BlockSpec((rows, D_MODEL), lambda *ids: (index(*ids), 0))


def _untile_kernel(x_ref, o_ref):
    o_ref[...] = _read_tokens(x_ref, True)


def _untile(x, m, tm):
    return pl.pallas_call(
        _untile_kernel,
        grid=(m // tm,),
        in_specs=[_token_spec(tm, lambda i: i, True)],
        out_specs=pl.BlockSpec((tm, D_MODEL), lambda i: (i, 0)),
        out_shape=jax.ShapeDtypeStruct((m, D_MODEL), F32),
        compiler_params=_cparams("parallel"),
        name="untile",
    )(x)


def _dot(a, b):
    return jnp.dot(a, b, preferred_element_type=F32)


def _dot_nt(a, b):
    return lax.dot_general(a, b, (((1,), (1,)), ((), ())), preferred_element_type=F32)


def _full(shape):
    nd = len(shape)
    return pl.BlockSpec(shape, lambda *_: (0,) * nd)


def _layer_spec(li, *shape, single_buffer=False):
    nd = len(shape)
    mode = pl.Buffered(1) if single_buffer else None
    return pl.BlockSpec((None,) + shape, lambda *_: (li,) + (0,) * nd, pipeline_mode=mode)


def _even_in_kernel(token_tiled, x_ref, g_ref, w_ref, gq_ref, gk_ref, bd_ref, q_ref, k_ref, v_ref, u_ref):
    h = _rms_rows(_read_tokens(x_ref, token_tiled), g_ref[...]).astype(BF)
    z = _dot(h, w_ref[...])
    q = z[:, :A_W]
    k = z[:, A_W:A_W + KV_W]
    bd = bd_ref[...]
    qms = _dot((q * q).astype(BF), bd) * (1.0 / HEAD_DIM)
    kms = _dot((k * k).astype(BF), bd[:KV_W, :KV_W]) * (1.0 / HEAD_DIM)
    q_ref[...] = q * lax.rsqrt(qms + EPS) * gq_ref[...]
    k_ref[...] = k * lax.rsqrt(kms + EPS) * gk_ref[...]
    v_ref[...] = z[:, A_W + KV_W:A_W + 2 * KV_W]
    u_ref[...] = z[:, A_W + 2 * KV_W:]


def _time_major_spec(tm, n, seq, width):
    per_seq = seq // tm
    return pl.BlockSpec((tm, width), lambda i: (i % per_seq, i // per_seq))


def _even_in_proj(x, li, e, g, w, gq, gk, bd, tm, n, seq, token_tiled=False):
    m = n * seq
    row = lambda w_: pl.BlockSpec((tm, w_), lambda i: (i, 0))
    return pl.pallas_call(
        functools.partial(_even_in_kernel, token_tiled),
        grid=(m // tm,),
        in_specs=[_token_spec(tm, lambda i: i, token_tiled), _layer_spec(li, 1, D_MODEL),
                  _layer_spec(e, D_MODEL, IN_EVEN), _layer_spec(e, 1, A_W),
                  _layer_spec(e, 1, KV_W), _full((A_W, A_W))],
        out_specs=[row(A_W), row(KV_W), row(KV_W), _time_major_spec(tm, n, seq, SSM_W) if seq % tm == 0 else row(SSM_W)],
        out_shape=[jax.ShapeDtypeStruct((m, A_W), F32), jax.ShapeDtypeStruct((m, KV_W), F32),
                   jax.ShapeDtypeStruct((m, KV_W), F32),
                   jax.ShapeDtypeStruct((seq, n * SSM_W) if seq % tm == 0 else (m, SSM_W), F32)],
        compiler_params=_cparams("parallel"),
        name="even_in_proj",
    )(x, g, w, gq, gk, bd)


def _head_rows(q, lane_lo):
    rows = []
    for kv in range(KV_HEADS):
        keep = lane_lo if kv == 0 else jnp.logical_not(lane_lo)
        for g in range(Q_PER_KV):
            rows.append(jnp.where(keep, q[:, g * LANES:(g + 1) * LANES], 0.0))
    return jnp.concatenate(rows, axis=0)


def _head_cols(o, r, lane_lo):
    cols = []
    for g in range(Q_PER_KV):
        cols.append(jnp.where(lane_lo, o[g * r:(g + 1) * r], o[(Q_PER_KV + g) * r:(Q_PER_KV + g + 1) * r]))
    return jnp.concatenate(cols, axis=1)


def _swa_prompt_kernel(e, sinks_ref, q_ref, kc_ref, kp_ref, vc_ref, vp_ref, o_ref):
    b = pl.program_id(1)
    lane_lo = lax.broadcasted_iota(jnp.int32, (1, LANES), 1) < HEAD_DIM
    qrows = _head_rows(q_ref[...], lane_lo).astype(BF)
    kcat = jnp.concatenate([kp_ref[...], kc_ref[...]], axis=0).astype(BF)
    vcat = jnp.concatenate([vp_ref[...], vc_ref[...]], axis=0).astype(BF)
    s = _dot_nt(qrows, kcat) * (HEAD_DIM ** -0.5)
    r = lax.broadcasted_iota(jnp.int32, (WINDOW, 2 * WINDOW), 0)
    c = lax.broadcasted_iota(jnp.int32, (WINDOW, 2 * WINDOW), 1)
    dist = WINDOW + r - c
    valid = (dist >= 0) & (dist <= WINDOW) & ((c >= WINDOW) | (b > 0))
    dist_f = dist.astype(F32)
    ps = []
    for h in range(A_HEADS):
        sh = s[h * WINDOW:(h + 1) * WINDOW] - ALIBI_SLOPES[h] * dist_f
        sh = jnp.where(valid, sh, NEG_INF)
        sk = sinks_ref[e * A_HEADS + h]
        m = jnp.maximum(jnp.max(sh, axis=-1, keepdims=True), sk)
        p = jnp.exp(sh - m)
        den = jnp.sum(p, axis=-1, keepdims=True) + jnp.exp(sk - m)
        ps.append((p / den).astype(BF))
    o = _dot(jnp.concatenate(ps, axis=0), vcat)
    o_ref[...] = _head_cols(o, WINDOW, lane_lo).astype(o_ref.dtype)


def _swa_prompt(q, k, v, sinks, e, n, seq):
    nb = seq // WINDOW
    cur = lambda w_: pl.BlockSpec((WINDOW, w_), lambda i, b: (i * nb + b, 0))
    prev = lambda w_: pl.BlockSpec((WINDOW, w_), lambda i, b: (i * nb + jnp.maximum(b - 1, 0), 0))
    return pl.pallas_call(
        functools.partial(_swa_prompt_kernel, e),
        grid=(n, nb),
        in_specs=[pl.BlockSpec(memory_space=pltpu.SMEM), cur(A_W), cur(KV_W), prev(KV_W), cur(KV_W), prev(KV_W)],
        out_specs=cur(A_W),
        out_shape=jax.ShapeDtypeStruct((n * seq, A_W), BF),
        compiler_params=_cparams("parallel", "arbitrary"),
        name="swa_prompt",
    )(sinks, q, k, k, v, v)


SWA_SEQ_BLOCK = 8


def _swa_sample_kernel(t_new, e, sinks_ref, q_ref, kn_ref, vn_ref, wk_ref, wv_ref, o_ref, nwk_ref, nwv_ref):
    nseq = SWA_SEQ_BLOCK
    pair_rows = 2 * t_new
    npair = nseq // 2
    kn = kn_ref[...]
    vn = vn_ref[...]
    for i in range(nseq):
        nwk_ref[i, 0:WINDOW - t_new, :] = wk_ref[i, t_new:WINDOW, :]
        nwk_ref[i, WINDOW - t_new:WINDOW, :] = kn[i * t_new:(i + 1) * t_new, :]
        nwv_ref[i, 0:WINDOW - t_new, :] = wv_ref[i, t_new:WINDOW, :]
        nwv_ref[i, WINDOW - t_new:WINDOW, :] = vn[i * t_new:(i + 1) * t_new, :]

    lane_lo = lax.broadcasted_iota(jnp.int32, (1, LANES), 1) < HEAD_DIM
    q = q_ref[...]
    knb = kn.astype(BF)
    vnb = vn.astype(BF)
    hr = A_HEADS * pair_rows
    qrows = [_head_rows(q[j * pair_rows:(j + 1) * pair_rows], lane_lo).astype(BF) for j in range(npair)]
    scale = HEAD_DIM ** -0.5
    s_new_all = _dot_nt(jnp.concatenate(qrows, axis=0), knb) * scale

    tq = lax.broadcasted_iota(jnp.int32, (pair_rows, 2 * WINDOW), 0)
    cw = lax.broadcasted_iota(jnp.int32, (pair_rows, 2 * WINDOW), 1)
    dist_w = WINDOW + (tq % t_new) - (cw % WINDOW)
    valid_w = ((tq // t_new) == (cw // WINDOW)) & (dist_w <= WINDOW)
    dist_wf = dist_w.astype(F32)
    tqn = lax.broadcasted_iota(jnp.int32, (pair_rows, nseq * t_new), 0)
    cn = lax.broadcasted_iota(jnp.int32, (pair_rows, nseq * t_new), 1)
    dist_n = (tqn % t_new) - (cn % t_new)
    dist_nf = dist_n.astype(F32)

    p_new_all = []
    o_win_all = []
    for j in range(npair):
        kwin = jnp.concatenate([wk_ref[2 * j], wk_ref[2 * j + 1]], axis=0).astype(BF)
        vwin = jnp.concatenate([wv_ref[2 * j], wv_ref[2 * j + 1]], axis=0).astype(BF)
        s_win = _dot_nt(qrows[j], kwin) * scale
        valid_n = ((2 * j + tqn // t_new) == (cn // t_new)) & (dist_n >= 0)
        p_win = []
        for h in range(A_HEADS):
            sw = jnp.where(valid_w, s_win[h * pair_rows:(h + 1) * pair_rows] - ALIBI_SLOPES[h] * dist_wf, NEG_INF)
            sn = s_new_all[j * hr + h * pair_rows:j * hr + (h + 1) * pair_rows]
            sn = jnp.where(valid_n, sn - ALIBI_SLOPES[h] * dist_nf, NEG_INF)
            sk = sinks_ref[e * A_HEADS + h]
            m = jnp.maximum(jnp.maximum(jnp.max(sw, axis=-1, keepdims=True), jnp.max(sn, axis=-1, keepdims=True)), sk)
            pw = jnp.exp(sw - m)
            pn = jnp.exp(sn - m)
            den = jnp.sum(pw, axis=-1, keepdims=True) + jnp.sum(pn, axis=-1, keepdims=True) + jnp.exp(sk - m)
            p_win.append((pw / den).astype(BF))
            p_new_all.append((pn / den).astype(BF))
        o_win_all.append(_dot(jnp.concatenate(p_win, axis=0), vwin))
    o_new = _dot(jnp.concatenate(p_new_all, axis=0), vnb)
    outs = [_head_cols(o_win_all[j] + o_new[j * hr:(j + 1) * hr], pair_rows, lane_lo) for j in range(npair)]
    o_ref[...] = jnp.concatenate(outs, axis=0).astype(o_ref.dtype)


def _swa_sample(q, kn, vn, win_k, win_v, sinks, e, n, t_new):
    rows = SWA_SEQ_BLOCK * t_new
    row = lambda w_: pl.BlockSpec((rows, w_), lambda i: (i, 0))
    win = pl.BlockSpec((SWA_SEQ_BLOCK, WINDOW, KV_W), lambda i: (i, 0, 0))
    win_in = pl.BlockSpec((None, SWA_SEQ_BLOCK, WINDOW, KV_W), lambda i: (e, i, 0, 0))
    return pl.pallas_call(
        functools.partial(_swa_sample_kernel, t_new, e),
        grid=(n // SWA_SEQ_BLOCK,),
        in_specs=[pl.BlockSpec(memory_space=pltpu.SMEM), row(A_W), row(KV_W), row(KV_W), win_in, win_in],
        out_specs=[row(A_W), win, win],
        out_shape=[jax.ShapeDtypeStruct((n * t_new, A_W), BF),
                   jax.ShapeDtypeStruct((n, WINDOW, KV_W), F32), jax.ShapeDtypeStruct((n, WINDOW, KV_W), F32)],
        compiler_params=_cparams("parallel"),
        name="swa_sample",
    )(sinks, q, kn, vn, win_k, win_v)


SSM_LANE_CHUNK = 1024
SSM_IN_TILE = LANES
SSM_IN_TILES = SSM_W // SSM_IN_TILE
SSM_ST_TILE = SSM_IN_TILE // SSM_CH * SSM_STATE
SSM_OUT_TILE = 256
SSM_OUT_TILES = SSM_W // SSM_OUT_TILE
SSM_OUT_ST = SSM_OUT_TILE // SSM_CH * SSM_STATE


def _s5_kernel(tc, has_h0, *refs):
    if has_h0:
        (u_ref, h0r_ref, h0i_ref, abr_ref, abi_ref, bcat_ref, ccat_ref, d_ref, wglu_ref,
         o_ref, hr_ref, hi_ref, st_scr, car_scr) = refs
    else:
        (u_ref, abr_ref, abi_ref, bcat_ref, ccat_ref, d_ref, wglu_ref,
         o_ref, hr_ref, hi_ref, st_scr, car_scr) = refs
    nb = SSM_SEQ_BLOCK
    ci = pl.program_id(1)

    @pl.when(ci == 0)
    def _():
        if has_h0:
            car_scr[:, :SSM_FLAT] = h0r_ref[...]
            car_scr[:, SSM_FLAT:] = h0i_ref[...]
        else:
            car_scr[...] = jnp.zeros_like(car_scr)

    ut = u_ref[...].reshape(tc * nb, SSM_W)
    ub = ut.astype(BF)
    for jt in range(SSM_IN_TILES):
        bu = _dot(ub[:, jt * SSM_IN_TILE:(jt + 1) * SSM_IN_TILE], bcat_ref[jt])
        st_scr[:, jt * SSM_ST_TILE:(jt + 1) * SSM_ST_TILE] = bu[:, :SSM_ST_TILE]
        st_scr[:, SSM_FLAT + jt * SSM_ST_TILE:SSM_FLAT + (jt + 1) * SSM_ST_TILE] = bu[:, SSM_ST_TILE:]

    for lc in range(SSM_FLAT // SSM_LANE_CHUNK):
        lo = lc * SSM_LANE_CHUNK
        re_sl = slice(lo, lo + SSM_LANE_CHUNK)
        im_sl = slice(SSM_FLAT + lo, SSM_FLAT + lo + SSM_LANE_CHUNK)
        ar = jnp.broadcast_to(abr_ref[:, re_sl], (nb, SSM_LANE_CHUNK))
        ai = jnp.broadcast_to(abi_ref[:, re_sl], (nb, SSM_LANE_CHUNK))

        def step(t, carry):
            hr, hi = carry
            rows = pl.ds(pl.multiple_of(t * nb, nb), nb)
            nr = ar * hr - ai * hi + st_scr[rows, re_sl]
            ni = ar * hi + ai * hr + st_scr[rows, im_sl]
            st_scr[rows, re_sl] = nr
            st_scr[rows, im_sl] = ni
            return nr, ni

        hr, hi = lax.fori_loop(0, tc, step, (car_scr[:, re_sl], car_scr[:, im_sl]))
        car_scr[:, re_sl] = hr
        car_scr[:, im_sl] = hi

    ys = []
    for ot in range(SSM_OUT_TILES):
        re_sl = slice(ot * SSM_OUT_ST, (ot + 1) * SSM_OUT_ST)
        im_sl = slice(SSM_FLAT + ot * SSM_OUT_ST, SSM_FLAT + (ot + 1) * SSM_OUT_ST)
        ys.append(_dot(st_scr[:, re_sl].astype(BF), ccat_ref[ot, :SSM_OUT_ST, :])
                  + _dot(st_scr[:, im_sl].astype(BF), ccat_ref[ot, SSM_OUT_ST:, :]))
    y = jnp.concatenate(ys, axis=1) + d_ref[...] * ut
    y = jax.nn.gelu(y).astype(BF)
    g = _dot(y, wglu_ref[...])
    o_ref[...] = (g[:, :SSM_W] * jax.nn.sigmoid(g[:, SSM_W:])).reshape(tc, nb, SSM_W)

    @pl.when(ci == pl.num_programs(1) - 1)
    def _():
        hr_ref[...] = car_scr[:, :SSM_FLAT]
        hi_ref[...] = car_scr[:, SSM_FLAT:]


def _s5_glu(u, h0, ssm, e, n, seq, tc):
    nb = SSM_SEQ_BLOCK
    abr, abi, bcat, ccat, dsk, wglu = ssm
    u_spec = pl.BlockSpec((tc, nb, SSM_W), lambda i, c: (c, i, 0))
    st_spec = pl.BlockSpec((nb, SSM_FLAT), lambda i, c: (i, 0))
    h0_spec = pl.BlockSpec((None, nb, SSM_FLAT), lambda i, c: (e, i, 0))
    consts = [_layer_spec(e, 1, SSM_FLAT), _layer_spec(e, 1, SSM_FLAT),
              _layer_spec(e, SSM_IN_TILES, SSM_IN_TILE, 2 * SSM_ST_TILE),
              _layer_spec(e, SSM_OUT_TILES, 2 * SSM_OUT_ST, SSM_OUT_TILE), _layer_spec(e, 1, SSM_W),
              _layer_spec(e, SSM_W, 2 * SSM_W)]
    has_h0 = h0 is not None
    in_specs = [u_spec] + ([h0_spec, h0_spec] if has_h0 else []) + consts
    args = [u] + (list(h0) if has_h0 else []) + [abr, abi, bcat, ccat, dsk, wglu]
    return pl.pallas_call(
        functools.partial(_s5_kernel, tc, has_h0),
        grid=(n // nb, seq // tc),
        in_specs=in_specs,
        out_specs=[u_spec, st_spec, st_spec],
        out_shape=[jax.ShapeDtypeStruct((seq, n, SSM_W), F32), jax.ShapeDtypeStruct((n, SSM_FLAT), F32),
                   jax.ShapeDtypeStruct((n, SSM_FLAT), F32)],
        scratch_shapes=[pltpu.VMEM((nb * tc, 2 * SSM_FLAT), F32), pltpu.VMEM((nb, 2 * SSM_FLAT), F32)],
        compiler_params=_cparams("parallel", "arbitrary"),
        name="s5_glu",
    )(*args)


def _even_out_kernel(x_ref, a_ref, s_ref, wa_ref, ws_ref, o_ref):
    o_ref[...] = x_ref[...] + _dot(a_ref[...], wa_ref[...]) + _dot(s_ref[...].astype(BF), ws_ref[...])


def _even_out_proj(x, att, ssm, e, wa, ws, tm, n, seq):
    m = x.shape[0]
    row = lambda w_: pl.BlockSpec((tm, w_), lambda i: (i, 0))
    ssm_spec = _time_major_spec(tm, n, seq, SSM_W) if seq % tm == 0 else row(SSM_W)
    return pl.pallas_call(
        _even_out_kernel,
        grid=(m // tm,),
        in_specs=[row(D_MODEL), row(A_W), ssm_spec, _layer_spec(e, A_W, D_MODEL), _layer_spec(e, SSM_W, D_MODEL)],
        out_specs=row(D_MODEL),
        out_shape=jax.ShapeDtypeStruct((m, D_MODEL), F32),
        compiler_params=_cparams("parallel"),
        name="even_out_proj",
    )(x, att, ssm, wa, ws)


def _odd_in_kernel(token_tiled, x_ref, g_ref, w_ref, gv_ref, u_ref, v_ref):
    h = _rms_rows(_read_tokens(x_ref, token_tiled), g_ref[...]).astype(BF)
    u_ref[...] = jax.nn.gelu(_dot(h, w_ref[:, :GM_W])).astype(u_ref.dtype)
    zv = jax.nn.gelu(_dot(h, w_ref[:, GM_W:]))
    v_ref[...] = _rms_rows(zv, gv_ref[...]).astype(v_ref.dtype)


def _odd_in_proj(x, m, li, e, g, w, gv, tm, v_dtype, token_tiled=False):
    row = lambda w_: pl.BlockSpec((tm, w_), lambda i: (i, 0))
    return pl.pallas_call(
        functools.partial(_odd_in_kernel, token_tiled),
        grid=(m // tm,),
        in_specs=[_token_spec(tm, lambda i: i, token_tiled), _layer_spec(li, 1, D_MODEL),
                  _layer_spec(e, D_MODEL, 2 * GM_W, single_buffer=True),
                  _layer_spec(e, 1, GM_W)],
        out_specs=[row(GM_W), row(GM_W)],
        out_shape=[jax.ShapeDtypeStruct((m, GM_W), BF), jax.ShapeDtypeStruct((m, GM_W), v_dtype)],
        compiler_params=_cparams("parallel"),
        name="odd_in_proj",
    )(x, g, w, gv)


def _gmlp_rows(csize, x, u_ref, v_ref, wsp_ref, bsp_ref, wo_ref, gated_scr):
    tm = u_ref.shape[0]
    i = lax.broadcasted_iota(jnp.int32, (CHUNK, CHUNK), 0)
    j = lax.broadcasted_iota(jnp.int32, (CHUNK, CHUNK), 1)
    keep = (j <= i) & ((i // csize) == (j // csize))
    bsp = bsp_ref[...]
    for h in range(GM_HEADS):
        ws = jnp.where(keep, wsp_ref[h], 0.0).astype(BF)
        b_col = bsp[:, h:h + 1]
        cols = slice(h * GM_HD, (h + 1) * GM_HD)
        for c in range(tm // CHUNK):
            rows = slice(c * CHUNK, (c + 1) * CHUNK)
            mix = _dot(ws, v_ref[rows, cols].astype(BF)) + b_col
            gated_scr[rows, cols] = (u_ref[rows, cols].astype(F32) * mix).astype(BF)
    return x + _dot(gated_scr[...], wo_ref[...])


def _gmlp_mix_kernel(csize, x_ref, u_ref, v_ref, wsp_ref, bsp_ref, wo_ref, o_ref, gated_scr):
    o_ref[...] = _gmlp_rows(csize, x_ref[...], u_ref, v_ref, wsp_ref, bsp_ref, wo_ref, gated_scr)


def _gmlp_mix(x, u, v, e, wsp, bsp, wo, csize, tm):
    m = x.shape[0]
    row = lambda w_: pl.BlockSpec((tm, w_), lambda i: (i, 0))
    return pl.pallas_call(
        functools.partial(_gmlp_mix_kernel, csize),
        grid=(m // tm,),
        in_specs=[row(D_MODEL), row(GM_W), row(GM_W), _layer_spec(e, GM_HEADS, CHUNK, CHUNK),
                  _layer_spec(e, CHUNK, GM_HEADS), _layer_spec(e, GM_W, D_MODEL)],
        out_specs=row(D_MODEL),
        out_shape=jax.ShapeDtypeStruct((m, D_MODEL), F32),
        scratch_shapes=[pltpu.VMEM((tm, GM_W), BF)],
        compiler_params=_cparams("parallel"),
        name="gmlp_mix",
    )(x, u, v, wsp, bsp, wo)


def _head_norm(z, g):
    cols = []
    for h in range(X_HEADS):
        zh = z[:, h * X_HEAD_DIM:(h + 1) * X_HEAD_DIM]
        cols.append(_rms_rows(zh, g))
    return jnp.concatenate(cols, axis=1)


MEMKV_SEQ_BLOCK = 2
MEM_ROWS = MEM_LEN * X_HEADS


def _head_rows_of(h):
    return pl.ds(h, MEM_LEN, stride=X_HEADS)


def _memory_kv_kernel(mem_ref, g_ref, wk_ref, gk_ref, wv_ref, k_ref, v_ref):
    m = _rms_rows(mem_ref[...], g_ref[...]).astype(BF)
    k = _head_norm(_dot(m, wk_ref[...]), gk_ref[...])
    v = _dot(m, wv_ref[...])
    for s in range(MEMKV_SEQ_BLOCK):
        rows = slice(s * MEM_LEN, (s + 1) * MEM_LEN)
        for h in range(X_HEADS):
            cols = slice(h * X_HEAD_DIM, (h + 1) * X_HEAD_DIM)
            k_ref[s, _head_rows_of(h), :] = k[rows, cols]
            v_ref[s, _head_rows_of(h), :] = v[rows, cols]


def _memory_kv(mem, g_mem, w_k, g_k, w_v, n):
    tm = MEMKV_SEQ_BLOCK * MEM_LEN
    per_layer = lambda a, b: pl.BlockSpec((None, a, b), lambda l, i: (l, 0, 0))
    out_spec = pl.BlockSpec((None, MEMKV_SEQ_BLOCK, MEM_ROWS, X_HEAD_DIM), lambda l, i: (l, i, 0, 0))
    out_sds = jax.ShapeDtypeStruct((DEPTH, n, MEM_ROWS, X_HEAD_DIM), F32)
    return pl.pallas_call(
        _memory_kv_kernel,
        grid=(DEPTH, n // MEMKV_SEQ_BLOCK),
        in_specs=[pl.BlockSpec((tm, D_MODEL), lambda l, i: (i, 0)), per_layer(1, D_MODEL), per_layer(D_MODEL, X_W),
                  per_layer(1, X_HEAD_DIM), per_layer(D_MODEL, X_W)],
        out_specs=[out_spec, out_spec],
        out_shape=[out_sds, out_sds],
        compiler_params=_cparams("parallel", "parallel"),
        name="memory_kv",
    )(mem, g_mem, w_k, g_k, w_v)


def _xattn_rows(x, g_ref, wq_ref, gq_ref, mk_ref, mv_ref, wo_ref):
    q = _head_norm(_dot(_rms_rows(x, g_ref[...]).astype(BF), wq_ref[...]), gq_ref[...]).astype(BF)
    outs = []
    for h in range(X_HEADS):
        cols = slice(h * X_HEAD_DIM, (h + 1) * X_HEAD_DIM)
        s = _dot_nt(q[:, cols], mk_ref[_head_rows_of(h), :].astype(BF)) * (X_HEAD_DIM ** -0.5)
        m = jnp.max(s, axis=-1, keepdims=True)
        p = jnp.exp(s - m)
        p = (p / jnp.sum(p, axis=-1, keepdims=True)).astype(BF)
        outs.append(_dot(p, mv_ref[_head_rows_of(h), :].astype(BF)))
    o = jnp.concatenate(outs, axis=1).astype(BF)
    return x + _dot(o, wo_ref[...])


N_XATTN_REFS = 6
N_ROUTE_REFS = 4
GROUP_ROW = 0


def _xattn_route_store(x, refs, o_ref, grp_ref):
    y = _xattn_rows(x, *refs[:N_XATTN_REFS])
    _write_token_tiles(o_ref, y)
    gf_ref, whi_ref, wlo_ref, b_ref = refs[N_XATTN_REFS:]
    grp = _top_group(_router_logits(_rms_rows(y, gf_ref[...]), whi_ref[...], wlo_ref[...], b_ref[...]))
    grp_ref[...] = jnp.concatenate([grp, jnp.zeros((7, grp.shape[1]), jnp.int32)], axis=0)


def _even_out_xattn_kernel(token_tiled, x_ref, a_ref, s_ref, wa_ref, ws_ref, *rest):
    refs, o_ref, grp_ref = rest[:-2], rest[-2], rest[-1]
    x = (_read_tokens(x_ref, token_tiled) + _dot(a_ref[...], wa_ref[...])
         + _dot(s_ref[...].astype(BF), ws_ref[...]))
    _xattn_route_store(x, refs, o_ref, grp_ref)


def _gmlp_xattn_kernel(csize, token_tiled, x_ref, u_ref, v_ref, wsp_ref, bsp_ref, wo_ref, *rest):
    refs, o_ref, grp_ref, gated_scr = rest[:-3], rest[-3], rest[-2], rest[-1]
    x = _gmlp_rows(csize, _read_tokens(x_ref, token_tiled), u_ref, v_ref, wsp_ref, bsp_ref, wo_ref, gated_scr)
    _xattn_route_store(x, refs, o_ref, grp_ref)


def _xattn_route_specs(li):
    mem = pl.BlockSpec((None, None, MEM_ROWS, X_HEAD_DIM), lambda i, j: (li, i, 0, 0))
    return [_layer_spec(li, 1, D_MODEL), _layer_spec(li, D_MODEL, X_W), _layer_spec(li, 1, X_HEAD_DIM), mem, mem,
            _layer_spec(li, X_W, D_MODEL), _layer_spec(li, 1, D_MODEL), _layer_spec(li, ROUTER_ROWS, D_MODEL),
            _layer_spec(li, ROUTER_ROWS, D_MODEL), _layer_spec(li, ROUTER_ROWS, 1)]


def _token_tile_outputs(n, seq, tq):
    nq = seq // tq
    specs = [_token_spec(tq, lambda i, j: i * nq + j, True), pl.BlockSpec((None, 8, tq), lambda i, j: (i * nq + j, 0, 0))]
    shapes = [jax.ShapeDtypeStruct((n * seq * TT_CHUNKS, LANES), F32), jax.ShapeDtypeStruct((n * nq, 8, tq), jnp.int32)]
    return specs, shapes


def _even_out_xattn(x, token_tiled, att, ssm, e, wa, ws, li, xattn_route_args, n, seq, tq):
    nq = seq // tq
    row = lambda w_: pl.BlockSpec((tq, w_), lambda i, j: (i * nq + j, 0))
    ssm_spec = pl.BlockSpec((tq, SSM_W), lambda i, j: (j, i))
    out_specs, out_shape = _token_tile_outputs(n, seq, tq)
    y, grp = pl.pallas_call(
        functools.partial(_even_out_xattn_kernel, token_tiled),
        grid=(n, nq),
        in_specs=[_token_spec(tq, lambda i, j: i * nq + j, token_tiled), row(A_W), ssm_spec,
                  _layer_spec(e, A_W, D_MODEL), _layer_spec(e, SSM_W, D_MODEL)] + _xattn_route_specs(li),
        out_specs=out_specs,
        out_shape=out_shape,
        compiler_params=_cparams("parallel", "arbitrary"),
        name="even_out_xattn",
    )(x, att, ssm, wa, ws, *xattn_route_args)
    return y, grp[:, GROUP_ROW, :].reshape(n * seq)


def _gmlp_xattn(x, token_tiled, u, v, e, wsp, bsp, wo, csize, li, xattn_route_args, n, seq, tq):
    nq = seq // tq
    row = lambda w_: pl.BlockSpec((tq, w_), lambda i, j: (i * nq + j, 0))
    out_specs, out_shape = _token_tile_outputs(n, seq, tq)
    y, grp = pl.pallas_call(
        functools.partial(_gmlp_xattn_kernel, csize, token_tiled),
        grid=(n, nq),
        in_specs=[_token_spec(tq, lambda i, j: i * nq + j, token_tiled), row(GM_W), row(GM_W),
                  _layer_spec(e, GM_HEADS, CHUNK, CHUNK), _layer_spec(e, CHUNK, GM_HEADS),
                  _layer_spec(e, GM_W, D_MODEL)] + _xattn_route_specs(li),
        out_specs=out_specs,
        out_shape=out_shape,
        scratch_shapes=[pltpu.VMEM((tq, GM_W), BF)],
        compiler_params=_cparams("parallel", "arbitrary"),
        name="gmlp_xattn",
    )(x, u, v, wsp, bsp, wo, *xattn_route_args)
    return y, grp[:, GROUP_ROW, :].reshape(n * seq)


XATTN_SEQ_BLOCK = 8


def _xattn_sample_kernel(t_new, x_ref, g_ref, wq_ref, gq_ref, mk_ref, mv_ref, wo_ref, o_ref):
    pair_rows = 2 * t_new
    x = x_ref[...]
    q = _head_norm(_dot(_rms_rows(x, g_ref[...]).astype(BF), wq_ref[...]), gq_ref[...]).astype(BF)
    tq = lax.broadcasted_iota(jnp.int32, (pair_rows, 2 * MEM_LEN), 0)
    cm = lax.broadcasted_iota(jnp.int32, (pair_rows, 2 * MEM_LEN), 1)
    same_seq = (tq // t_new) == (cm // MEM_LEN)
    npair = XATTN_SEQ_BLOCK // 2
    scores = []
    for j in range(npair):
        qp = q[j * pair_rows:(j + 1) * pair_rows]
        for h in range(X_HEADS):
            hr = _head_rows_of(h)
            kp = jnp.concatenate([mk_ref[2 * j, hr, :], mk_ref[2 * j + 1, hr, :]], axis=0).astype(BF)
            s = _dot_nt(qp[:, h * X_HEAD_DIM:(h + 1) * X_HEAD_DIM], kp)
            scores.append(jnp.where(same_seq, s, NEG_INF))
    s = jnp.concatenate(scores, axis=0) * (X_HEAD_DIM ** -0.5)
    m = jnp.max(s, axis=-1, keepdims=True)
    p = jnp.exp(s - m)
    p = p / jnp.sum(p, axis=-1, keepdims=True)
    outs = []
    for j in range(npair):
        heads = []
        for h in range(X_HEADS):
            hr = _head_rows_of(h)
            vp = jnp.concatenate([mv_ref[2 * j, hr, :], mv_ref[2 * j + 1, hr, :]], axis=0).astype(BF)
            r0 = (j * X_HEADS + h) * pair_rows
            heads.append(_dot(p[r0:r0 + pair_rows].astype(BF), vp))
        outs.append(jnp.concatenate(heads, axis=1))
    o_all = jnp.concatenate(outs, axis=0).astype(BF)
    o_ref[...] = x + _dot(o_all, wo_ref[...])


def _xattn_sample(x, li, g, wq, gq, mk, mv, wo, n, t_new):
    rows = XATTN_SEQ_BLOCK * t_new
    row = pl.BlockSpec((rows, D_MODEL), lambda i: (i, 0))
    mem = pl.BlockSpec((None, XATTN_SEQ_BLOCK, MEM_ROWS, X_HEAD_DIM), lambda i: (li, i, 0, 0))
    return pl.pallas_call(
        functools.partial(_xattn_sample_kernel, t_new),
        grid=(n // XATTN_SEQ_BLOCK,),
        in_specs=[row, _layer_spec(li, 1, D_MODEL), _layer_spec(li, D_MODEL, X_W), _layer_spec(li, 1, X_HEAD_DIM),
                  mem, mem, _layer_spec(li, X_W, D_MODEL)],
        out_specs=row,
        out_shape=jax.ShapeDtypeStruct((n * t_new, D_MODEL), F32),
        compiler_params=_cparams("parallel"),
        name="xattn_sample",
    )(x, g, wq, gq, mk, mv, wo)


ROUTER_ROWS = 32
ROUTER_COARSE_ROW = N_EXPERTS
BIG = 3.0e38


def _first_argmax(vals, vmax):
    idx = jnp.full(vmax.shape, len(vals) - 1, jnp.int32)
    for j in range(len(vals) - 2, -1, -1):
        idx = jnp.where(vals[j] == vmax, j, idx)
    return idx


def _router_logits(h, whi, wlo, bias):
    h_hi = h.astype(BF)
    h_lo = (h - h_hi.astype(F32)).astype(BF)
    return _dot_nt(whi, h_hi) + _dot_nt(whi, h_lo) + _dot_nt(wlo, h_hi) + bias


def _coarse_rows(lg):
    return [lg[ROUTER_COARSE_ROW + g:ROUTER_COARSE_ROW + g + 1, :] for g in range(N_GROUPS)]


def _top_group(lg):
    coarse = _coarse_rows(lg)
    return _first_argmax(coarse, functools.reduce(jnp.maximum, coarse))


def _in_group_gates(lg, grp):
    pick = lambda rows: functools.reduce(lambda f, g: jnp.where(grp == g, rows[g], f),
                                         range(N_GROUPS - 2, -1, -1), rows[N_GROUPS - 1])
    coarse = _coarse_rows(lg)
    csel = pick(coarse)
    p_grp = 1.0 / functools.reduce(lambda a, b: a + b, [jnp.exp(c - csel) for c in coarse])
    fine = [pick([lg[g * EXPERTS_PER_GROUP + j:g * EXPERTS_PER_GROUP + j + 1, :] for g in range(N_GROUPS)])
            for j in range(EXPERTS_PER_GROUP)]
    v1 = functools.reduce(jnp.maximum, fine)
    i1 = _first_argmax(fine, v1)
    rest = [jnp.where(i1 == j, -BIG, fine[j]) for j in range(EXPERTS_PER_GROUP)]
    v2 = functools.reduce(jnp.maximum, rest)
    i2 = _first_argmax(rest, v2)
    e2 = jnp.exp(v2 - v1)
    w1 = p_grp / (1.0 + e2)
    w2 = p_grp * e2 / (1.0 + e2)
    return [jnp.where(i1 == j, w1, 0.0) + jnp.where(i2 == j, w2, 0.0) for j in range(EXPERTS_PER_GROUP)]


def _rows_to_lanes(rows, tm):
    pad = jnp.zeros((LANES - len(rows), tm), F32)
    return jnp.concatenate(list(rows) + [pad], axis=0).T


def _router_gates(h, whi, wlo, bias):
    lg = _router_logits(h, whi, wlo, bias)
    grp = _top_group(lg)
    in_group = _in_group_gates(lg, grp)
    rows = [jnp.where(grp == e // EXPERTS_PER_GROUP, in_group[e % EXPERTS_PER_GROUP], 0.0) for e in range(N_EXPERTS)]
    return _rows_to_lanes(rows, lg.shape[1])


def _moe_kernel(x_ref, g_ref, whi_ref, wlo_ref, b_ref, w1_ref, w3_ref, w2_ref, o_ref, h_scr, gates_scr, acc_scr):
    gi = pl.program_id(1)

    @pl.when(gi == 0)
    def _():
        h = _rms_rows(x_ref[...], g_ref[...])
        h_scr[...] = h.astype(BF)
        gates_scr[...] = _router_gates(h, whi_ref[...], wlo_ref[...], b_ref[...])
        acc_scr[...] = jnp.zeros_like(acc_scr)

    h = h_scr[...]
    gates = gates_scr[...]
    lane = lax.broadcasted_iota(jnp.int32, gates.shape, 1)
    for j in range(EXPERTS_PER_GROUP):
        gcol = jnp.sum(jnp.where(lane == gi * EXPERTS_PER_GROUP + j, gates, 0.0), axis=-1, keepdims=True)
        a = _dot(h, w1_ref[j])
        b = _dot(h, w3_ref[j])
        hid = (a * jax.nn.sigmoid(a)) * b * gcol
        acc_scr[...] += _dot(hid.astype(BF), w2_ref[j])

    @pl.when(gi == pl.num_programs(1) - 1)
    def _():
        o_ref[...] = x_ref[...] + acc_scr[...]


def _moe(x, li, g, whi, wlo, b, w1, w3, w2, tm):
    m = x.shape[0]
    row = lambda w_: pl.BlockSpec((tm, w_), lambda i, e: (i, 0))
    group_w = lambda a, b_: pl.BlockSpec((None, EXPERTS_PER_GROUP, a, b_), lambda i, e: (li, e, 0, 0))
    return pl.pallas_call(
        _moe_kernel,
        grid=(m // tm, N_GROUPS),
        in_specs=[row(D_MODEL), _layer_spec(li, 1, D_MODEL), _layer_spec(li, ROUTER_ROWS, D_MODEL),
                  _layer_spec(li, ROUTER_ROWS, D_MODEL), _layer_spec(li, ROUTER_ROWS, 1),
                  group_w(D_MODEL, D_EXPERT), group_w(D_MODEL, D_EXPERT), group_w(D_EXPERT, D_MODEL)],
        out_specs=row(D_MODEL),
        out_shape=jax.ShapeDtypeStruct((m, D_MODEL), F32),
        scratch_shapes=[pltpu.VMEM((tm, D_MODEL), BF), pltpu.VMEM((tm, LANES), F32), pltpu.VMEM((tm, D_MODEL), F32)],
        compiler_params=_cparams("parallel", "arbitrary"),
        name="moe",
    )(x, g, whi, wlo, b, w1, w3, w2)


MOE_TILE = 256
WRITE_BACK_DMA_PRIORITY = 1


def _sorted_plan(grp, m):
    t = MOE_TILE
    n_tiles = m // t + N_GROUPS
    order = jnp.argsort(grp, stable=True).astype(jnp.int32)
    counts = jnp.sum((grp[None, :] == jnp.arange(N_GROUPS, dtype=jnp.int32)[:, None]).astype(jnp.int32), axis=1)
    tiles_per_group = (counts + t - 1) // t
    tile_end = jnp.cumsum(tiles_per_group)
    tile_start = tile_end - tiles_per_group
    first_token = jnp.cumsum(counts) - counts
    s = jnp.arange(n_tiles, dtype=jnp.int32)
    tile_grp = jnp.minimum(jnp.sum((s[:, None] >= tile_end[None, :]).astype(jnp.int32), axis=1), N_GROUPS - 1)
    n_active = tile_end[N_GROUPS - 1]
    tile_in_group = s - tile_start[tile_grp]
    n_valid = jnp.where(s < n_active, jnp.clip(counts[tile_grp] - tile_in_group * t, 0, t), 0)
    r = jnp.arange(t, dtype=jnp.int32)
    slot_token = order[jnp.clip(first_token[tile_grp][:, None] + tile_in_group[:, None] * t + r[None, :], 0, m - 1)]
    valid = r[None, :] < n_valid[:, None]
    src = jnp.where(valid, slot_token, 0).reshape(-1)
    dst = jnp.where(valid, slot_token, m + (s % 2)[:, None] * t + r[None, :]).reshape(-1)
    src = jnp.concatenate([src, jnp.zeros((t,), jnp.int32)])
    dst = jnp.concatenate([m + t + r, dst])
    return ((src * TT_CHUNKS).astype(jnp.int32), (dst * TT_CHUNKS).astype(jnp.int32), tile_grp.astype(jnp.int32))


def _sorted_moe_kernel(m, src_ref, dst_ref, tgrp_ref, x_hbm, g_ref, whi_ref, wlo_ref, b_ref,
                       w1_ref, w3_ref, w2_ref, out_hbm, xbuf0, xbuf1, obuf0, obuf1, sem_g, sem_s, fence_sem):
    t = MOE_TILE
    i = pl.program_id(0)
    last = pl.num_programs(0) - 1
    xbuf = (xbuf0, xbuf1)
    obuf = (obuf0, obuf1)

    c8 = TT_CHUNKS

    def gather_row(tile, slot, r):
        src = pl.multiple_of(src_ref[tile * t + r], c8)
        return pltpu.make_async_copy(x_hbm.at[pl.ds(src, c8), :], xbuf[slot].at[pl.ds(r * c8, c8), :], sem_g.at[slot])

    def scatter_row(tile, slot, r):
        dst = pl.multiple_of(dst_ref[(tile + 1) * t + r], c8)
        return pltpu.make_async_copy(obuf[slot].at[pl.ds(r * c8, c8), :], out_hbm.at[pl.ds(dst, c8), :],
                                     sem_s.at[slot])

    def whole_gather(slot):
        return pltpu.make_async_copy(x_hbm.at[pl.ds(0, t * c8), :], xbuf[slot], sem_g.at[slot])

    def whole_scatter(slot, row0):
        return pltpu.make_async_copy(obuf[slot], out_hbm.at[pl.ds(row0, t * c8), :], sem_s.at[slot])

    @pl.when(i == 0)
    def _():
        for slot in range(2):
            obuf[slot][...] = jnp.zeros((t * c8, LANES), F32)
        for slot in range(2):
            whole_scatter(slot, (m + slot * t) * c8).start()
        for slot in range(2):
            whole_scatter(slot, (m + slot * t) * c8).wait()

        def issue(r, carry):
            gather_row(0, 0, r).start()
            return carry
        lax.fori_loop(0, t, issue, 0)

    def step(cur, nxt):
        whole_gather(cur).wait()
        rows_per_phase = t // EXPERTS_PER_GROUP

        def issue_copies(phase):
            for r in range(phase * rows_per_phase, (phase + 1) * rows_per_phase):
                gather_row(i + 1, nxt, r).start()
                scatter_row(i - 1, nxt, r).start(priority=WRITE_BACK_DMA_PRIORITY)
            pl.semaphore_signal(fence_sem, 1)
            pl.semaphore_wait(fence_sem, 1)

        x = _read_tokens(xbuf[cur], True)
        h = _rms_rows(x, g_ref[...])
        lg = _router_logits(h, whi_ref[...], wlo_ref[...], b_ref[...])
        gates = _rows_to_lanes(_in_group_gates(lg, tgrp_ref[i]), t)
        hb = h.astype(BF)
        acc = jnp.zeros((t, D_MODEL), F32)
        for j in range(EXPERTS_PER_GROUP):
            issue_copies(j)
            a = _dot(hb, w1_ref[j])
            b = _dot(hb, w3_ref[j])
            hid = (a * jax.nn.sigmoid(a)) * b * gates[:, j:j + 1]
            acc = acc + _dot(hid.astype(BF), w2_ref[j])
        res = x + acc

        @pl.when(i >= 1)
        def _():
            whole_scatter(cur, 0).wait()

        _write_token_tiles(obuf[cur], res)

        @pl.when(i == last)
        def _():
            def issue(r, carry):
                scatter_row(i, cur, r).start()
                return carry
            lax.fori_loop(0, t, issue, 0)
            whole_scatter(cur, 0).wait()
            whole_scatter(nxt, 0).wait()
            whole_gather(nxt).wait()

    for parity in range(2):
        @pl.when(i % 2 == parity)
        def _():
            step(parity, 1 - parity)


def _sorted_moe(x, grp, m, li, g, whi, wlo, b, w1, w3, w2):
    t = MOE_TILE
    n_tiles = m // t + N_GROUPS
    src, dst, tile_grp = _sorted_plan(grp, m)
    group_w = lambda a, b_: pl.BlockSpec((None, EXPERTS_PER_GROUP, a, b_), lambda i, src_, dst_, tg: (li, tg[i], 0, 0))
    return pl.pallas_call(
        functools.partial(_sorted_moe_kernel, m),
        grid_spec=pltpu.PrefetchScalarGridSpec(
            num_scalar_prefetch=3,
            grid=(n_tiles,),
            in_specs=[pl.BlockSpec(memory_space=pl.ANY), _layer_spec(li, 1, D_MODEL),
                      _layer_spec(li, ROUTER_ROWS, D_MODEL), _layer_spec(li, ROUTER_ROWS, D_MODEL),
                      _layer_spec(li, ROUTER_ROWS, 1),
                      group_w(D_MODEL, D_EXPERT), group_w(D_MODEL, D_EXPERT), group_w(D_EXPERT, D_MODEL)],
            out_specs=pl.BlockSpec(memory_space=pl.ANY),
            scratch_shapes=[pltpu.VMEM((t * TT_CHUNKS, LANES), F32)] * 4
            + [pltpu.SemaphoreType.DMA((2,)), pltpu.SemaphoreType.DMA((2,)), pltpu.SemaphoreType.REGULAR]),
        out_shape=jax.ShapeDtypeStruct(((m + 2 * t) * TT_CHUNKS, LANES), F32),
        compiler_params=_cparams("arbitrary"),
        name="moe_sorted",
    )(src, dst, tile_grp, x, g, whi, wlo, b, w1, w3, w2)


def _heads_kv_major_to_g_major(w, axis):
    shape = w.shape
    split = shape[:axis] + (KV_HEADS, Q_PER_KV, HEAD_DIM) + shape[axis + 1:]
    return jnp.swapaxes(w.reshape(split), axis, axis + 1).reshape(shape)


def _prep_ssm(a_re, a_im, log_dt, b_re, b_im, c_re, c_im, d_skip, w_glu):
    ar, ai = a_re.astype(F32), a_im.astype(F32)
    dt = jnp.exp(log_dt.astype(F32))[..., None]
    mag = jnp.exp(ar * dt)
    abr, abi = mag * jnp.cos(ai * dt), mag * jnp.sin(ai * dt)
    den = ar * ar + ai * ai
    kr = ((abr - 1.0) * ar + abi * ai) / den
    ki = (abi * ar - (abr - 1.0) * ai) / den
    br, bi = b_re.astype(F32), b_im.astype(F32)
    bbr = kr[..., None] * br - ki[..., None] * bi
    bbi = kr[..., None] * bi + ki[..., None] * br
    n_layers = ar.shape[0]
    eye = jnp.eye(SSM_GROUPS, dtype=F32)
    blockdiag_in = lambda b: jnp.einsum('egph,gk->eghkp', b, eye).reshape(n_layers, SSM_W, SSM_FLAT)
    blockdiag_out = lambda c: jnp.einsum('eghp,gk->egpkh', c, eye).reshape(n_layers, SSM_FLAT, SSM_W)
    in_tiles = lambda b: jnp.stack([b[:, j * SSM_IN_TILE:(j + 1) * SSM_IN_TILE, j * SSM_ST_TILE:(j + 1) * SSM_ST_TILE]
                                    for j in range(SSM_IN_TILES)], axis=1)
    out_tiles = lambda c: jnp.stack([c[:, j * SSM_OUT_ST:(j + 1) * SSM_OUT_ST, j * SSM_OUT_TILE:(j + 1) * SSM_OUT_TILE]
                                     for j in range(SSM_OUT_TILES)], axis=1)
    bcat = jnp.concatenate([in_tiles(blockdiag_in(bbr)), in_tiles(blockdiag_in(bbi))], axis=3).astype(BF)
    ccat = jnp.concatenate([out_tiles(blockdiag_out(c_re.astype(F32))), -out_tiles(blockdiag_out(c_im.astype(F32)))],
                           axis=2).astype(BF)
    return (abr.reshape(n_layers, 1, SSM_FLAT), abi.reshape(n_layers, 1, SSM_FLAT), bcat, ccat,
            d_skip.astype(F32).reshape(n_layers, 1, SSM_W), w_glu.astype(BF))


def _run_trunk(x, n, seq, mem_k, mem_v, caches, p):
    prompt = caches is None
    tm = 512
    m = n * seq
    wk_out, wv_out, hr_out, hi_out, cv_out = [], [], [], [], []
    xattn_args = (p['g_xattn'], p['w_xq'], p['g_xq'], mem_k, mem_v, p['w_xo'])
    router_args = (p['g_ffn'], p['w_router_hi'], p['w_router_lo'], p['b_router'])
    expert_args = (p['w_e1'], p['w_e3'], p['w_e2'])
    tiled = False
    for li in range(DEPTH):
        e = li // 2
        if li % 2 == 0:
            q, k, v, u = _even_in_proj(x, li, e, p['g_mix'], p['w_in_even'], p['g_q'], p['g_k'], p['bd'], tm, n, seq,
                                       token_tiled=tiled)
            if prompt:
                att = _swa_prompt(q, k, v, p['sinks'], e, n, seq)
                k3, v3 = k.reshape(n, seq, KV_W), v.reshape(n, seq, KV_W)
                new_k, new_v = k3[:, seq - WINDOW:], v3[:, seq - WINDOW:]
                ssm, h_re, h_im = _s5_glu(u.reshape(seq, n, SSM_W), None, p['ssm'], e, n, seq, 128)
                x, grp = _even_out_xattn(x, tiled, att, ssm.reshape(seq, n * SSM_W), e, p['w_out_att'], p['w_out_ssm'],
                                         li, xattn_args + router_args, n, seq, tm)
            else:
                win_k, win_v, ssm_re, ssm_im = caches
                att, new_k, new_v = _swa_sample(q, k, v, win_k, win_v, p['sinks'], e, n, seq)
                u_tm = jnp.swapaxes(u.reshape(n, seq, SSM_W), 0, 1)
                ssm, h_re, h_im = _s5_glu(u_tm, (ssm_re, ssm_im), p['ssm'], e, n, seq, seq)
                ssm = jnp.swapaxes(ssm, 0, 1).reshape(m, SSM_W)
                x = _even_out_proj(x, att, ssm, e, p['w_out_att'], p['w_out_ssm'], tm, n, seq)
            wk_out.append(new_k.reshape(n, WINDOW, KV_HEADS, HEAD_DIM))
            wv_out.append(new_v.reshape(n, WINDOW, KV_HEADS, HEAD_DIM))
            hr_out.append(h_re.reshape(n, SSM_GROUPS, SSM_STATE))
            hi_out.append(h_im.reshape(n, SSM_GROUPS, SSM_STATE))
        else:
            u, v = _odd_in_proj(x, m, li, e, p['g_mix'], p['w_in_odd'], p['g_v'], tm, BF if prompt else F32,
                                token_tiled=tiled)
            wsp, bsp, csize = p['gmlp_prompt'] if prompt else p['gmlp_sample']
            if prompt:
                x, grp = _gmlp_xattn(x, tiled, u, v, e, wsp, bsp, p['w_out_odd'], csize, li, xattn_args + router_args,
                                     n, seq, tm)
            else:
                x = _gmlp_mix(x, u, v, e, wsp, bsp, p['w_out_odd'], csize, tm)
                cv_out.append(v.reshape(n, seq, GM_W))
        if prompt:
            x = _sorted_moe(x, grp, m, li, *router_args, *expert_args)
            tiled = True
        else:
            x = _xattn_sample(x, li, *xattn_args, n, seq)
            x = _moe(x, li, *router_args, *expert_args, tm)
    if tiled:
        x = _untile(x, m, tm)
    cv = None if prompt else jnp.stack(cv_out)
    return x, jnp.stack(wk_out), jnp.stack(wv_out), jnp.stack(hr_out), jnp.stack(hi_out), cv


def kernel(x_prompt, x_sample, cache_win_k, cache_win_v, state_ssm_re, state_ssm_im, cache_mem_k, cache_mem_v, mem_prompt, g_mix, g_xattn, g_ffn, g_mem, w_in_even, g_q, g_k, sinks, ssm_a_re, ssm_a_im, ssm_log_dt, ssm_b_re, ssm_b_im, ssm_c_re, ssm_c_im, ssm_d, w_glu, w_out_even, w_in_odd, g_v, w_spatial, b_spatial, w_out_odd, w_xq, g_xq, w_xk, g_xk, w_xv, w_xo, w_coarse, b_coarse, w_fine, b_fine, w_e1, w_e3, w_e2):
    batch, seq = x_prompt.shape[0], x_prompt.shape[1]
    dec_batch, dec_seq = x_sample.shape[0], x_sample.shape[1]
    n_even = w_in_even.shape[0]

    row1 = lambda a: a.astype(F32)[:, None, :]
    p = {}
    p['g_mix'], p['g_xattn'], p['g_ffn'] = row1(g_mix), row1(g_xattn), row1(g_ffn)
    w_in_q = _heads_kv_major_to_g_major(w_in_even[:, :, :A_W], 2)
    p['w_in_even'] = jnp.concatenate([w_in_q, w_in_even[:, :, A_W:]], axis=2).astype(BF)
    p['g_q'] = jnp.tile(g_q.astype(F32), (1, A_HEADS))[:, None, :]
    p['g_k'] = jnp.tile(g_k.astype(F32), (1, KV_HEADS))[:, None, :]
    head_id = jnp.arange(A_W) // HEAD_DIM
    p['bd'] = (head_id[:, None] == head_id[None, :]).astype(BF)
    p['sinks'] = sinks.astype(F32).reshape(-1)
    p['ssm'] = _prep_ssm(ssm_a_re, ssm_a_im, ssm_log_dt, ssm_b_re, ssm_b_im, ssm_c_re, ssm_c_im, ssm_d, w_glu)
    p['w_out_att'] = _heads_kv_major_to_g_major(w_out_even[:, :A_W, :], 1).astype(BF)
    p['w_out_ssm'] = w_out_even[:, A_W:, :].astype(BF)
    p['w_in_odd'] = w_in_odd.astype(BF)
    p['g_v'] = row1(g_v)
    p['w_out_odd'] = w_out_odd.astype(BF)
    cs = min(dec_seq, CHUNK)
    reps = CHUNK // cs
    p['gmlp_prompt'] = (w_spatial.astype(F32), jnp.swapaxes(b_spatial.astype(F32), 1, 2), CHUNK)
    p['gmlp_sample'] = (jnp.tile(w_spatial[:, :, :cs, :cs].astype(F32), (1, 1, reps, reps)),
                        jnp.tile(jnp.swapaxes(b_spatial[:, :, :cs].astype(F32), 1, 2), (1, reps, 1)), cs)
    p['w_xq'] = w_xq.astype(BF)
    p['g_xq'] = row1(g_xq)
    p['w_xo'] = w_xo.astype(BF)
    w_router = jnp.swapaxes(jnp.concatenate([w_fine, w_coarse], axis=2).astype(F32), 1, 2)
    w_router = jnp.pad(w_router, ((0, 0), (0, ROUTER_ROWS - w_router.shape[1]), (0, 0)))
    p['w_router_hi'] = w_router.astype(BF)
    p['w_router_lo'] = (w_router - p['w_router_hi'].astype(F32)).astype(BF)
    b_router = jnp.concatenate([b_fine, b_coarse], axis=1).astype(F32)
    p['b_router'] = jnp.pad(b_router, ((0, 0), (0, ROUTER_ROWS - b_router.shape[1])))[:, :, None]
    p['w_e1'], p['w_e3'], p['w_e2'] = w_e1.astype(BF), w_e3.astype(BF), w_e2.astype(BF)

    mem2d = mem_prompt.reshape(batch * MEM_LEN, D_MODEL)
    pk, pv = _memory_kv(mem2d, row1(g_mem), w_xk.astype(BF), row1(g_xk), w_xv.astype(BF), batch)
    y_p, p_wk, p_wv, p_hr, p_hi, _ = _run_trunk(x_prompt.reshape(batch * seq, D_MODEL), batch, seq, pk, pv, None, p)

    caches = (cache_win_k.reshape(n_even, dec_batch, WINDOW, KV_W), cache_win_v.reshape(n_even, dec_batch, WINDOW, KV_W),
              state_ssm_re.reshape(n_even, dec_batch, SSM_FLAT), state_ssm_im.reshape(n_even, dec_batch, SSM_FLAT))
    y_s, s_wk, s_wv, s_hr, s_hi, s_cv = _run_trunk(x_sample.reshape(dec_batch * dec_seq, D_MODEL), dec_batch, dec_seq,
                                                   cache_mem_k.reshape(DEPTH, dec_batch, MEM_ROWS, X_HEAD_DIM),
                                                   cache_mem_v.reshape(DEPTH, dec_batch, MEM_ROWS, X_HEAD_DIM), caches, p)

    mem_shape = (DEPTH, batch, MEM_LEN, X_HEADS, X_HEAD_DIM)
    return (y_p.reshape(batch, seq, D_MODEL), y_s.reshape(dec_batch, dec_seq, D_MODEL), p_wk, p_wv, p_hr, p_hi,
            pk.reshape(mem_shape), pv.reshape(mem_shape), s_wk, s_wv, s_hr, s_hi, s_cv)
```

```python
import functools
import math

import jax
import jax.numpy as jnp
from jax import lax
from jax.experimental import pallas as pl
from jax.experimental.pallas import tpu as pltpu

F32 = jnp.float32
BF = jnp.bfloat16

D_MODEL = 1024
DEPTH = 4
A_W = 512
HEAD_DIM = 64
A_HEADS = 8
KV_HEADS = 2
Q_PER_KV = 4
KV_W = 128
WINDOW = 128
SSM_W = 512
SSM_CH = 16
SSM_GROUPS = 32
SSM_STATE = 64
SSM_FLAT = SSM_GROUPS * SSM_STATE
IN_EVEN = A_W + 2 * KV_W + SSM_W
CHUNK = 128
GM_W = 2048
GM_HEADS = 8
GM_HD = 256
MEM_LEN = 256
X_HEADS = 4
X_HEAD_DIM = 128
X_W = 512
N_GROUPS = 4
EXPERTS_PER_GROUP = 4
N_EXPERTS = 16
D_EXPERT = 256
EPS = 1e-6
NEG_INF = -1e30
LANES = 128
VMEM_LIMIT_BYTES = 52 * 1024 * 1024
ALIBI_SLOPES = tuple(2.0 ** (-8.0 * (h + 1) / A_HEADS) for h in range(A_HEADS))
SSM_SEQ_BLOCK = 8


def _cparams(*sem):
    return pltpu.CompilerParams(dimension_semantics=sem, vmem_limit_bytes=VMEM_LIMIT_BYTES)


def _rms_rows(x, g):
    return x * lax.rsqrt(jnp.mean(x * x, axis=-1, keepdims=True) + EPS) * g


TT_CHUNKS = D_MODEL // LANES


def _read_tokens(ref, token_tiled):
    if not token_tiled:
        return ref[...]
    rows = ref.shape[0] // TT_CHUNKS
    return jnp.concatenate([ref[pl.ds(c, rows, stride=TT_CHUNKS), :] for c in range(TT_CHUNKS)], axis=1)


def _write_token_tiles(ref, val):
    rows = val.shape[0]
    for c in range(TT_CHUNKS):
        ref[pl.ds(c, rows, stride=TT_CHUNKS), :] = val[:, c * LANES:(c + 1) * LANES]


def _token_spec(rows, index, token_tiled):
    if token_tiled:
        return pl.BlockSpec((rows * TT_CHUNKS, LANES), lambda *ids: (index(*ids), 0))
    return pl.BlockSpec((rows, D_MODEL), lambda *ids: (index(*ids), 0))


def _untile_kernel(x_ref, o_ref):
    o_ref[...] = _read_tokens(x_ref, True)


def _untile(x, m, tm):
    return pl.pallas_call(
        _untile_kernel,
        grid=(m // tm,),
        in_specs=[_token_spec(tm, lambda i: i, True)],
        out_specs=pl.BlockSpec((tm, D_MODEL), lambda i: (i, 0)),
        out_shape=jax.ShapeDtypeStruct((m, D_MODEL), F32),
        compiler_params=_cparams("parallel"),
        name="untile",
    )(x)


def _dot(a, b):
    return jnp.dot(a, b, preferred_element_type=F32)


def _dot_nt(a, b):
    return lax.dot_general(a, b, (((1,), (1,)), ((), ())), preferred_element_type=F32)


def _full(shape):
    nd = len(shape)
    return pl.BlockSpec(shape, lambda *_: (0,) * nd)


def _layer_spec(li, *shape, single_buffer=False):
    nd = len(shape)
    mode = pl.Buffered(1) if single_buffer else None
    return pl.BlockSpec((None,) + shape, lambda *_: (li,) + (0,) * nd, pipeline_mode=mode)


def _even_in_kernel(token_tiled, x_ref, g_ref, w_ref, gq_ref, gk_ref, bd_ref, q_ref, k_ref, v_ref, u_ref):
    h = _rms_rows(_read_tokens(x_ref, token_tiled), g_ref[...]).astype(BF)
    z = _dot(h, w_ref[...])
    q = z[:, :A_W]
    k = z[:, A_W:A_W + KV_W]
    bd = bd_ref[...]
    qms = _dot((q * q).astype(BF), bd) * (1.0 / HEAD_DIM)
    kms = _dot((k * k).astype(BF), bd[:KV_W, :KV_W]) * (1.0 / HEAD_DIM)
    q_ref[...] = q * lax.rsqrt(qms + EPS) * gq_ref[...]
    k_ref[...] = k * lax.rsqrt(kms + EPS) * gk_ref[...]
    v_ref[...] = z[:, A_W + KV_W:A_W + 2 * KV_W]
    u_ref[...] = z[:, A_W + 2 * KV_W:]


def _time_major_spec(tm, n, seq, width):
    per_seq = seq // tm
    return pl.BlockSpec((tm, width), lambda i: (i % per_seq, i // per_seq))


def _even_in_proj(x, li, e, g, w, gq, gk, bd, tm, n, seq, token_tiled=False):
    m = n * seq
    row = lambda w_: pl.BlockSpec((tm, w_), lambda i: (i, 0))
    return pl.pallas_call(
        functools.partial(_even_in_kernel, token_tiled),
        grid=(m // tm,),
        in_specs=[_token_spec(tm, lambda i: i, token_tiled), _layer_spec(li, 1, D_MODEL),
                  _layer_spec(e, D_MODEL, IN_EVEN), _layer_spec(e, 1, A_W),
                  _layer_spec(e, 1, KV_W), _full((A_W, A_W))],
        out_specs=[row(A_W), row(KV_W), row(KV_W), _time_major_spec(tm, n, seq, SSM_W) if seq % tm == 0 else row(SSM_W)],
        out_shape=[jax.ShapeDtypeStruct((m, A_W), F32), jax.ShapeDtypeStruct((m, KV_W), F32),
                   jax.ShapeDtypeStruct((m, KV_W), F32),
                   jax.ShapeDtypeStruct((seq, n * SSM_W) if seq % tm == 0 else (m, SSM_W), F32)],
        compiler_params=_cparams("parallel"),
        name="even_in_proj",
    )(x, g, w, gq, gk, bd)


def _head_rows(q, lane_lo):
    rows = []
    for kv in range(KV_HEADS):
        keep = lane_lo if kv == 0 else jnp.logical_not(lane_lo)
        for g in range(Q_PER_KV):
            rows.append(jnp.where(keep, q[:, g * LANES:(g + 1) * LANES], 0.0))
    return jnp.concatenate(rows, axis=0)


def _head_cols(o, r, lane_lo):
    cols = []
    for g in range(Q_PER_KV):
        cols.append(jnp.where(lane_lo, o[g * r:(g + 1) * r], o[(Q_PER_KV + g) * r:(Q_PER_KV + g + 1) * r]))
    return jnp.concatenate(cols, axis=1)


def _swa_prompt_kernel(e, sinks_ref, q_ref, kc_ref, kp_ref, vc_ref, vp_ref, o_ref):
    b = pl.program_id(1)
    lane_lo = lax.broadcasted_iota(jnp.int32, (1, LANES), 1) < HEAD_DIM
    qrows = _head_rows(q_ref[...], lane_lo).astype(BF)
    kcat = jnp.concatenate([kp_ref[...], kc_ref[...]], axis=0).astype(BF)
    vcat = jnp.concatenate([vp_ref[...], vc_ref[...]], axis=0).astype(BF)
    s = _dot_nt(qrows, kcat) * (HEAD_DIM ** -0.5)
    r = lax.broadcasted_iota(jnp.int32, (WINDOW, 2 * WINDOW), 0)
    c = lax.broadcasted_iota(jnp.int32, (WINDOW, 2 * WINDOW), 1)
    dist = WINDOW + r - c
    valid = (dist >= 0) & (dist <= WINDOW) & ((c >= WINDOW) | (b > 0))
    dist_f = dist.astype(F32)
    ps = []
    for h in range(A_HEADS):
        sh = s[h * WINDOW:(h + 1) * WINDOW] - ALIBI_SLOPES[h] * dist_f
        sh = jnp.where(valid, sh, NEG_INF)
        sk = sinks_ref[e * A_HEADS + h]
        m = jnp.maximum(jnp.max(sh, axis=-1, keepdims=True), sk)
        p = jnp.exp(sh - m)
        den = jnp.sum(p, axis=-1, keepdims=True) + jnp.exp(sk - m)
        ps.append((p / den).astype(BF))
    o = _dot(jnp.concatenate(ps, axis=0), vcat)
    o_ref[...] = _head_cols(o, WINDOW, lane_lo).astype(o_ref.dtype)


def _swa_prompt(q, k, v, sinks, e, n, seq):
    nb = seq // WINDOW
    cur = lambda w_: pl.BlockSpec((WINDOW, w_), lambda i, b: (i * nb + b, 0))
    prev = lambda w_: pl.BlockSpec((WINDOW, w_), lambda i, b: (i * nb + jnp.maximum(b - 1, 0), 0))
    return pl.pallas_call(
        functools.partial(_swa_prompt_kernel, e),
        grid=(n, nb),
        in_specs=[pl.BlockSpec(memory_space=pltpu.SMEM), cur(A_W), cur(KV_W), prev(KV_W), cur(KV_W), prev(KV_W)],
        out_specs=cur(A_W),
        out_shape=jax.ShapeDtypeStruct((n * seq, A_W), BF),
        compiler_params=_cparams("parallel", "arbitrary"),
        name="swa_prompt",
    )(sinks, q, k, k, v, v)


SWA_SEQ_BLOCK = 8


def _swa_sample_kernel(t_new, e, sinks_ref, q_ref, kn_ref, vn_ref, wk_ref, wv_ref, o_ref, nwk_ref, nwv_ref):
    nseq = SWA_SEQ_BLOCK
    pair_rows = 2 * t_new
    npair = nseq // 2
    kn = kn_ref[...]
    vn = vn_ref[...]
    for i in range(nseq):
        nwk_ref[i, 0:WINDOW - t_new, :] = wk_ref[i, t_new:WINDOW, :]
        nwk_ref[i, WINDOW - t_new:WINDOW, :] = kn[i * t_new:(i + 1) * t_new, :]
        nwv_ref[i, 0:WINDOW - t_new, :] = wv_ref[i, t_new:WINDOW, :]
        nwv_ref[i, WINDOW - t_new:WINDOW, :] = vn[i * t_new:(i + 1) * t_new, :]

    lane_lo = lax.broadcasted_iota(jnp.int32, (1, LANES), 1) < HEAD_DIM
    q = q_ref[...]
    knb = kn.astype(BF)
    vnb = vn.astype(BF)
    hr = A_HEADS * pair_rows
    qrows = [_head_rows(q[j * pair_rows:(j + 1) * pair_rows], lane_lo).astype(BF) for j in range(npair)]
    scale = HEAD_DIM ** -0.5
    s_new_all = _dot_nt(jnp.concatenate(qrows, axis=0), knb) * scale

    tq = lax.broadcasted_iota(jnp.int32, (pair_rows, 2 * WINDOW), 0)
    cw = lax.broadcasted_iota(jnp.int32, (pair_rows, 2 * WINDOW), 1)
    dist_w = WINDOW + (tq % t_new) - (cw % WINDOW)
    valid_w = ((tq // t_new) == (cw // WINDOW)) & (dist_w <= WINDOW)
    dist_wf = dist_w.astype(F32)
    tqn = lax.broadcasted_iota(jnp.int32, (pair_rows, nseq * t_new), 0)
    cn = lax.broadcasted_iota(jnp.int32, (pair_rows, nseq * t_new), 1)
    dist_n = (tqn % t_new) - (cn % t_new)
    dist_nf = dist_n.astype(F32)

    p_new_all = []
    o_win_all = []
    for j in range(npair):
        kwin = jnp.concatenate([wk_ref[2 * j], wk_ref[2 * j + 1]], axis=0).astype(BF)
        vwin = jnp.concatenate([wv_ref[2 * j], wv_ref[2 * j + 1]], axis=0).astype(BF)
        s_win = _dot_nt(qrows[j], kwin) * scale
        valid_n = ((2 * j + tqn // t_new) == (cn // t_new)) & (dist_n >= 0)
        p_win = []
        for h in range(A_HEADS):
            sw = jnp.where(valid_w, s_win[h * pair_rows:(h + 1) * pair_rows] - ALIBI_SLOPES[h] * dist_wf, NEG_INF)
            sn = s_new_all[j * hr + h * pair_rows:j * hr + (h + 1) * pair_rows]
            sn = jnp.where(valid_n, sn - ALIBI_SLOPES[h] * dist_nf, NEG_INF)
            sk = sinks_ref[e * A_HEADS + h]
            m = jnp.maximum(jnp.maximum(jnp.max(sw, axis=-1, keepdims=True), jnp.max(sn, axis=-1, keepdims=True)), sk)
            pw = jnp.exp(sw - m)
            pn = jnp.exp(sn - m)
            den = jnp.sum(pw, axis=-1, keepdims=True) + jnp.sum(pn, axis=-1, keepdims=True) + jnp.exp(sk - m)
            p_win.append((pw / den).astype(BF))
            p_new_all.append((pn / den).astype(BF))
        o_win_all.append(_dot(jnp.concatenate(p_win, axis=0), vwin))
    o_new = _dot(jnp.concatenate(p_new_all, axis=0), vnb)
    outs = [_head_cols(o_win_all[j] + o_new[j * hr:(j + 1) * hr], pair_rows, lane_lo) for j in range(npair)]
    o_ref[...] = jnp.concatenate(outs, axis=0).astype(o_ref.dtype)


def _swa_sample(q, kn, vn, win_k, win_v, sinks, e, n, t_new):
    rows = SWA_SEQ_BLOCK * t_new
    row = lambda w_: pl.BlockSpec((rows, w_), lambda i: (i, 0))
    win = pl.BlockSpec((SWA_SEQ_BLOCK, WINDOW, KV_W), lambda i: (i, 0, 0))
    win_in = pl.BlockSpec((None, SWA_SEQ_BLOCK, WINDOW, KV_W), lambda i: (e, i, 0, 0))
    return pl.pallas_call(
        functools.partial(_swa_sample_kernel, t_new, e),
        grid=(n // SWA_SEQ_BLOCK,),
        in_specs=[pl.BlockSpec(memory_space=pltpu.SMEM), row(A_W), row(KV_W), row(KV_W), win_in, win_in],
        out_specs=[row(A_W), win, win],
        out_shape=[jax.ShapeDtypeStruct((n * t_new, A_W), BF),
                   jax.ShapeDtypeStruct((n, WINDOW, KV_W), F32), jax.ShapeDtypeStruct((n, WINDOW, KV_W), F32)],
        compiler_params=_cparams("parallel"),
        name="swa_sample",
    )(sinks, q, kn, vn, win_k, win_v)


SSM_LANE_CHUNK = 1024
SSM_IN_TILE = LANES
SSM_IN_TILES = SSM_W // SSM_IN_TILE
SSM_ST_TILE = SSM_IN_TILE // SSM_CH * SSM_STATE
SSM_OUT_TILE = 256
SSM_OUT_TILES = SSM_W // SSM_OUT_TILE
SSM_OUT_ST = SSM_OUT_TILE // SSM_CH * SSM_STATE


def _s5_kernel(tc, has_h0, *refs):
    if has_h0:
        (u_ref, h0r_ref, h0i_ref, abr_ref, abi_ref, bcat_ref, ccat_ref, d_ref, wglu_ref,
         o_ref, hr_ref, hi_ref, st_scr, car_scr) = refs
    else:
        (u_ref, abr_ref, abi_ref, bcat_ref, ccat_ref, d_ref, wglu_ref,
         o_ref, hr_ref, hi_ref, st_scr, car_scr) = refs
    nb = SSM_SEQ_BLOCK
    ci = pl.program_id(1)

    @pl.when(ci == 0)
    def _():
        if has_h0:
            car_scr[:, :SSM_FLAT] = h0r_ref[...]
            car_scr[:, SSM_FLAT:] = h0i_ref[...]
        else:
            car_scr[...] = jnp.zeros_like(car_scr)

    ut = u_ref[...].reshape(tc * nb, SSM_W)
    ub = ut.astype(BF)
    for jt in range(SSM_IN_TILES):
        bu = _dot(ub[:, jt * SSM_IN_TILE:(jt + 1) * SSM_IN_TILE], bcat_ref[jt])
        st_scr[:, jt * SSM_ST_TILE:(jt + 1) * SSM_ST_TILE] = bu[:, :SSM_ST_TILE]
        st_scr[:, SSM_FLAT + jt * SSM_ST_TILE:SSM_FLAT + (jt + 1) * SSM_ST_TILE] = bu[:, SSM_ST_TILE:]

    for lc in range(SSM_FLAT // SSM_LANE_CHUNK):
        lo = lc * SSM_LANE_CHUNK
        re_sl = slice(lo, lo + SSM_LANE_CHUNK)
        im_sl = slice(SSM_FLAT + lo, SSM_FLAT + lo + SSM_LANE_CHUNK)
        ar = jnp.broadcast_to(abr_ref[:, re_sl], (nb, SSM_LANE_CHUNK))
        ai = jnp.broadcast_to(abi_ref[:, re_sl], (nb, SSM_LANE_CHUNK))

        def step(t, carry):
            hr, hi = carry
            rows = pl.ds(pl.multiple_of(t * nb, nb), nb)
            nr = ar * hr - ai * hi + st_scr[rows, re_sl]
            ni = ar * hi + ai * hr + st_scr[rows, im_sl]
            st_scr[rows, re_sl] = nr
            st_scr[rows, im_sl] = ni
            return nr, ni

        hr, hi = lax.fori_loop(0, tc, step, (car_scr[:, re_sl], car_scr[:, im_sl]))
        car_scr[:, re_sl] = hr
        car_scr[:, im_sl] = hi

    ys = []
    for ot in range(SSM_OUT_TILES):
        re_sl = slice(ot * SSM_OUT_ST, (ot + 1) * SSM_OUT_ST)
        im_sl = slice(SSM_FLAT + ot * SSM_OUT_ST, SSM_FLAT + (ot + 1) * SSM_OUT_ST)
        ys.append(_dot(st_scr[:, re_sl].astype(BF), ccat_ref[ot, :SSM_OUT_ST, :])
                  + _dot(st_scr[:, im_sl].astype(BF), ccat_ref[ot, SSM_OUT_ST:, :]))
    y = jnp.concatenate(ys, axis=1) + d_ref[...] * ut
    y = jax.nn.gelu(y).astype(BF)
    g = _dot(y, wglu_ref[...])
    o_ref[...] = (g[:, :SSM_W] * jax.nn.sigmoid(g[:, SSM_W:])).reshape(tc, nb, SSM_W)

    @pl.when(ci == pl.num_programs(1) - 1)
    def _():
        hr_ref[...] = car_scr[:, :SSM_FLAT]
        hi_ref[...] = car_scr[:, SSM_FLAT:]


def _s5_glu(u, h0, ssm, e, n, seq, tc):
    nb = SSM_SEQ_BLOCK
    abr, abi, bcat, ccat, dsk, wglu = ssm
    u_spec = pl.BlockSpec((tc, nb, SSM_W), lambda i, c: (c, i, 0))
    st_spec = pl.BlockSpec((nb, SSM_FLAT), lambda i, c: (i, 0))
    h0_spec = pl.BlockSpec((None, nb, SSM_FLAT), lambda i, c: (e, i, 0))
    consts = [_layer_spec(e, 1, SSM_FLAT), _layer_spec(e, 1, SSM_FLAT),
              _layer_spec(e, SSM_IN_TILES, SSM_IN_TILE, 2 * SSM_ST_TILE),
              _layer_spec(e, SSM_OUT_TILES, 2 * SSM_OUT_ST, SSM_OUT_TILE), _layer_spec(e, 1, SSM_W),
              _layer_spec(e, SSM_W, 2 * SSM_W)]
    has_h0 = h0 is not None
    in_specs = [u_spec] + ([h0_spec, h0_spec] if has_h0 else []) + consts
    args = [u] + (list(h0) if has_h0 else []) + [abr, abi, bcat, ccat, dsk, wglu]
    return pl.pallas_call(
        functools.partial(_s5_kernel, tc, has_h0),
        grid=(n // nb, seq // tc),
        in_specs=in_specs,
        out_specs=[u_spec, st_spec, st_spec],
        out_shape=[jax.ShapeDtypeStruct((seq, n, SSM_W), F32), jax.ShapeDtypeStruct((n, SSM_FLAT), F32),
                   jax.ShapeDtypeStruct((n, SSM_FLAT), F32)],
        scratch_shapes=[pltpu.VMEM((nb * tc, 2 * SSM_FLAT), F32), pltpu.VMEM((nb, 2 * SSM_FLAT), F32)],
        compiler_params=_cparams("parallel", "arbitrary"),
        name="s5_glu",
    )(*args)


def _even_out_kernel(x_ref, a_ref, s_ref, wa_ref, ws_ref, o_ref):
    o_ref[...] = x_ref[...] + _dot(a_ref[...], wa_ref[...]) + _dot(s_ref[...].astype(BF), ws_ref[...])


def _even_out_proj(x, att, ssm, e, wa, ws, tm, n, seq):
    m = x.shape[0]
    row = lambda w_: pl.BlockSpec((tm, w_), lambda i: (i, 0))
    ssm_spec = _time_major_spec(tm, n, seq, SSM_W) if seq % tm == 0 else row(SSM_W)
    return pl.pallas_call(
        _even_out_kernel,
        grid=(m // tm,),
        in_specs=[row(D_MODEL), row(A_W), ssm_spec, _layer_spec(e, A_W, D_MODEL), _layer_spec(e, SSM_W, D_MODEL)],
        out_specs=row(D_MODEL),
        out_shape=jax.ShapeDtypeStruct((m, D_MODEL), F32),
        compiler_params=_cparams("parallel"),
        name="even_out_proj",
    )(x, att, ssm, wa, ws)


def _odd_in_kernel(token_tiled, x_ref, g_ref, w_ref, gv_ref, u_ref, v_ref):
    h = _rms_rows(_read_tokens(x_ref, token_tiled), g_ref[...]).astype(BF)
    u_ref[...] = jax.nn.gelu(_dot(h, w_ref[:, :GM_W])).astype(u_ref.dtype)
    zv = jax.nn.gelu(_dot(h, w_ref[:, GM_W:]))
    v_ref[...] = _rms_rows(zv, gv_ref[...]).astype(v_ref.dtype)


def _odd_in_proj(x, m, li, e, g, w, gv, tm, v_dtype, token_tiled=False):
    row = lambda w_: pl.BlockSpec((tm, w_), lambda i: (i, 0))
    return pl.pallas_call(
        functools.partial(_odd_in_kernel, token_tiled),
        grid=(m // tm,),
        in_specs=[_token_spec(tm, lambda i: i, token_tiled), _layer_spec(li, 1, D_MODEL),
                  _layer_spec(e, D_MODEL, 2 * GM_W, single_buffer=True),
                  _layer_spec(e, 1, GM_W)],
        out_specs=[row(GM_W), row(GM_W)],
        out_shape=[jax.ShapeDtypeStruct((m, GM_W), BF), jax.ShapeDtypeStruct((m, GM_W), v_dtype)],
        compiler_params=_cparams("parallel"),
        name="odd_in_proj",
    )(x, g, w, gv)


def _gmlp_rows(csize, x, u_ref, v_ref, wsp_ref, bsp_ref, wo_ref, gated_scr):
    tm = u_ref.shape[0]
    i = lax.broadcasted_iota(jnp.int32, (CHUNK, CHUNK), 0)
    j = lax.broadcasted_iota(jnp.int32, (CHUNK, CHUNK), 1)
    keep = (j <= i) & ((i // csize) == (j // csize))
    bsp = bsp_ref[...]
    for h in range(GM_HEADS):
        ws = jnp.where(keep, wsp_ref[h], 0.0).astype(BF)
        b_col = bsp[:, h:h + 1]
        cols = slice(h * GM_HD, (h + 1) * GM_HD)
        for c in range(tm // CHUNK):
            rows = slice(c * CHUNK, (c + 1) * CHUNK)
            mix = _dot(ws, v_ref[rows, cols].astype(BF)) + b_col
            gated_scr[rows, cols] = (u_ref[rows, cols].astype(F32) * mix).astype(BF)
    return x + _dot(gated_scr[...], wo_ref[...])


def _gmlp_mix_kernel(csize, x_ref, u_ref, v_ref, wsp_ref, bsp_ref, wo_ref, o_ref, gated_scr):
    o_ref[...] = _gmlp_rows(csize, x_ref[...], u_ref, v_ref, wsp_ref, bsp_ref, wo_ref, gated_scr)


def _gmlp_mix(x, u, v, e, wsp, bsp, wo, csize, tm):
    m = x.shape[0]
    row = lambda w_: pl.BlockSpec((tm, w_), lambda i: (i, 0))
    return pl.pallas_call(
        functools.partial(_gmlp_mix_kernel, csize),
        grid=(m // tm,),
        in_specs=[row(D_MODEL), row(GM_W), row(GM_W), _layer_spec(e, GM_HEADS, CHUNK, CHUNK),
                  _layer_spec(e, CHUNK, GM_HEADS), _layer_spec(e, GM_W, D_MODEL)],
        out_specs=row(D_MODEL),
        out_shape=jax.ShapeDtypeStruct((m, D_MODEL), F32),
        scratch_shapes=[pltpu.VMEM((tm, GM_W), BF)],
        compiler_params=_cparams("parallel"),
        name="gmlp_mix",
    )(x, u, v, wsp, bsp, wo)


def _head_norm(z, g):
    cols = []
    for h in range(X_HEADS):
        zh = z[:, h * X_HEAD_DIM:(h + 1) * X_HEAD_DIM]
        cols.append(_rms_rows(zh, g))
    return jnp.concatenate(cols, axis=1)


MEMKV_SEQ_BLOCK = 2
MEM_ROWS = MEM_LEN * X_HEADS


def _head_rows_of(h):
    return pl.ds(h, MEM_LEN, stride=X_HEADS)


def _memory_kv_kernel(mem_ref, g_ref, wk_ref, gk_ref, wv_ref, k_ref, v_ref):
    m = _rms_rows(mem_ref[...], g_ref[...]).astype(BF)
    k = _head_norm(_dot(m, wk_ref[...]), gk_ref[...])
    v = _dot(m, wv_ref[...])
    for s in range(MEMKV_SEQ_BLOCK):
        rows = slice(s * MEM_LEN, (s + 1) * MEM_LEN)
        for h in range(X_HEADS):
            cols = slice(h * X_HEAD_DIM, (h + 1) * X_HEAD_DIM)
            k_ref[s, _head_rows_of(h), :] = k[rows, cols]
            v_ref[s, _head_rows_of(h), :] = v[rows, cols]


def _memory_kv(mem, g_mem, w_k, g_k, w_v, n):
    tm = MEMKV_SEQ_BLOCK * MEM_LEN
    per_layer = lambda a, b: pl.BlockSpec((None, a, b), lambda l, i: (l, 0, 0))
    out_spec = pl.BlockSpec((None, MEMKV_SEQ_BLOCK, MEM_ROWS, X_HEAD_DIM), lambda l, i: (l, i, 0, 0))
    out_sds = jax.ShapeDtypeStruct((DEPTH, n, MEM_ROWS, X_HEAD_DIM), F32)
    return pl.pallas_call(
        _memory_kv_kernel,
        grid=(DEPTH, n // MEMKV_SEQ_BLOCK),
        in_specs=[pl.BlockSpec((tm, D_MODEL), lambda l, i: (i, 0)), per_layer(1, D_MODEL), per_layer(D_MODEL, X_W),
                  per_layer(1, X_HEAD_DIM), per_layer(D_MODEL, X_W)],
        out_specs=[out_spec, out_spec],
        out_shape=[out_sds, out_sds],
        compiler_params=_cparams("parallel", "parallel"),
        name="memory_kv",
    )(mem, g_mem, w_k, g_k, w_v)


def _xattn_rows(x, g_ref, wq_ref, gq_ref, mk_ref, mv_ref, wo_ref):
    q = _head_norm(_dot(_rms_rows(x, g_ref[...]).astype(BF), wq_ref[...]), gq_ref[...]).astype(BF)
    outs = []
    for h in range(X_HEADS):
        cols = slice(h * X_HEAD_DIM, (h + 1) * X_HEAD_DIM)
        s = _dot_nt(q[:, cols], mk_ref[_head_rows_of(h), :].astype(BF)) * (X_HEAD_DIM ** -0.5)
        m = jnp.max(s, axis=-1, keepdims=True)
        p = jnp.exp(s - m)
        p = (p / jnp.sum(p, axis=-1, keepdims=True)).astype(BF)
        outs.append(_dot(p, mv_ref[_head_rows_of(h), :].astype(BF)))
    o = jnp.concatenate(outs, axis=1).astype(BF)
    return x + _dot(o, wo_ref[...])


N_XATTN_REFS = 6
N_ROUTE_REFS = 4
GROUP_ROW = 0


def _xattn_route_store(x, refs, o_ref, grp_ref):
    y = _xattn_rows(x, *refs[:N_XATTN_REFS])
    _write_token_tiles(o_ref, y)
    gf_ref, whi_ref, wlo_ref, b_ref = refs[N_XATTN_REFS:]
    grp = _top_group(_router_logits(_rms_rows(y, gf_ref[...]), whi_ref[...], wlo_ref[...], b_ref[...]))
    grp_ref[...] = jnp.concatenate([grp, jnp.zeros((7, grp.shape[1]), jnp.int32)], axis=0)


def _even_out_xattn_kernel(token_tiled, x_ref, a_ref, s_ref, wa_ref, ws_ref, *rest):
    refs, o_ref, grp_ref = rest[:-2], rest[-2], rest[-1]
    x = (_read_tokens(x_ref, token_tiled) + _dot(a_ref[...], wa_ref[...])
         + _dot(s_ref[...].astype(BF), ws_ref[...]))
    _xattn_route_store(x, refs, o_ref, grp_ref)


def _gmlp_xattn_kernel(csize, token_tiled, x_ref, u_ref, v_ref, wsp_ref, bsp_ref, wo_ref, *rest):
    refs, o_ref, grp_ref, gated_scr = rest[:-3], rest[-3], rest[-2], rest[-1]
    x = _gmlp_rows(csize, _read_tokens(x_ref, token_tiled), u_ref, v_ref, wsp_ref, bsp_ref, wo_ref, gated_scr)
    _xattn_route_store(x, refs, o_ref, grp_ref)


def _xattn_route_specs(li):
    mem = pl.BlockSpec((None, None, MEM_ROWS, X_HEAD_DIM), lambda i, j: (li, i, 0, 0))
    return [_layer_spec(li, 1, D_MODEL), _layer_spec(li, D_MODEL, X_W), _layer_spec(li, 1, X_HEAD_DIM), mem, mem,
            _layer_spec(li, X_W, D_MODEL), _layer_spec(li, 1, D_MODEL), _layer_spec(li, ROUTER_ROWS, D_MODEL),
            _layer_spec(li, ROUTER_ROWS, D_MODEL), _layer_spec(li, ROUTER_ROWS, 1)]


def _token_tile_outputs(n, seq, tq):
    nq = seq // tq
    specs = [_token_spec(tq, lambda i, j: i * nq + j, True), pl.BlockSpec((None, 8, tq), lambda i, j: (i * nq + j, 0, 0))]
    shapes = [jax.ShapeDtypeStruct((n * seq * TT_CHUNKS, LANES), F32), jax.ShapeDtypeStruct((n * nq, 8, tq), jnp.int32)]
    return specs, shapes


def _even_out_xattn(x, token_tiled, att, ssm, e, wa, ws, li, xattn_route_args, n, seq, tq):
    nq = seq // tq
    row = lambda w_: pl.BlockSpec((tq, w_), lambda i, j: (i * nq + j, 0))
    ssm_spec = pl.BlockSpec((tq, SSM_W), lambda i, j: (j, i))
    out_specs, out_shape = _token_tile_outputs(n, seq, tq)
    y, grp = pl.pallas_call(
        functools.partial(_even_out_xattn_kernel, token_tiled),
        grid=(n, nq),
        in_specs=[_token_spec(tq, lambda i, j: i * nq + j, token_tiled), row(A_W), ssm_spec,
                  _layer_spec(e, A_W, D_MODEL), _layer_spec(e, SSM_W, D_MODEL)] + _xattn_route_specs(li),
        out_specs=out_specs,
        out_shape=out_shape,
        compiler_params=_cparams("parallel", "arbitrary"),
        name="even_out_xattn",
    )(x, att, ssm, wa, ws, *xattn_route_args)
    return y, grp[:, GROUP_ROW, :].reshape(n * seq)


def _gmlp_xattn(x, token_tiled, u, v, e, wsp, bsp, wo, csize, li, xattn_route_args, n, seq, tq):
    nq = seq // tq
    row = lambda w_: pl.BlockSpec((tq, w_), lambda i, j: (i * nq + j, 0))
    out_specs, out_shape = _token_tile_outputs(n, seq, tq)
    y, grp = pl.pallas_call(
        functools.partial(_gmlp_xattn_kernel, csize, token_tiled),
        grid=(n, nq),
        in_specs=[_token_spec(tq, lambda i, j: i * nq + j, token_tiled), row(GM_W), row(GM_W),
                  _layer_spec(e, GM_HEADS, CHUNK, CHUNK), _layer_spec(e, CHUNK, GM_HEADS),
                  _layer_spec(e, GM_W, D_MODEL)] + _xattn_route_specs(li),
        out_specs=out_specs,
        out_shape=out_shape,
        scratch_shapes=[pltpu.VMEM((tq, GM_W), BF)],
        compiler_params=_cparams("parallel", "arbitrary"),
        name="gmlp_xattn",
    )(x, u, v, wsp, bsp, wo, *xattn_route_args)
    return y, grp[:, GROUP_ROW, :].reshape(n * seq)


XATTN_SEQ_BLOCK = 8


def _xattn_sample_kernel(t_new, x_ref, g_ref, wq_ref, gq_ref, mk_ref, mv_ref, wo_ref, o_ref):
    pair_rows = 2 * t_new
    x = x_ref[...]
    q = _head_norm(_dot(_rms_rows(x, g_ref[...]).astype(BF), wq_ref[...]), gq_ref[...]).astype(BF)
    tq = lax.broadcasted_iota(jnp.int32, (pair_rows, 2 * MEM_LEN), 0)
    cm = lax.broadcasted_iota(jnp.int32, (pair_rows, 2 * MEM_LEN), 1)
    same_seq = (tq // t_new) == (cm // MEM_LEN)
    npair = XATTN_SEQ_BLOCK // 2
    scores = []
    for j in range(npair):
        qp = q[j * pair_rows:(j + 1) * pair_rows]
        for h in range(X_HEADS):
            hr = _head_rows_of(h)
            kp = jnp.concatenate([mk_ref[2 * j, hr, :], mk_ref[2 * j + 1, hr, :]], axis=0).astype(BF)
            s = _dot_nt(qp[:, h * X_HEAD_DIM:(h + 1) * X_HEAD_DIM], kp)
            scores.append(jnp.where(same_seq, s, NEG_INF))
    s = jnp.concatenate(scores, axis=0) * (X_HEAD_DIM ** -0.5)
    m = jnp.max(s, axis=-1, keepdims=True)
    p = jnp.exp(s - m)
    p = p / jnp.sum(p, axis=-1, keepdims=True)
    outs = []
    for j in range(npair):
        heads = []
        for h in range(X_HEADS):
            hr = _head_rows_of(h)
            vp = jnp.concatenate([mv_ref[2 * j, hr, :], mv_ref[2 * j + 1, hr, :]], axis=0).astype(BF)
            r0 = (j * X_HEADS + h) * pair_rows
            heads.append(_dot(p[r0:r0 + pair_rows].astype(BF), vp))
        outs.append(jnp.concatenate(heads, axis=1))
    o_all = jnp.concatenate(outs, axis=0).astype(BF)
    o_ref[...] = x + _dot(o_all, wo_ref[...])


def _xattn_sample(x, li, g, wq, gq, mk, mv, wo, n, t_new):
    rows = XATTN_SEQ_BLOCK * t_new
    row = pl.BlockSpec((rows, D_MODEL), lambda i: (i, 0))
    mem = pl.BlockSpec((None, XATTN_SEQ_BLOCK, MEM_ROWS, X_HEAD_DIM), lambda i: (li, i, 0, 0))
    return pl.pallas_call(
        functools.partial(_xattn_sample_kernel, t_new),
        grid=(n // XATTN_SEQ_BLOCK,),
        in_specs=[row, _layer_spec(li, 1, D_MODEL), _layer_spec(li, D_MODEL, X_W), _layer_spec(li, 1, X_HEAD_DIM),
                  mem, mem, _layer_spec(li, X_W, D_MODEL)],
        out_specs=row,
        out_shape=jax.ShapeDtypeStruct((n * t_new, D_MODEL), F32),
        compiler_params=_cparams("parallel"),
        name="xattn_sample",
    )(x, g, wq, gq, mk, mv, wo)


ROUTER_ROWS = 32
ROUTER_COARSE_ROW = N_EXPERTS
BIG = 3.0e38


def _first_argmax(vals, vmax):
    idx = jnp.full(vmax.shape, len(vals) - 1, jnp.int32)
    for j in range(len(vals) - 2, -1, -1):
        idx = jnp.where(vals[j] == vmax, j, idx)
    return idx


def _router_logits(h, whi, wlo, bias):
    h_hi = h.astype(BF)
    h_lo = (h - h_hi.astype(F32)).astype(BF)
    return _dot_nt(whi, h_hi) + _dot_nt(whi, h_lo) + _dot_nt(wlo, h_hi) + bias


def _coarse_rows(lg):
    return [lg[ROUTER_COARSE_ROW + g:ROUTER_COARSE_ROW + g + 1, :] for g in range(N_GROUPS)]


def _top_group(lg):
    coarse = _coarse_rows(lg)
    return _first_argmax(coarse, functools.reduce(jnp.maximum, coarse))


def _in_group_gates(lg, grp):
    pick = lambda rows: functools.reduce(lambda f, g: jnp.where(grp == g, rows[g], f),
                                         range(N_GROUPS - 2, -1, -1), rows[N_GROUPS - 1])
    coarse = _coarse_rows(lg)
    csel = pick(coarse)
    p_grp = 1.0 / functools.reduce(lambda a, b: a + b, [jnp.exp(c - csel) for c in coarse])
    fine = [pick([lg[g * EXPERTS_PER_GROUP + j:g * EXPERTS_PER_GROUP + j + 1, :] for g in range(N_GROUPS)])
            for j in range(EXPERTS_PER_GROUP)]
    v1 = functools.reduce(jnp.maximum, fine)
    i1 = _first_argmax(fine, v1)
    rest = [jnp.where(i1 == j, -BIG, fine[j]) for j in range(EXPERTS_PER_GROUP)]
    v2 = functools.reduce(jnp.maximum, rest)
    i2 = _first_argmax(rest, v2)
    e2 = jnp.exp(v2 - v1)
    w1 = p_grp / (1.0 + e2)
    w2 = p_grp * e2 / (1.0 + e2)
    return [jnp.where(i1 == j, w1, 0.0) + jnp.where(i2 == j, w2, 0.0) for j in range(EXPERTS_PER_GROUP)]


def _rows_to_lanes(rows, tm):
    pad = jnp.zeros((LANES - len(rows), tm), F32)
    return jnp.concatenate(list(rows) + [pad], axis=0).T


def _router_gates(h, whi, wlo, bias):
    lg = _router_logits(h, whi, wlo, bias)
    grp = _top_group(lg)
    in_group = _in_group_gates(lg, grp)
    rows = [jnp.where(grp == e // EXPERTS_PER_GROUP, in_group[e % EXPERTS_PER_GROUP], 0.0) for e in range(N_EXPERTS)]
    return _rows_to_lanes(rows, lg.shape[1])


def _moe_kernel(x_ref, g_ref, whi_ref, wlo_ref, b_ref, w1_ref, w3_ref, w2_ref, o_ref, h_scr, gates_scr, acc_scr):
    gi = pl.program_id(1)

    @pl.when(gi == 0)
    def _():
        h = _rms_rows(x_ref[...], g_ref[...])
        h_scr[...] = h.astype(BF)
        gates_scr[...] = _router_gates(h, whi_ref[...], wlo_ref[...], b_ref[...])
        acc_scr[...] = jnp.zeros_like(acc_scr)

    h = h_scr[...]
    gates = gates_scr[...]
    lane = lax.broadcasted_iota(jnp.int32, gates.shape, 1)
    for j in range(EXPERTS_PER_GROUP):
        gcol = jnp.sum(jnp.where(lane == gi * EXPERTS_PER_GROUP + j, gates, 0.0), axis=-1, keepdims=True)
        a = _dot(h, w1_ref[j])
        b = _dot(h, w3_ref[j])
        hid = (a * jax.nn.sigmoid(a)) * b * gcol
        acc_scr[...] += _dot(hid.astype(BF), w2_ref[j])

    @pl.when(gi == pl.num_programs(1) - 1)
    def _():
        o_ref[...] = x_ref[...] + acc_scr[...]


def _moe(x, li, g, whi, wlo, b, w1, w3, w2, tm):
    m = x.shape[0]
    row = lambda w_: pl.BlockSpec((tm, w_), lambda i, e: (i, 0))
    group_w = lambda a, b_: pl.BlockSpec((None, EXPERTS_PER_GROUP, a, b_), lambda i, e: (li, e, 0, 0))
    return pl.pallas_call(
        _moe_kernel,
        grid=(m // tm, N_GROUPS),
        in_specs=[row(D_MODEL), _layer_spec(li, 1, D_MODEL), _layer_spec(li, ROUTER_ROWS, D_MODEL),
                  _layer_spec(li, ROUTER_ROWS, D_MODEL), _layer_spec(li, ROUTER_ROWS, 1),
                  group_w(D_MODEL, D_EXPERT), group_w(D_MODEL, D_EXPERT), group_w(D_EXPERT, D_MODEL)],
        out_specs=row(D_MODEL),
        out_shape=jax.ShapeDtypeStruct((m, D_MODEL), F32),
        scratch_shapes=[pltpu.VMEM((tm, D_MODEL), BF), pltpu.VMEM((tm, LANES), F32), pltpu.VMEM((tm, D_MODEL), F32)],
        compiler_params=_cparams("parallel", "arbitrary"),
        name="moe",
    )(x, g, whi, wlo, b, w1, w3, w2)


MOE_TILE = 256
DMA_PRIORITIES = 2


def _sorted_plan(grp, m):
    t = MOE_TILE
    n_tiles = m // t + N_GROUPS
    order = jnp.argsort(grp, stable=True).astype(jnp.int32)
    counts = jnp.sum((grp[None, :] == jnp.arange(N_GROUPS, dtype=jnp.int32)[:, None]).astype(jnp.int32), axis=1)
    tiles_per_group = (counts + t - 1) // t
    tile_end = jnp.cumsum(tiles_per_group)
    tile_start = tile_end - tiles_per_group
    first_token = jnp.cumsum(counts) - counts
    s = jnp.arange(n_tiles, dtype=jnp.int32)
    tile_grp = jnp.minimum(jnp.sum((s[:, None] >= tile_end[None, :]).astype(jnp.int32), axis=1), N_GROUPS - 1)
    n_active = tile_end[N_GROUPS - 1]
    tile_in_group = s - tile_start[tile_grp]
    n_valid = jnp.where(s < n_active, jnp.clip(counts[tile_grp] - tile_in_group * t, 0, t), 0)
    r = jnp.arange(t, dtype=jnp.int32)
    slot_token = order[jnp.clip(first_token[tile_grp][:, None] + tile_in_group[:, None] * t + r[None, :], 0, m - 1)]
    valid = r[None, :] < n_valid[:, None]
    src = jnp.where(valid, slot_token, 0).reshape(-1)
    dst = jnp.where(valid, slot_token, m + (s % 2)[:, None] * t + r[None, :]).reshape(-1)
    src = jnp.concatenate([src, jnp.zeros((t,), jnp.int32)])
    dst = jnp.concatenate([m + t + r, dst])
    return ((src * TT_CHUNKS).astype(jnp.int32), (dst * TT_CHUNKS).astype(jnp.int32), tile_grp.astype(jnp.int32))


def _sorted_moe_kernel(m, src_ref, dst_ref, tgrp_ref, x_hbm, g_ref, whi_ref, wlo_ref, b_ref,
                       w1_ref, w3_ref, w2_ref, out_hbm, xbuf0, xbuf1, obuf0, obuf1, sem_g, sem_s):
    t = MOE_TILE
    i = pl.program_id(0)
    last = pl.num_programs(0) - 1
    xbuf = (xbuf0, xbuf1)
    obuf = (obuf0, obuf1)

    c8 = TT_CHUNKS

    def gather_row(tile, slot, r):
        src = pl.multiple_of(src_ref[tile * t + r], c8)
        return pltpu.make_async_copy(x_hbm.at[pl.ds(src, c8), :], xbuf[slot].at[pl.ds(r * c8, c8), :], sem_g.at[slot])

    def scatter_row(tile, slot, r):
        dst = pl.multiple_of(dst_ref[(tile + 1) * t + r], c8)
        return pltpu.make_async_copy(obuf[slot].at[pl.ds(r * c8, c8), :], out_hbm.at[pl.ds(dst, c8), :],
                                     sem_s.at[slot])

    def whole_gather(slot):
        return pltpu.make_async_copy(x_hbm.at[pl.ds(0, t * c8), :], xbuf[slot], sem_g.at[slot])

    def whole_scatter(slot, row0):
        return pltpu.make_async_copy(obuf[slot], out_hbm.at[pl.ds(row0, t * c8), :], sem_s.at[slot])

    @pl.when(i == 0)
    def _():
        for slot in range(2):
            obuf[slot][...] = jnp.zeros((t * c8, LANES), F32)
        for slot in range(2):
            whole_scatter(slot, (m + slot * t) * c8).start()
        for slot in range(2):
            whole_scatter(slot, (m + slot * t) * c8).wait()

        def issue(r, carry):
            gather_row(0, 0, r).start()
            return carry
        lax.fori_loop(0, t, issue, 0)

    def step(cur, nxt):
        whole_gather(cur).wait()
        for r in range(t):
            gather_row(i + 1, nxt, r).start(priority=r % DMA_PRIORITIES)
        for r in range(t):
            scatter_row(i - 1, nxt, r).start(priority=r % DMA_PRIORITIES)

        x = _read_tokens(xbuf[cur], True)
        h = _rms_rows(x, g_ref[...])
        lg = _router_logits(h, whi_ref[...], wlo_ref[...], b_ref[...])
        gates = _rows_to_lanes(_in_group_gates(lg, tgrp_ref[i]), t)
        hb = h.astype(BF)
        acc = jnp.zeros((t, D_MODEL), F32)
        for j in range(EXPERTS_PER_GROUP):
            a = _dot(hb, w1_ref[j])
            b = _dot(hb, w3_ref[j])
            hid = (a * jax.nn.sigmoid(a)) * b * gates[:, j:j + 1]
            acc = acc + _dot(hid.astype(BF), w2_ref[j])
        res = x + acc

        @pl.when(i >= 1)
        def _():
            whole_scatter(cur, 0).wait()

        _write_token_tiles(obuf[cur], res)

        @pl.when(i == last)
        def _():
            def issue(r, carry):
                scatter_row(i, cur, r).start()
                return carry
            lax.fori_loop(0, t, issue, 0)
            whole_scatter(cur, 0).wait()
            whole_scatter(nxt, 0).wait()
            whole_gather(nxt).wait()

    for parity in range(2):
        @pl.when(i % 2 == parity)
        def _():
            step(parity, 1 - parity)


def _sorted_moe(x, grp, m, li, g, whi, wlo, b, w1, w3, w2):
    t = MOE_TILE
    n_tiles = m // t + N_GROUPS
    src, dst, tile_grp = _sorted_plan(grp, m)
    group_w = lambda a, b_: pl.BlockSpec((None, EXPERTS_PER_GROUP, a, b_), lambda i, src_, dst_, tg: (li, tg[i], 0, 0))
    return pl.pallas_call(
        functools.partial(_sorted_moe_kernel, m),
        grid_spec=pltpu.PrefetchScalarGridSpec(
            num_scalar_prefetch=3,
            grid=(n_tiles,),
            in_specs=[pl.BlockSpec(memory_space=pl.ANY), _layer_spec(li, 1, D_MODEL),
                      _layer_spec(li, ROUTER_ROWS, D_MODEL), _layer_spec(li, ROUTER_ROWS, D_MODEL),
                      _layer_spec(li, ROUTER_ROWS, 1),
                      group_w(D_MODEL, D_EXPERT), group_w(D_MODEL, D_EXPERT), group_w(D_EXPERT, D_MODEL)],
            out_specs=pl.BlockSpec(memory_space=pl.ANY),
            scratch_shapes=[pltpu.VMEM((t * TT_CHUNKS, LANES), F32)] * 4
            + [pltpu.SemaphoreType.DMA((2,)), pltpu.SemaphoreType.DMA((2,))]),
        out_shape=jax.ShapeDtypeStruct(((m + 2 * t) * TT_CHUNKS, LANES), F32),
        compiler_params=_cparams("arbitrary"),
        name="moe_sorted",
    )(src, dst, tile_grp, x, g, whi, wlo, b, w1, w3, w2)


def _heads_kv_major_to_g_major(w, axis):
    shape = w.shape
    split = shape[:axis] + (KV_HEADS, Q_PER_KV, HEAD_DIM) + shape[axis + 1:]
    return jnp.swapaxes(w.reshape(split), axis, axis + 1).reshape(shape)


def _prep_ssm(a_re, a_im, log_dt, b_re, b_im, c_re, c_im, d_skip, w_glu):
    ar, ai = a_re.astype(F32), a_im.astype(F32)
    dt = jnp.exp(log_dt.astype(F32))[..., None]
    mag = jnp.exp(ar * dt)
    abr, abi = mag * jnp.cos(ai * dt), mag * jnp.sin(ai * dt)
    den = ar * ar + ai * ai
    kr = ((abr - 1.0) * ar + abi * ai) / den
    ki = (abi * ar - (abr - 1.0) * ai) / den
    br, bi = b_re.astype(F32), b_im.astype(F32)
    bbr = kr[..., None] * br - ki[..., None] * bi
    bbi = kr[..., None] * bi + ki[..., None] * br
    n_layers = ar.shape[0]
    eye = jnp.eye(SSM_GROUPS, dtype=F32)
    blockdiag_in = lambda b: jnp.einsum('egph,gk->eghkp', b, eye).reshape(n_layers, SSM_W, SSM_FLAT)
    blockdiag_out = lambda c: jnp.einsum('eghp,gk->egpkh', c, eye).reshape(n_layers, SSM_FLAT, SSM_W)
    in_tiles = lambda b: jnp.stack([b[:, j * SSM_IN_TILE:(j + 1) * SSM_IN_TILE, j * SSM_ST_TILE:(j + 1) * SSM_ST_TILE]
                                    for j in range(SSM_IN_TILES)], axis=1)
    out_tiles = lambda c: jnp.stack([c[:, j * SSM_OUT_ST:(j + 1) * SSM_OUT_ST, j * SSM_OUT_TILE:(j + 1) * SSM_OUT_TILE]
                                     for j in range(SSM_OUT_TILES)], axis=1)
    bcat = jnp.concatenate([in_tiles(blockdiag_in(bbr)), in_tiles(blockdiag_in(bbi))], axis=3).astype(BF)
    ccat = jnp.concatenate([out_tiles(blockdiag_out(c_re.astype(F32))), -out_tiles(blockdiag_out(c_im.astype(F32)))],
                           axis=2).astype(BF)
    return (abr.reshape(n_layers, 1, SSM_FLAT), abi.reshape(n_layers, 1, SSM_FLAT), bcat, ccat,
            d_skip.astype(F32).reshape(n_layers, 1, SSM_W), w_glu.astype(BF))


def _run_trunk(x, n, seq, mem_k, mem_v, caches, p):
    prompt = caches is None
    tm = 512
    m = n * seq
    wk_out, wv_out, hr_out, hi_out, cv_out = [], [], [], [], []
    xattn_args = (p['g_xattn'], p['w_xq'], p['g_xq'], mem_k, mem_v, p['w_xo'])
    router_args = (p['g_ffn'], p['w_router_hi'], p['w_router_lo'], p['b_router'])
    expert_args = (p['w_e1'], p['w_e3'], p['w_e2'])
    tiled = False
    for li in range(DEPTH):
        e = li // 2
        if li % 2 == 0:
            q, k, v, u = _even_in_proj(x, li, e, p['g_mix'], p['w_in_even'], p['g_q'], p['g_k'], p['bd'], tm, n, seq,
                                       token_tiled=tiled)
            if prompt:
                att = _swa_prompt(q, k, v, p['sinks'], e, n, seq)
                k3, v3 = k.reshape(n, seq, KV_W), v.reshape(n, seq, KV_W)
                new_k, new_v = k3[:, seq - WINDOW:], v3[:, seq - WINDOW:]
                ssm, h_re, h_im = _s5_glu(u.reshape(seq, n, SSM_W), None, p['ssm'], e, n, seq, 128)
                x, grp = _even_out_xattn(x, tiled, att, ssm.reshape(seq, n * SSM_W), e, p['w_out_att'], p['w_out_ssm'],
                                         li, xattn_args + router_args, n, seq, tm)
            else:
                win_k, win_v, ssm_re, ssm_im = caches
                att, new_k, new_v = _swa_sample(q, k, v, win_k, win_v, p['sinks'], e, n, seq)
                u_tm = jnp.swapaxes(u.reshape(n, seq, SSM_W), 0, 1)
                ssm, h_re, h_im = _s5_glu(u_tm, (ssm_re, ssm_im), p['ssm'], e, n, seq, seq)
                ssm = jnp.swapaxes(ssm, 0, 1).reshape(m, SSM_W)
                x = _even_out_proj(x, att, ssm, e, p['w_out_att'], p['w_out_ssm'], tm, n, seq)
            wk_out.append(new_k.reshape(n, WINDOW, KV_HEADS, HEAD_DIM))
            wv_out.append(new_v.reshape(n, WINDOW, KV_HEADS, HEAD_DIM))
            hr_out.append(h_re.reshape(n, SSM_GROUPS, SSM_STATE))
            hi_out.append(h_im.reshape(n, SSM_GROUPS, SSM_STATE))
        else:
            u, v = _odd_in_proj(x, m, li, e, p['g_mix'], p['w_in_odd'], p['g_v'], tm, BF if prompt else F32,
                                token_tiled=tiled)
            wsp, bsp, csize = p['gmlp_prompt'] if prompt else p['gmlp_sample']
            if prompt:
                x, grp = _gmlp_xattn(x, tiled, u, v, e, wsp, bsp, p['w_out_odd'], csize, li, xattn_args + router_args,
                                     n, seq, tm)
            else:
                x = _gmlp_mix(x, u, v, e, wsp, bsp, p['w_out_odd'], csize, tm)
                cv_out.append(v.reshape(n, seq, GM_W))
        if prompt:
            x = _sorted_moe(x, grp, m, li, *router_args, *expert_args)
            tiled = True
        else:
            x = _xattn_sample(x, li, *xattn_args, n, seq)
            x = _moe(x, li, *router_args, *expert_args, tm)
    if tiled:
        x = _untile(x, m, tm)
    cv = None if prompt else jnp.stack(cv_out)
    return x, jnp.stack(wk_out), jnp.stack(wv_out), jnp.stack(hr_out), jnp.stack(hi_out), cv


def kernel(x_prompt, x_sample, cache_win_k, cache_win_v, state_ssm_re, state_ssm_im, cache_mem_k, cache_mem_v, mem_prompt, g_mix, g_xattn, g_ffn, g_mem, w_in_even, g_q, g_k, sinks, ssm_a_re, ssm_a_im, ssm_log_dt, ssm_b_re, ssm_b_im, ssm_c_re, ssm_c_im, ssm_d, w_glu, w_out_even, w_in_odd, g_v, w_spatial, b_spatial, w_out_odd, w_xq, g_xq, w_xk, g_xk, w_xv, w_xo, w_coarse, b_coarse, w_fine, b_fine, w_e1, w_e3, w_e2):
    batch, seq = x_prompt.shape[0], x_prompt.shape[1]
    dec_batch, dec_seq = x_sample.shape[0], x_sample.shape[1]
    n_even = w_in_even.shape[0]

    row1 = lambda a: a.astype(F32)[:, None, :]
    p = {}
    p['g_mix'], p['g_xattn'], p['g_ffn'] = row1(g_mix), row1(g_xattn), row1(g_ffn)
    w_in_q = _heads_kv_major_to_g_major(w_in_even[:, :, :A_W], 2)
    p['w_in_even'] = jnp.concatenate([w_in_q, w_in_even[:, :, A_W:]], axis=2).astype(BF)
    p['g_q'] = jnp.tile(g_q.astype(F32), (1, A_HEADS))[:, None, :]
    p['g_k'] = jnp.tile(g_k.astype(F32), (1, KV_HEADS))[:, None, :]
    head_id = jnp.arange(A_W) // HEAD_DIM
    p['bd'] = (head_id[:, None] == head_id[None, :]).astype(BF)
    p['sinks'] = sinks.astype(F32).reshape(-1)
    p['ssm'] = _prep_ssm(ssm_a_re, ssm_a_im, ssm_log_dt, ssm_b_re, ssm_b_im, ssm_c_re, ssm_c_im, ssm_d, w_glu)
    p['w_out_att'] = _heads_kv_major_to_g_major(w_out_even[:, :A_W, :], 1).astype(BF)
    p['w_out_ssm'] = w_out_even[:, A_W:, :].astype(BF)
    p['w_in_odd'] = w_in_odd.astype(BF)
    p['g_v'] = row1(g_v)
    p['w_out_odd'] = w_out_odd.astype(BF)
    cs = min(dec_seq, CHUNK)
    reps = CHUNK // cs
    p['gmlp_prompt'] = (w_spatial.astype(F32), jnp.swapaxes(b_spatial.astype(F32), 1, 2), CHUNK)
    p['gmlp_sample'] = (jnp.tile(w_spatial[:, :, :cs, :cs].astype(F32), (1, 1, reps, reps)),
                        jnp.tile(jnp.swapaxes(b_spatial[:, :, :cs].astype(F32), 1, 2), (1, reps, 1)), cs)
    p['w_xq'] = w_xq.astype(BF)
    p['g_xq'] = row1(g_xq)
    p['w_xo'] = w_xo.astype(BF)
    w_router = jnp.swapaxes(jnp.concatenate([w_fine, w_coarse], axis=2).astype(F32), 1, 2)
    w_router = jnp.pad(w_router, ((0, 0), (0, ROUTER_ROWS - w_router.shape[1]), (0, 0)))
    p['w_router_hi'] = w_router.astype(BF)
    p['w_router_lo'] = (w_router - p['w_router_hi'].astype(F32)).astype(BF)
    b_router = jnp.concatenate([b_fine, b_coarse], axis=1).astype(F32)
    p['b_router'] = jnp.pad(b_router, ((0, 0), (0, ROUTER_ROWS - b_router.shape[1])))[:, :, None]
    p['w_e1'], p['w_e3'], p['w_e2'] = w_e1.astype(BF), w_e3.astype(BF), w_e2.astype(BF)

    mem2d = mem_prompt.reshape(batch * MEM_LEN, D_MODEL)
    pk, pv = _memory_kv(mem2d, row1(g_mem), w_xk.astype(BF), row1(g_xk), w_xv.astype(BF), batch)
    y_p, p_wk, p_wv, p_hr, p_hi, _ = _run_trunk(x_prompt.reshape(batch * seq, D_MODEL), batch, seq, pk, pv, None, p)

    caches = (cache_win_k.reshape(n_even, dec_batch, WINDOW, KV_W), cache_win_v.reshape(n_even, dec_batch, WINDOW, KV_W),
              state_ssm_re.reshape(n_even, dec_batch, SSM_FLAT), state_ssm_im.reshape(n_even, dec_batch, SSM_FLAT))
    y_s, s_wk, s_wv, s_hr, s_hi, s_cv = _run_trunk(x_sample.reshape(dec_batch * dec_seq, D_MODEL), dec_batch, dec_seq,
                                                   cache_mem_k.reshape(DEPTH, dec_batch, MEM_ROWS, X_HEAD_DIM),
                                                   cache_mem_v.reshape(DEPTH, dec_batch, MEM_ROWS, X_HEAD_DIM), caches, p)

    mem_shape = (DEPTH, batch, MEM_LEN, X_HEADS, X_HEAD_DIM)
    return (y_p.reshape(batch, seq, D_MODEL), y_s.reshape(dec_batch, dec_seq, D_MODEL), p_wk, p_wv, p_hr, p_hi,
            pk.reshape(mem_shape), pv.reshape(mem_shape), s_wk, s_wv, s_hr, s_hi, s_cv)
```

```python
import functools
import math

import jax
import jax.numpy as jnp
from jax import lax
from jax.experimental import pallas as pl
from jax.experimental.pallas import tpu as pltpu

F32 = jnp.float32
BF = jnp.bfloat16

D_MODEL = 1024
DEPTH = 4
A_W = 512
HEAD_DIM = 64
A_HEADS = 8
KV_HEADS = 2
Q_PER_KV = 4
KV_W = 128
WINDOW = 128
SSM_W = 512
SSM_CH = 16
SSM_GROUPS = 32
SSM_STATE = 64
SSM_FLAT = SSM_GROUPS * SSM_STATE
IN_EVEN = A_W + 2 * KV_W + SSM_W
CHUNK = 128
GM_W = 2048
GM_HEADS = 8
GM_HD = 256
MEM_LEN = 256
X_HEADS = 4
X_HEAD_DIM = 128
X_W = 512
N_GROUPS = 4
EXPERTS_PER_GROUP = 4
N_EXPERTS = 16
D_EXPERT = 256
EPS = 1e-6
NEG_INF = -1e30
LANES = 128
VMEM_LIMIT_BYTES = 52 * 1024 * 1024
ALIBI_SLOPES = tuple(2.0 ** (-8.0 * (h + 1) / A_HEADS) for h in range(A_HEADS))
SSM_SEQ_BLOCK = 8


def _cparams(*sem):
    return pltpu.CompilerParams(dimension_semantics=sem, vmem_limit_bytes=VMEM_LIMIT_BYTES)


def _rms_rows(x, g):
    return x * lax.rsqrt(jnp.mean(x * x, axis=-1, keepdims=True) + EPS) * g


TT_CHUNKS = D_MODEL // LANES


def _read_tokens(ref, token_tiled):
    if not token_tiled:
        return ref[...]
    rows = ref.shape[0] // TT_CHUNKS
    return jnp.concatenate([ref[pl.ds(c, rows, stride=TT_CHUNKS), :] for c in range(TT_CHUNKS)], axis=1)


def _write_token_tiles(ref, val):
    rows = val.shape[0]
    for c in range(TT_CHUNKS):
        ref[pl.ds(c, rows, stride=TT_CHUNKS), :] = val[:, c * LANES:(c + 1) * LANES]


def _token_spec(rows, index, token_tiled):
    if token_tiled:
        return pl.BlockSpec((rows * TT_CHUNKS, LANES), lambda *ids: (index(*ids), 0))
    return pl.BlockSpec((rows, D_MODEL), lambda *ids: (index(*ids), 0))


def _untile_kernel(x_ref, o_ref):
    o_ref[...] = _read_tokens(x_ref, True)


def _untile(x, m, tm):
    return pl.pallas_call(
        _untile_kernel,
        grid=(m // tm,),
        in_specs=[_token_spec(tm, lambda i: i, True)],
        out_specs=pl.BlockSpec((tm, D_MODEL), lambda i: (i, 0)),
        out_shape=jax.ShapeDtypeStruct((m, D_MODEL), F32),
        compiler_params=_cparams("parallel"),
        name="untile",
    )(x)


def _dot(a, b):
    return jnp.dot(a, b, preferred_element_type=F32)


def _dot_nt(a, b):
    return lax.dot_general(a, b, (((1,), (1,)), ((), ())), preferred_element_type=F32)


def _full(shape):
    nd = len(shape)
    return pl.BlockSpec(shape, lambda *_: (0,) * nd)


def _layer_spec(li, *shape, single_buffer=False):
    nd = len(shape)
    mode = pl.Buffered(1) if single_buffer else None
    return pl.BlockSpec((None,) + shape, lambda *_: (li,) + (0,) * nd, pipeline_mode=mode)


def _even_in_kernel(token_tiled, x_ref, g_ref, w_ref, gq_ref, gk_ref, bd_ref, q_ref, k_ref, v_ref, u_ref):
    h = _rms_rows(_read_tokens(x_ref, token_tiled), g_ref[...]).astype(BF)
    z = _dot(h, w_ref[...])
    q = z[:, :A_W]
    k = z[:, A_W:A_W + KV_W]
    bd = bd_ref[...]
    qms = _dot((q * q).astype(BF), bd) * (1.0 / HEAD_DIM)
    kms = _dot((k * k).astype(BF), bd[:KV_W, :KV_W]) * (1.0 / HEAD_DIM)
    q_ref[...] = q * lax.rsqrt(qms + EPS) * gq_ref[...]
    k_ref[...] = k * lax.rsqrt(kms + EPS) * gk_ref[...]
    v_ref[...] = z[:, A_W + KV_W:A_W + 2 * KV_W]
    u_ref[...] = z[:, A_W + 2 * KV_W:]


def _time_major_spec(tm, n, seq, width):
    per_seq = seq // tm
    return pl.BlockSpec((tm, width), lambda i: (i % per_seq, i // per_seq))


def _even_in_proj(x, li, e, g, w, gq, gk, bd, tm, n, seq, token_tiled=False):
    m = n * seq
    row = lambda w_: pl.BlockSpec((tm, w_), lambda i: (i, 0))
    return pl.pallas_call(
        functools.partial(_even_in_kernel, token_tiled),
        grid=(m // tm,),
        in_specs=[_token_spec(tm, lambda i: i, token_tiled), _layer_spec(li, 1, D_MODEL),
                  _layer_spec(e, D_MODEL, IN_EVEN), _layer_spec(e, 1, A_W),
                  _layer_spec(e, 1, KV_W), _full((A_W, A_W))],
        out_specs=[row(A_W), row(KV_W), row(KV_W), _time_major_spec(tm, n, seq, SSM_W) if seq % tm == 0 else row(SSM_W)],
        out_shape=[jax.ShapeDtypeStruct((m, A_W), F32), jax.ShapeDtypeStruct((m, KV_W), F32),
                   jax.ShapeDtypeStruct((m, KV_W), F32),
                   jax.ShapeDtypeStruct((seq, n * SSM_W) if seq % tm == 0 else (m, SSM_W), F32)],
        compiler_params=_cparams("parallel"),
        name="even_in_proj",
    )(x, g, w, gq, gk, bd)


def _head_rows(q, lane_lo):
    rows = []
    for kv in range(KV_HEADS):
        keep = lane_lo if kv == 0 else jnp.logical_not(lane_lo)
        for g in range(Q_PER_KV):
            rows.append(jnp.where(keep, q[:, g * LANES:(g + 1) * LANES], 0.0))
    return jnp.concatenate(rows, axis=0)


def _head_cols(o, r, lane_lo):
    cols = []
    for g in range(Q_PER_KV):
        cols.append(jnp.where(lane_lo, o[g * r:(g + 1) * r], o[(Q_PER_KV + g) * r:(Q_PER_KV + g + 1) * r]))
    return jnp.concatenate(cols, axis=1)


def _swa_prompt_kernel(e, sinks_ref, q_ref, kc_ref, kp_ref, vc_ref, vp_ref, o_ref):
    b = pl.program_id(1)
    lane_lo = lax.broadcasted_iota(jnp.int32, (1, LANES), 1) < HEAD_DIM
    qrows = _head_rows(q_ref[...], lane_lo).astype(BF)
    kcat = jnp.concatenate([kp_ref[...], kc_ref[...]], axis=0).astype(BF)
    vcat = jnp.concatenate([vp_ref[...], vc_ref[...]], axis=0).astype(BF)
    s = _dot_nt(qrows, kcat) * (HEAD_DIM ** -0.5)
    r = lax.broadcasted_iota(jnp.int32, (WINDOW, 2 * WINDOW), 0)
    c = lax.broadcasted_iota(jnp.int32, (WINDOW, 2 * WINDOW), 1)
    dist = WINDOW + r - c
    valid = (dist >= 0) & (dist <= WINDOW) & ((c >= WINDOW) | (b > 0))
    dist_f = dist.astype(F32)
    ps = []
    for h in range(A_HEADS):
        sh = s[h * WINDOW:(h + 1) * WINDOW] - ALIBI_SLOPES[h] * dist_f
        sh = jnp.where(valid, sh, NEG_INF)
        sk = sinks_ref[e * A_HEADS + h]
        m = jnp.maximum(jnp.max(sh, axis=-1, keepdims=True), sk)
        p = jnp.exp(sh - m)
        den = jnp.sum(p, axis=-1, keepdims=True) + jnp.exp(sk - m)
        ps.append((p / den).astype(BF))
    o = _dot(jnp.concatenate(ps, axis=0), vcat)
    o_ref[...] = _head_cols(o, WINDOW, lane_lo).astype(o_ref.dtype)


def _swa_prompt(q, k, v, sinks, e, n, seq):
    nb = seq // WINDOW
    cur = lambda w_: pl.BlockSpec((WINDOW, w_), lambda i, b: (i * nb + b, 0))
    prev = lambda w_: pl.BlockSpec((WINDOW, w_), lambda i, b: (i * nb + jnp.maximum(b - 1, 0), 0))
    return pl.pallas_call(
        functools.partial(_swa_prompt_kernel, e),
        grid=(n, nb),
        in_specs=[pl.BlockSpec(memory_space=pltpu.SMEM), cur(A_W), cur(KV_W), prev(KV_W), cur(KV_W), prev(KV_W)],
        out_specs=cur(A_W),
        out_shape=jax.ShapeDtypeStruct((n * seq, A_W), BF),
        compiler_params=_cparams("parallel", "arbitrary"),
        name="swa_prompt",
    )(sinks, q, k, k, v, v)


SWA_SEQ_BLOCK = 8


def _swa_sample_kernel(t_new, e, sinks_ref, q_ref, kn_ref, vn_ref, wk_ref, wv_ref, o_ref, nwk_ref, nwv_ref):
    nseq = SWA_SEQ_BLOCK
    pair_rows = 2 * t_new
    npair = nseq // 2
    kn = kn_ref[...]
    vn = vn_ref[...]
    for i in range(nseq):
        nwk_ref[i, 0:WINDOW - t_new, :] = wk_ref[i, t_new:WINDOW, :]
        nwk_ref[i, WINDOW - t_new:WINDOW, :] = kn[i * t_new:(i + 1) * t_new, :]
        nwv_ref[i, 0:WINDOW - t_new, :] = wv_ref[i, t_new:WINDOW, :]
        nwv_ref[i, WINDOW - t_new:WINDOW, :] = vn[i * t_new:(i + 1) * t_new, :]

    lane_lo = lax.broadcasted_iota(jnp.int32, (1, LANES), 1) < HEAD_DIM
    q = q_ref[...]
    knb = kn.astype(BF)
    vnb = vn.astype(BF)
    hr = A_HEADS * pair_rows
    qrows = [_head_rows(q[j * pair_rows:(j + 1) * pair_rows], lane_lo).astype(BF) for j in range(npair)]
    scale = HEAD_DIM ** -0.5
    s_new_all = _dot_nt(jnp.concatenate(qrows, axis=0), knb) * scale

    tq = lax.broadcasted_iota(jnp.int32, (pair_rows, 2 * WINDOW), 0)
    cw = lax.broadcasted_iota(jnp.int32, (pair_rows, 2 * WINDOW), 1)
    dist_w = WINDOW + (tq % t_new) - (cw % WINDOW)
    valid_w = ((tq // t_new) == (cw // WINDOW)) & (dist_w <= WINDOW)
    dist_wf = dist_w.astype(F32)
    tqn = lax.broadcasted_iota(jnp.int32, (pair_rows, nseq * t_new), 0)
    cn = lax.broadcasted_iota(jnp.int32, (pair_rows, nseq * t_new), 1)
    dist_n = (tqn % t_new) - (cn % t_new)
    dist_nf = dist_n.astype(F32)

    p_new_all = []
    o_win_all = []
    for j in range(npair):
        kwin = jnp.concatenate([wk_ref[2 * j], wk_ref[2 * j + 1]], axis=0).astype(BF)
        vwin = jnp.concatenate([wv_ref[2 * j], wv_ref[2 * j + 1]], axis=0).astype(BF)
        s_win = _dot_nt(qrows[j], kwin) * scale
        valid_n = ((2 * j + tqn // t_new) == (cn // t_new)) & (dist_n >= 0)
        p_win = []
        for h in range(A_HEADS):
            sw = jnp.where(valid_w, s_win[h * pair_rows:(h + 1) * pair_rows] - ALIBI_SLOPES[h] * dist_wf, NEG_INF)
            sn = s_new_all[j * hr + h * pair_rows:j * hr + (h + 1) * pair_rows]
            sn = jnp.where(valid_n, sn - ALIBI_SLOPES[h] * dist_nf, NEG_INF)
            sk = sinks_ref[e * A_HEADS + h]
            m = jnp.maximum(jnp.maximum(jnp.max(sw, axis=-1, keepdims=True), jnp.max(sn, axis=-1, keepdims=True)), sk)
            pw = jnp.exp(sw - m)
            pn = jnp.exp(sn - m)
            den = jnp.sum(pw, axis=-1, keepdims=True) + jnp.sum(pn, axis=-1, keepdims=True) + jnp.exp(sk - m)
            p_win.append((pw / den).astype(BF))
            p_new_all.append((pn / den).astype(BF))
        o_win_all.append(_dot(jnp.concatenate(p_win, axis=0), vwin))
    o_new = _dot(jnp.concatenate(p_new_all, axis=0), vnb)
    outs = [_head_cols(o_win_all[j] + o_new[j * hr:(j + 1) * hr], pair_rows, lane_lo) for j in range(npair)]
    o_ref[...] = jnp.concatenate(outs, axis=0).astype(o_ref.dtype)


def _swa_sample(q, kn, vn, win_k, win_v, sinks, e, n, t_new):
    rows = SWA_SEQ_BLOCK * t_new
    row = lambda w_: pl.BlockSpec((rows, w_), lambda i: (i, 0))
    win = pl.BlockSpec((SWA_SEQ_BLOCK, WINDOW, KV_W), lambda i: (i, 0, 0))
    win_in = pl.BlockSpec((None, SWA_SEQ_BLOCK, WINDOW, KV_W), lambda i: (e, i, 0, 0))
    return pl.pallas_call(
        functools.partial(_swa_sample_kernel, t_new, e),
        grid=(n // SWA_SEQ_BLOCK,),
        in_specs=[pl.BlockSpec(memory_space=pltpu.SMEM), row(A_W), row(KV_W), row(KV_W), win_in, win_in],
        out_specs=[row(A_W), win, win],
        out_shape=[jax.ShapeDtypeStruct((n * t_new, A_W), BF),
                   jax.ShapeDtypeStruct((n, WINDOW, KV_W), F32), jax.ShapeDtypeStruct((n, WINDOW, KV_W), F32)],
        compiler_params=_cparams("parallel"),
        name="swa_sample",
    )(sinks, q, kn, vn, win_k, win_v)


SSM_LANE_CHUNK = 1024
SSM_IN_TILE = LANES
SSM_IN_TILES = SSM_W // SSM_IN_TILE
SSM_ST_TILE = SSM_IN_TILE // SSM_CH * SSM_STATE
SSM_OUT_TILE = 256
SSM_OUT_TILES = SSM_W // SSM_OUT_TILE
SSM_OUT_ST = SSM_OUT_TILE // SSM_CH * SSM_STATE


def _s5_kernel(tc, has_h0, *refs):
    if has_h0:
        (u_ref, h0r_ref, h0i_ref, abr_ref, abi_ref, bcat_ref, ccat_ref, d_ref, wglu_ref,
         o_ref, hr_ref, hi_ref, st_scr, car_scr) = refs
    else:
        (u_ref, abr_ref, abi_ref, bcat_ref, ccat_ref, d_ref, wglu_ref,
         o_ref, hr_ref, hi_ref, st_scr, car_scr) = refs
    nb = SSM_SEQ_BLOCK
    ci = pl.program_id(1)

    @pl.when(ci == 0)
    def _():
        if has_h0:
            car_scr[:, :SSM_FLAT] = h0r_ref[...]
            car_scr[:, SSM_FLAT:] = h0i_ref[...]
        else:
            car_scr[...] = jnp.zeros_like(car_scr)

    ut = u_ref[...].reshape(tc * nb, SSM_W)
    ub = ut.astype(BF)
    for jt in range(SSM_IN_TILES):
        bu = _dot(ub[:, jt * SSM_IN_TILE:(jt + 1) * SSM_IN_TILE], bcat_ref[jt])
        st_scr[:, jt * SSM_ST_TILE:(jt + 1) * SSM_ST_TILE] = bu[:, :SSM_ST_TILE]
        st_scr[:, SSM_FLAT + jt * SSM_ST_TILE:SSM_FLAT + (jt + 1) * SSM_ST_TILE] = bu[:, SSM_ST_TILE:]

    for lc in range(SSM_FLAT // SSM_LANE_CHUNK):
        lo = lc * SSM_LANE_CHUNK
        re_sl = slice(lo, lo + SSM_LANE_CHUNK)
        im_sl = slice(SSM_FLAT + lo, SSM_FLAT + lo + SSM_LANE_CHUNK)
        ar = jnp.broadcast_to(abr_ref[:, re_sl], (nb, SSM_LANE_CHUNK))
        ai = jnp.broadcast_to(abi_ref[:, re_sl], (nb, SSM_LANE_CHUNK))

        def step(t, carry):
            hr, hi = carry
            rows = pl.ds(pl.multiple_of(t * nb, nb), nb)
            nr = ar * hr - ai * hi + st_scr[rows, re_sl]
            ni = ar * hi + ai * hr + st_scr[rows, im_sl]
            st_scr[rows, re_sl] = nr
            st_scr[rows, im_sl] = ni
            return nr, ni

        hr, hi = lax.fori_loop(0, tc, step, (car_scr[:, re_sl], car_scr[:, im_sl]))
        car_scr[:, re_sl] = hr
        car_scr[:, im_sl] = hi

    ys = []
    for ot in range(SSM_OUT_TILES):
        re_sl = slice(ot * SSM_OUT_ST, (ot + 1) * SSM_OUT_ST)
        im_sl = slice(SSM_FLAT + ot * SSM_OUT_ST, SSM_FLAT + (ot + 1) * SSM_OUT_ST)
        ys.append(_dot(st_scr[:, re_sl].astype(BF), ccat_ref[ot, :SSM_OUT_ST, :])
                  + _dot(st_scr[:, im_sl].astype(BF), ccat_ref[ot, SSM_OUT_ST:, :]))
    y = jnp.concatenate(ys, axis=1) + d_ref[...] * ut
    y = jax.nn.gelu(y).astype(BF)
    g = _dot(y, wglu_ref[...])
    o_ref[...] = (g[:, :SSM_W] * jax.nn.sigmoid(g[:, SSM_W:])).reshape(tc, nb, SSM_W)

    @pl.when(ci == pl.num_programs(1) - 1)
    def _():
        hr_ref[...] = car_scr[:, :SSM_FLAT]
        hi_ref[...] = car_scr[:, SSM_FLAT:]


def _s5_glu(u, h0, ssm, e, n, seq, tc):
    nb = SSM_SEQ_BLOCK
    abr, abi, bcat, ccat, dsk, wglu = ssm
    u_spec = pl.BlockSpec((tc, nb, SSM_W), lambda i, c: (c, i, 0))
    st_spec = pl.BlockSpec((nb, SSM_FLAT), lambda i, c: (i, 0))
    h0_spec = pl.BlockSpec((None, nb, SSM_FLAT), lambda i, c: (e, i, 0))
    consts = [_layer_spec(e, 1, SSM_FLAT), _layer_spec(e, 1, SSM_FLAT),
              _layer_spec(e, SSM_IN_TILES, SSM_IN_TILE, 2 * SSM_ST_TILE),
              _layer_spec(e, SSM_OUT_TILES, 2 * SSM_OUT_ST, SSM_OUT_TILE), _layer_spec(e, 1, SSM_W),
              _layer_spec(e, SSM_W, 2 * SSM_W)]
    has_h0 = h0 is not None
    in_specs = [u_spec] + ([h0_spec, h0_spec] if has_h0 else []) + consts
    args = [u] + (list(h0) if has_h0 else []) + [abr, abi, bcat, ccat, dsk, wglu]
    return pl.pallas_call(
        functools.partial(_s5_kernel, tc, has_h0),
        grid=(n // nb, seq // tc),
        in_specs=in_specs,
        out_specs=[u_spec, st_spec, st_spec],
        out_shape=[jax.ShapeDtypeStruct((seq, n, SSM_W), F32), jax.ShapeDtypeStruct((n, SSM_FLAT), F32),
                   jax.ShapeDtypeStruct((n, SSM_FLAT), F32)],
        scratch_shapes=[pltpu.VMEM((nb * tc, 2 * SSM_FLAT), F32), pltpu.VMEM((nb, 2 * SSM_FLAT), F32)],
        compiler_params=_cparams("parallel", "arbitrary"),
        name="s5_glu",
    )(*args)


def _even_out_kernel(x_ref, a_ref, s_ref, wa_ref, ws_ref, o_ref):
    o_ref[...] = x_ref[...] + _dot(a_ref[...], wa_ref[...]) + _dot(s_ref[...].astype(BF), ws_ref[...])


def _even_out_proj(x, att, ssm, e, wa, ws, tm, n, seq):
    m = x.shape[0]
    row = lambda w_: pl.BlockSpec((tm, w_), lambda i: (i, 0))
    ssm_spec = _time_major_spec(tm, n, seq, SSM_W) if seq % tm == 0 else row(SSM_W)
    return pl.pallas_call(
        _even_out_kernel,
        grid=(m // tm,),
        in_specs=[row(D_MODEL), row(A_W), ssm_spec, _layer_spec(e, A_W, D_MODEL), _layer_spec(e, SSM_W, D_MODEL)],
        out_specs=row(D_MODEL),
        out_shape=jax.ShapeDtypeStruct((m, D_MODEL), F32),
        compiler_params=_cparams("parallel"),
        name="even_out_proj",
    )(x, att, ssm, wa, ws)


def _odd_in_kernel(token_tiled, x_ref, g_ref, w_ref, gv_ref, u_ref, v_ref):
    h = _rms_rows(_read_tokens(x_ref, token_tiled), g_ref[...]).astype(BF)
    u_ref[...] = jax.nn.gelu(_dot(h, w_ref[:, :GM_W])).astype(u_ref.dtype)
    zv = jax.nn.gelu(_dot(h, w_ref[:, GM_W:]))
    v_ref[...] = _rms_rows(zv, gv_ref[...]).astype(v_ref.dtype)


def _odd_in_proj(x, m, li, e, g, w, gv, tm, v_dtype, token_tiled=False):
    row = lambda w_: pl.BlockSpec((tm, w_), lambda i: (i, 0))
    return pl.pallas_call(
        functools.partial(_odd_in_kernel, token_tiled),
        grid=(m // tm,),
        in_specs=[_token_spec(tm, lambda i: i, token_tiled), _layer_spec(li, 1, D_MODEL),
                  _layer_spec(e, D_MODEL, 2 * GM_W, single_buffer=True),
                  _layer_spec(e, 1, GM_W)],
        out_specs=[row(GM_W), row(GM_W)],
        out_shape=[jax.ShapeDtypeStruct((m, GM_W), BF), jax.ShapeDtypeStruct((m, GM_W), v_dtype)],
        compiler_params=_cparams("parallel"),
        name="odd_in_proj",
    )(x, g, w, gv)


def _gmlp_rows(csize, x, u_ref, v_ref, wsp_ref, bsp_ref, wo_ref, gated_scr):
    tm = u_ref.shape[0]
    i = lax.broadcasted_iota(jnp.int32, (CHUNK, CHUNK), 0)
    j = lax.broadcasted_iota(jnp.int32, (CHUNK, CHUNK), 1)
    keep = (j <= i) & ((i // csize) == (j // csize))
    bsp = bsp_ref[...]
    for h in range(GM_HEADS):
        ws = jnp.where(keep, wsp_ref[h], 0.0).astype(BF)
        b_col = bsp[:, h:h + 1]
        cols = slice(h * GM_HD, (h + 1) * GM_HD)
        for c in range(tm // CHUNK):
            rows = slice(c * CHUNK, (c + 1) * CHUNK)
            mix = _dot(ws, v_ref[rows, cols].astype(BF)) + b_col
            gated_scr[rows, cols] = (u_ref[rows, cols].astype(F32) * mix).astype(BF)
    return x + _dot(gated_scr[...], wo_ref[...])


def _gmlp_mix_kernel(csize, x_ref, u_ref, v_ref, wsp_ref, bsp_ref, wo_ref, o_ref, gated_scr):
    o_ref[...] = _gmlp_rows(csize, x_ref[...], u_ref, v_ref, wsp_ref, bsp_ref, wo_ref, gated_scr)


def _gmlp_mix(x, u, v, e, wsp, bsp, wo, csize, tm):
    m = x.shape[0]
    row = lambda w_: pl.BlockSpec((tm, w_), lambda i: (i, 0))
    return pl.pallas_call(
        functools.partial(_gmlp_mix_kernel, csize),
        grid=(m // tm,),
        in_specs=[row(D_MODEL), row(GM_W), row(GM_W), _layer_spec(e, GM_HEADS, CHUNK, CHUNK),
                  _layer_spec(e, CHUNK, GM_HEADS), _layer_spec(e, GM_W, D_MODEL)],
        out_specs=row(D_MODEL),
        out_shape=jax.ShapeDtypeStruct((m, D_MODEL), F32),
        scratch_shapes=[pltpu.VMEM((tm, GM_W), BF)],
        compiler_params=_cparams("parallel"),
        name="gmlp_mix",
    )(x, u, v, wsp, bsp, wo)


def _head_norm(z, g):
    cols = []
    for h in range(X_HEADS):
        zh = z[:, h * X_HEAD_DIM:(h + 1) * X_HEAD_DIM]
        cols.append(_rms_rows(zh, g))
    return jnp.concatenate(cols, axis=1)


MEMKV_SEQ_BLOCK = 2
MEM_ROWS = MEM_LEN * X_HEADS


def _head_rows_of(h):
    return pl.ds(h, MEM_LEN, stride=X_HEADS)


def _memory_kv_kernel(mem_ref, g_ref, wk_ref, gk_ref, wv_ref, k_ref, v_ref):
    m = _rms_rows(mem_ref[...], g_ref[...]).astype(BF)
    k = _head_norm(_dot(m, wk_ref[...]), gk_ref[...])
    v = _dot(m, wv_ref[...])
    for s in range(MEMKV_SEQ_BLOCK):
        rows = slice(s * MEM_LEN, (s + 1) * MEM_LEN)
        for h in range(X_HEADS):
            cols = slice(h * X_HEAD_DIM, (h + 1) * X_HEAD_DIM)
            k_ref[s, _head_rows_of(h), :] = k[rows, cols]
            v_ref[s, _head_rows_of(h), :] = v[rows, cols]


def _memory_kv(mem, g_mem, w_k, g_k, w_v, n):
    tm = MEMKV_SEQ_BLOCK * MEM_LEN
    per_layer = lambda a, b: pl.BlockSpec((None, a, b), lambda l, i: (l, 0, 0))
    out_spec = pl.BlockSpec((None, MEMKV_SEQ_BLOCK, MEM_ROWS, X_HEAD_DIM), lambda l, i: (l, i, 0, 0))
    out_sds = jax.ShapeDtypeStruct((DEPTH, n, MEM_ROWS, X_HEAD_DIM), F32)
    return pl.pallas_call(
        _memory_kv_kernel,
        grid=(DEPTH, n // MEMKV_SEQ_BLOCK),
        in_specs=[pl.BlockSpec((tm, D_MODEL), lambda l, i: (i, 0)), per_layer(1, D_MODEL), per_layer(D_MODEL, X_W),
                  per_layer(1, X_HEAD_DIM), per_layer(D_MODEL, X_W)],
        out_specs=[out_spec, out_spec],
        out_shape=[out_sds, out_sds],
        compiler_params=_cparams("parallel", "parallel"),
        name="memory_kv",
    )(mem, g_mem, w_k, g_k, w_v)


def _xattn_rows(x, g_ref, wq_ref, gq_ref, mk_ref, mv_ref, wo_ref):
    q = _head_norm(_dot(_rms_rows(x, g_ref[...]).astype(BF), wq_ref[...]), gq_ref[...]).astype(BF)
    outs = []
    for h in range(X_HEADS):
        cols = slice(h * X_HEAD_DIM, (h + 1) * X_HEAD_DIM)
        s = _dot_nt(q[:, cols], mk_ref[_head_rows_of(h), :].astype(BF)) * (X_HEAD_DIM ** -0.5)
        m = jnp.max(s, axis=-1, keepdims=True)
        p = jnp.exp(s - m)
        p = (p / jnp.sum(p, axis=-1, keepdims=True)).astype(BF)
        outs.append(_dot(p, mv_ref[_head_rows_of(h), :].astype(BF)))
    o = jnp.concatenate(outs, axis=1).astype(BF)
    return x + _dot(o, wo_ref[...])


N_XATTN_REFS = 6
N_ROUTE_REFS = 4
GROUP_ROW = 0


def _xattn_route_store(x, refs, o_ref, grp_ref):
    y = _xattn_rows(x, *refs[:N_XATTN_REFS])
    _write_token_tiles(o_ref, y)
    gf_ref, whi_ref, wlo_ref, b_ref = refs[N_XATTN_REFS:]
    grp = _top_group(_router_logits(_rms_rows(y, gf_ref[...]), whi_ref[...], wlo_ref[...], b_ref[...]))
    grp_ref[...] = jnp.concatenate([grp, jnp.zeros((7, grp.shape[1]), jnp.int32)], axis=0)


def _even_out_xattn_kernel(token_tiled, x_ref, a_ref, s_ref, wa_ref, ws_ref, *rest):
    refs, o_ref, grp_ref = rest[:-2], rest[-2], rest[-1]
    x = (_read_tokens(x_ref, token_tiled) + _dot(a_ref[...], wa_ref[...])
         + _dot(s_ref[...].astype(BF), ws_ref[...]))
    _xattn_route_store(x, refs, o_ref, grp_ref)


def _gmlp_xattn_kernel(csize, token_tiled, x_ref, u_ref, v_ref, wsp_ref, bsp_ref, wo_ref, *rest):
    refs, o_ref, grp_ref, gated_scr = rest[:-3], rest[-3], rest[-2], rest[-1]
    x = _gmlp_rows(csize, _read_tokens(x_ref, token_tiled), u_ref, v_ref, wsp_ref, bsp_ref, wo_ref, gated_scr)
    _xattn_route_store(x, refs, o_ref, grp_ref)


def _xattn_route_specs(li):
    mem = pl.BlockSpec((None, None, MEM_ROWS, X_HEAD_DIM), lambda i, j: (li, i, 0, 0))
    return [_layer_spec(li, 1, D_MODEL), _layer_spec(li, D_MODEL, X_W), _layer_spec(li, 1, X_HEAD_DIM), mem, mem,
            _layer_spec(li, X_W, D_MODEL), _layer_spec(li, 1, D_MODEL), _layer_spec(li, ROUTER_ROWS, D_MODEL),
            _layer_spec(li, ROUTER_ROWS, D_MODEL), _layer_spec(li, ROUTER_ROWS, 1)]


def _token_tile_outputs(n, seq, tq):
    nq = seq // tq
    specs = [_token_spec(tq, lambda i, j: i * nq + j, True), pl.BlockSpec((None, 8, tq), lambda i, j: (i * nq + j, 0, 0))]
    shapes = [jax.ShapeDtypeStruct((n * seq * TT_CHUNKS, LANES), F32), jax.ShapeDtypeStruct((n * nq, 8, tq), jnp.int32)]
    return specs, shapes


def _even_out_xattn(x, token_tiled, att, ssm, e, wa, ws, li, xattn_route_args, n, seq, tq):
    nq = seq // tq
    row = lambda w_: pl.BlockSpec((tq, w_), lambda i, j: (i * nq + j, 0))
    ssm_spec = pl.BlockSpec((tq, SSM_W), lambda i, j: (j, i))
    out_specs, out_shape = _token_tile_outputs(n, seq, tq)
    y, grp = pl.pallas_call(
        functools.partial(_even_out_xattn_kernel, token_tiled),
        grid=(n, nq),
        in_specs=[_token_spec(tq, lambda i, j: i * nq + j, token_tiled), row(A_W), ssm_spec,
                  _layer_spec(e, A_W, D_MODEL), _layer_spec(e, SSM_W, D_MODEL)] + _xattn_route_specs(li),
        out_specs=out_specs,
        out_shape=out_shape,
        compiler_params=_cparams("parallel", "arbitrary"),
        name="even_out_xattn",
    )(x, att, ssm, wa, ws, *xattn_route_args)
    return y, grp[:, GROUP_ROW, :].reshape(n * seq)


def _gmlp_xattn(x, token_tiled, u, v, e, wsp, bsp, wo, csize, li, xattn_route_args, n, seq, tq):
    nq = seq // tq
    row = lambda w_: pl.BlockSpec((tq, w_), lambda i, j: (i * nq + j, 0))
    out_specs, out_shape = _token_tile_outputs(n, seq, tq)
    y, grp = pl.pallas_call(
        functools.partial(_gmlp_xattn_kernel, csize, token_tiled),
        grid=(n, nq),
        in_specs=[_token_spec(tq, lambda i, j: i * nq + j, token_tiled), row(GM_W), row(GM_W),
                  _layer_spec(e, GM_HEADS, CHUNK, CHUNK), _layer_spec(e, CHUNK, GM_HEADS),
                  _layer_spec(e, GM_W, D_MODEL)] + _xattn_route_specs(li),
        out_specs=out_specs,
        out_shape=out_shape,
        scratch_shapes=[pltpu.VMEM((tq, GM_W), BF)],
        compiler_params=_cparams("parallel", "arbitrary"),
        name="gmlp_xattn",
    )(x, u, v, wsp, bsp, wo, *xattn_route_args)
    return y, grp[:, GROUP_ROW, :].reshape(n * seq)


XATTN_SEQ_BLOCK = 8


def _xattn_sample_kernel(t_new, x_ref, g_ref, wq_ref, gq_ref, mk_ref, mv_ref, wo_ref, o_ref):
    pair_rows = 2 * t_new
    x = x_ref[...]
    q = _head_norm(_dot(_rms_rows(x, g_ref[...]).astype(BF), wq_ref[...]), gq_ref[...]).astype(BF)
    tq = lax.broadcasted_iota(jnp.int32, (pair_rows, 2 * MEM_LEN), 0)
    cm = lax.broadcasted_iota(jnp.int32, (pair_rows, 2 * MEM_LEN), 1)
    same_seq = (tq // t_new) == (cm // MEM_LEN)
    npair = XATTN_SEQ_BLOCK // 2
    scores = []
    for j in range(npair):
        qp = q[j * pair_rows:(j + 1) * pair_rows]
        for h in range(X_HEADS):
            hr = _head_rows_of(h)
            kp = jnp.concatenate([mk_ref[2 * j, hr, :], mk_ref[2 * j + 1, hr, :]], axis=0).astype(BF)
            s = _dot_nt(qp[:, h * X_HEAD_DIM:(h + 1) * X_HEAD_DIM], kp)
            scores.append(jnp.where(same_seq, s, NEG_INF))
    s = jnp.concatenate(scores, axis=0) * (X_HEAD_DIM ** -0.5)
    m = jnp.max(s, axis=-1, keepdims=True)
    p = jnp.exp(s - m)
    p = p / jnp.sum(p, axis=-1, keepdims=True)
    outs = []
    for j in range(npair):
        heads = []
        for h in range(X_HEADS):
            hr = _head_rows_of(h)
            vp = jnp.concatenate([mv_ref[2 * j, hr, :], mv_ref[2 * j + 1, hr, :]], axis=0).astype(BF)
            r0 = (j * X_HEADS + h) * pair_rows
            heads.append(_dot(p[r0:r0 + pair_rows].astype(BF), vp))
        outs.append(jnp.concatenate(heads, axis=1))
    o_all = jnp.concatenate(outs, axis=0).astype(BF)
    o_ref[...] = x + _dot(o_all, wo_ref[...])


def _xattn_sample(x, li, g, wq, gq, mk, mv, wo, n, t_new):
    rows = XATTN_SEQ_BLOCK * t_new
    row = pl.BlockSpec((rows, D_MODEL), lambda i: (i, 0))
    mem = pl.BlockSpec((None, XATTN_SEQ_BLOCK, MEM_ROWS, X_HEAD_DIM), lambda i: (li, i, 0, 0))
    return pl.pallas_call(
        functools.partial(_xattn_sample_kernel, t_new),
        grid=(n // XATTN_SEQ_BLOCK,),
        in_specs=[row, _layer_spec(li, 1, D_MODEL), _layer_spec(li, D_MODEL, X_W), _layer_spec(li, 1, X_HEAD_DIM),
                  mem, mem, _layer_spec(li, X_W, D_MODEL)],
        out_specs=row,
        out_shape=jax.ShapeDtypeStruct((n * t_new, D_MODEL), F32),
        compiler_params=_cparams("parallel"),
        name="xattn_sample",
    )(x, g, wq, gq, mk, mv, wo)


ROUTER_ROWS = 32
ROUTER_COARSE_ROW = N_EXPERTS
BIG = 3.0e38


def _first_argmax(vals, vmax):
    idx = jnp.full(vmax.shape, len(vals) - 1, jnp.int32)
    for j in range(len(vals) - 2, -1, -1):
        idx = jnp.where(vals[j] == vmax, j, idx)
    return idx


def _router_logits(h, whi, wlo, bias):
    h_hi = h.astype(BF)
    h_lo = (h - h_hi.astype(F32)).astype(BF)
    return _dot_nt(whi, h_hi) + _dot_nt(whi, h_lo) + _dot_nt(wlo, h_hi) + bias


def _coarse_rows(lg):
    return [lg[ROUTER_COARSE_ROW + g:ROUTER_COARSE_ROW + g + 1, :] for g in range(N_GROUPS)]


def _top_group(lg):
    coarse = _coarse_rows(lg)
    return _first_argmax(coarse, functools.reduce(jnp.maximum, coarse))


def _in_group_gates(lg, grp):
    pick = lambda rows: functools.reduce(lambda f, g: jnp.where(grp == g, rows[g], f),
                                         range(N_GROUPS - 2, -1, -1), rows[N_GROUPS - 1])
    coarse = _coarse_rows(lg)
    csel = pick(coarse)
    p_grp = 1.0 / functools.reduce(lambda a, b: a + b, [jnp.exp(c - csel) for c in coarse])
    fine = [pick([lg[g * EXPERTS_PER_GROUP + j:g * EXPERTS_PER_GROUP + j + 1, :] for g in range(N_GROUPS)])
            for j in range(EXPERTS_PER_GROUP)]
    v1 = functools.reduce(jnp.maximum, fine)
    i1 = _first_argmax(fine, v1)
    rest = [jnp.where(i1 == j, -BIG, fine[j]) for j in range(EXPERTS_PER_GROUP)]
    v2 = functools.reduce(jnp.maximum, rest)
    i2 = _first_argmax(rest, v2)
    e2 = jnp.exp(v2 - v1)
    w1 = p_grp / (1.0 + e2)
    w2 = p_grp * e2 / (1.0 + e2)
    return [jnp.where(i1 == j, w1, 0.0) + jnp.where(i2 == j, w2, 0.0) for j in range(EXPERTS_PER_GROUP)]


def _rows_to_lanes(rows, tm):
    pad = jnp.zeros((LANES - len(rows), tm), F32)
    return jnp.concatenate(list(rows) + [pad], axis=0).T


def _router_gates(h, whi, wlo, bias):
    lg = _router_logits(h, whi, wlo, bias)
    grp = _top_group(lg)
    in_group = _in_group_gates(lg, grp)
    rows = [jnp.where(grp == e // EXPERTS_PER_GROUP, in_group[e % EXPERTS_PER_GROUP], 0.0) for e in range(N_EXPERTS)]
    return _rows_to_lanes(rows, lg.shape[1])


def _moe_kernel(x_ref, g_ref, whi_ref, wlo_ref, b_ref, w1_ref, w3_ref, w2_ref, o_ref, h_scr, gates_scr, acc_scr):
    gi = pl.program_id(1)

    @pl.when(gi == 0)
    def _():
        h = _rms_rows(x_ref[...], g_ref[...])
        h_scr[...] = h.astype(BF)
        gates_scr[...] = _router_gates(h, whi_ref[...], wlo_ref[...], b_ref[...])
        acc_scr[...] = jnp.zeros_like(acc_scr)

    h = h_scr[...]
    gates = gates_scr[...]
    lane = lax.broadcasted_iota(jnp.int32, gates.shape, 1)
    for j in range(EXPERTS_PER_GROUP):
        gcol = jnp.sum(jnp.where(lane == gi * EXPERTS_PER_GROUP + j, gates, 0.0), axis=-1, keepdims=True)
        a = _dot(h, w1_ref[j])
        b = _dot(h, w3_ref[j])
        hid = (a * jax.nn.sigmoid(a)) * b * gcol
        acc_scr[...] += _dot(hid.astype(BF), w2_ref[j])

    @pl.when(gi == pl.num_programs(1) - 1)
    def _():
        o_ref[...] = x_ref[...] + acc_scr[...]


def _moe(x, li, g, whi, wlo, b, w1, w3, w2, tm):
    m = x.shape[0]
    row = lambda w_: pl.BlockSpec((tm, w_), lambda i, e: (i, 0))
    group_w = lambda a, b_: pl.BlockSpec((None, EXPERTS_PER_GROUP, a, b_), lambda i, e: (li, e, 0, 0))
    return pl.pallas_call(
        _moe_kernel,
        grid=(m // tm, N_GROUPS),
        in_specs=[row(D_MODEL), _layer_spec(li, 1, D_MODEL), _layer_spec(li, ROUTER_ROWS, D_MODEL),
                  _layer_spec(li, ROUTER_ROWS, D_MODEL), _layer_spec(li, ROUTER_ROWS, 1),
                  group_w(D_MODEL, D_EXPERT), group_w(D_MODEL, D_EXPERT), group_w(D_EXPERT, D_MODEL)],
        out_specs=row(D_MODEL),
        out_shape=jax.ShapeDtypeStruct((m, D_MODEL), F32),
        scratch_shapes=[pltpu.VMEM((tm, D_MODEL), BF), pltpu.VMEM((tm, LANES), F32), pltpu.VMEM((tm, D_MODEL), F32)],
        compiler_params=_cparams("parallel", "arbitrary"),
        name="moe",
    )(x, g, whi, wlo, b, w1, w3, w2)


MOE_TILE = 256
MOE_SPAN = 1536


def _sorted_plan(grp, m):
    t = MOE_TILE
    n_tiles = m // t + N_GROUPS
    order = jnp.argsort(grp, stable=True).astype(jnp.int32)
    counts = jnp.sum((grp[None, :] == jnp.arange(N_GROUPS, dtype=jnp.int32)[:, None]).astype(jnp.int32), axis=1)
    tiles_per_group = (counts + t - 1) // t
    tile_end = jnp.cumsum(tiles_per_group)
    tile_start = tile_end - tiles_per_group
    first_token = jnp.cumsum(counts) - counts
    s = jnp.arange(n_tiles, dtype=jnp.int32)
    tile_grp = jnp.minimum(jnp.sum((s[:, None] >= tile_end[None, :]).astype(jnp.int32), axis=1), N_GROUPS - 1)
    n_active = tile_end[N_GROUPS - 1]
    tile_in_group = s - tile_start[tile_grp]
    n_valid = jnp.where(s < n_active, jnp.clip(counts[tile_grp] - tile_in_group * t, 0, t), 0)
    r = jnp.arange(t, dtype=jnp.int32)
    slot_token = order[jnp.clip(first_token[tile_grp][:, None] + tile_in_group[:, None] * t + r[None, :], 0, m - 1)]
    valid = r[None, :] < n_valid[:, None]
    lo = jnp.where(n_valid > 0, slot_token[:, 0], 0)
    hi = jnp.where(n_valid > 0, jnp.take_along_axis(slot_token, jnp.maximum(n_valid - 1, 0)[:, None], axis=1)[:, 0], 0)
    base = jnp.clip(jnp.minimum(lo, m - MOE_SPAN), 0, None)
    fits = (hi - base < MOE_SPAN).astype(jnp.int32)
    src = jnp.where(valid, slot_token, lo[:, None])
    rel = jnp.where(fits[:, None] == 1, src - base[:, None], 0)
    dst = jnp.where(valid, slot_token, m + (s % 2)[:, None] * t + r[None, :]).reshape(-1)
    dst = jnp.concatenate([m + t + r, dst])
    zero = jnp.zeros((1,), jnp.int32)
    rows = lambda a: (a * TT_CHUNKS).astype(jnp.int32)
    return (rows(src.reshape(-1)), rows(rel.reshape(-1)), rows(jnp.concatenate([base, zero])),
            jnp.concatenate([fits, zero]).astype(jnp.int32), rows(dst), tile_grp.astype(jnp.int32))


def _sorted_moe_kernel(m, src_ref, rel_ref, base_ref, fits_ref, dst_ref, tgrp_ref, x_hbm, g_ref, whi_ref, wlo_ref,
                       b_ref, w1_ref, w3_ref, w2_ref, out_hbm, span0, span1, xbuf, obuf0, obuf1, sem_f, sem_g, sem_s):
    t = MOE_TILE
    i = pl.program_id(0)
    last = pl.num_programs(0) - 1
    span = (span0, span1)
    obuf = (obuf0, obuf1)
    c8 = TT_CHUNKS

    def span_fetch(tile, slot):
        start = pl.multiple_of(base_ref[tile], c8)
        return pltpu.make_async_copy(x_hbm.at[pl.ds(start, MOE_SPAN * c8), :], span[slot], sem_f.at[slot])

    def gather_row(tile, r):
        src = pl.multiple_of(src_ref[tile * t + r], c8)
        return pltpu.make_async_copy(x_hbm.at[pl.ds(src, c8), :], xbuf.at[pl.ds(pl.multiple_of(r * c8, c8), c8), :],
                                     sem_g)

    def scatter_row(tile, slot, r):
        dst = pl.multiple_of(dst_ref[(tile + 1) * t + r], c8)
        return pltpu.make_async_copy(obuf[slot].at[pl.ds(r * c8, c8), :], out_hbm.at[pl.ds(dst, c8), :],
                                     sem_s.at[slot])

    def whole_gather():
        return pltpu.make_async_copy(x_hbm.at[pl.ds(0, t * c8), :], xbuf, sem_g)

    def whole_scatter(slot, row0):
        return pltpu.make_async_copy(obuf[slot], out_hbm.at[pl.ds(row0, t * c8), :], sem_s.at[slot])

    @pl.when(i == 0)
    def _():
        for slot in range(2):
            obuf[slot][...] = jnp.zeros((t * c8, LANES), F32)
        for slot in range(2):
            whole_scatter(slot, (m + slot * t) * c8).start()
        for slot in range(2):
            whole_scatter(slot, (m + slot * t) * c8).wait()

        @pl.when(fits_ref[0] == 1)
        def _():
            span_fetch(0, 0).start()

    def step(cur, nxt):
        @pl.when(fits_ref[i + 1] == 1)
        def _():
            span_fetch(i + 1, nxt).start()

        for r in range(t):
            scatter_row(i - 1, nxt, r).start()

        @pl.when(fits_ref[i] == 1)
        def _():
            span_fetch(i, cur).wait()

            def compact(r, carry):
                row = pl.multiple_of(r * c8, c8)
                xbuf[pl.ds(row, c8), :] = span[cur][pl.ds(pl.multiple_of(rel_ref[i * t + r], c8), c8), :]
                return carry
            lax.fori_loop(0, t, compact, 0, unroll=8)

        @pl.when(fits_ref[i] == 0)
        def _():
            def issue(r, carry):
                gather_row(i, r).start()
                return carry
            lax.fori_loop(0, t, issue, 0)
            whole_gather().wait()

        x = _read_tokens(xbuf, True)
        h = _rms_rows(x, g_ref[...])
        lg = _router_logits(h, whi_ref[...], wlo_ref[...], b_ref[...])
        gates = _rows_to_lanes(_in_group_gates(lg, tgrp_ref[i]), t)
        hb = h.astype(BF)
        acc = jnp.zeros((t, D_MODEL), F32)
        for j in range(EXPERTS_PER_GROUP):
            a = _dot(hb, w1_ref[j])
            b = _dot(hb, w3_ref[j])
            hid = (a * jax.nn.sigmoid(a)) * b * gates[:, j:j + 1]
            acc = acc + _dot(hid.astype(BF), w2_ref[j])
        res = x + acc

        @pl.when(i >= 1)
        def _():
            whole_scatter(cur, 0).wait()

        _write_token_tiles(obuf[cur], res)

        @pl.when(i == last)
        def _():
            def issue(r, carry):
                scatter_row(i, cur, r).start()
                return carry
            lax.fori_loop(0, t, issue, 0)
            whole_scatter(cur, 0).wait()
            whole_scatter(nxt, 0).wait()

    for parity in range(2):
        @pl.when(i % 2 == parity)
        def _():
            step(parity, 1 - parity)


def _sorted_moe(x, grp, m, li, g, whi, wlo, b, w1, w3, w2):
    t = MOE_TILE
    assert m >= MOE_SPAN
    n_tiles = m // t + N_GROUPS
    plan = _sorted_plan(grp, m)
    group_w = lambda a, b_: pl.BlockSpec((None, EXPERTS_PER_GROUP, a, b_),
                                         lambda i, *prefetch: (li, prefetch[-1][i], 0, 0))
    tile_buf = pltpu.VMEM((t * TT_CHUNKS, LANES), F32)
    span_buf = pltpu.VMEM((MOE_SPAN * TT_CHUNKS, LANES), F32)
    return pl.pallas_call(
        functools.partial(_sorted_moe_kernel, m),
        grid_spec=pltpu.PrefetchScalarGridSpec(
            num_scalar_prefetch=len(plan),
            grid=(n_tiles,),
            in_specs=[pl.BlockSpec(memory_space=pl.ANY), _layer_spec(li, 1, D_MODEL),
                      _layer_spec(li, ROUTER_ROWS, D_MODEL), _layer_spec(li, ROUTER_ROWS, D_MODEL),
                      _layer_spec(li, ROUTER_ROWS, 1),
                      group_w(D_MODEL, D_EXPERT), group_w(D_MODEL, D_EXPERT), group_w(D_EXPERT, D_MODEL)],
            out_specs=pl.BlockSpec(memory_space=pl.ANY),
            scratch_shapes=[span_buf, span_buf, tile_buf, tile_buf, tile_buf, pltpu.SemaphoreType.DMA((2,)),
                            pltpu.SemaphoreType.DMA, pltpu.SemaphoreType.DMA((2,))]),
        out_shape=jax.ShapeDtypeStruct(((m + 2 * t) * TT_CHUNKS, LANES), F32),
        compiler_params=_cparams("arbitrary"),
        name="moe_sorted",
    )(*plan, x, g, whi, wlo, b, w1, w3, w2)


def _heads_kv_major_to_g_major(w, axis):
    shape = w.shape
    split = shape[:axis] + (KV_HEADS, Q_PER_KV, HEAD_DIM) + shape[axis + 1:]
    return jnp.swapaxes(w.reshape(split), axis, axis + 1).reshape(shape)


def _prep_ssm(a_re, a_im, log_dt, b_re, b_im, c_re, c_im, d_skip, w_glu):
    ar, ai = a_re.astype(F32), a_im.astype(F32)
    dt = jnp.exp(log_dt.astype(F32))[..., None]
    mag = jnp.exp(ar * dt)
    abr, abi = mag * jnp.cos(ai * dt), mag * jnp.sin(ai * dt)
    den = ar * ar + ai * ai
    kr = ((abr - 1.0) * ar + abi * ai) / den
    ki = (abi * ar - (abr - 1.0) * ai) / den
    br, bi = b_re.astype(F32), b_im.astype(F32)
    bbr = kr[..., None] * br - ki[..., None] * bi
    bbi = kr[..., None] * bi + ki[..., None] * br
    n_layers = ar.shape[0]
    eye = jnp.eye(SSM_GROUPS, dtype=F32)
    blockdiag_in = lambda b: jnp.einsum('egph,gk->eghkp', b, eye).reshape(n_layers, SSM_W, SSM_FLAT)
    blockdiag_out = lambda c: jnp.einsum('eghp,gk->egpkh', c, eye).reshape(n_layers, SSM_FLAT, SSM_W)
    in_tiles = lambda b: jnp.stack([b[:, j * SSM_IN_TILE:(j + 1) * SSM_IN_TILE, j * SSM_ST_TILE:(j + 1) * SSM_ST_TILE]
                                    for j in range(SSM_IN_TILES)], axis=1)
    out_tiles = lambda c: jnp.stack([c[:, j * SSM_OUT_ST:(j + 1) * SSM_OUT_ST, j * SSM_OUT_TILE:(j + 1) * SSM_OUT_TILE]
                                     for j in range(SSM_OUT_TILES)], axis=1)
    bcat = jnp.concatenate([in_tiles(blockdiag_in(bbr)), in_tiles(blockdiag_in(bbi))], axis=3).astype(BF)
    ccat = jnp.concatenate([out_tiles(blockdiag_out(c_re.astype(F32))), -out_tiles(blockdiag_out(c_im.astype(F32)))],
                           axis=2).astype(BF)
    return (abr.reshape(n_layers, 1, SSM_FLAT), abi.reshape(n_layers, 1, SSM_FLAT), bcat, ccat,
            d_skip.astype(F32).reshape(n_layers, 1, SSM_W), w_glu.astype(BF))


def _run_trunk(x, n, seq, mem_k, mem_v, caches, p):
    prompt = caches is None
    tm = 512
    m = n * seq
    wk_out, wv_out, hr_out, hi_out, cv_out = [], [], [], [], []
    xattn_args = (p['g_xattn'], p['w_xq'], p['g_xq'], mem_k, mem_v, p['w_xo'])
    router_args = (p['g_ffn'], p['w_router_hi'], p['w_router_lo'], p['b_router'])
    expert_args = (p['w_e1'], p['w_e3'], p['w_e2'])
    tiled = False
    for li in range(DEPTH):
        e = li // 2
        if li % 2 == 0:
            q, k, v, u = _even_in_proj(x, li, e, p['g_mix'], p['w_in_even'], p['g_q'], p['g_k'], p['bd'], tm, n, seq,
                                       token_tiled=tiled)
            if prompt:
                att = _swa_prompt(q, k, v, p['sinks'], e, n, seq)
                k3, v3 = k.reshape(n, seq, KV_W), v.reshape(n, seq, KV_W)
                new_k, new_v = k3[:, seq - WINDOW:], v3[:, seq - WINDOW:]
                ssm, h_re, h_im = _s5_glu(u.reshape(seq, n, SSM_W), None, p['ssm'], e, n, seq, 128)
                x, grp = _even_out_xattn(x, tiled, att, ssm.reshape(seq, n * SSM_W), e, p['w_out_att'], p['w_out_ssm'],
                                         li, xattn_args + router_args, n, seq, tm)
            else:
                win_k, win_v, ssm_re, ssm_im = caches
                att, new_k, new_v = _swa_sample(q, k, v, win_k, win_v, p['sinks'], e, n, seq)
                u_tm = jnp.swapaxes(u.reshape(n, seq, SSM_W), 0, 1)
                ssm, h_re, h_im = _s5_glu(u_tm, (ssm_re, ssm_im), p['ssm'], e, n, seq, seq)
                ssm = jnp.swapaxes(ssm, 0, 1).reshape(m, SSM_W)
                x = _even_out_proj(x, att, ssm, e, p['w_out_att'], p['w_out_ssm'], tm, n, seq)
            wk_out.append(new_k.reshape(n, WINDOW, KV_HEADS, HEAD_DIM))
            wv_out.append(new_v.reshape(n, WINDOW, KV_HEADS, HEAD_DIM))
            hr_out.append(h_re.reshape(n, SSM_GROUPS, SSM_STATE))
            hi_out.append(h_im.reshape(n, SSM_GROUPS, SSM_STATE))
        else:
            u, v = _odd_in_proj(x, m, li, e, p['g_mix'], p['w_in_odd'], p['g_v'], tm, BF if prompt else F32,
                                token_tiled=tiled)
            wsp, bsp, csize = p['gmlp_prompt'] if prompt else p['gmlp_sample']
            if prompt:
                x, grp = _gmlp_xattn(x, tiled, u, v, e, wsp, bsp, p['w_out_odd'], csize, li, xattn_args + router_args,
                                     n, seq, tm)
            else:
                x = _gmlp_mix(x, u, v, e, wsp, bsp, p['w_out_odd'], csize, tm)
                cv_out.append(v.reshape(n, seq, GM_W))
        if prompt:
            x = _sorted_moe(x, grp, m, li, *router_args, *expert_args)
            tiled = True
        else:
            x = _xattn_sample(x, li, *xattn_args, n, seq)
            x = _moe(x, li, *router_args, *expert_args, tm)
    if tiled:
        x = _untile(x, m, tm)
    cv = None if prompt else jnp.stack(cv_out)
    return x, jnp.stack(wk_out), jnp.stack(wv_out), jnp.stack(hr_out), jnp.stack(hi_out), cv


def kernel(x_prompt, x_sample, cache_win_k, cache_win_v, state_ssm_re, state_ssm_im, cache_mem_k, cache_mem_v, mem_prompt, g_mix, g_xattn, g_ffn, g_mem, w_in_even, g_q, g_k, sinks, ssm_a_re, ssm_a_im, ssm_log_dt, ssm_b_re, ssm_b_im, ssm_c_re, ssm_c_im, ssm_d, w_glu, w_out_even, w_in_odd, g_v, w_spatial, b_spatial, w_out_odd, w_xq, g_xq, w_xk, g_xk, w_xv, w_xo, w_coarse, b_coarse, w_fine, b_fine, w_e1, w_e3, w_e2):
    batch, seq = x_prompt.shape[0], x_prompt.shape[1]
    dec_batch, dec_seq = x_sample.shape[0], x_sample.shape[1]
    n_even = w_in_even.shape[0]

    row1 = lambda a: a.astype(F32)[:, None, :]
    p = {}
    p['g_mix'], p['g_xattn'], p['g_ffn'] = row1(g_mix), row1(g_xattn), row1(g_ffn)
    w_in_q = _heads_kv_major_to_g_major(w_in_even[:, :, :A_W], 2)
    p['w_in_even'] = jnp.concatenate([w_in_q, w_in_even[:, :, A_W:]], axis=2).astype(BF)
    p['g_q'] = jnp.tile(g_q.astype(F32), (1, A_HEADS))[:, None, :]
    p['g_k'] = jnp.tile(g_k.astype(F32), (1, KV_HEADS))[:, None, :]
    head_id = jnp.arange(A_W) // HEAD_DIM
    p['bd'] = (head_id[:, None] == head_id[None, :]).astype(BF)
    p['sinks'] = sinks.astype(F32).reshape(-1)
    p['ssm'] = _prep_ssm(ssm_a_re, ssm_a_im, ssm_log_dt, ssm_b_re, ssm_b_im, ssm_c_re, ssm_c_im, ssm_d, w_glu)
    p['w_out_att'] = _heads_kv_major_to_g_major(w_out_even[:, :A_W, :], 1).astype(BF)
    p['w_out_ssm'] = w_out_even[:, A_W:, :].astype(BF)
    p['w_in_odd'] = w_in_odd.astype(BF)
    p['g_v'] = row1(g_v)
    p['w_out_odd'] = w_out_odd.astype(BF)
    cs = min(dec_seq, CHUNK)
    reps = CHUNK // cs
    p['gmlp_prompt'] = (w_spatial.astype(F32), jnp.swapaxes(b_spatial.astype(F32), 1, 2), CHUNK)
    p['gmlp_sample'] = (jnp.tile(w_spatial[:, :, :cs, :cs].astype(F32), (1, 1, reps, reps)),
                        jnp.tile(jnp.swapaxes(b_spatial[:, :, :cs].astype(F32), 1, 2), (1, reps, 1)), cs)
    p['w_xq'] = w_xq.astype(BF)
    p['g_xq'] = row1(g_xq)
    p['w_xo'] = w_xo.astype(BF)
    w_router = jnp.swapaxes(jnp.concatenate([w_fine, w_coarse], axis=2).astype(F32), 1, 2)
    w_router = jnp.pad(w_router, ((0, 0), (0, ROUTER_ROWS - w_router.shape[1]), (0, 0)))
    p['w_router_hi'] = w_router.astype(BF)
    p['w_router_lo'] = (w_router - p['w_router_hi'].astype(F32)).astype(BF)
    b_router = jnp.concatenate([b_fine, b_coarse], axis=1).astype(F32)
    p['b_router'] = jnp.pad(b_router, ((0, 0), (0, ROUTER_ROWS - b_router.shape[1])))[:, :, None]
    p['w_e1'], p['w_e3'], p['w_e2'] = w_e1.astype(BF), w_e3.astype(BF), w_e2.astype(BF)

    mem2d = mem_prompt.reshape(batch * MEM_LEN, D_MODEL)
    pk, pv = _memory_kv(mem2d, row1(g_mem), w_xk.astype(BF), row1(g_xk), w_xv.astype(BF), batch)
    y_p, p_wk, p_wv, p_hr, p_hi, _ = _run_trunk(x_prompt.reshape(batch * seq, D_MODEL), batch, seq, pk, pv, None, p)

    caches = (cache_win_k.reshape(n_even, dec_batch, WINDOW, KV_W), cache_win_v.reshape(n_even, dec_batch, WINDOW, KV_W),
              state_ssm_re.reshape(n_even, dec_batch, SSM_FLAT), state_ssm_im.reshape(n_even, dec_batch, SSM_FLAT))
    y_s, s_wk, s_wv, s_hr, s_hi, s_cv = _run_trunk(x_sample.reshape(dec_batch * dec_seq, D_MODEL), dec_batch, dec_seq,
                                                   cache_mem_k.reshape(DEPTH, dec_batch, MEM_ROWS, X_HEAD_DIM),
                                                   cache_mem_v.reshape(DEPTH, dec_batch, MEM_ROWS, X_HEAD_DIM), caches, p)

    mem_shape = (DEPTH, batch, MEM_LEN, X_HEADS, X_HEAD_DIM)
    return (y_p.reshape(batch, seq, D_MODEL), y_s.reshape(dec_batch, dec_seq, D_MODEL), p_wk, p_wv, p_hr, p_hi,
            pk.reshape(mem_shape), pv.reshape(mem_shape), s_wk, s_wv, s_hr, s_hi, s_cv)
```

```python
import functools
import math

import jax
import jax.numpy as jnp
from jax import lax
from jax.experimental import pallas as pl
from jax.experimental.pallas import tpu as pltpu

F32 = jnp.float32
BF = jnp.bfloat16

D_MODEL = 1024
DEPTH = 4
A_W = 512
HEAD_DIM = 64
A_HEADS = 8
KV_HEADS = 2
Q_PER_KV = 4
KV_W = 128
WINDOW = 128
SSM_W = 512
SSM_CH = 16
SSM_GROUPS = 32
SSM_STATE = 64
SSM_FLAT = SSM_GROUPS * SSM_STATE
IN_EVEN = A_W + 2 * KV_W + SSM_W
CHUNK = 128
GM_W = 2048
GM_HEADS = 8
GM_HD = 256
MEM_LEN = 256
X_HEADS = 4
X_HEAD_DIM = 128
X_W = 512
N_GROUPS = 4
EXPERTS_PER_GROUP = 4
N_EXPERTS = 16
D_EXPERT = 256
EPS = 1e-6
NEG_INF = -1e30
LANES = 128
VMEM_LIMIT_BYTES = 52 * 1024 * 1024
ALIBI_SLOPES = tuple(2.0 ** (-8.0 * (h + 1) / A_HEADS) for h in range(A_HEADS))
SSM_SEQ_BLOCK = 8


def _cparams(*sem):
    return pltpu.CompilerParams(dimension_semantics=sem, vmem_limit_bytes=VMEM_LIMIT_BYTES)


def _rms_rows(x, g):
    return x * lax.rsqrt(jnp.mean(x * x, axis=-1, keepdims=True) + EPS) * g


TT_CHUNKS = D_MODEL // LANES


def _read_tokens(ref, token_tiled):
    if not token_tiled:
        return ref[...]
    rows = ref.shape[0] // TT_CHUNKS
    return jnp.concatenate([ref[pl.ds(c, rows, stride=TT_CHUNKS), :] for c in range(TT_CHUNKS)], axis=1)


def _write_token_tiles(ref, val):
    rows = val.shape[0]
    for c in range(TT_CHUNKS):
        ref[pl.ds(c, rows, stride=TT_CHUNKS), :] = val[:, c * LANES:(c + 1) * LANES]


def _token_spec(rows, index, token_tiled):
    if token_tiled:
        return pl.BlockSpec((rows * TT_CHUNKS, LANES), lambda *ids: (index(*ids), 0))
    return pl.BlockSpec((rows, D_MODEL), lambda *ids: (index(*ids), 0))


def _untile_kernel(x_ref, o_ref):
    o_ref[...] = _read_tokens(x_ref, True)


def _untile(x, m, tm):
    return pl.pallas_call(
        _untile_kernel,
        grid=(m // tm,),
        in_specs=[_token_spec(tm, lambda i: i, True)],
        out_specs=pl.BlockSpec((tm, D_MODEL), lambda i: (i, 0)),
        out_shape=jax.ShapeDtypeStruct((m, D_MODEL), F32),
        compiler_params=_cparams("parallel"),
        name="untile",
    )(x)


def _dot(a, b):
    return jnp.dot(a, b, preferred_element_type=F32)


def _dot_nt(a, b):
    return lax.dot_general(a, b, (((1,), (1,)), ((), ())), preferred_element_type=F32)


def _full(shape):
    nd = len(shape)
    return pl.BlockSpec(shape, lambda *_: (0,) * nd)


def _layer_spec(li, *shape, single_buffer=False):
    nd = len(shape)
    mode = pl.Buffered(1) if single_buffer else None
    return pl.BlockSpec((None,) + shape, lambda *_: (li,) + (0,) * nd, pipeline_mode=mode)


def _even_in_kernel(token_tiled, x_ref, g_ref, w_ref, gq_ref, gk_ref, bd_ref, q_ref, k_ref, v_ref, u_ref):
    h = _rms_rows(_read_tokens(x_ref, token_tiled), g_ref[...]).astype(BF)
    z = _dot(h, w_ref[...])
    q = z[:, :A_W]
    k = z[:, A_W:A_W + KV_W]
    bd = bd_ref[...]
    qms = _dot((q * q).astype(BF), bd) * (1.0 / HEAD_DIM)
    kms = _dot((k * k).astype(BF), bd[:KV_W, :KV_W]) * (1.0 / HEAD_DIM)
    q_ref[...] = q * lax.rsqrt(qms + EPS) * gq_ref[...]
    k_ref[...] = k * lax.rsqrt(kms + EPS) * gk_ref[...]
    v_ref[...] = z[:, A_W + KV_W:A_W + 2 * KV_W]
    u_ref[...] = z[:, A_W + 2 * KV_W:]


def _time_major_spec(tm, n, seq, width):
    per_seq = seq // tm
    return pl.BlockSpec((tm, width), lambda i: (i % per_seq, i // per_seq))


def _even_in_proj(x, li, e, g, w, gq, gk, bd, tm, n, seq, token_tiled=False):
    m = n * seq
    row = lambda w_: pl.BlockSpec((tm, w_), lambda i: (i, 0))
    return pl.pallas_call(
        functools.partial(_even_in_kernel, token_tiled),
        grid=(m // tm,),
        in_specs=[_token_spec(tm, lambda i: i, token_tiled), _layer_spec(li, 1, D_MODEL),
                  _layer_spec(e, D_MODEL, IN_EVEN), _layer_spec(e, 1, A_W),
                  _layer_spec(e, 1, KV_W), _full((A_W, A_W))],
        out_specs=[row(A_W), row(KV_W), row(KV_W), _time_major_spec(tm, n, seq, SSM_W) if seq % tm == 0 else row(SSM_W)],
        out_shape=[jax.ShapeDtypeStruct((m, A_W), F32), jax.ShapeDtypeStruct((m, KV_W), F32),
                   jax.ShapeDtypeStruct((m, KV_W), F32),
                   jax.ShapeDtypeStruct((seq, n * SSM_W) if seq % tm == 0 else (m, SSM_W), F32)],
        compiler_params=_cparams("parallel"),
        name="even_in_proj",
    )(x, g, w, gq, gk, bd)


def _head_rows(q, lane_lo):
    rows = []
    for kv in range(KV_HEADS):
        keep = lane_lo if kv == 0 else jnp.logical_not(lane_lo)
        for g in range(Q_PER_KV):
            rows.append(jnp.where(keep, q[:, g * LANES:(g + 1) * LANES], 0.0))
    return jnp.concatenate(rows, axis=0)


def _head_cols(o, r, lane_lo):
    cols = []
    for g in range(Q_PER_KV):
        cols.append(jnp.where(lane_lo, o[g * r:(g + 1) * r], o[(Q_PER_KV + g) * r:(Q_PER_KV + g + 1) * r]))
    return jnp.concatenate(cols, axis=1)


def _swa_prompt_kernel(e, sinks_ref, q_ref, kc_ref, kp_ref, vc_ref, vp_ref, o_ref):
    b = pl.program_id(1)
    lane_lo = lax.broadcasted_iota(jnp.int32, (1, LANES), 1) < HEAD_DIM
    qrows = _head_rows(q_ref[...], lane_lo).astype(BF)
    kcat = jnp.concatenate([kp_ref[...], kc_ref[...]], axis=0).astype(BF)
    vcat = jnp.concatenate([vp_ref[...], vc_ref[...]], axis=0).astype(BF)
    s = _dot_nt(qrows, kcat) * (HEAD_DIM ** -0.5)
    r = lax.broadcasted_iota(jnp.int32, (WINDOW, 2 * WINDOW), 0)
    c = lax.broadcasted_iota(jnp.int32, (WINDOW, 2 * WINDOW), 1)
    dist = WINDOW + r - c
    valid = (dist >= 0) & (dist <= WINDOW) & ((c >= WINDOW) | (b > 0))
    dist_f = dist.astype(F32)
    ps = []
    for h in range(A_HEADS):
        sh = s[h * WINDOW:(h + 1) * WINDOW] - ALIBI_SLOPES[h] * dist_f
        sh = jnp.where(valid, sh, NEG_INF)
        sk = sinks_ref[e * A_HEADS + h]
        m = jnp.maximum(jnp.max(sh, axis=-1, keepdims=True), sk)
        p = jnp.exp(sh - m)
        den = jnp.sum(p, axis=-1, keepdims=True) + jnp.exp(sk - m)
        ps.append((p / den).astype(BF))
    o = _dot(jnp.concatenate(ps, axis=0), vcat)
    o_ref[...] = _head_cols(o, WINDOW, lane_lo).astype(o_ref.dtype)


def _swa_prompt(q, k, v, sinks, e, n, seq):
    nb = seq // WINDOW
    cur = lambda w_: pl.BlockSpec((WINDOW, w_), lambda i, b: (i * nb + b, 0))
    prev = lambda w_: pl.BlockSpec((WINDOW, w_), lambda i, b: (i * nb + jnp.maximum(b - 1, 0), 0))
    return pl.pallas_call(
        functools.partial(_swa_prompt_kernel, e),
        grid=(n, nb),
        in_specs=[pl.BlockSpec(memory_space=pltpu.SMEM), cur(A_W), cur(KV_W), prev(KV_W), cur(KV_W), prev(KV_W)],
        out_specs=cur(A_W),
        out_shape=jax.ShapeDtypeStruct((n * seq, A_W), BF),
        compiler_params=_cparams("parallel", "arbitrary"),
        name="swa_prompt",
    )(sinks, q, k, k, v, v)


SWA_SEQ_BLOCK = 8


def _swa_sample_kernel(t_new, e, sinks_ref, q_ref, kn_ref, vn_ref, wk_ref, wv_ref, o_ref, nwk_ref, nwv_ref):
    nseq = SWA_SEQ_BLOCK
    pair_rows = 2 * t_new
    npair = nseq // 2
    kn = kn_ref[...]
    vn = vn_ref[...]
    for i in range(nseq):
        nwk_ref[i, 0:WINDOW - t_new, :] = wk_ref[i, t_new:WINDOW, :]
        nwk_ref[i, WINDOW - t_new:WINDOW, :] = kn[i * t_new:(i + 1) * t_new, :]
        nwv_ref[i, 0:WINDOW - t_new, :] = wv_ref[i, t_new:WINDOW, :]
        nwv_ref[i, WINDOW - t_new:WINDOW, :] = vn[i * t_new:(i + 1) * t_new, :]

    lane_lo = lax.broadcasted_iota(jnp.int32, (1, LANES), 1) < HEAD_DIM
    q = q_ref[...]
    knb = kn.astype(BF)
    vnb = vn.astype(BF)
    hr = A_HEADS * pair_rows
    qrows = [_head_rows(q[j * pair_rows:(j + 1) * pair_rows], lane_lo).astype(BF) for j in range(npair)]
    scale = HEAD_DIM ** -0.5
    s_new_all = _dot_nt(jnp.concatenate(qrows, axis=0), knb) * scale

    tq = lax.broadcasted_iota(jnp.int32, (pair_rows, 2 * WINDOW), 0)
    cw = lax.broadcasted_iota(jnp.int32, (pair_rows, 2 * WINDOW), 1)
    dist_w = WINDOW + (tq % t_new) - (cw % WINDOW)
    valid_w = ((tq // t_new) == (cw // WINDOW)) & (dist_w <= WINDOW)
    dist_wf = dist_w.astype(F32)
    tqn = lax.broadcasted_iota(jnp.int32, (pair_rows, nseq * t_new), 0)
    cn = lax.broadcasted_iota(jnp.int32, (pair_rows, nseq * t_new), 1)
    dist_n = (tqn % t_new) - (cn % t_new)
    dist_nf = dist_n.astype(F32)

    p_new_all = []
    o_win_all = []
    for j in range(npair):
        kwin = jnp.concatenate([wk_ref[2 * j], wk_ref[2 * j + 1]], axis=0).astype(BF)
        vwin = jnp.concatenate([wv_ref[2 * j], wv_ref[2 * j + 1]], axis=0).astype(BF)
        s_win = _dot_nt(qrows[j], kwin) * scale
        valid_n = ((2 * j + tqn // t_new) == (cn // t_new)) & (dist_n >= 0)
        p_win = []
        for h in range(A_HEADS):
            sw = jnp.where(valid_w, s_win[h * pair_rows:(h + 1) * pair_rows] - ALIBI_SLOPES[h] * dist_wf, NEG_INF)
            sn = s_new_all[j * hr + h * pair_rows:j * hr + (h + 1) * pair_rows]
            sn = jnp.where(valid_n, sn - ALIBI_SLOPES[h] * dist_nf, NEG_INF)
            sk = sinks_ref[e * A_HEADS + h]
            m = jnp.maximum(jnp.maximum(jnp.max(sw, axis=-1, keepdims=True), jnp.max(sn, axis=-1, keepdims=True)), sk)
            pw = jnp.exp(sw - m)
            pn = jnp.exp(sn - m)
            den = jnp.sum(pw, axis=-1, keepdims=True) + jnp.sum(pn, axis=-1, keepdims=True) + jnp.exp(sk - m)
            p_win.append((pw / den).astype(BF))
            p_new_all.append((pn / den).astype(BF))
        o_win_all.append(_dot(jnp.concatenate(p_win, axis=0), vwin))
    o_new = _dot(jnp.concatenate(p_new_all, axis=0), vnb)
    outs = [_head_cols(o_win_all[j] + o_new[j * hr:(j + 1) * hr], pair_rows, lane_lo) for j in range(npair)]
    o_ref[...] = jnp.concatenate(outs, axis=0).astype(o_ref.dtype)


def _swa_sample(q, kn, vn, win_k, win_v, sinks, e, n, t_new):
    rows = SWA_SEQ_BLOCK * t_new
    row = lambda w_: pl.BlockSpec((rows, w_), lambda i: (i, 0))
    win = pl.BlockSpec((SWA_SEQ_BLOCK, WINDOW, KV_W), lambda i: (i, 0, 0))
    win_in = pl.BlockSpec((None, SWA_SEQ_BLOCK, WINDOW, KV_W), lambda i: (e, i, 0, 0))
    return pl.pallas_call(
        functools.partial(_swa_sample_kernel, t_new, e),
        grid=(n // SWA_SEQ_BLOCK,),
        in_specs=[pl.BlockSpec(memory_space=pltpu.SMEM), row(A_W), row(KV_W), row(KV_W), win_in, win_in],
        out_specs=[row(A_W), win, win],
        out_shape=[jax.ShapeDtypeStruct((n * t_new, A_W), BF),
                   jax.ShapeDtypeStruct((n, WINDOW, KV_W), F32), jax.ShapeDtypeStruct((n, WINDOW, KV_W), F32)],
        compiler_params=_cparams("parallel"),
        name="swa_sample",
    )(sinks, q, kn, vn, win_k, win_v)


SSM_LANE_CHUNK = 1024
SSM_IN_TILE = LANES
SSM_IN_TILES = SSM_W // SSM_IN_TILE
SSM_ST_TILE = SSM_IN_TILE // SSM_CH * SSM_STATE
SSM_OUT_TILE = 256
SSM_OUT_TILES = SSM_W // SSM_OUT_TILE
SSM_OUT_ST = SSM_OUT_TILE // SSM_CH * SSM_STATE


def _s5_kernel(tc, has_h0, *refs):
    if has_h0:
        (u_ref, h0r_ref, h0i_ref, abr_ref, abi_ref, bcat_ref, ccat_ref, d_ref, wglu_ref,
         o_ref, hr_ref, hi_ref, st_scr, car_scr) = refs
    else:
        (u_ref, abr_ref, abi_ref, bcat_ref, ccat_ref, d_ref, wglu_ref,
         o_ref, hr_ref, hi_ref, st_scr, car_scr) = refs
    nb = SSM_SEQ_BLOCK
    ci = pl.program_id(1)

    @pl.when(ci == 0)
    def _():
        if has_h0:
            car_scr[:, :SSM_FLAT] = h0r_ref[...]
            car_scr[:, SSM_FLAT:] = h0i_ref[...]
        else:
            car_scr[...] = jnp.zeros_like(car_scr)

    ut = u_ref[...].reshape(tc * nb, SSM_W)
    ub = ut.astype(BF)
    for jt in range(SSM_IN_TILES):
        bu = _dot(ub[:, jt * SSM_IN_TILE:(jt + 1) * SSM_IN_TILE], bcat_ref[jt])
        st_scr[:, jt * SSM_ST_TILE:(jt + 1) * SSM_ST_TILE] = bu[:, :SSM_ST_TILE]
        st_scr[:, SSM_FLAT + jt * SSM_ST_TILE:SSM_FLAT + (jt + 1) * SSM_ST_TILE] = bu[:, SSM_ST_TILE:]

    for lc in range(SSM_FLAT // SSM_LANE_CHUNK):
        lo = lc * SSM_LANE_CHUNK
        re_sl = slice(lo, lo + SSM_LANE_CHUNK)
        im_sl = slice(SSM_FLAT + lo, SSM_FLAT + lo + SSM_LANE_CHUNK)
        ar = jnp.broadcast_to(abr_ref[:, re_sl], (nb, SSM_LANE_CHUNK))
        ai = jnp.broadcast_to(abi_ref[:, re_sl], (nb, SSM_LANE_CHUNK))

        def step(t, carry):
            hr, hi = carry
            rows = pl.ds(pl.multiple_of(t * nb, nb), nb)
            nr = ar * hr - ai * hi + st_scr[rows, re_sl]
            ni = ar * hi + ai * hr + st_scr[rows, im_sl]
            st_scr[rows, re_sl] = nr
            st_scr[rows, im_sl] = ni
            return nr, ni

        hr, hi = lax.fori_loop(0, tc, step, (car_scr[:, re_sl], car_scr[:, im_sl]))
        car_scr[:, re_sl] = hr
        car_scr[:, im_sl] = hi

    ys = []
    for ot in range(SSM_OUT_TILES):
        re_sl = slice(ot * SSM_OUT_ST, (ot + 1) * SSM_OUT_ST)
        im_sl = slice(SSM_FLAT + ot * SSM_OUT_ST, SSM_FLAT + (ot + 1) * SSM_OUT_ST)
        ys.append(_dot(st_scr[:, re_sl].astype(BF), ccat_ref[ot, :SSM_OUT_ST, :])
                  + _dot(st_scr[:, im_sl].astype(BF), ccat_ref[ot, SSM_OUT_ST:, :]))
    y = jnp.concatenate(ys, axis=1) + d_ref[...] * ut
    y = jax.nn.gelu(y).astype(BF)
    g = _dot(y, wglu_ref[...])
    o_ref[...] = (g[:, :SSM_W] * jax.nn.sigmoid(g[:, SSM_W:])).reshape(tc, nb, SSM_W)

    @pl.when(ci == pl.num_programs(1) - 1)
    def _():
        hr_ref[...] = car_scr[:, :SSM_FLAT]
        hi_ref[...] = car_scr[:, SSM_FLAT:]


def _s5_glu(u, h0, ssm, e, n, seq, tc):
    nb = SSM_SEQ_BLOCK
    abr, abi, bcat, ccat, dsk, wglu = ssm
    u_spec = pl.BlockSpec((tc, nb, SSM_W), lambda i, c: (c, i, 0))
    st_spec = pl.BlockSpec((nb, SSM_FLAT), lambda i, c: (i, 0))
    h0_spec = pl.BlockSpec((None, nb, SSM_FLAT), lambda i, c: (e, i, 0))
    consts = [_layer_spec(e, 1, SSM_FLAT), _layer_spec(e, 1, SSM_FLAT),
              _layer_spec(e, SSM_IN_TILES, SSM_IN_TILE, 2 * SSM_ST_TILE),
              _layer_spec(e, SSM_OUT_TILES, 2 * SSM_OUT_ST, SSM_OUT_TILE), _layer_spec(e, 1, SSM_W),
              _layer_spec(e, SSM_W, 2 * SSM_W)]
    has_h0 = h0 is not None
    in_specs = [u_spec] + ([h0_spec, h0_spec] if has_h0 else []) + consts
    args = [u] + (list(h0) if has_h0 else []) + [abr, abi, bcat, ccat, dsk, wglu]
    return pl.pallas_call(
        functools.partial(_s5_kernel, tc, has_h0),
        grid=(n // nb, seq // tc),
        in_specs=in_specs,
        out_specs=[u_spec, st_spec, st_spec],
        out_shape=[jax.ShapeDtypeStruct((seq, n, SSM_W), F32), jax.ShapeDtypeStruct((n, SSM_FLAT), F32),
                   jax.ShapeDtypeStruct((n, SSM_FLAT), F32)],
        scratch_shapes=[pltpu.VMEM((nb * tc, 2 * SSM_FLAT), F32), pltpu.VMEM((nb, 2 * SSM_FLAT), F32)],
        compiler_params=_cparams("parallel", "arbitrary"),
        name="s5_glu",
    )(*args)


def _even_out_kernel(x_ref, a_ref, s_ref, wa_ref, ws_ref, o_ref):
    o_ref[...] = x_ref[...] + _dot(a_ref[...], wa_ref[...]) + _dot(s_ref[...].astype(BF), ws_ref[...])


def _even_out_proj(x, att, ssm, e, wa, ws, tm, n, seq):
    m = x.shape[0]
    row = lambda w_: pl.BlockSpec((tm, w_), lambda i: (i, 0))
    ssm_spec = _time_major_spec(tm, n, seq, SSM_W) if seq % tm == 0 else row(SSM_W)
    return pl.pallas_call(
        _even_out_kernel,
        grid=(m // tm,),
        in_specs=[row(D_MODEL), row(A_W), ssm_spec, _layer_spec(e, A_W, D_MODEL), _layer_spec(e, SSM_W, D_MODEL)],
        out_specs=row(D_MODEL),
        out_shape=jax.ShapeDtypeStruct((m, D_MODEL), F32),
        compiler_params=_cparams("parallel"),
        name="even_out_proj",
    )(x, att, ssm, wa, ws)


def _odd_in_kernel(token_tiled, x_ref, g_ref, w_ref, gv_ref, u_ref, v_ref):
    h = _rms_rows(_read_tokens(x_ref, token_tiled), g_ref[...]).astype(BF)
    u_ref[...] = jax.nn.gelu(_dot(h, w_ref[:, :GM_W])).astype(u_ref.dtype)
    zv = jax.nn.gelu(_dot(h, w_ref[:, GM_W:]))
    v_ref[...] = _rms_rows(zv, gv_ref[...]).astype(v_ref.dtype)


def _odd_in_proj(x, m, li, e, g, w, gv, tm, v_dtype, token_tiled=False):
    row = lambda w_: pl.BlockSpec((tm, w_), lambda i: (i, 0))
    return pl.pallas_call(
        functools.partial(_odd_in_kernel, token_tiled),
        grid=(m // tm,),
        in_specs=[_token_spec(tm, lambda i: i, token_tiled), _layer_spec(li, 1, D_MODEL),
                  _layer_spec(e, D_MODEL, 2 * GM_W, single_buffer=True),
                  _layer_spec(e, 1, GM_W)],
        out_specs=[row(GM_W), row(GM_W)],
        out_shape=[jax.ShapeDtypeStruct((m, GM_W), BF), jax.ShapeDtypeStruct((m, GM_W), v_dtype)],
        compiler_params=_cparams("parallel"),
        name="odd_in_proj",
    )(x, g, w, gv)


def _gmlp_rows(csize, x, u_ref, v_ref, wsp_ref, bsp_ref, wo_ref, gated_scr):
    tm = u_ref.shape[0]
    i = lax.broadcasted_iota(jnp.int32, (CHUNK, CHUNK), 0)
    j = lax.broadcasted_iota(jnp.int32, (CHUNK, CHUNK), 1)
    keep = (j <= i) & ((i // csize) == (j // csize))
    bsp = bsp_ref[...]
    for h in range(GM_HEADS):
        ws = jnp.where(keep, wsp_ref[h], 0.0).astype(BF)
        b_col = bsp[:, h:h + 1]
        cols = slice(h * GM_HD, (h + 1) * GM_HD)
        for c in range(tm // CHUNK):
            rows = slice(c * CHUNK, (c + 1) * CHUNK)
            mix = _dot(ws, v_ref[rows, cols].astype(BF)) + b_col
            gated_scr[rows, cols] = (u_ref[rows, cols].astype(F32) * mix).astype(BF)
    return x + _dot(gated_scr[...], wo_ref[...])


def _gmlp_mix_kernel(csize, x_ref, u_ref, v_ref, wsp_ref, bsp_ref, wo_ref, o_ref, gated_scr):
    o_ref[...] = _gmlp_rows(csize, x_ref[...], u_ref, v_ref, wsp_ref, bsp_ref, wo_ref, gated_scr)


def _gmlp_mix(x, u, v, e, wsp, bsp, wo, csize, tm):
    m = x.shape[0]
    row = lambda w_: pl.BlockSpec((tm, w_), lambda i: (i, 0))
    return pl.pallas_call(
        functools.partial(_gmlp_mix_kernel, csize),
        grid=(m // tm,),
        in_specs=[row(D_MODEL), row(GM_W), row(GM_W), _layer_spec(e, GM_HEADS, CHUNK, CHUNK),
                  _layer_spec(e, CHUNK, GM_HEADS), _layer_spec(e, GM_W, D_MODEL)],
        out_specs=row(D_MODEL),
        out_shape=jax.ShapeDtypeStruct((m, D_MODEL), F32),
        scratch_shapes=[pltpu.VMEM((tm, GM_W), BF)],
        compiler_params=_cparams("parallel"),
        name="gmlp_mix",
    )(x, u, v, wsp, bsp, wo)


def _head_norm(z, g):
    cols = []
    for h in range(X_HEADS):
        zh = z[:, h * X_HEAD_DIM:(h + 1) * X_HEAD_DIM]
        cols.append(_rms_rows(zh, g))
    return jnp.concatenate(cols, axis=1)


MEMKV_SEQ_BLOCK = 2
MEM_ROWS = MEM_LEN * X_HEADS


def _head_rows_of(h):
    return pl.ds(h, MEM_LEN, stride=X_HEADS)


def _memory_kv_kernel(mem_ref, g_ref, wk_ref, gk_ref, wv_ref, k_ref, v_ref):
    m = _rms_rows(mem_ref[...], g_ref[...]).astype(BF)
    k = _head_norm(_dot(m, wk_ref[...]), gk_ref[...])
    v = _dot(m, wv_ref[...])
    for s in range(MEMKV_SEQ_BLOCK):
        rows = slice(s * MEM_LEN, (s + 1) * MEM_LEN)
        for h in range(X_HEADS):
            cols = slice(h * X_HEAD_DIM, (h + 1) * X_HEAD_DIM)
            k_ref[s, _head_rows_of(h), :] = k[rows, cols]
            v_ref[s, _head_rows_of(h), :] = v[rows, cols]


def _memory_kv(mem, g_mem, w_k, g_k, w_v, n):
    tm = MEMKV_SEQ_BLOCK * MEM_LEN
    per_layer = lambda a, b: pl.BlockSpec((None, a, b), lambda l, i: (l, 0, 0))
    out_spec = pl.BlockSpec((None, MEMKV_SEQ_BLOCK, MEM_ROWS, X_HEAD_DIM), lambda l, i: (l, i, 0, 0))
    out_sds = jax.ShapeDtypeStruct((DEPTH, n, MEM_ROWS, X_HEAD_DIM), F32)
    return pl.pallas_call(
        _memory_kv_kernel,
        grid=(DEPTH, n // MEMKV_SEQ_BLOCK),
        in_specs=[pl.BlockSpec((tm, D_MODEL), lambda l, i: (i, 0)), per_layer(1, D_MODEL), per_layer(D_MODEL, X_W),
                  per_layer(1, X_HEAD_DIM), per_layer(D_MODEL, X_W)],
        out_specs=[out_spec, out_spec],
        out_shape=[out_sds, out_sds],
        compiler_params=_cparams("parallel", "parallel"),
        name="memory_kv",
    )(mem, g_mem, w_k, g_k, w_v)


def _xattn_rows(x, g_ref, wq_ref, gq_ref, mk_ref, mv_ref, wo_ref):
    q = _head_norm(_dot(_rms_rows(x, g_ref[...]).astype(BF), wq_ref[...]), gq_ref[...]).astype(BF)
    outs = []
    for h in range(X_HEADS):
        cols = slice(h * X_HEAD_DIM, (h + 1) * X_HEAD_DIM)
        s = _dot_nt(q[:, cols], mk_ref[_head_rows_of(h), :].astype(BF)) * (X_HEAD_DIM ** -0.5)
        m = jnp.max(s, axis=-1, keepdims=True)
        p = jnp.exp(s - m)
        p = (p / jnp.sum(p, axis=-1, keepdims=True)).astype(BF)
        outs.append(_dot(p, mv_ref[_head_rows_of(h), :].astype(BF)))
    o = jnp.concatenate(outs, axis=1).astype(BF)
    return x + _dot(o, wo_ref[...])


N_XATTN_REFS = 6
N_ROUTE_REFS = 4
GROUP_ROW = 0


def _xattn_route_store(x, refs, o_ref, grp_ref):
    y = _xattn_rows(x, *refs[:N_XATTN_REFS])
    _write_token_tiles(o_ref, y)
    gf_ref, whi_ref, wlo_ref, b_ref = refs[N_XATTN_REFS:]
    grp = _top_group(_router_logits(_rms_rows(y, gf_ref[...]), whi_ref[...], wlo_ref[...], b_ref[...]))
    grp_ref[...] = jnp.concatenate([grp, jnp.zeros((7, grp.shape[1]), jnp.int32)], axis=0)


def _even_out_xattn_kernel(token_tiled, x_ref, a_ref, s_ref, wa_ref, ws_ref, *rest):
    refs, o_ref, grp_ref = rest[:-2], rest[-2], rest[-1]
    x = (_read_tokens(x_ref, token_tiled) + _dot(a_ref[...], wa_ref[...])
         + _dot(s_ref[...].astype(BF), ws_ref[...]))
    _xattn_route_store(x, refs, o_ref, grp_ref)


def _gmlp_xattn_kernel(csize, token_tiled, x_ref, u_ref, v_ref, wsp_ref, bsp_ref, wo_ref, *rest):
    refs, o_ref, grp_ref, gated_scr = rest[:-3], rest[-3], rest[-2], rest[-1]
    x = _gmlp_rows(csize, _read_tokens(x_ref, token_tiled), u_ref, v_ref, wsp_ref, bsp_ref, wo_ref, gated_scr)
    _xattn_route_store(x, refs, o_ref, grp_ref)


def _xattn_route_specs(li):
    mem = pl.BlockSpec((None, None, MEM_ROWS, X_HEAD_DIM), lambda i, j: (li, i, 0, 0))
    return [_layer_spec(li, 1, D_MODEL), _layer_spec(li, D_MODEL, X_W), _layer_spec(li, 1, X_HEAD_DIM), mem, mem,
            _layer_spec(li, X_W, D_MODEL), _layer_spec(li, 1, D_MODEL), _layer_spec(li, ROUTER_ROWS, D_MODEL),
            _layer_spec(li, ROUTER_ROWS, D_MODEL), _layer_spec(li, ROUTER_ROWS, 1)]


def _token_tile_outputs(n, seq, tq):
    nq = seq // tq
    specs = [_token_spec(tq, lambda i, j: i * nq + j, True), pl.BlockSpec((None, 8, tq), lambda i, j: (i * nq + j, 0, 0))]
    shapes = [jax.ShapeDtypeStruct((n * seq * TT_CHUNKS, LANES), F32), jax.ShapeDtypeStruct((n * nq, 8, tq), jnp.int32)]
    return specs, shapes


def _even_out_xattn(x, token_tiled, att, ssm, e, wa, ws, li, xattn_route_args, n, seq, tq):
    nq = seq // tq
    row = lambda w_: pl.BlockSpec((tq, w_), lambda i, j: (i * nq + j, 0))
    ssm_spec = pl.BlockSpec((tq, SSM_W), lambda i, j: (j, i))
    out_specs, out_shape = _token_tile_outputs(n, seq, tq)
    y, grp = pl.pallas_call(
        functools.partial(_even_out_xattn_kernel, token_tiled),
        grid=(n, nq),
        in_specs=[_token_spec(tq, lambda i, j: i * nq + j, token_tiled), row(A_W), ssm_spec,
                  _layer_spec(e, A_W, D_MODEL), _layer_spec(e, SSM_W, D_MODEL)] + _xattn_route_specs(li),
        out_specs=out_specs,
        out_shape=out_shape,
        compiler_params=_cparams("parallel", "arbitrary"),
        name="even_out_xattn",
    )(x, att, ssm, wa, ws, *xattn_route_args)
    return y, grp[:, GROUP_ROW, :].reshape(n * seq)


def _gmlp_xattn(x, token_tiled, u, v, e, wsp, bsp, wo, csize, li, xattn_route_args, n, seq, tq):
    nq = seq // tq
    row = lambda w_: pl.BlockSpec((tq, w_), lambda i, j: (i * nq + j, 0))
    out_specs, out_shape = _token_tile_outputs(n, seq, tq)
    y, grp = pl.pallas_call(
        functools.partial(_gmlp_xattn_kernel, csize, token_tiled),
        grid=(n, nq),
        in_specs=[_token_spec(tq, lambda i, j: i * nq + j, token_tiled), row(GM_W), row(GM_W),
                  _layer_spec(e, GM_HEADS, CHUNK, CHUNK), _layer_spec(e, CHUNK, GM_HEADS),
                  _layer_spec(e, GM_W, D_MODEL)] + _xattn_route_specs(li),
        out_specs=out_specs,
        out_shape=out_shape,
        scratch_shapes=[pltpu.VMEM((tq, GM_W), BF)],
        compiler_params=_cparams("parallel", "arbitrary"),
        name="gmlp_xattn",
    )(x, u, v, wsp, bsp, wo, *xattn_route_args)
    return y, grp[:, GROUP_ROW, :].reshape(n * seq)


XATTN_SEQ_BLOCK = 8


def _xattn_sample_kernel(t_new, x_ref, g_ref, wq_ref, gq_ref, mk_ref, mv_ref, wo_ref, o_ref):
    pair_rows = 2 * t_new
    x = x_ref[...]
    q = _head_norm(_dot(_rms_rows(x, g_ref[...]).astype(BF), wq_ref[...]), gq_ref[...]).astype(BF)
    tq = lax.broadcasted_iota(jnp.int32, (pair_rows, 2 * MEM_LEN), 0)
    cm = lax.broadcasted_iota(jnp.int32, (pair_rows, 2 * MEM_LEN), 1)
    same_seq = (tq // t_new) == (cm // MEM_LEN)
    npair = XATTN_SEQ_BLOCK // 2
    scores = []
    for j in range(npair):
        qp = q[j * pair_rows:(j + 1) * pair_rows]
        for h in range(X_HEADS):
            hr = _head_rows_of(h)
            kp = jnp.concatenate([mk_ref[2 * j, hr, :], mk_ref[2 * j + 1, hr, :]], axis=0).astype(BF)
            s = _dot_nt(qp[:, h * X_HEAD_DIM:(h + 1) * X_HEAD_DIM], kp)
            scores.append(jnp.where(same_seq, s, NEG_INF))
    s = jnp.concatenate(scores, axis=0) * (X_HEAD_DIM ** -0.5)
    m = jnp.max(s, axis=-1, keepdims=True)
    p = jnp.exp(s - m)
    p = p / jnp.sum(p, axis=-1, keepdims=True)
    outs = []
    for j in range(npair):
        heads = []
        for h in range(X_HEADS):
            hr = _head_rows_of(h)
            vp = jnp.concatenate([mv_ref[2 * j, hr, :], mv_ref[2 * j + 1, hr, :]], axis=0).astype(BF)
            r0 = (j * X_HEADS + h) * pair_rows
            heads.append(_dot(p[r0:r0 + pair_rows].astype(BF), vp))
        outs.append(jnp.concatenate(heads, axis=1))
    o_all = jnp.concatenate(outs, axis=0).astype(BF)
    o_ref[...] = x + _dot(o_all, wo_ref[...])


def _xattn_sample(x, li, g, wq, gq, mk, mv, wo, n, t_new):
    rows = XATTN_SEQ_BLOCK * t_new
    row = pl.BlockSpec((rows, D_MODEL), lambda i: (i, 0))
    mem = pl.BlockSpec((None, XATTN_SEQ_BLOCK, MEM_ROWS, X_HEAD_DIM), lambda i: (li, i, 0, 0))
    return pl.pallas_call(
        functools.partial(_xattn_sample_kernel, t_new),
        grid=(n // XATTN_SEQ_BLOCK,),
        in_specs=[row, _layer_spec(li, 1, D_MODEL), _layer_spec(li, D_MODEL, X_W), _layer_spec(li, 1, X_HEAD_DIM),
                  mem, mem, _layer_spec(li, X_W, D_MODEL)],
        out_specs=row,
        out_shape=jax.ShapeDtypeStruct((n * t_new, D_MODEL), F32),
        compiler_params=_cparams("parallel"),
        name="xattn_sample",
    )(x, g, wq, gq, mk, mv, wo)


ROUTER_ROWS = 32
ROUTER_COARSE_ROW = N_EXPERTS
BIG = 3.0e38


def _first_argmax(vals, vmax):
    idx = jnp.full(vmax.shape, len(vals) - 1, jnp.int32)
    for j in range(len(vals) - 2, -1, -1):
        idx = jnp.where(vals[j] == vmax, j, idx)
    return idx


def _router_logits(h, whi, wlo, bias):
    h_hi = h.astype(BF)
    h_lo = (h - h_hi.astype(F32)).astype(BF)
    return _dot_nt(whi, h_hi) + _dot_nt(whi, h_lo) + _dot_nt(wlo, h_hi) + bias


def _coarse_rows(lg):
    return [lg[ROUTER_COARSE_ROW + g:ROUTER_COARSE_ROW + g + 1, :] for g in range(N_GROUPS)]


def _top_group(lg):
    coarse = _coarse_rows(lg)
    return _first_argmax(coarse, functools.reduce(jnp.maximum, coarse))


def _in_group_gates(lg, grp):
    pick = lambda rows: functools.reduce(lambda f, g: jnp.where(grp == g, rows[g], f),
                                         range(N_GROUPS - 2, -1, -1), rows[N_GROUPS - 1])
    coarse = _coarse_rows(lg)
    csel = pick(coarse)
    p_grp = 1.0 / functools.reduce(lambda a, b: a + b, [jnp.exp(c - csel) for c in coarse])
    fine = [pick([lg[g * EXPERTS_PER_GROUP + j:g * EXPERTS_PER_GROUP + j + 1, :] for g in range(N_GROUPS)])
            for j in range(EXPERTS_PER_GROUP)]
    v1 = functools.reduce(jnp.maximum, fine)
    i1 = _first_argmax(fine, v1)
    rest = [jnp.where(i1 == j, -BIG, fine[j]) for j in range(EXPERTS_PER_GROUP)]
    v2 = functools.reduce(jnp.maximum, rest)
    i2 = _first_argmax(rest, v2)
    e2 = jnp.exp(v2 - v1)
    w1 = p_grp / (1.0 + e2)
    w2 = p_grp * e2 / (1.0 + e2)
    return [jnp.where(i1 == j, w1, 0.0) + jnp.where(i2 == j, w2, 0.0) for j in range(EXPERTS_PER_GROUP)]


def _rows_to_lanes(rows, tm):
    pad = jnp.zeros((LANES - len(rows), tm), F32)
    return jnp.concatenate(list(rows) + [pad], axis=0).T


def _router_gates(h, whi, wlo, bias):
    lg = _router_logits(h, whi, wlo, bias)
    grp = _top_group(lg)
    in_group = _in_group_gates(lg, grp)
    rows = [jnp.where(grp == e // EXPERTS_PER_GROUP, in_group[e % EXPERTS_PER_GROUP], 0.0) for e in range(N_EXPERTS)]
    return _rows_to_lanes(rows, lg.shape[1])


def _moe_kernel(x_ref, g_ref, whi_ref, wlo_ref, b_ref, w1_ref, w3_ref, w2_ref, o_ref, h_scr, gates_scr, acc_scr):
    gi = pl.program_id(1)

    @pl.when(gi == 0)
    def _():
        h = _rms_rows(x_ref[...], g_ref[...])
        h_scr[...] = h.astype(BF)
        gates_scr[...] = _router_gates(h, whi_ref[...], wlo_ref[...], b_ref[...])
        acc_scr[...] = jnp.zeros_like(acc_scr)

    h = h_scr[...]
    gates = gates_scr[...]
    lane = lax.broadcasted_iota(jnp.int32, gates.shape, 1)
    for j in range(EXPERTS_PER_GROUP):
        gcol = jnp.sum(jnp.where(lane == gi * EXPERTS_PER_GROUP + j, gates, 0.0), axis=-1, keepdims=True)
        a = _dot(h, w1_ref[j])
        b = _dot(h, w3_ref[j])
        hid = (a * jax.nn.sigmoid(a)) * b * gcol
        acc_scr[...] += _dot(hid.astype(BF), w2_ref[j])

    @pl.when(gi == pl.num_programs(1) - 1)
    def _():
        o_ref[...] = x_ref[...] + acc_scr[...]


def _moe(x, li, g, whi, wlo, b, w1, w3, w2, tm):
    m = x.shape[0]
    row = lambda w_: pl.BlockSpec((tm, w_), lambda i, e: (i, 0))
    group_w = lambda a, b_: pl.BlockSpec((None, EXPERTS_PER_GROUP, a, b_), lambda i, e: (li, e, 0, 0))
    return pl.pallas_call(
        _moe_kernel,
        grid=(m // tm, N_GROUPS),
        in_specs=[row(D_MODEL), _layer_spec(li, 1, D_MODEL), _layer_spec(li, ROUTER_ROWS, D_MODEL),
                  _layer_spec(li, ROUTER_ROWS, D_MODEL), _layer_spec(li, ROUTER_ROWS, 1),
                  group_w(D_MODEL, D_EXPERT), group_w(D_MODEL, D_EXPERT), group_w(D_EXPERT, D_MODEL)],
        out_specs=row(D_MODEL),
        out_shape=jax.ShapeDtypeStruct((m, D_MODEL), F32),
        scratch_shapes=[pltpu.VMEM((tm, D_MODEL), BF), pltpu.VMEM((tm, LANES), F32), pltpu.VMEM((tm, D_MODEL), F32)],
        compiler_params=_cparams("parallel", "arbitrary"),
        name="moe",
    )(x, g, whi, wlo, b, w1, w3, w2)


MOE_TILE = 256
MOE_SPAN = 1280


def _sorted_plan(grp, m):
    t = MOE_TILE
    n_tiles = m // t + N_GROUPS
    order = jnp.argsort(grp, stable=True).astype(jnp.int32)
    counts = jnp.sum((grp[None, :] == jnp.arange(N_GROUPS, dtype=jnp.int32)[:, None]).astype(jnp.int32), axis=1)
    tiles_per_group = (counts + t - 1) // t
    tile_end = jnp.cumsum(tiles_per_group)
    tile_start = tile_end - tiles_per_group
    first_token = jnp.cumsum(counts) - counts
    s = jnp.arange(n_tiles, dtype=jnp.int32)
    tile_grp = jnp.minimum(jnp.sum((s[:, None] >= tile_end[None, :]).astype(jnp.int32), axis=1), N_GROUPS - 1)
    n_active = tile_end[N_GROUPS - 1]
    tile_in_group = s - tile_start[tile_grp]
    n_valid = jnp.where(s < n_active, jnp.clip(counts[tile_grp] - tile_in_group * t, 0, t), 0)
    r = jnp.arange(t, dtype=jnp.int32)
    slot_token = order[jnp.clip(first_token[tile_grp][:, None] + tile_in_group[:, None] * t + r[None, :], 0, m - 1)]
    valid = r[None, :] < n_valid[:, None]
    lo = jnp.where(n_valid > 0, slot_token[:, 0], 0)
    hi = jnp.where(n_valid > 0, jnp.take_along_axis(slot_token, jnp.maximum(n_valid - 1, 0)[:, None], axis=1)[:, 0], 0)
    base = jnp.clip(jnp.minimum(lo, m - MOE_SPAN), 0, None)
    fits = (hi - base < MOE_SPAN).astype(jnp.int32)
    src = jnp.where(valid, slot_token, lo[:, None])
    rel = jnp.where(fits[:, None] == 1, src - base[:, None], 0)
    dst = jnp.where(valid, slot_token, m + (s % 2)[:, None] * t + r[None, :]).reshape(-1)
    dst = jnp.concatenate([m + t + r, dst])
    zero = jnp.zeros((1,), jnp.int32)
    rows = lambda a: (a * TT_CHUNKS).astype(jnp.int32)
    return (rows(src.reshape(-1)), rows(rel.reshape(-1)), rows(jnp.concatenate([base, zero])),
            jnp.concatenate([fits, zero]).astype(jnp.int32), rows(dst), tile_grp.astype(jnp.int32))


def _sorted_moe_kernel(m, src_ref, rel_ref, base_ref, fits_ref, dst_ref, tgrp_ref, x_hbm, g_ref, whi_ref, wlo_ref,
                       b_ref, w1_ref, w3_ref, w2_ref, out_hbm, span0, span1, xbuf, obuf0, obuf1, sem_f, sem_g, sem_s):
    t = MOE_TILE
    i = pl.program_id(0)
    last = pl.num_programs(0) - 1
    span = (span0, span1)
    obuf = (obuf0, obuf1)
    c8 = TT_CHUNKS

    def span_fetch(tile, slot):
        start = pl.multiple_of(base_ref[tile], c8)
        return pltpu.make_async_copy(x_hbm.at[pl.ds(start, MOE_SPAN * c8), :], span[slot], sem_f.at[slot])

    def gather_row(tile, r):
        src = pl.multiple_of(src_ref[tile * t + r], c8)
        return pltpu.make_async_copy(x_hbm.at[pl.ds(src, c8), :], xbuf.at[pl.ds(pl.multiple_of(r * c8, c8), c8), :],
                                     sem_g)

    def scatter_row(tile, slot, r):
        dst = pl.multiple_of(dst_ref[(tile + 1) * t + r], c8)
        return pltpu.make_async_copy(obuf[slot].at[pl.ds(r * c8, c8), :], out_hbm.at[pl.ds(dst, c8), :],
                                     sem_s.at[slot])

    def whole_gather():
        return pltpu.make_async_copy(x_hbm.at[pl.ds(0, t * c8), :], xbuf, sem_g)

    def whole_scatter(slot, row0):
        return pltpu.make_async_copy(obuf[slot], out_hbm.at[pl.ds(row0, t * c8), :], sem_s.at[slot])

    @pl.when(i == 0)
    def _():
        for slot in range(2):
            obuf[slot][...] = jnp.zeros((t * c8, LANES), F32)
        for slot in range(2):
            whole_scatter(slot, (m + slot * t) * c8).start()
        for slot in range(2):
            whole_scatter(slot, (m + slot * t) * c8).wait()

        @pl.when(fits_ref[0] == 1)
        def _():
            span_fetch(0, 0).start()

    def step(cur, nxt):
        @pl.when(fits_ref[i + 1] == 1)
        def _():
            span_fetch(i + 1, nxt).start()

        @pl.when(fits_ref[i] == 1)
        def _():
            span_fetch(i, cur).wait()

            def compact(r, carry):
                row = pl.multiple_of(r * c8, c8)
                xbuf[pl.ds(row, c8), :] = span[cur][pl.ds(pl.multiple_of(rel_ref[i * t + r], c8), c8), :]
                return carry
            lax.fori_loop(0, t, compact, 0, unroll=8)

        @pl.when(fits_ref[i] == 0)
        def _():
            def issue(r, carry):
                gather_row(i, r).start()
                return carry
            lax.fori_loop(0, t, issue, 0)
            whole_gather().wait()

        for r in range(t):
            scatter_row(i - 1, nxt, r).start()

        x = _read_tokens(xbuf, True)
        h = _rms_rows(x, g_ref[...])
        lg = _router_logits(h, whi_ref[...], wlo_ref[...], b_ref[...])
        gates = _rows_to_lanes(_in_group_gates(lg, tgrp_ref[i]), t)
        hb = h.astype(BF)
        acc = jnp.zeros((t, D_MODEL), F32)
        for j in range(EXPERTS_PER_GROUP):
            a = _dot(hb, w1_ref[j])
            b = _dot(hb, w3_ref[j])
            hid = (a * jax.nn.sigmoid(a)) * b * gates[:, j:j + 1]
            acc = acc + _dot(hid.astype(BF), w2_ref[j])
        res = x + acc

        @pl.when(i >= 1)
        def _():
            whole_scatter(cur, 0).wait()

        _write_token_tiles(obuf[cur], res)

        @pl.when(i == last)
        def _():
            def issue(r, carry):
                scatter_row(i, cur, r).start()
                return carry
            lax.fori_loop(0, t, issue, 0)
            whole_scatter(cur, 0).wait()
            whole_scatter(nxt, 0).wait()

    for parity in range(2):
        @pl.when(i % 2 == parity)
        def _():
            step(parity, 1 - parity)


def _sorted_moe(x, grp, m, li, g, whi, wlo, b, w1, w3, w2):
    t = MOE_TILE
    assert m >= MOE_SPAN
    n_tiles = m // t + N_GROUPS
    plan = _sorted_plan(grp, m)
    group_w = lambda a, b_: pl.BlockSpec((None, EXPERTS_PER_GROUP, a, b_),
                                         lambda i, *prefetch: (li, prefetch[-1][i], 0, 0))
    tile_buf = pltpu.VMEM((t * TT_CHUNKS, LANES), F32)
    span_buf = pltpu.VMEM((MOE_SPAN * TT_CHUNKS, LANES), F32)
    return pl.pallas_call(
        functools.partial(_sorted_moe_kernel, m),
        grid_spec=pltpu.PrefetchScalarGridSpec(
            num_scalar_prefetch=len(plan),
            grid=(n_tiles,),
            in_specs=[pl.BlockSpec(memory_space=pl.ANY), _layer_spec(li, 1, D_MODEL),
                      _layer_spec(li, ROUTER_ROWS, D_MODEL), _layer_spec(li, ROUTER_ROWS, D_MODEL),
                      _layer_spec(li, ROUTER_ROWS, 1),
                      group_w(D_MODEL, D_EXPERT), group_w(D_MODEL, D_EXPERT), group_w(D_EXPERT, D_MODEL)],
            out_specs=pl.BlockSpec(memory_space=pl.ANY),
            scratch_shapes=[span_buf, span_buf, tile_buf, tile_buf, tile_buf, pltpu.SemaphoreType.DMA((2,)),
                            pltpu.SemaphoreType.DMA, pltpu.SemaphoreType.DMA((2,))]),
        out_shape=jax.ShapeDtypeStruct(((m + 2 * t) * TT_CHUNKS, LANES), F32),
        compiler_params=_cparams("arbitrary"),
        name="moe_sorted",
    )(*plan, x, g, whi, wlo, b, w1, w3, w2)


def _heads_kv_major_to_g_major(w, axis):
    shape = w.shape
    split = shape[:axis] + (KV_HEADS, Q_PER_KV, HEAD_DIM) + shape[axis + 1:]
    return jnp.swapaxes(w.reshape(split), axis, axis + 1).reshape(shape)


def _prep_ssm(a_re, a_im, log_dt, b_re, b_im, c_re, c_im, d_skip, w_glu):
    ar, ai = a_re.astype(F32), a_im.astype(F32)
    dt = jnp.exp(log_dt.astype(F32))[..., None]
    mag = jnp.exp(ar * dt)
    abr, abi = mag * jnp.cos(ai * dt), mag * jnp.sin(ai * dt)
    den = ar * ar + ai * ai
    kr = ((abr - 1.0) * ar + abi * ai) / den
    ki = (abi * ar - (abr - 1.0) * ai) / den
    br, bi = b_re.astype(F32), b_im.astype(F32)
    bbr = kr[..., None] * br - ki[..., None] * bi
    bbi = kr[..., None] * bi + ki[..., None] * br
    n_layers = ar.shape[0]
    eye = jnp.eye(SSM_GROUPS, dtype=F32)
    blockdiag_in = lambda b: jnp.einsum('egph,gk->eghkp', b, eye).reshape(n_layers, SSM_W, SSM_FLAT)
    blockdiag_out = lambda c: jnp.einsum('eghp,gk->egpkh', c, eye).reshape(n_layers, SSM_FLAT, SSM_W)
    in_tiles = lambda b: jnp.stack([b[:, j * SSM_IN_TILE:(j + 1) * SSM_IN_TILE, j * SSM_ST_TILE:(j + 1) * SSM_ST_TILE]
                                    for j in range(SSM_IN_TILES)], axis=1)
    out_tiles = lambda c: jnp.stack([c[:, j * SSM_OUT_ST:(j + 1) * SSM_OUT_ST, j * SSM_OUT_TILE:(j + 1) * SSM_OUT_TILE]
                                     for j in range(SSM_OUT_TILES)], axis=1)
    bcat = jnp.concatenate([in_tiles(blockdiag_in(bbr)), in_tiles(blockdiag_in(bbi))], axis=3).astype(BF)
    ccat = jnp.concatenate([out_tiles(blockdiag_out(c_re.astype(F32))), -out_tiles(blockdiag_out(c_im.astype(F32)))],
                           axis=2).astype(BF)
    return (abr.reshape(n_layers, 1, SSM_FLAT), abi.reshape(n_layers, 1, SSM_FLAT), bcat, ccat,
            d_skip.astype(F32).reshape(n_layers, 1, SSM_W), w_glu.astype(BF))


def _run_trunk(x, n, seq, mem_k, mem_v, caches, p):
    prompt = caches is None
    tm = 512
    m = n * seq
    wk_out, wv_out, hr_out, hi_out, cv_out = [], [], [], [], []
    xattn_args = (p['g_xattn'], p['w_xq'], p['g_xq'], mem_k, mem_v, p['w_xo'])
    router_args = (p['g_ffn'], p['w_router_hi'], p['w_router_lo'], p['b_router'])
    expert_args = (p['w_e1'], p['w_e3'], p['w_e2'])
    tiled = False
    for li in range(DEPTH):
        e = li // 2
        if li % 2 == 0:
            q, k, v, u = _even_in_proj(x, li, e, p['g_mix'], p['w_in_even'], p['g_q'], p['g_k'], p['bd'], tm, n, seq,
                                       token_tiled=tiled)
            if prompt:
                att = _swa_prompt(q, k, v, p['sinks'], e, n, seq)
                k3, v3 = k.reshape(n, seq, KV_W), v.reshape(n, seq, KV_W)
                new_k, new_v = k3[:, seq - WINDOW:], v3[:, seq - WINDOW:]
                ssm, h_re, h_im = _s5_glu(u.reshape(seq, n, SSM_W), None, p['ssm'], e, n, seq, 128)
                x, grp = _even_out_xattn(x, tiled, att, ssm.reshape(seq, n * SSM_W), e, p['w_out_att'], p['w_out_ssm'],
                                         li, xattn_args + router_args, n, seq, tm)
            else:
                win_k, win_v, ssm_re, ssm_im = caches
                att, new_k, new_v = _swa_sample(q, k, v, win_k, win_v, p['sinks'], e, n, seq)
                u_tm = jnp.swapaxes(u.reshape(n, seq, SSM_W), 0, 1)
                ssm, h_re, h_im = _s5_glu(u_tm, (ssm_re, ssm_im), p['ssm'], e, n, seq, seq)
                ssm = jnp.swapaxes(ssm, 0, 1).reshape(m, SSM_W)
                x = _even_out_proj(x, att, ssm, e, p['w_out_att'], p['w_out_ssm'], tm, n, seq)
            wk_out.append(new_k.reshape(n, WINDOW, KV_HEADS, HEAD_DIM))
            wv_out.append(new_v.reshape(n, WINDOW, KV_HEADS, HEAD_DIM))
            hr_out.append(h_re.reshape(n, SSM_GROUPS, SSM_STATE))
            hi_out.append(h_im.reshape(n, SSM_GROUPS, SSM_STATE))
        else:
            u, v = _odd_in_proj(x, m, li, e, p['g_mix'], p['w_in_odd'], p['g_v'], tm, BF if prompt else F32,
                                token_tiled=tiled)
            wsp, bsp, csize = p['gmlp_prompt'] if prompt else p['gmlp_sample']
            if prompt:
                x, grp = _gmlp_xattn(x, tiled, u, v, e, wsp, bsp, p['w_out_odd'], csize, li, xattn_args + router_args,
                                     n, seq, tm)
            else:
                x = _gmlp_mix(x, u, v, e, wsp, bsp, p['w_out_odd'], csize, tm)
                cv_out.append(v.reshape(n, seq, GM_W))
        if prompt:
            x = _sorted_moe(x, grp, m, li, *router_args, *expert_args)
            tiled = True
        else:
            x = _xattn_sample(x, li, *xattn_args, n, seq)
            x = _moe(x, li, *router_args, *expert_args, tm)
    if tiled:
        x = _untile(x, m, tm)
    cv = None if prompt else jnp.stack(cv_out)
    return x, jnp.stack(wk_out), jnp.stack(wv_out), jnp.stack(hr_out), jnp.stack(hi_out), cv


def kernel(x_prompt, x_sample, cache_win_k, cache_win_v, state_ssm_re, state_ssm_im, cache_mem_k, cache_mem_v, mem_prompt, g_mix, g_xattn, g_ffn, g_mem, w_in_even, g_q, g_k, sinks, ssm_a_re, ssm_a_im, ssm_log_dt, ssm_b_re, ssm_b_im, ssm_c_re, ssm_c_im, ssm_d, w_glu, w_out_even, w_in_odd, g_v, w_spatial, b_spatial, w_out_odd, w_xq, g_xq, w_xk, g_xk, w_xv, w_xo, w_coarse, b_coarse, w_fine, b_fine, w_e1, w_e3, w_e2):
    batch, seq = x_prompt.shape[0], x_prompt.shape[1]
    dec_batch, dec_seq = x_sample.shape[0], x_sample.shape[1]
    n_even = w_in_even.shape[0]

    row1 = lambda a: a.astype(F32)[:, None, :]
    p = {}
    p['g_mix'], p['g_xattn'], p['g_ffn'] = row1(g_mix), row1(g_xattn), row1(g_ffn)
    w_in_q = _heads_kv_major_to_g_major(w_in_even[:, :, :A_W], 2)
    p['w_in_even'] = jnp.concatenate([w_in_q, w_in_even[:, :, A_W:]], axis=2).astype(BF)
    p['g_q'] = jnp.tile(g_q.astype(F32), (1, A_HEADS))[:, None, :]
    p['g_k'] = jnp.tile(g_k.astype(F32), (1, KV_HEADS))[:, None, :]
    head_id = jnp.arange(A_W) // HEAD_DIM
    p['bd'] = (head_id[:, None] == head_id[None, :]).astype(BF)
    p['sinks'] = sinks.astype(F32).reshape(-1)
    p['ssm'] = _prep_ssm(ssm_a_re, ssm_a_im, ssm_log_dt, ssm_b_re, ssm_b_im, ssm_c_re, ssm_c_im, ssm_d, w_glu)
    p['w_out_att'] = _heads_kv_major_to_g_major(w_out_even[:, :A_W, :], 1).astype(BF)
    p['w_out_ssm'] = w_out_even[:, A_W:, :].astype(BF)
    p['w_in_odd'] = w_in_odd.astype(BF)
    p['g_v'] = row1(g_v)
    p['w_out_odd'] = w_out_odd.astype(BF)
    cs = min(dec_seq, CHUNK)
    reps = CHUNK // cs
    p['gmlp_prompt'] = (w_spatial.astype(F32), jnp.swapaxes(b_spatial.astype(F32), 1, 2), CHUNK)
    p['gmlp_sample'] = (jnp.tile(w_spatial[:, :, :cs, :cs].astype(F32), (1, 1, reps, reps)),
                        jnp.tile(jnp.swapaxes(b_spatial[:, :, :cs].astype(F32), 1, 2), (1, reps, 1)), cs)
    p['w_xq'] = w_xq.astype(BF)
    p['g_xq'] = row1(g_xq)
    p['w_xo'] = w_xo.astype(BF)
    w_router = jnp.swapaxes(jnp.concatenate([w_fine, w_coarse], axis=2).astype(F32), 1, 2)
    w_router = jnp.pad(w_router, ((0, 0), (0, ROUTER_ROWS - w_router.shape[1]), (0, 0)))
    p['w_router_hi'] = w_router.astype(BF)
    p['w_router_lo'] = (w_router - p['w_router_hi'].astype(F32)).astype(BF)
    b_router = jnp.concatenate([b_fine, b_coarse], axis=1).astype(F32)
    p['b_router'] = jnp.pad(b_router, ((0, 0), (0, ROUTER_ROWS - b_router.shape[1])))[:, :, None]
    p['w_e1'], p['w_e3'], p['w_e2'] = w_e1.astype(BF), w_e3.astype(BF), w_e2.astype(BF)

    mem2d = mem_prompt.reshape(batch * MEM_LEN, D_MODEL)
    pk, pv = _memory_kv(mem2d, row1(g_mem), w_xk.astype(BF), row1(g_xk), w_xv.astype(BF), batch)
    y_p, p_wk, p_wv, p_hr, p_hi, _ = _run_trunk(x_prompt.reshape(batch * seq, D_MODEL), batch, seq, pk, pv, None, p)

    caches = (cache_win_k.reshape(n_even, dec_batch, WINDOW, KV_W), cache_win_v.reshape(n_even, dec_batch, WINDOW, KV_W),
              state_ssm_re.reshape(n_even, dec_batch, SSM_FLAT), state_ssm_im.reshape(n_even, dec_batch, SSM_FLAT))
    y_s, s_wk, s_wv, s_hr, s_hi, s_cv = _run_trunk(x_sample.reshape(dec_batch * dec_seq, D_MODEL), dec_batch, dec_seq,
                                                   cache_mem_k.reshape(DEPTH, dec_batch, MEM_ROWS, X_HEAD_DIM),
                                                   cache_mem_v.reshape(DEPTH, dec_batch, MEM_ROWS, X_HEAD_DIM), caches, p)

    mem_shape = (DEPTH, batch, MEM_LEN, X_HEADS, X_HEAD_DIM)
    return (y_p.reshape(batch, seq, D_MODEL), y_s.reshape(dec_batch, dec_seq, D_MODEL), p_wk, p_wv, p_hr, p_hi,
            pk.reshape(mem_shape), pv.reshape(mem_shape), s_wk, s_wv, s_hr, s_hi, s_cv)
```

```python
import functools
import math

import jax
import jax.numpy as jnp
from jax import lax
from jax.experimental import pallas as pl
from jax.experimental.pallas import tpu as pltpu

F32 = jnp.float32
BF = jnp.bfloat16

D_MODEL = 1024
DEPTH = 4
A_W = 512
HEAD_DIM = 64
A_HEADS = 8
KV_HEADS = 2
Q_PER_KV = 4
KV_W = 128
WINDOW = 128
SSM_W = 512
SSM_CH = 16
SSM_GROUPS = 32
SSM_STATE = 64
SSM_FLAT = SSM_GROUPS * SSM_STATE
IN_EVEN = A_W + 2 * KV_W + SSM_W
CHUNK = 128
GM_W = 2048
GM_HEADS = 8
GM_HD = 256
MEM_LEN = 256
X_HEADS = 4
X_HEAD_DIM = 128
X_W = 512
N_GROUPS = 4
EXPERTS_PER_GROUP = 4
N_EXPERTS = 16
D_EXPERT = 256
EPS = 1e-6
NEG_INF = -1e30
LANES = 128
VMEM_LIMIT_BYTES = 52 * 1024 * 1024
ALIBI_SLOPES = tuple(2.0 ** (-8.0 * (h + 1) / A_HEADS) for h in range(A_HEADS))
SSM_SEQ_BLOCK = 8


def _cparams(*sem):
    return pltpu.CompilerParams(dimension_semantics=sem, vmem_limit_bytes=VMEM_LIMIT_BYTES)


def _rms_rows(x, g):
    return x * lax.rsqrt(jnp.mean(x * x, axis=-1, keepdims=True) + EPS) * g


TT_CHUNKS = D_MODEL // LANES


def _read_tokens(ref, token_tiled):
    if not token_tiled:
        return ref[...]
    rows = ref.shape[0] // TT_CHUNKS
    return jnp.concatenate([ref[pl.ds(c, rows, stride=TT_CHUNKS), :] for c in range(TT_CHUNKS)], axis=1)


def _write_token_tiles(ref, val):
    rows = val.shape[0]
    for c in range(TT_CHUNKS):
        ref[pl.ds(c, rows, stride=TT_CHUNKS), :] = val[:, c * LANES:(c + 1) * LANES]


def _token_spec(rows, index, token_tiled):
    if token_tiled:
        return pl.BlockSpec((rows * TT_CHUNKS, LANES), lambda *ids: (index(*ids), 0))
    return pl.BlockSpec((rows, D_MODEL), lambda *ids: (index(*ids), 0))


def _untile_kernel(x_ref, o_ref):
    o_ref[...] = _read_tokens(x_ref, True)


def _untile(x, m, tm):
    return pl.pallas_call(
        _untile_kernel,
        grid=(m // tm,),
        in_specs=[_token_spec(tm, lambda i: i, True)],
        out_specs=pl.BlockSpec((tm, D_MODEL), lambda i: (i, 0)),
        out_shape=jax.ShapeDtypeStruct((m, D_MODEL), F32),
        compiler_params=_cparams("parallel"),
        name="untile",
    )(x)


def _dot(a, b):
    return jnp.dot(a, b, preferred_element_type=F32)


def _dot_nt(a, b):
    return lax.dot_general(a, b, (((1,), (1,)), ((), ())), preferred_element_type=F32)


def _full(shape):
    nd = len(shape)
    return pl.BlockSpec(shape, lambda *_: (0,) * nd)


def _layer_spec(li, *shape, single_buffer=False):
    nd = len(shape)
    mode = pl.Buffered(1) if single_buffer else None
    return pl.BlockSpec((None,) + shape, lambda *_: (li,) + (0,) * nd, pipeline_mode=mode)


def _even_in_kernel(token_tiled, x_ref, g_ref, w_ref, gq_ref, gk_ref, bd_ref, q_ref, k_ref, v_ref, u_ref):
    h = _rms_rows(_read_tokens(x_ref, token_tiled), g_ref[...]).astype(BF)
    z = _dot(h, w_ref[...])
    q = z[:, :A_W]
    k = z[:, A_W:A_W + KV_W]
    bd = bd_ref[...]
    qms = _dot((q * q).astype(BF), bd) * (1.0 / HEAD_DIM)
    kms = _dot((k * k).astype(BF), bd[:KV_W, :KV_W]) * (1.0 / HEAD_DIM)
    q_ref[...] = q * lax.rsqrt(qms + EPS) * gq_ref[...]
    k_ref[...] = k * lax.rsqrt(kms + EPS) * gk_ref[...]
    v_ref[...] = z[:, A_W + KV_W:A_W + 2 * KV_W]
    u_ref[...] = z[:, A_W + 2 * KV_W:]


def _time_major_spec(tm, n, seq, width):
    per_seq = seq // tm
    return pl.BlockSpec((tm, width), lambda i: (i % per_seq, i // per_seq))


def _even_in_proj(x, li, e, g, w, gq, gk, bd, tm, n, seq, token_tiled=False):
    m = n * seq
    row = lambda w_: pl.BlockSpec((tm, w_), lambda i: (i, 0))
    return pl.pallas_call(
        functools.partial(_even_in_kernel, token_tiled),
        grid=(m // tm,),
        in_specs=[_token_spec(tm, lambda i: i, token_tiled), _layer_spec(li, 1, D_MODEL),
                  _layer_spec(e, D_MODEL, IN_EVEN), _layer_spec(e, 1, A_W),
                  _layer_spec(e, 1, KV_W), _full((A_W, A_W))],
        out_specs=[row(A_W), row(KV_W), row(KV_W), _time_major_spec(tm, n, seq, SSM_W) if seq % tm == 0 else row(SSM_W)],
        out_shape=[jax.ShapeDtypeStruct((m, A_W), F32), jax.ShapeDtypeStruct((m, KV_W), F32),
                   jax.ShapeDtypeStruct((m, KV_W), F32),
                   jax.ShapeDtypeStruct((seq, n * SSM_W) if seq % tm == 0 else (m, SSM_W), F32)],
        compiler_params=_cparams("parallel"),
        name="even_in_proj",
    )(x, g, w, gq, gk, bd)


def _head_rows(q, lane_lo):
    rows = []
    for kv in range(KV_HEADS):
        keep = lane_lo if kv == 0 else jnp.logical_not(lane_lo)
        for g in range(Q_PER_KV):
            rows.append(jnp.where(keep, q[:, g * LANES:(g + 1) * LANES], 0.0))
    return jnp.concatenate(rows, axis=0)


def _head_cols(o, r, lane_lo):
    cols = []
    for g in range(Q_PER_KV):
        cols.append(jnp.where(lane_lo, o[g * r:(g + 1) * r], o[(Q_PER_KV + g) * r:(Q_PER_KV + g + 1) * r]))
    return jnp.concatenate(cols, axis=1)


def _swa_prompt_kernel(e, sinks_ref, q_ref, kc_ref, kp_ref, vc_ref, vp_ref, o_ref):
    b = pl.program_id(1)
    lane_lo = lax.broadcasted_iota(jnp.int32, (1, LANES), 1) < HEAD_DIM
    qrows = _head_rows(q_ref[...], lane_lo).astype(BF)
    kcat = jnp.concatenate([kp_ref[...], kc_ref[...]], axis=0).astype(BF)
    vcat = jnp.concatenate([vp_ref[...], vc_ref[...]], axis=0).astype(BF)
    s = _dot_nt(qrows, kcat) * (HEAD_DIM ** -0.5)
    r = lax.broadcasted_iota(jnp.int32, (WINDOW, 2 * WINDOW), 0)
    c = lax.broadcasted_iota(jnp.int32, (WINDOW, 2 * WINDOW), 1)
    dist = WINDOW + r - c
    valid = (dist >= 0) & (dist <= WINDOW) & ((c >= WINDOW) | (b > 0))
    dist_f = dist.astype(F32)
    ps = []
    for h in range(A_HEADS):
        sh = s[h * WINDOW:(h + 1) * WINDOW] - ALIBI_SLOPES[h] * dist_f
        sh = jnp.where(valid, sh, NEG_INF)
        sk = sinks_ref[e * A_HEADS + h]
        m = jnp.maximum(jnp.max(sh, axis=-1, keepdims=True), sk)
        p = jnp.exp(sh - m)
        den = jnp.sum(p, axis=-1, keepdims=True) + jnp.exp(sk - m)
        ps.append((p / den).astype(BF))
    o = _dot(jnp.concatenate(ps, axis=0), vcat)
    o_ref[...] = _head_cols(o, WINDOW, lane_lo).astype(o_ref.dtype)


def _swa_prompt(q, k, v, sinks, e, n, seq):
    nb = seq // WINDOW
    cur = lambda w_: pl.BlockSpec((WINDOW, w_), lambda i, b: (i * nb + b, 0))
    prev = lambda w_: pl.BlockSpec((WINDOW, w_), lambda i, b: (i * nb + jnp.maximum(b - 1, 0), 0))
    return pl.pallas_call(
        functools.partial(_swa_prompt_kernel, e),
        grid=(n, nb),
        in_specs=[pl.BlockSpec(memory_space=pltpu.SMEM), cur(A_W), cur(KV_W), prev(KV_W), cur(KV_W), prev(KV_W)],
        out_specs=cur(A_W),
        out_shape=jax.ShapeDtypeStruct((n * seq, A_W), BF),
        compiler_params=_cparams("parallel", "arbitrary"),
        name="swa_prompt",
    )(sinks, q, k, k, v, v)


SWA_SEQ_BLOCK = 8


def _swa_sample_kernel(t_new, e, sinks_ref, q_ref, kn_ref, vn_ref, wk_ref, wv_ref, o_ref, nwk_ref, nwv_ref):
    nseq = SWA_SEQ_BLOCK
    pair_rows = 2 * t_new
    npair = nseq // 2
    kn = kn_ref[...]
    vn = vn_ref[...]
    for i in range(nseq):
        nwk_ref[i, 0:WINDOW - t_new, :] = wk_ref[i, t_new:WINDOW, :]
        nwk_ref[i, WINDOW - t_new:WINDOW, :] = kn[i * t_new:(i + 1) * t_new, :]
        nwv_ref[i, 0:WINDOW - t_new, :] = wv_ref[i, t_new:WINDOW, :]
        nwv_ref[i, WINDOW - t_new:WINDOW, :] = vn[i * t_new:(i + 1) * t_new, :]

    lane_lo = lax.broadcasted_iota(jnp.int32, (1, LANES), 1) < HEAD_DIM
    q = q_ref[...]
    knb = kn.astype(BF)
    vnb = vn.astype(BF)
    hr = A_HEADS * pair_rows
    qrows = [_head_rows(q[j * pair_rows:(j + 1) * pair_rows], lane_lo).astype(BF) for j in range(npair)]
    scale = HEAD_DIM ** -0.5
    s_new_all = _dot_nt(jnp.concatenate(qrows, axis=0), knb) * scale

    tq = lax.broadcasted_iota(jnp.int32, (pair_rows, 2 * WINDOW), 0)
    cw = lax.broadcasted_iota(jnp.int32, (pair_rows, 2 * WINDOW), 1)
    dist_w = WINDOW + (tq % t_new) - (cw % WINDOW)
    valid_w = ((tq // t_new) == (cw // WINDOW)) & (dist_w <= WINDOW)
    dist_wf = dist_w.astype(F32)
    tqn = lax.broadcasted_iota(jnp.int32, (pair_rows, nseq * t_new), 0)
    cn = lax.broadcasted_iota(jnp.int32, (pair_rows, nseq * t_new), 1)
    dist_n = (tqn % t_new) - (cn % t_new)
    dist_nf = dist_n.astype(F32)

    p_new_all = []
    o_win_all = []
    for j in range(npair):
        kwin = jnp.concatenate([wk_ref[2 * j], wk_ref[2 * j + 1]], axis=0).astype(BF)
        vwin = jnp.concatenate([wv_ref[2 * j], wv_ref[2 * j + 1]], axis=0).astype(BF)
        s_win = _dot_nt(qrows[j], kwin) * scale
        valid_n = ((2 * j + tqn // t_new) == (cn // t_new)) & (dist_n >= 0)
        p_win = []
        for h in range(A_HEADS):
            sw = jnp.where(valid_w, s_win[h * pair_rows:(h + 1) * pair_rows] - ALIBI_SLOPES[h] * dist_wf, NEG_INF)
            sn = s_new_all[j * hr + h * pair_rows:j * hr + (h + 1) * pair_rows]
            sn = jnp.where(valid_n, sn - ALIBI_SLOPES[h] * dist_nf, NEG_INF)
            sk = sinks_ref[e * A_HEADS + h]
            m = jnp.maximum(jnp.maximum(jnp.max(sw, axis=-1, keepdims=True), jnp.max(sn, axis=-1, keepdims=True)), sk)
            pw = jnp.exp(sw - m)
            pn = jnp.exp(sn - m)
            den = jnp.sum(pw, axis=-1, keepdims=True) + jnp.sum(pn, axis=-1, keepdims=True) + jnp.exp(sk - m)
            p_win.append((pw / den).astype(BF))
            p_new_all.append((pn / den).astype(BF))
        o_win_all.append(_dot(jnp.concatenate(p_win, axis=0), vwin))
    o_new = _dot(jnp.concatenate(p_new_all, axis=0), vnb)
    outs = [_head_cols(o_win_all[j] + o_new[j * hr:(j + 1) * hr], pair_rows, lane_lo) for j in range(npair)]
    o_ref[...] = jnp.concatenate(outs, axis=0).astype(o_ref.dtype)


def _swa_sample(q, kn, vn, win_k, win_v, sinks, e, n, t_new):
    rows = SWA_SEQ_BLOCK * t_new
    row = lambda w_: pl.BlockSpec((rows, w_), lambda i: (i, 0))
    win = pl.BlockSpec((SWA_SEQ_BLOCK, WINDOW, KV_W), lambda i: (i, 0, 0))
    win_in = pl.BlockSpec((None, SWA_SEQ_BLOCK, WINDOW, KV_W), lambda i: (e, i, 0, 0))
    return pl.pallas_call(
        functools.partial(_swa_sample_kernel, t_new, e),
        grid=(n // SWA_SEQ_BLOCK,),
        in_specs=[pl.BlockSpec(memory_space=pltpu.SMEM), row(A_W), row(KV_W), row(KV_W), win_in, win_in],
        out_specs=[row(A_W), win, win],
        out_shape=[jax.ShapeDtypeStruct((n * t_new, A_W), BF),
                   jax.ShapeDtypeStruct((n, WINDOW, KV_W), F32), jax.ShapeDtypeStruct((n, WINDOW, KV_W), F32)],
        compiler_params=_cparams("parallel"),
        name="swa_sample",
    )(sinks, q, kn, vn, win_k, win_v)


SSM_LANE_CHUNK = 1024
SSM_IN_TILE = LANES
SSM_IN_TILES = SSM_W // SSM_IN_TILE
SSM_ST_TILE = SSM_IN_TILE // SSM_CH * SSM_STATE
SSM_OUT_TILE = 256
SSM_OUT_TILES = SSM_W // SSM_OUT_TILE
SSM_OUT_ST = SSM_OUT_TILE // SSM_CH * SSM_STATE


def _s5_kernel(tc, has_h0, *refs):
    if has_h0:
        (u_ref, h0r_ref, h0i_ref, abr_ref, abi_ref, bcat_ref, ccat_ref, d_ref, wglu_ref,
         o_ref, hr_ref, hi_ref, st_scr, car_scr) = refs
    else:
        (u_ref, abr_ref, abi_ref, bcat_ref, ccat_ref, d_ref, wglu_ref,
         o_ref, hr_ref, hi_ref, st_scr, car_scr) = refs
    nb = SSM_SEQ_BLOCK
    ci = pl.program_id(1)

    @pl.when(ci == 0)
    def _():
        if has_h0:
            car_scr[:, :SSM_FLAT] = h0r_ref[...]
            car_scr[:, SSM_FLAT:] = h0i_ref[...]
        else:
            car_scr[...] = jnp.zeros_like(car_scr)

    ut = u_ref[...].reshape(tc * nb, SSM_W)
    ub = ut.astype(BF)
    for jt in range(SSM_IN_TILES):
        bu = _dot(ub[:, jt * SSM_IN_TILE:(jt + 1) * SSM_IN_TILE], bcat_ref[jt])
        st_scr[:, jt * SSM_ST_TILE:(jt + 1) * SSM_ST_TILE] = bu[:, :SSM_ST_TILE]
        st_scr[:, SSM_FLAT + jt * SSM_ST_TILE:SSM_FLAT + (jt + 1) * SSM_ST_TILE] = bu[:, SSM_ST_TILE:]

    for lc in range(SSM_FLAT // SSM_LANE_CHUNK):
        lo = lc * SSM_LANE_CHUNK
        re_sl = slice(lo, lo + SSM_LANE_CHUNK)
        im_sl = slice(SSM_FLAT + lo, SSM_FLAT + lo + SSM_LANE_CHUNK)
        ar = jnp.broadcast_to(abr_ref[:, re_sl], (nb, SSM_LANE_CHUNK))
        ai = jnp.broadcast_to(abi_ref[:, re_sl], (nb, SSM_LANE_CHUNK))

        def step(t, carry):
            hr, hi = carry
            rows = pl.ds(pl.multiple_of(t * nb, nb), nb)
            nr = ar * hr - ai * hi + st_scr[rows, re_sl]
            ni = ar * hi + ai * hr + st_scr[rows, im_sl]
            st_scr[rows, re_sl] = nr
            st_scr[rows, im_sl] = ni
            return nr, ni

        hr, hi = lax.fori_loop(0, tc, step, (car_scr[:, re_sl], car_scr[:, im_sl]))
        car_scr[:, re_sl] = hr
        car_scr[:, im_sl] = hi

    ys = []
    for ot in range(SSM_OUT_TILES):
        re_sl = slice(ot * SSM_OUT_ST, (ot + 1) * SSM_OUT_ST)
        im_sl = slice(SSM_FLAT + ot * SSM_OUT_ST, SSM_FLAT + (ot + 1) * SSM_OUT_ST)
        ys.append(_dot(st_scr[:, re_sl].astype(BF), ccat_ref[ot, :SSM_OUT_ST, :])
                  + _dot(st_scr[:, im_sl].astype(BF), ccat_ref[ot, SSM_OUT_ST:, :]))
    y = jnp.concatenate(ys, axis=1) + d_ref[...] * ut
    y = jax.nn.gelu(y).astype(BF)
    g = _dot(y, wglu_ref[...])
    o_ref[...] = (g[:, :SSM_W] * jax.nn.sigmoid(g[:, SSM_W:])).reshape(tc, nb, SSM_W)

    @pl.when(ci == pl.num_programs(1) - 1)
    def _():
        hr_ref[...] = car_scr[:, :SSM_FLAT]
        hi_ref[...] = car_scr[:, SSM_FLAT:]


def _s5_glu(u, h0, ssm, e, n, seq, tc):
    nb = SSM_SEQ_BLOCK
    abr, abi, bcat, ccat, dsk, wglu = ssm
    u_spec = pl.BlockSpec((tc, nb, SSM_W), lambda i, c: (c, i, 0))
    st_spec = pl.BlockSpec((nb, SSM_FLAT), lambda i, c: (i, 0))
    h0_spec = pl.BlockSpec((None, nb, SSM_FLAT), lambda i, c: (e, i, 0))
    consts = [_layer_spec(e, 1, SSM_FLAT), _layer_spec(e, 1, SSM_FLAT),
              _layer_spec(e, SSM_IN_TILES, SSM_IN_TILE, 2 * SSM_ST_TILE),
              _layer_spec(e, SSM_OUT_TILES, 2 * SSM_OUT_ST, SSM_OUT_TILE), _layer_spec(e, 1, SSM_W),
              _layer_spec(e, SSM_W, 2 * SSM_W)]
    has_h0 = h0 is not None
    in_specs = [u_spec] + ([h0_spec, h0_spec] if has_h0 else []) + consts
    args = [u] + (list(h0) if has_h0 else []) + [abr, abi, bcat, ccat, dsk, wglu]
    return pl.pallas_call(
        functools.partial(_s5_kernel, tc, has_h0),
        grid=(n // nb, seq // tc),
        in_specs=in_specs,
        out_specs=[u_spec, st_spec, st_spec],
        out_shape=[jax.ShapeDtypeStruct((seq, n, SSM_W), F32), jax.ShapeDtypeStruct((n, SSM_FLAT), F32),
                   jax.ShapeDtypeStruct((n, SSM_FLAT), F32)],
        scratch_shapes=[pltpu.VMEM((nb * tc, 2 * SSM_FLAT), F32), pltpu.VMEM((nb, 2 * SSM_FLAT), F32)],
        compiler_params=_cparams("parallel", "arbitrary"),
        name="s5_glu",
    )(*args)


def _even_out_kernel(x_ref, a_ref, s_ref, wa_ref, ws_ref, o_ref):
    o_ref[...] = x_ref[...] + _dot(a_ref[...], wa_ref[...]) + _dot(s_ref[...].astype(BF), ws_ref[...])


def _even_out_proj(x, att, ssm, e, wa, ws, tm, n, seq):
    m = x.shape[0]
    row = lambda w_: pl.BlockSpec((tm, w_), lambda i: (i, 0))
    ssm_spec = _time_major_spec(tm, n, seq, SSM_W) if seq % tm == 0 else row(SSM_W)
    return pl.pallas_call(
        _even_out_kernel,
        grid=(m // tm,),
        in_specs=[row(D_MODEL), row(A_W), ssm_spec, _layer_spec(e, A_W, D_MODEL), _layer_spec(e, SSM_W, D_MODEL)],
        out_specs=row(D_MODEL),
        out_shape=jax.ShapeDtypeStruct((m, D_MODEL), F32),
        compiler_params=_cparams("parallel"),
        name="even_out_proj",
    )(x, att, ssm, wa, ws)


def _odd_in_kernel(token_tiled, x_ref, g_ref, w_ref, gv_ref, u_ref, v_ref):
    h = _rms_rows(_read_tokens(x_ref, token_tiled), g_ref[...]).astype(BF)
    u_ref[...] = jax.nn.gelu(_dot(h, w_ref[:, :GM_W])).astype(u_ref.dtype)
    zv = jax.nn.gelu(_dot(h, w_ref[:, GM_W:]))
    v_ref[...] = _rms_rows(zv, gv_ref[...]).astype(v_ref.dtype)


def _odd_in_proj(x, m, li, e, g, w, gv, tm, v_dtype, token_tiled=False):
    row = lambda w_: pl.BlockSpec((tm, w_), lambda i: (i, 0))
    return pl.pallas_call(
        functools.partial(_odd_in_kernel, token_tiled),
        grid=(m // tm,),
        in_specs=[_token_spec(tm, lambda i: i, token_tiled), _layer_spec(li, 1, D_MODEL),
                  _layer_spec(e, D_MODEL, 2 * GM_W, single_buffer=True),
                  _layer_spec(e, 1, GM_W)],
        out_specs=[row(GM_W), row(GM_W)],
        out_shape=[jax.ShapeDtypeStruct((m, GM_W), BF), jax.ShapeDtypeStruct((m, GM_W), v_dtype)],
        compiler_params=_cparams("parallel"),
        name="odd_in_proj",
    )(x, g, w, gv)


def _gmlp_rows(csize, x, u_ref, v_ref, wsp_ref, bsp_ref, wo_ref, gated_scr):
    tm = u_ref.shape[0]
    i = lax.broadcasted_iota(jnp.int32, (CHUNK, CHUNK), 0)
    j = lax.broadcasted_iota(jnp.int32, (CHUNK, CHUNK), 1)
    keep = (j <= i) & ((i // csize) == (j // csize))
    bsp = bsp_ref[...]
    for h in range(GM_HEADS):
        ws = jnp.where(keep, wsp_ref[h], 0.0).astype(BF)
        b_col = bsp[:, h:h + 1]
        cols = slice(h * GM_HD, (h + 1) * GM_HD)
        for c in range(tm // CHUNK):
            rows = slice(c * CHUNK, (c + 1) * CHUNK)
            mix = _dot(ws, v_ref[rows, cols].astype(BF)) + b_col
            gated_scr[rows, cols] = (u_ref[rows, cols].astype(F32) * mix).astype(BF)
    return x + _dot(gated_scr[...], wo_ref[...])


def _gmlp_mix_kernel(csize, x_ref, u_ref, v_ref, wsp_ref, bsp_ref, wo_ref, o_ref, gated_scr):
    o_ref[...] = _gmlp_rows(csize, x_ref[...], u_ref, v_ref, wsp_ref, bsp_ref, wo_ref, gated_scr)


def _gmlp_mix(x, u, v, e, wsp, bsp, wo, csize, tm):
    m = x.shape[0]
    row = lambda w_: pl.BlockSpec((tm, w_), lambda i: (i, 0))
    return pl.pallas_call(
        functools.partial(_gmlp_mix_kernel, csize),
        grid=(m // tm,),
        in_specs=[row(D_MODEL), row(GM_W), row(GM_W), _layer_spec(e, GM_HEADS, CHUNK, CHUNK),
                  _layer_spec(e, CHUNK, GM_HEADS), _layer_spec(e, GM_W, D_MODEL)],
        out_specs=row(D_MODEL),
        out_shape=jax.ShapeDtypeStruct((m, D_MODEL), F32),
        scratch_shapes=[pltpu.VMEM((tm, GM_W), BF)],
        compiler_params=_cparams("parallel"),
        name="gmlp_mix",
    )(x, u, v, wsp, bsp, wo)


def _head_norm(z, g):
    cols = []
    for h in range(X_HEADS):
        zh = z[:, h * X_HEAD_DIM:(h + 1) * X_HEAD_DIM]
        cols.append(_rms_rows(zh, g))
    return jnp.concatenate(cols, axis=1)


MEMKV_SEQ_BLOCK = 2
MEM_ROWS = MEM_LEN * X_HEADS


def _head_rows_of(h):
    return pl.ds(h, MEM_LEN, stride=X_HEADS)


def _memory_kv_kernel(mem_ref, g_ref, wk_ref, gk_ref, wv_ref, k_ref, v_ref):
    m = _rms_rows(mem_ref[...], g_ref[...]).astype(BF)
    k = _head_norm(_dot(m, wk_ref[...]), gk_ref[...])
    v = _dot(m, wv_ref[...])
    for s in range(MEMKV_SEQ_BLOCK):
        rows = slice(s * MEM_LEN, (s + 1) * MEM_LEN)
        for h in range(X_HEADS):
            cols = slice(h * X_HEAD_DIM, (h + 1) * X_HEAD_DIM)
            k_ref[s, _head_rows_of(h), :] = k[rows, cols]
            v_ref[s, _head_rows_of(h), :] = v[rows, cols]


def _memory_kv(mem, g_mem, w_k, g_k, w_v, n):
    tm = MEMKV_SEQ_BLOCK * MEM_LEN
    per_layer = lambda a, b: pl.BlockSpec((None, a, b), lambda l, i: (l, 0, 0))
    out_spec = pl.BlockSpec((None, MEMKV_SEQ_BLOCK, MEM_ROWS, X_HEAD_DIM), lambda l, i: (l, i, 0, 0))
    out_sds = jax.ShapeDtypeStruct((DEPTH, n, MEM_ROWS, X_HEAD_DIM), F32)
    return pl.pallas_call(
        _memory_kv_kernel,
        grid=(DEPTH, n // MEMKV_SEQ_BLOCK),
        in_specs=[pl.BlockSpec((tm, D_MODEL), lambda l, i: (i, 0)), per_layer(1, D_MODEL), per_layer(D_MODEL, X_W),
                  per_layer(1, X_HEAD_DIM), per_layer(D_MODEL, X_W)],
        out_specs=[out_spec, out_spec],
        out_shape=[out_sds, out_sds],
        compiler_params=_cparams("parallel", "parallel"),
        name="memory_kv",
    )(mem, g_mem, w_k, g_k, w_v)


def _xattn_rows(x, g_ref, wq_ref, gq_ref, mk_ref, mv_ref, wo_ref):
    q = _head_norm(_dot(_rms_rows(x, g_ref[...]).astype(BF), wq_ref[...]), gq_ref[...]).astype(BF)
    outs = []
    for h in range(X_HEADS):
        cols = slice(h * X_HEAD_DIM, (h + 1) * X_HEAD_DIM)
        s = _dot_nt(q[:, cols], mk_ref[_head_rows_of(h), :].astype(BF)) * (X_HEAD_DIM ** -0.5)
        m = jnp.max(s, axis=-1, keepdims=True)
        p = jnp.exp(s - m)
        p = (p / jnp.sum(p, axis=-1, keepdims=True)).astype(BF)
        outs.append(_dot(p, mv_ref[_head_rows_of(h), :].astype(BF)))
    o = jnp.concatenate(outs, axis=1).astype(BF)
    return x + _dot(o, wo_ref[...])


N_XATTN_REFS = 6
N_ROUTE_REFS = 4
GROUP_ROW = 0


def _xattn_route_store(x, refs, o_ref, grp_ref):
    y = _xattn_rows(x, *refs[:N_XATTN_REFS])
    _write_token_tiles(o_ref, y)
    gf_ref, whi_ref, wlo_ref, b_ref = refs[N_XATTN_REFS:]
    grp = _top_group(_router_logits(_rms_rows(y, gf_ref[...]), whi_ref[...], wlo_ref[...], b_ref[...]))
    grp_ref[...] = jnp.concatenate([grp, jnp.zeros((7, grp.shape[1]), jnp.int32)], axis=0)


def _even_out_xattn_kernel(token_tiled, x_ref, a_ref, s_ref, wa_ref, ws_ref, *rest):
    refs, o_ref, grp_ref = rest[:-2], rest[-2], rest[-1]
    x = (_read_tokens(x_ref, token_tiled) + _dot(a_ref[...], wa_ref[...])
         + _dot(s_ref[...].astype(BF), ws_ref[...]))
    _xattn_route_store(x, refs, o_ref, grp_ref)


def _gmlp_xattn_kernel(csize, token_tiled, x_ref, u_ref, v_ref, wsp_ref, bsp_ref, wo_ref, *rest):
    refs, o_ref, grp_ref, gated_scr = rest[:-3], rest[-3], rest[-2], rest[-1]
    x = _gmlp_rows(csize, _read_tokens(x_ref, token_tiled), u_ref, v_ref, wsp_ref, bsp_ref, wo_ref, gated_scr)
    _xattn_route_store(x, refs, o_ref, grp_ref)


def _xattn_route_specs(li):
    mem = pl.BlockSpec((None, None, MEM_ROWS, X_HEAD_DIM), lambda i, j: (li, i, 0, 0))
    return [_layer_spec(li, 1, D_MODEL), _layer_spec(li, D_MODEL, X_W), _layer_spec(li, 1, X_HEAD_DIM), mem, mem,
            _layer_spec(li, X_W, D_MODEL), _layer_spec(li, 1, D_MODEL), _layer_spec(li, ROUTER_ROWS, D_MODEL),
            _layer_spec(li, ROUTER_ROWS, D_MODEL), _layer_spec(li, ROUTER_ROWS, 1)]


def _token_tile_outputs(n, seq, tq):
    nq = seq // tq
    specs = [_token_spec(tq, lambda i, j: i * nq + j, True), pl.BlockSpec((None, 8, tq), lambda i, j: (i * nq + j, 0, 0))]
    shapes = [jax.ShapeDtypeStruct((n * seq * TT_CHUNKS, LANES), F32), jax.ShapeDtypeStruct((n * nq, 8, tq), jnp.int32)]
    return specs, shapes


def _even_out_xattn(x, token_tiled, att, ssm, e, wa, ws, li, xattn_route_args, n, seq, tq):
    nq = seq // tq
    row = lambda w_: pl.BlockSpec((tq, w_), lambda i, j: (i * nq + j, 0))
    ssm_spec = pl.BlockSpec((tq, SSM_W), lambda i, j: (j, i))
    out_specs, out_shape = _token_tile_outputs(n, seq, tq)
    y, grp = pl.pallas_call(
        functools.partial(_even_out_xattn_kernel, token_tiled),
        grid=(n, nq),
        in_specs=[_token_spec(tq, lambda i, j: i * nq + j, token_tiled), row(A_W), ssm_spec,
                  _layer_spec(e, A_W, D_MODEL), _layer_spec(e, SSM_W, D_MODEL)] + _xattn_route_specs(li),
        out_specs=out_specs,
        out_shape=out_shape,
        compiler_params=_cparams("parallel", "arbitrary"),
        name="even_out_xattn",
    )(x, att, ssm, wa, ws, *xattn_route_args)
    return y, grp[:, GROUP_ROW, :].reshape(n * seq)


def _gmlp_xattn(x, token_tiled, u, v, e, wsp, bsp, wo, csize, li, xattn_route_args, n, seq, tq):
    nq = seq // tq
    row = lambda w_: pl.BlockSpec((tq, w_), lambda i, j: (i * nq + j, 0))
    out_specs, out_shape = _token_tile_outputs(n, seq, tq)
    y, grp = pl.pallas_call(
        functools.partial(_gmlp_xattn_kernel, csize, token_tiled),
        grid=(n, nq),
        in_specs=[_token_spec(tq, lambda i, j: i * nq + j, token_tiled), row(GM_W), row(GM_W),
                  _layer_spec(e, GM_HEADS, CHUNK, CHUNK), _layer_spec(e, CHUNK, GM_HEADS),
                  _layer_spec(e, GM_W, D_MODEL)] + _xattn_route_specs(li),
        out_specs=out_specs,
        out_shape=out_shape,
        scratch_shapes=[pltpu.VMEM((tq, GM_W), BF)],
        compiler_params=_cparams("parallel", "arbitrary"),
        name="gmlp_xattn",
    )(x, u, v, wsp, bsp, wo, *xattn_route_args)
    return y, grp[:, GROUP_ROW, :].reshape(n * seq)


XATTN_SEQ_BLOCK = 8


def _xattn_sample_kernel(t_new, x_ref, g_ref, wq_ref, gq_ref, mk_ref, mv_ref, wo_ref, o_ref):
    pair_rows = 2 * t_new
    x = x_ref[...]
    q = _head_norm(_dot(_rms_rows(x, g_ref[...]).astype(BF), wq_ref[...]), gq_ref[...]).astype(BF)
    tq = lax.broadcasted_iota(jnp.int32, (pair_rows, 2 * MEM_LEN), 0)
    cm = lax.broadcasted_iota(jnp.int32, (pair_rows, 2 * MEM_LEN), 1)
    same_seq = (tq // t_new) == (cm // MEM_LEN)
    npair = XATTN_SEQ_BLOCK // 2
    scores = []
    for j in range(npair):
        qp = q[j * pair_rows:(j + 1) * pair_rows]
        for h in range(X_HEADS):
            hr = _head_rows_of(h)
            kp = jnp.concatenate([mk_ref[2 * j, hr, :], mk_ref[2 * j + 1, hr, :]], axis=0).astype(BF)
            s = _dot_nt(qp[:, h * X_HEAD_DIM:(h + 1) * X_HEAD_DIM], kp)
            scores.append(jnp.where(same_seq, s, NEG_INF))
    s = jnp.concatenate(scores, axis=0) * (X_HEAD_DIM ** -0.5)
    m = jnp.max(s, axis=-1, keepdims=True)
    p = jnp.exp(s - m)
    p = p / jnp.sum(p, axis=-1, keepdims=True)
    outs = []
    for j in range(npair):
        heads = []
        for h in range(X_HEADS):
            hr = _head_rows_of(h)
            vp = jnp.concatenate([mv_ref[2 * j, hr, :], mv_ref[2 * j + 1, hr, :]], axis=0).astype(BF)
            r0 = (j * X_HEADS + h) * pair_rows
            heads.append(_dot(p[r0:r0 + pair_rows].astype(BF), vp))
        outs.append(jnp.concatenate(heads, axis=1))
    o_all = jnp.concatenate(outs, axis=0).astype(BF)
    o_ref[...] = x + _dot(o_all, wo_ref[...])


def _xattn_sample(x, li, g, wq, gq, mk, mv, wo, n, t_new):
    rows = XATTN_SEQ_BLOCK * t_new
    row = pl.BlockSpec((rows, D_MODEL), lambda i: (i, 0))
    mem = pl.BlockSpec((None, XATTN_SEQ_BLOCK, MEM_ROWS, X_HEAD_DIM), lambda i: (li, i, 0, 0))
    return pl.pallas_call(
        functools.partial(_xattn_sample_kernel, t_new),
        grid=(n // XATTN_SEQ_BLOCK,),
        in_specs=[row, _layer_spec(li, 1, D_MODEL), _layer_spec(li, D_MODEL, X_W), _layer_spec(li, 1, X_HEAD_DIM),
                  mem, mem, _layer_spec(li, X_W, D_MODEL)],
        out_specs=row,
        out_shape=jax.ShapeDtypeStruct((n * t_new, D_MODEL), F32),
        compiler_params=_cparams("parallel"),
        name="xattn_sample",
    )(x, g, wq, gq, mk, mv, wo)


ROUTER_ROWS = 32
ROUTER_COARSE_ROW = N_EXPERTS
BIG = 3.0e38


def _first_argmax(vals, vmax):
    idx = jnp.full(vmax.shape, len(vals) - 1, jnp.int32)
    for j in range(len(vals) - 2, -1, -1):
        idx = jnp.where(vals[j] == vmax, j, idx)
    return idx


def _router_logits(h, whi, wlo, bias):
    h_hi = h.astype(BF)
    h_lo = (h - h_hi.astype(F32)).astype(BF)
    return _dot_nt(whi, h_hi) + _dot_nt(whi, h_lo) + _dot_nt(wlo, h_hi) + bias


def _coarse_rows(lg):
    return [lg[ROUTER_COARSE_ROW + g:ROUTER_COARSE_ROW + g + 1, :] for g in range(N_GROUPS)]


def _top_group(lg):
    coarse = _coarse_rows(lg)
    return _first_argmax(coarse, functools.reduce(jnp.maximum, coarse))


def _in_group_gates(lg, grp):
    pick = lambda rows: functools.reduce(lambda f, g: jnp.where(grp == g, rows[g], f),
                                         range(N_GROUPS - 2, -1, -1), rows[N_GROUPS - 1])
    coarse = _coarse_rows(lg)
    csel = pick(coarse)
    p_grp = 1.0 / functools.reduce(lambda a, b: a + b, [jnp.exp(c - csel) for c in coarse])
    fine = [pick([lg[g * EXPERTS_PER_GROUP + j:g * EXPERTS_PER_GROUP + j + 1, :] for g in range(N_GROUPS)])
            for j in range(EXPERTS_PER_GROUP)]
    v1 = functools.reduce(jnp.maximum, fine)
    i1 = _first_argmax(fine, v1)
    rest = [jnp.where(i1 == j, -BIG, fine[j]) for j in range(EXPERTS_PER_GROUP)]
    v2 = functools.reduce(jnp.maximum, rest)
    i2 = _first_argmax(rest, v2)
    e2 = jnp.exp(v2 - v1)
    w1 = p_grp / (1.0 + e2)
    w2 = p_grp * e2 / (1.0 + e2)
    return [jnp.where(i1 == j, w1, 0.0) + jnp.where(i2 == j, w2, 0.0) for j in range(EXPERTS_PER_GROUP)]


def _rows_to_lanes(rows, tm):
    pad = jnp.zeros((LANES - len(rows), tm), F32)
    return jnp.concatenate(list(rows) + [pad], axis=0).T


def _router_gates(h, whi, wlo, bias):
    lg = _router_logits(h, whi, wlo, bias)
    grp = _top_group(lg)
    in_group = _in_group_gates(lg, grp)
    rows = [jnp.where(grp == e // EXPERTS_PER_GROUP, in_group[e % EXPERTS_PER_GROUP], 0.0) for e in range(N_EXPERTS)]
    return _rows_to_lanes(rows, lg.shape[1])


def _moe_kernel(x_ref, g_ref, whi_ref, wlo_ref, b_ref, w1_ref, w3_ref, w2_ref, o_ref, h_scr, gates_scr, acc_scr):
    gi = pl.program_id(1)

    @pl.when(gi == 0)
    def _():
        h = _rms_rows(x_ref[...], g_ref[...])
        h_scr[...] = h.astype(BF)
        gates_scr[...] = _router_gates(h, whi_ref[...], wlo_ref[...], b_ref[...])
        acc_scr[...] = jnp.zeros_like(acc_scr)

    h = h_scr[...]
    gates = gates_scr[...]
    lane = lax.broadcasted_iota(jnp.int32, gates.shape, 1)
    for j in range(EXPERTS_PER_GROUP):
        gcol = jnp.sum(jnp.where(lane == gi * EXPERTS_PER_GROUP + j, gates, 0.0), axis=-1, keepdims=True)
        a = _dot(h, w1_ref[j])
        b = _dot(h, w3_ref[j])
        hid = (a * jax.nn.sigmoid(a)) * b * gcol
        acc_scr[...] += _dot(hid.astype(BF), w2_ref[j])

    @pl.when(gi == pl.num_programs(1) - 1)
    def _():
        o_ref[...] = x_ref[...] + acc_scr[...]


def _moe(x, li, g, whi, wlo, b, w1, w3, w2, tm):
    m = x.shape[0]
    row = lambda w_: pl.BlockSpec((tm, w_), lambda i, e: (i, 0))
    group_w = lambda a, b_: pl.BlockSpec((None, EXPERTS_PER_GROUP, a, b_), lambda i, e: (li, e, 0, 0))
    return pl.pallas_call(
        _moe_kernel,
        grid=(m // tm, N_GROUPS),
        in_specs=[row(D_MODEL), _layer_spec(li, 1, D_MODEL), _layer_spec(li, ROUTER_ROWS, D_MODEL),
                  _layer_spec(li, ROUTER_ROWS, D_MODEL), _layer_spec(li, ROUTER_ROWS, 1),
                  group_w(D_MODEL, D_EXPERT), group_w(D_MODEL, D_EXPERT), group_w(D_EXPERT, D_MODEL)],
        out_specs=row(D_MODEL),
        out_shape=jax.ShapeDtypeStruct((m, D_MODEL), F32),
        scratch_shapes=[pltpu.VMEM((tm, D_MODEL), BF), pltpu.VMEM((tm, LANES), F32), pltpu.VMEM((tm, D_MODEL), F32)],
        compiler_params=_cparams("parallel", "arbitrary"),
        name="moe",
    )(x, g, whi, wlo, b, w1, w3, w2)


MOE_TILE = 256
MOE_SPAN = 1280


def _sorted_plan(grp, m):
    t = MOE_TILE
    n_tiles = m // t + N_GROUPS
    order = jnp.argsort(grp, stable=True).astype(jnp.int32)
    counts = jnp.sum((grp[None, :] == jnp.arange(N_GROUPS, dtype=jnp.int32)[:, None]).astype(jnp.int32), axis=1)
    tiles_per_group = (counts + t - 1) // t
    tile_end = jnp.cumsum(tiles_per_group)
    tile_start = tile_end - tiles_per_group
    first_token = jnp.cumsum(counts) - counts
    s = jnp.arange(n_tiles, dtype=jnp.int32)
    tile_grp = jnp.minimum(jnp.sum((s[:, None] >= tile_end[None, :]).astype(jnp.int32), axis=1), N_GROUPS - 1)
    n_active = tile_end[N_GROUPS - 1]
    tile_in_group = s - tile_start[tile_grp]
    n_valid = jnp.where(s < n_active, jnp.clip(counts[tile_grp] - tile_in_group * t, 0, t), 0)
    r = jnp.arange(t, dtype=jnp.int32)
    slot_token = order[jnp.clip(first_token[tile_grp][:, None] + tile_in_group[:, None] * t + r[None, :], 0, m - 1)]
    valid = r[None, :] < n_valid[:, None]
    lo = jnp.where(n_valid > 0, slot_token[:, 0], 0)
    hi = jnp.where(n_valid > 0, jnp.take_along_axis(slot_token, jnp.maximum(n_valid - 1, 0)[:, None], axis=1)[:, 0], 0)
    base = jnp.clip(jnp.minimum(lo, m - MOE_SPAN), 0, None)
    fits = (hi - base < MOE_SPAN).astype(jnp.int32)
    src = jnp.where(valid, slot_token, lo[:, None])
    rel = jnp.where(fits[:, None] == 1, src - base[:, None], 0)
    dst = jnp.where(valid, slot_token, m + (s % 2)[:, None] * t + r[None, :]).reshape(-1)
    dst = jnp.concatenate([m + t + r, dst])
    zero = jnp.zeros((1,), jnp.int32)
    rows = lambda a: (a * TT_CHUNKS).astype(jnp.int32)
    return (rows(src.reshape(-1)), rows(rel.reshape(-1)), rows(jnp.concatenate([base, zero])),
            jnp.concatenate([fits, zero]).astype(jnp.int32), rows(dst), tile_grp.astype(jnp.int32))


def _sorted_moe_kernel(m, src_ref, rel_ref, base_ref, fits_ref, dst_ref, tgrp_ref, x_hbm, g_ref, whi_ref, wlo_ref,
                       b_ref, w1_ref, w3_ref, w2_ref, out_hbm, span0, span1, xbuf, obuf0, obuf1, sem_f, sem_g, sem_s):
    t = MOE_TILE
    i = pl.program_id(0)
    last = pl.num_programs(0) - 1
    span = (span0, span1)
    obuf = (obuf0, obuf1)
    c8 = TT_CHUNKS

    def span_fetch(tile, slot):
        start = pl.multiple_of(base_ref[tile], c8)
        return pltpu.make_async_copy(x_hbm.at[pl.ds(start, MOE_SPAN * c8), :], span[slot], sem_f.at[slot])

    def gather_row(tile, r):
        src = pl.multiple_of(src_ref[tile * t + r], c8)
        return pltpu.make_async_copy(x_hbm.at[pl.ds(src, c8), :], xbuf.at[pl.ds(pl.multiple_of(r * c8, c8), c8), :],
                                     sem_g)

    def scatter_row(tile, slot, r):
        dst = pl.multiple_of(dst_ref[(tile + 1) * t + r], c8)
        return pltpu.make_async_copy(obuf[slot].at[pl.ds(r * c8, c8), :], out_hbm.at[pl.ds(dst, c8), :],
                                     sem_s.at[slot])

    def whole_gather():
        return pltpu.make_async_copy(x_hbm.at[pl.ds(0, t * c8), :], xbuf, sem_g)

    def whole_scatter(slot, row0):
        return pltpu.make_async_copy(obuf[slot], out_hbm.at[pl.ds(row0, t * c8), :], sem_s.at[slot])

    @pl.when(i == 0)
    def _():
        for slot in range(2):
            obuf[slot][...] = jnp.zeros((t * c8, LANES), F32)
        for slot in range(2):
            whole_scatter(slot, (m + slot * t) * c8).start()
        for slot in range(2):
            whole_scatter(slot, (m + slot * t) * c8).wait()

        @pl.when(fits_ref[0] == 1)
        def _():
            span_fetch(0, 0).start()

    def step(cur, nxt):
        @pl.when(fits_ref[i + 1] == 1)
        def _():
            span_fetch(i + 1, nxt).start()

        @pl.when(fits_ref[i] == 1)
        def _():
            span_fetch(i, cur).wait()

            def compact(r, carry):
                row = pl.multiple_of(r * c8, c8)
                xbuf[pl.ds(row, c8), :] = span[cur][pl.ds(pl.multiple_of(rel_ref[i * t + r], c8), c8), :]
                return carry
            lax.fori_loop(0, t, compact, 0, unroll=True)

        @pl.when(fits_ref[i] == 0)
        def _():
            def issue(r, carry):
                gather_row(i, r).start()
                return carry
            lax.fori_loop(0, t, issue, 0)
            whole_gather().wait()

        for r in range(t):
            scatter_row(i - 1, nxt, r).start()

        x = _read_tokens(xbuf, True)
        h = _rms_rows(x, g_ref[...])
        lg = _router_logits(h, whi_ref[...], wlo_ref[...], b_ref[...])
        gates = _rows_to_lanes(_in_group_gates(lg, tgrp_ref[i]), t)
        hb = h.astype(BF)
        acc = jnp.zeros((t, D_MODEL), F32)
        for j in range(EXPERTS_PER_GROUP):
            a = _dot(hb, w1_ref[j])
            b = _dot(hb, w3_ref[j])
            hid = (a * jax.nn.sigmoid(a)) * b * gates[:, j:j + 1]
            acc = acc + _dot(hid.astype(BF), w2_ref[j])
        res = x + acc

        @pl.when(i >= 1)
        def _():
            whole_scatter(cur, 0).wait()

        _write_token_tiles(obuf[cur], res)

        @pl.when(i == last)
        def _():
            def issue(r, carry):
                scatter_row(i, cur, r).start()
                return carry
            lax.fori_loop(0, t, issue, 0)
            whole_scatter(cur, 0).wait()
            whole_scatter(nxt, 0).wait()

    for parity in range(2):
        @pl.when(i % 2 == parity)
        def _():
            step(parity, 1 - parity)


def _sorted_moe(x, grp, m, li, g, whi, wlo, b, w1, w3, w2):
    t = MOE_TILE
    assert m >= MOE_SPAN
    n_tiles = m // t + N_GROUPS
    plan = _sorted_plan(grp, m)
    group_w = lambda a, b_: pl.BlockSpec((None, EXPERTS_PER_GROUP, a, b_),
                                         lambda i, *prefetch: (li, prefetch[-1][i], 0, 0))
    tile_buf = pltpu.VMEM((t * TT_CHUNKS, LANES), F32)
    span_buf = pltpu.VMEM((MOE_SPAN * TT_CHUNKS, LANES), F32)
    return pl.pallas_call(
        functools.partial(_sorted_moe_kernel, m),
        grid_spec=pltpu.PrefetchScalarGridSpec(
            num_scalar_prefetch=len(plan),
            grid=(n_tiles,),
            in_specs=[pl.BlockSpec(memory_space=pl.ANY), _layer_spec(li, 1, D_MODEL),
                      _layer_spec(li, ROUTER_ROWS, D_MODEL), _layer_spec(li, ROUTER_ROWS, D_MODEL),
                      _layer_spec(li, ROUTER_ROWS, 1),
                      group_w(D_MODEL, D_EXPERT), group_w(D_MODEL, D_EXPERT), group_w(D_EXPERT, D_MODEL)],
            out_specs=pl.BlockSpec(memory_space=pl.ANY),
            scratch_shapes=[span_buf, span_buf, tile_buf, tile_buf, tile_buf, pltpu.SemaphoreType.DMA((2,)),
                            pltpu.SemaphoreType.DMA, pltpu.SemaphoreType.DMA((2,))]),
        out_shape=jax.ShapeDtypeStruct(((m + 2 * t) * TT_CHUNKS, LANES), F32),
        compiler_params=_cparams("arbitrary"),
        name="moe_sorted",
    )(*plan, x, g, whi, wlo, b, w1, w3, w2)


def _heads_kv_major_to_g_major(w, axis):
    shape = w.shape
    split = shape[:axis] + (KV_HEADS, Q_PER_KV, HEAD_DIM) + shape[axis + 1:]
    return jnp.swapaxes(w.reshape(split), axis, axis + 1).reshape(shape)


def _prep_ssm(a_re, a_im, log_dt, b_re, b_im, c_re, c_im, d_skip, w_glu):
    ar, ai = a_re.astype(F32), a_im.astype(F32)
    dt = jnp.exp(log_dt.astype(F32))[..., None]
    mag = jnp.exp(ar * dt)
    abr, abi = mag * jnp.cos(ai * dt), mag * jnp.sin(ai * dt)
    den = ar * ar + ai * ai
    kr = ((abr - 1.0) * ar + abi * ai) / den
    ki = (abi * ar - (abr - 1.0) * ai) / den
    br, bi = b_re.astype(F32), b_im.astype(F32)
    bbr = kr[..., None] * br - ki[..., None] * bi
    bbi = kr[..., None] * bi + ki[..., None] * br
    n_layers = ar.shape[0]
    eye = jnp.eye(SSM_GROUPS, dtype=F32)
    blockdiag_in = lambda b: jnp.einsum('egph,gk->eghkp', b, eye).reshape(n_layers, SSM_W, SSM_FLAT)
    blockdiag_out = lambda c: jnp.einsum('eghp,gk->egpkh', c, eye).reshape(n_layers, SSM_FLAT, SSM_W)
    in_tiles = lambda b: jnp.stack([b[:, j * SSM_IN_TILE:(j + 1) * SSM_IN_TILE, j * SSM_ST_TILE:(j + 1) * SSM_ST_TILE]
                                    for j in range(SSM_IN_TILES)], axis=1)
    out_tiles = lambda c: jnp.stack([c[:, j * SSM_OUT_ST:(j + 1) * SSM_OUT_ST, j * SSM_OUT_TILE:(j + 1) * SSM_OUT_TILE]
                                     for j in range(SSM_OUT_TILES)], axis=1)
    bcat = jnp.concatenate([in_tiles(blockdiag_in(bbr)), in_tiles(blockdiag_in(bbi))], axis=3).astype(BF)
    ccat = jnp.concatenate([out_tiles(blockdiag_out(c_re.astype(F32))), -out_tiles(blockdiag_out(c_im.astype(F32)))],
                           axis=2).astype(BF)
    return (abr.reshape(n_layers, 1, SSM_FLAT), abi.reshape(n_layers, 1, SSM_FLAT), bcat, ccat,
            d_skip.astype(F32).reshape(n_layers, 1, SSM_W), w_glu.astype(BF))


def _run_trunk(x, n, seq, mem_k, mem_v, caches, p):
    prompt = caches is None
    tm = 512
    m = n * seq
    wk_out, wv_out, hr_out, hi_out, cv_out = [], [], [], [], []
    xattn_args = (p['g_xattn'], p['w_xq'], p['g_xq'], mem_k, mem_v, p['w_xo'])
    router_args = (p['g_ffn'], p['w_router_hi'], p['w_router_lo'], p['b_router'])
    expert_args = (p['w_e1'], p['w_e3'], p['w_e2'])
    tiled = False
    for li in range(DEPTH):
        e = li // 2
        if li % 2 == 0:
            q, k, v, u = _even_in_proj(x, li, e, p['g_mix'], p['w_in_even'], p['g_q'], p['g_k'], p['bd'], tm, n, seq,
                                       token_tiled=tiled)
            if prompt:
                att = _swa_prompt(q, k, v, p['sinks'], e, n, seq)
                k3, v3 = k.reshape(n, seq, KV_W), v.reshape(n, seq, KV_W)
                new_k, new_v = k3[:, seq - WINDOW:], v3[:, seq - WINDOW:]
                ssm, h_re, h_im = _s5_glu(u.reshape(seq, n, SSM_W), None, p['ssm'], e, n, seq, 128)
                x, grp = _even_out_xattn(x, tiled, att, ssm.reshape(seq, n * SSM_W), e, p['w_out_att'], p['w_out_ssm'],
                                         li, xattn_args + router_args, n, seq, tm)
            else:
                win_k, win_v, ssm_re, ssm_im = caches
                att, new_k, new_v = _swa_sample(q, k, v, win_k, win_v, p['sinks'], e, n, seq)
                u_tm = jnp.swapaxes(u.reshape(n, seq, SSM_W), 0, 1)
                ssm, h_re, h_im = _s5_glu(u_tm, (ssm_re, ssm_im), p['ssm'], e, n, seq, seq)
                ssm = jnp.swapaxes(ssm, 0, 1).reshape(m, SSM_W)
                x = _even_out_proj(x, att, ssm, e, p['w_out_att'], p['w_out_ssm'], tm, n, seq)
            wk_out.append(new_k.reshape(n, WINDOW, KV_HEADS, HEAD_DIM))
            wv_out.append(new_v.reshape(n, WINDOW, KV_HEADS, HEAD_DIM))
            hr_out.append(h_re.reshape(n, SSM_GROUPS, SSM_STATE))
            hi_out.append(h_im.reshape(n, SSM_GROUPS, SSM_STATE))
        else:
            u, v = _odd_in_proj(x, m, li, e, p['g_mix'], p['w_in_odd'], p['g_v'], tm, BF if prompt else F32,
                                token_tiled=tiled)
            wsp, bsp, csize = p['gmlp_prompt'] if prompt else p['gmlp_sample']
            if prompt:
                x, grp = _gmlp_xattn(x, tiled, u, v, e, wsp, bsp, p['w_out_odd'], csize, li, xattn_args + router_args,
                                     n, seq, tm)
            else:
                x = _gmlp_mix(x, u, v, e, wsp, bsp, p['w_out_odd'], csize, tm)
                cv_out.append(v.reshape(n, seq, GM_W))
        if prompt:
            x = _sorted_moe(x, grp, m, li, *router_args, *expert_args)
            tiled = True
        else:
            x = _xattn_sample(x, li, *xattn_args, n, seq)
            x = _moe(x, li, *router_args, *expert_args, tm)
    if tiled:
        x = _untile(x, m, tm)
    cv = None if prompt else jnp.stack(cv_out)
    return x, jnp.stack(wk_out), jnp.stack(wv_out), jnp.stack(hr_out), jnp.stack(hi_out), cv


def kernel(x_prompt, x_sample, cache_win_k, cache_win_v, state_ssm_re, state_ssm_im, cache_mem_k, cache_mem_v, mem_prompt, g_mix, g_xattn, g_ffn, g_mem, w_in_even, g_q, g_k, sinks, ssm_a_re, ssm_a_im, ssm_log_dt, ssm_b_re, ssm_b_im, ssm_c_re, ssm_c_im, ssm_d, w_glu, w_out_even, w_in_odd, g_v, w_spatial, b_spatial, w_out_odd, w_xq, g_xq, w_xk, g_xk, w_xv, w_xo, w_coarse, b_coarse, w_fine, b_fine, w_e1, w_e3, w_e2):
    batch, seq = x_prompt.shape[0], x_prompt.shape[1]
    dec_batch, dec_seq = x_sample.shape[0], x_sample.shape[1]
    n_even = w_in_even.shape[0]

    row1 = lambda a: a.astype(F32)[:, None, :]
    p = {}
    p['g_mix'], p['g_xattn'], p['g_ffn'] = row1(g_mix), row1(g_xattn), row1(g_ffn)
    w_in_q = _heads_kv_major_to_g_major(w_in_even[:, :, :A_W], 2)
    p['w_in_even'] = jnp.concatenate([w_in_q, w_in_even[:, :, A_W:]], axis=2).astype(BF)
    p['g_q'] = jnp.tile(g_q.astype(F32), (1, A_HEADS))[:, None, :]
    p['g_k'] = jnp.tile(g_k.astype(F32), (1, KV_HEADS))[:, None, :]
    head_id = jnp.arange(A_W) // HEAD_DIM
    p['bd'] = (head_id[:, None] == head_id[None, :]).astype(BF)
    p['sinks'] = sinks.astype(F32).reshape(-1)
    p['ssm'] = _prep_ssm(ssm_a_re, ssm_a_im, ssm_log_dt, ssm_b_re, ssm_b_im, ssm_c_re, ssm_c_im, ssm_d, w_glu)
    p['w_out_att'] = _heads_kv_major_to_g_major(w_out_even[:, :A_W, :], 1).astype(BF)
    p['w_out_ssm'] = w_out_even[:, A_W:, :].astype(BF)
    p['w_in_odd'] = w_in_odd.astype(BF)
    p['g_v'] = row1(g_v)
    p['w_out_odd'] = w_out_odd.astype(BF)
    cs = min(dec_seq, CHUNK)
    reps = CHUNK // cs
    p['gmlp_prompt'] = (w_spatial.astype(F32), jnp.swapaxes(b_spatial.astype(F32), 1, 2), CHUNK)
    p['gmlp_sample'] = (jnp.tile(w_spatial[:, :, :cs, :cs].astype(F32), (1, 1, reps, reps)),
                        jnp.tile(jnp.swapaxes(b_spatial[:, :, :cs].astype(F32), 1, 2), (1, reps, 1)), cs)
    p['w_xq'] = w_xq.astype(BF)
    p['g_xq'] = row1(g_xq)
    p['w_xo'] = w_xo.astype(BF)
    w_router = jnp.swapaxes(jnp.concatenate([w_fine, w_coarse], axis=2).astype(F32), 1, 2)
    w_router = jnp.pad(w_router, ((0, 0), (0, ROUTER_ROWS - w_router.shape[1]), (0, 0)))
    p['w_router_hi'] = w_router.astype(BF)
    p['w_router_lo'] = (w_router - p['w_router_hi'].astype(F32)).astype(BF)
    b_router = jnp.concatenate([b_fine, b_coarse], axis=1).astype(F32)
    p['b_router'] = jnp.pad(b_router, ((0, 0), (0, ROUTER_ROWS - b_router.shape[1])))[:, :, None]
    p['w_e1'], p['w_e3'], p['w_e2'] = w_e1.astype(BF), w_e3.astype(BF), w_e2.astype(BF)

    mem2d = mem_prompt.reshape(batch * MEM_LEN, D_MODEL)
    pk, pv = _memory_kv(mem2d, row1(g_mem), w_xk.astype(BF), row1(g_xk), w_xv.astype(BF), batch)
    y_p, p_wk, p_wv, p_hr, p_hi, _ = _run_trunk(x_prompt.reshape(batch * seq, D_MODEL), batch, seq, pk, pv, None, p)

    caches = (cache_win_k.reshape(n_even, dec_batch, WINDOW, KV_W), cache_win_v.reshape(n_even, dec_batch, WINDOW, KV_W),
              state_ssm_re.reshape(n_even, dec_batch, SSM_FLAT), state_ssm_im.reshape(n_even, dec_batch, SSM_FLAT))
    y_s, s_wk, s_wv, s_hr, s_hi, s_cv = _run_trunk(x_sample.reshape(dec_batch * dec_seq, D_MODEL), dec_batch, dec_seq,
                                                   cache_mem_k.reshape(DEPTH, dec_batch, MEM_ROWS, X_HEAD_DIM),
                                                   cache_mem_v.reshape(DEPTH, dec_batch, MEM_ROWS, X_HEAD_DIM), caches, p)

    mem_shape = (DEPTH, batch, MEM_LEN, X_HEADS, X_HEAD_DIM)
    return (y_p.reshape(batch, seq, D_MODEL), y_s.reshape(dec_batch, dec_seq, D_MODEL), p_wk, p_wv, p_hr, p_hi,
            pk.reshape(mem_shape), pv.reshape(mem_shape), s_wk, s_wv, s_hr, s_hi, s_cv)
```
